```python
import jax, jax.numpy as jnp
from jax import lax
import numpy as np

D_MODEL = 1024
BATCH = 4
SEQ = 8192
DEPTH = 1

CHUNK = 64
MIX_WIDTH = D_MODEL
LRU_WIDTH = MIX_WIDTH // 2
LRU_HEADS = 8
LRU_HEAD_DIM = LRU_WIDTH // LRU_HEADS
CONV_WIDTH = 4
LRU_C = 8.0
RWKV_WIDTH = MIX_WIDTH - LRU_WIDTH
RWKV_HEAD_DIM = 64
RWKV_HEADS = RWKV_WIDTH // RWKV_HEAD_DIM
DECAY_LORA = 64
AAA_LORA = 64
GATE_LORA = 128
RWKV_PROJ = 3 * RWKV_WIDTH + DECAY_LORA + AAA_LORA + GATE_LORA
PROJ_WIDTH = 2 * LRU_WIDTH + RWKV_PROJ
N_GROUPS = 4
EXPERTS_PER_GROUP = 8
N_EXPERTS = N_GROUPS * EXPERTS_PER_GROUP
TOP_K = 2
D_EXPERT = 512
RMS_EPS = 1e-6
GN_EPS = 64e-5

kernel_name = 'hybrid_rglru_rwkv7_hmoe_adaln'


def _rmsnorm(x, g):
    x32 = x.astype(jnp.float32)
    y = x32 * lax.rsqrt(jnp.mean(x32 * x32, axis=-1, keepdims=True) + RMS_EPS)
    return (y * g.astype(jnp.float32)).astype(x.dtype)


def _adaln(x, g, shift, scale):
    return (_rmsnorm(x, g) * (1 + scale[:, None, :]) + shift[:, None, :]).astype(x.dtype)


def _lin_combine(e1, e2):
    a1, b1 = e1
    a2, b2 = e2
    return a1 * a2, a2 * b1 + b2


def _rglru_group(u_x, u_gate, conv_w, conv_b, wa, ba, wi, bi, lam, norm_g):
    f32 = jnp.float32
    bsz, t, w = u_x.shape
    xc = lax.conv_general_dilated(u_x, conv_w[:, None, :].astype(u_x.dtype), (1,), [(CONV_WIDTH - 1, 0)],
                                  dimension_numbers=('NWC', 'WIO', 'NWC'), feature_group_count=w)
    xc = (xc + conv_b).astype(f32)
    xh = xc.reshape(bsz, t, LRU_HEADS, LRU_HEAD_DIM)
    r = jax.nn.sigmoid(jnp.einsum('bthi,hij->bthj', xh, wa.astype(f32)) + ba.astype(f32)).reshape(bsz, t, w)
    i = jax.nn.sigmoid(jnp.einsum('bthi,hij->bthj', xh, wi.astype(f32)) + bi.astype(f32)).reshape(bsz, t, w)
    log_a = -LRU_C * r * jax.nn.softplus(-lam.astype(f32))
    a = jnp.exp(log_a)
    b = jnp.sqrt(-jnp.expm1(2.0 * log_a)) * (i * xc)
    _, h = lax.associative_scan(_lin_combine, (a, b), axis=1)
    y = h * jax.nn.gelu(u_gate.astype(f32))
    yh = y.reshape(bsz, t, LRU_HEADS, LRU_HEAD_DIM)
    yh = yh * lax.rsqrt(jnp.mean(yh * yh, axis=-1, keepdims=True) + RMS_EPS)
    return yh.reshape(bsz, t, w) * norm_g.astype(f32)


def _rwkv7_group(u, mu, w0, w_up, a0, a_up, g_up, k_k, k_a, r_k, ln_w, ln_b):
    f32 = jnp.float32
    bsz, t, _ = u.shape
    u = u.astype(f32)
    prev = jnp.pad(u, ((0, 0), (1, 0), (0, 0)))[:, :-1]
    u = u + (prev - u) * mu.astype(f32)
    s = [RWKV_WIDTH, 2 * RWKV_WIDTH, 3 * RWKV_WIDTH, 3 * RWKV_WIDTH + DECAY_LORA,
         3 * RWKV_WIDTH + DECAY_LORA + AAA_LORA]
    r, k, v, wd, ad, gd = jnp.split(u, s, axis=-1)
    w = -jax.nn.softplus(-(w0.astype(f32) + jnp.tanh(wd) @ w_up.astype(f32))) - 0.5
    decay = jnp.exp(-jnp.exp(w))
    a = jax.nn.sigmoid(a0.astype(f32) + ad @ a_up.astype(f32))
    g = jax.nn.sigmoid(gd) @ g_up.astype(f32)

    def heads(z):
        return z.reshape(bsz, t, RWKV_HEADS, RWKV_HEAD_DIM)

    kk = heads(k * k_k.astype(f32))
    kk = kk / jnp.maximum(jnp.sqrt(jnp.sum(kk * kk, axis=-1, keepdims=True)), 1e-12)
    k = k * (1 + (a - 1) * k_a.astype(f32))
    r, k, v, decay, a = heads(r), heads(k), heads(v), heads(decay), heads(a)
    n_chunks = t // CHUNK

    def to_chunks(z):
        return z.transpose(1, 0, 2, 3).reshape(n_chunks, CHUNK, bsz, RWKV_HEADS, RWKV_HEAD_DIM)

    def step(state, inp):
        r_t, w_t, k_t, v_t, kk_t, a_t = inp
        sa = jnp.einsum('bhvk,bhk->bhv', state, -kk_t)
        state = (state * w_t[:, :, None, :] + sa[..., None] * (kk_t * a_t)[:, :, None, :]
                 + v_t[..., None] * k_t[:, :, None, :])
        return state, jnp.einsum('bhvk,bhk->bhv', state, r_t)

    def chunk_step(state, chunk_inp):
        return lax.scan(step, state, chunk_inp)

    s0 = jnp.zeros((bsz, RWKV_HEADS, RWKV_HEAD_DIM, RWKV_HEAD_DIM), f32)
    xs = (to_chunks(r), to_chunks(decay), to_chunks(k), to_chunks(v), to_chunks(kk), to_chunks(a))
    _, o = lax.scan(chunk_step, s0, xs)
    o = o.reshape(t, bsz, RWKV_HEADS, RWKV_HEAD_DIM).transpose(1, 0, 2, 3)
    mean = jnp.mean(o, axis=-1, keepdims=True)
    var = jnp.mean(jnp.square(o - mean), axis=-1, keepdims=True)
    hshape = (RWKV_HEADS, RWKV_HEAD_DIM)
    o = (o - mean) * lax.rsqrt(var + GN_EPS) * ln_w.astype(f32).reshape(hshape) + ln_b.astype(f32).reshape(hshape)
    bonus = jnp.sum(r * k * r_k.astype(f32), axis=-1, keepdims=True) * v
    return (o + bonus).reshape(bsz, t, RWKV_WIDTH) * g


def _hier_moe(h, w_grp, b_grp, w_exp, b_exp, w1, w3, w2):
    f32 = jnp.float32
    bsz, t, d = h.shape
    hf = h.reshape(-1, d)
    n = hf.shape[0]
    grp_p = jax.nn.softmax((hf @ w_grp).astype(f32) + b_grp.astype(f32), axis=-1)
    g_w, g_idx = lax.top_k(grp_p, 1)
    exp_logits = ((hf @ w_exp).astype(f32) + b_exp.astype(f32)).reshape(n, N_GROUPS, EXPERTS_PER_GROUP)
    in_grp = jnp.take_along_axis(exp_logits, g_idx[:, :, None], axis=1)[:, 0]
    top_v, top_i = lax.top_k(in_grp, TOP_K)
    e_w = jax.nn.softmax(top_v, axis=-1) * g_w
    e_id = g_idx * EXPERTS_PER_GROUP + top_i
    flat_e = e_id.reshape(-1)
    flat_tok = jnp.repeat(jnp.arange(n, dtype=jnp.int32), TOP_K)
    order = jnp.argsort(flat_e)
    tok = flat_tok[order]
    sizes = jnp.bincount(flat_e, length=N_EXPERTS).astype(jnp.int32)
    xs = hf[tok]
    hid = jax.nn.silu(lax.ragged_dot(xs, w1, sizes)) * lax.ragged_dot(xs, w3, sizes)
    ys = lax.ragged_dot(hid, w2, sizes)
    ys = ys * e_w.reshape(-1)[order][:, None].astype(ys.dtype)
    out = jnp.zeros_like(hf).at[tok].add(ys)
    return out.reshape(bsz, t, d)


def setup_inputs(seed: int = 0) -> dict:
    key = jax.random.key(seed)
    ks = iter(jax.random.split(key, 48))
    f32 = jnp.float32
    L = DEPTH

    def nrm(shape, scale):
        return jax.random.normal(next(ks), shape, f32) * scale

    x = nrm((BATCH, SEQ, D_MODEL), 1.0)
    c = nrm((BATCH, D_MODEL), 1.0)
    w_ada = nrm((L, D_MODEL, 6 * D_MODEL), 0.3 * D_MODEL ** -0.5)
    b_ada = nrm((L, 6 * D_MODEL), 0.02)
    norm1_g = 1.0 + nrm((L, D_MODEL), 0.02)
    w_in = nrm((L, D_MODEL, PROJ_WIDTH), D_MODEL ** -0.5)
    conv_w = nrm((L, CONV_WIDTH, LRU_WIDTH), CONV_WIDTH ** -0.5)
    conv_b = nrm((L, LRU_WIDTH), 0.02)
    lru_wa = nrm((L, LRU_HEADS, LRU_HEAD_DIM, LRU_HEAD_DIM), LRU_HEAD_DIM ** -0.5)
    lru_ba = nrm((L, LRU_HEADS, LRU_HEAD_DIM), 0.02)
    lru_wi = nrm((L, LRU_HEADS, LRU_HEAD_DIM, LRU_HEAD_DIM), LRU_HEAD_DIM ** -0.5)
    lru_bi = nrm((L, LRU_HEADS, LRU_HEAD_DIM), 0.02)
    u = jax.random.uniform(next(ks), (L, LRU_WIDTH), f32, 0.9, 0.999)
    a_base = u ** (1.0 / LRU_C)
    lru_lam = jnp.log(a_base) - jnp.log1p(-a_base)
    lru_norm_g = 1.0 + nrm((L, LRU_WIDTH), 0.02)
    tok_mu = jax.random.uniform(next(ks), (L, RWKV_PROJ), f32, 0.0, 1.0)
    w0 = jnp.linspace(-6.5, -1.5, RWKV_WIDTH, dtype=f32)[None, :] + nrm((L, RWKV_WIDTH), 0.1)
    w_up = nrm((L, DECAY_LORA, RWKV_WIDTH), 0.5 * DECAY_LORA ** -0.5)
    a0 = nrm((L, RWKV_WIDTH), 0.1)
    a_up = nrm((L, AAA_LORA, RWKV_WIDTH), AAA_LORA ** -0.5)
    g_up = nrm((L, GATE_LORA, RWKV_WIDTH), GATE_LORA ** -0.5)
    k_k = 0.85 + nrm((L, RWKV_WIDTH), 0.02)
    k_a = 1.0 + nrm((L, RWKV_WIDTH), 0.02)
    r_k = -0.04 + nrm((L, RWKV_HEADS, RWKV_HEAD_DIM), 0.1)
    ln_x_w = 1.0 + nrm((L, RWKV_WIDTH), 0.02)
    ln_x_b = nrm((L, RWKV_WIDTH), 0.02)
    w_out = nrm((L, MIX_WIDTH, D_MODEL), MIX_WIDTH ** -0.5)
    norm2_g = 1.0 + nrm((L, D_MODEL), 0.02)
    w_grp = nrm((L, D_MODEL, N_GROUPS), D_MODEL ** -0.5)
    b_grp = nrm((L, N_GROUPS), 0.01)
    w_exp = nrm((L, D_MODEL, N_EXPERTS), D_MODEL ** -0.5)
    b_exp = nrm((L, N_EXPERTS), 0.01)
    w1 = nrm((L, N_EXPERTS, D_MODEL, D_EXPERT), D_MODEL ** -0.5)
    w3 = nrm((L, N_EXPERTS, D_MODEL, D_EXPERT), D_MODEL ** -0.5)
    w2 = nrm((L, N_EXPERTS, D_EXPERT, D_MODEL), D_EXPERT ** -0.5)
    final_g = 1.0 + nrm((D_MODEL,), 0.02)
    return {'x': x, 'c': c, 'w_ada': w_ada, 'b_ada': b_ada, 'norm1_g': norm1_g, 'w_in': w_in,
            'conv_w': conv_w, 'conv_b': conv_b, 'lru_wa': lru_wa, 'lru_ba': lru_ba, 'lru_wi': lru_wi,
            'lru_bi': lru_bi, 'lru_lam': lru_lam, 'lru_norm_g': lru_norm_g, 'tok_mu': tok_mu, 'w0': w0,
            'w_up': w_up, 'a0': a0, 'a_up': a_up, 'g_up': g_up, 'k_k': k_k, 'k_a': k_a, 'r_k': r_k,
            'ln_x_w': ln_x_w, 'ln_x_b': ln_x_b, 'w_out': w_out, 'norm2_g': norm2_g, 'w_grp': w_grp,
            'b_grp': b_grp, 'w_exp': w_exp, 'b_exp': b_exp, 'w1': w1, 'w3': w3, 'w2': w2,
            'final_g': final_g}


def reference(x, c, w_ada, b_ada, norm1_g, w_in, conv_w, conv_b, lru_wa, lru_ba, lru_wi, lru_bi,
              lru_lam, lru_norm_g, tok_mu, w0, w_up, a0, a_up, g_up, k_k, k_a, r_k, ln_x_w, ln_x_b,
              w_out, norm2_g, w_grp, b_grp, w_exp, b_exp, w1, w3, w2, final_g):
    cond = jax.nn.silu(c)
    for l in range(DEPTH):
        mod = cond @ w_ada[l] + b_ada[l]
        sh1, sc1, g1, sh2, sc2, g2 = jnp.split(mod, 6, axis=-1)
        h = _adaln(x, norm1_g[l], sh1, sc1)
        p = h @ w_in[l]
        y_lru = _rglru_group(p[..., :LRU_WIDTH], p[..., LRU_WIDTH:2 * LRU_WIDTH], conv_w[l], conv_b[l],
                             lru_wa[l], lru_ba[l], lru_wi[l], lru_bi[l], lru_lam[l], lru_norm_g[l])
        y_rwkv = _rwkv7_group(p[..., 2 * LRU_WIDTH:], tok_mu[l], w0[l], w_up[l], a0[l], a_up[l], g_up[l],
                              k_k[l], k_a[l], r_k[l], ln_x_w[l], ln_x_b[l])
        y = jnp.concatenate([y_lru, y_rwkv], axis=-1).astype(x.dtype) @ w_out[l]
        x = x + g1[:, None, :] * y
        h2 = _adaln(x, norm2_g[l], sh2, sc2)
        x = x + g2[:, None, :] * _hier_moe(h2, w_grp[l], b_grp[l], w_exp[l], b_exp[l], w1[l], w3[l], w2[l])
    return _rmsnorm(x, final_g)
```

```python
import functools

import jax
import jax.numpy as jnp
from jax import lax
from jax.experimental import pallas as pl
from jax.experimental.pallas import tpu as pltpu

F32 = jnp.float32
BF16 = jnp.bfloat16
I32 = jnp.int32

LRU_WIDTH = 512
LRU_HEAD_DIM = 64
CONV_WIDTH = 4
LRU_C = 8.0
RWKV_WIDTH = 512
HEAD_DIM = 64
DECAY_LORA = 64
AAA_LORA = 64
GATE_LORA = 128
RWKV_PROJ = 3 * RWKV_WIDTH + DECAY_LORA + AAA_LORA + GATE_LORA
N_GROUPS = 4
EXPERTS_PER_GROUP = 8
N_EXPERTS = N_GROUPS * EXPERTS_PER_GROUP
TOP_K = 2
RMS_EPS = 1e-6
GN_EPS = 64e-5

LANES = 128
SUBLANES = 8
CHUNK = 64
EXPERT_TILE = 256
VMEM_LIMIT = 48 * 1024 * 1024

NN = (((1,), (0,)), ((), ()))
NT = (((1,), (1,)), ((), ()))
TN = (((0,), (0,)), ((), ()))


def _split(x, n):
    if x.dtype == BF16:
        return [x]
    parts = []
    rem = x
    for i in range(n):
        p = rem.astype(BF16)
        parts.append(p)
        if i + 1 < n:
            rem = rem - p.astype(F32)
    return parts


def _mm(a, b, dn=NN, pa=1, pb=1):
    aps = _split(a, pa)
    bps = _split(b, pb)
    order = max(len(aps), len(bps))
    out = None
    for i, ai in enumerate(aps):
        for j, bj in enumerate(bps):
            if i + j >= order:
                continue
            t = lax.dot_general(ai, bj, dn, preferred_element_type=F32)
            out = t if out is None else out + t
    return out


def _softplus(x):
    return jnp.maximum(x, 0.0) + jnp.log1p(jnp.exp(-jnp.abs(x)))


def _cparams(sem):
    return pltpu.CompilerParams(dimension_semantics=sem, vmem_limit_bytes=VMEM_LIMIT)


def _mod_kernel(c_ref, w_ref, b_ref, o_ref):
    c = c_ref[...]
    s = c * jax.nn.sigmoid(c)
    o_ref[...] = _mm(s, w_ref[...], pa=2, pb=2) + b_ref[...]


def _modulation(c, w_ada, b_ada):
    bsz, d = c.shape
    n_out = w_ada.shape[1]
    rows = -(-bsz // SUBLANES) * SUBLANES
    c_pad = jnp.zeros((rows, d), F32).at[:bsz].set(c)
    bn = d
    out = pl.pallas_call(
        _mod_kernel,
        grid=(n_out // bn,),
        in_specs=[
            pl.BlockSpec((rows, d), lambda j: (0, 0)),
            pl.BlockSpec((d, bn), lambda j: (0, j)),
            pl.BlockSpec((1, bn), lambda j: (0, j)),
        ],
        out_specs=pl.BlockSpec((rows, bn), lambda j: (0, j)),
        out_shape=jax.ShapeDtypeStruct((rows, n_out), F32),
        compiler_params=_cparams(("arbitrary",)),
        name="adaln_mod",
    )(c_pad, w_ada, b_ada.reshape(1, n_out))
    return out[:bsz]


def _adaln(x, g, shift, scale):
    ms = jnp.mean(x * x, axis=-1, keepdims=True)
    y = x * lax.rsqrt(ms + RMS_EPS) * g
    return y * (1.0 + scale) + shift


def _inproj_kernel(x_ref, sh_ref, sc_ref, g_ref, wl_ref, wr_ref, ol_ref, or_ref):
    h = _adaln(x_ref[...], g_ref[...], sh_ref[...], sc_ref[...]).astype(BF16)
    ol_ref[...] = jnp.dot(h, wl_ref[...], preferred_element_type=F32)
    or_ref[...] = jnp.dot(h, wr_ref[...], preferred_element_type=F32)


def _inproj(x, sh1, sc1, g, w_in, tm=256):
    bsz, t, d = x.shape
    nl = 2 * LRU_WIDTH
    nr = RWKV_PROJ
    wl = w_in[:, :nl].astype(BF16)
    wr = w_in[:, nl:].astype(BF16)
    vec = pl.BlockSpec((None, 1, d), lambda b, i: (b, 0, 0))
    return pl.pallas_call(
        _inproj_kernel,
        grid=(bsz, t // tm),
        in_specs=[
            pl.BlockSpec((None, tm, d), lambda b, i: (b, i, 0)),
            vec, vec,
            pl.BlockSpec((1, d), lambda b, i: (0, 0)),
            pl.BlockSpec((d, nl), lambda b, i: (0, 0)),
            pl.BlockSpec((d, nr), lambda b, i: (0, 0)),
        ],
        out_specs=[
            pl.BlockSpec((None, tm, nl), lambda b, i: (b, i, 0)),
            pl.BlockSpec((None, tm, nr), lambda b, i: (b, i, 0)),
        ],
        out_shape=[
            jax.ShapeDtypeStruct((bsz, t, nl), F32),
            jax.ShapeDtypeStruct((bsz, t, nr), F32),
        ],
        compiler_params=_cparams(("arbitrary", "arbitrary")),
        name="adaln1_inproj",
    )(x, sh1, sc1, g.reshape(1, d), wl, wr)


def _gelu_tanh(x):
    c = 0.7978845608028654
    return x * (0.5 * (1.0 + jnp.tanh(c * (x + 0.044715 * (x * x * x)))))


def _lru_kernel(p_ref, cw_ref, cb_ref, wab_ref, bab_ref, lam_ref, ng_ref, ones_ref,
                o_ref, xprev_ref, hprev_ref):
    w = LRU_WIDTH
    tt = p_ref.shape[0]

    @pl.when(pl.program_id(1) == 0)
    def _():
        xprev_ref[...] = jnp.zeros_like(xprev_ref)
        hprev_ref[...] = jnp.zeros_like(hprev_ref)

    ux = p_ref[:, :w]
    ug = p_ref[:, w:]
    ext = jnp.concatenate([xprev_ref[...], ux], axis=0)
    xc = cb_ref[...] + cw_ref[CONV_WIDTH - 1:CONV_WIDTH, :] * ux
    for k in range(1, CONV_WIDTH):
        shifted = pltpu.roll(ext, k, 0)[SUBLANES:, :]
        xc = xc + cw_ref[CONV_WIDTH - 1 - k:CONV_WIDTH - k, :] * shifted
    xprev_ref[...] = ux[tt - SUBLANES:, :]

    gates = jnp.dot(xc.astype(BF16), wab_ref[...], preferred_element_type=F32) + bab_ref[...]
    r = jax.nn.sigmoid(gates[:, :w])
    ig = jax.nn.sigmoid(gates[:, w:])
    log_a = (-LRU_C) * r * _softplus(-lam_ref[...])
    a = jnp.exp(log_a)
    th = jnp.tanh(log_a)
    one_minus_a2 = (-2.0 * th) / (1.0 - th)
    b = jnp.sqrt(one_minus_a2) * (ig * xc)

    row = lax.broadcasted_iota(I32, (tt, w), 0)
    acc_a, acc_b = a, b
    s = 1
    while s < tt:
        sh_a = pltpu.roll(acc_a, s, 0)
        sh_b = pltpu.roll(acc_b, s, 0)
        live = row >= s
        acc_b = jnp.where(live, acc_a * sh_b + acc_b, acc_b)
        acc_a = jnp.where(live, acc_a * sh_a, acc_a)
        s *= 2
    h = acc_a * hprev_ref[SUBLANES - 1:SUBLANES, :] + acc_b
    hprev_ref[...] = h[tt - SUBLANES:, :]

    y = h * _gelu_tanh(ug)
    ms = _mm(y * y, ones_ref[...], pa=2) * (1.0 / LRU_HEAD_DIM)
    o_ref[...] = (y * lax.rsqrt(ms + RMS_EPS) * ng_ref[...]).astype(o_ref.dtype)


def _block_diag(w):
    h, n, _ = w.shape
    eye = jnp.eye(h, dtype=w.dtype)
    return (eye[:, None, :, None] * w[:, :, None, :]).reshape(h * n, h * n)


def _head_ones(width, head):
    idx = jnp.arange(width) // head
    return (idx[:, None] == idx[None, :]).astype(BF16)


def _rglru(p_lru, conv_w, conv_b, wa, ba, wi, bi, lam, norm_g, tt=256):
    bsz, t, _ = p_lru.shape
    w = LRU_WIDTH
    wab = jnp.concatenate([_block_diag(wa), _block_diag(wi)], axis=1).astype(BF16)
    bab = jnp.concatenate([ba.reshape(1, w), bi.reshape(1, w)], axis=1)
    const = lambda shape: pl.BlockSpec(shape, lambda b, i: (0, 0))
    return pl.pallas_call(
        _lru_kernel,
        grid=(bsz, t // tt),
        in_specs=[
            pl.BlockSpec((None, tt, 2 * w), lambda b, i: (b, i, 0)),
            const((CONV_WIDTH, w)), const((1, w)), const((w, 2 * w)), const((1, 2 * w)),
            const((1, w)), const((1, w)), const((w, w)),
        ],
        out_specs=pl.BlockSpec((None, tt, w), lambda b, i: (b, i, 0)),
        out_shape=jax.ShapeDtypeStruct((bsz, t, w), BF16),
        scratch_shapes=[pltpu.VMEM((SUBLANES, w), F32), pltpu.VMEM((SUBLANES, w), F32)],
        compiler_params=_cparams(("arbitrary", "arbitrary")),
        name="rglru",
    )(p_lru, conv_w, conv_b.reshape(1, w), wab, bab, lam.reshape(1, w),
      norm_g.reshape(1, w), _head_ones(w, LRU_HEAD_DIM))


PG = (2, 2)
PI = (2, 2)
PS = (2, 2)


def _unit_lower_inverse(a, ri, ci):
    eye = jnp.where(ri == ci, 1.0, 0.0)
    a8 = jnp.where((ri >> 3) == (ci >> 3), a, 0.0)
    a8_2 = _mm(a8, a8, pa=PI[0], pb=PI[1])
    a8_4 = _mm(a8_2, a8_2, pa=PI[0], pb=PI[1])
    t = _mm(eye + a8, eye + a8_2, pa=PI[0], pb=PI[1])
    t = _mm(t, eye + a8_4, pa=PI[0], pb=PI[1])
    for sh in (3, 4, 5):
        off = ((ri >> (sh + 1)) == (ci >> (sh + 1))) & ((ri >> sh) != (ci >> sh))
        a_off = jnp.where(off, a, 0.0)
        t = t + _mm(t, _mm(a_off, t, pa=PI[0], pb=PI[1]), pa=PI[0], pb=PI[1])
    return t


def _rwkv_kernel(p_ref, mu_ref, pv_ref, wlo_ref, gup_ref, tril_ref,
                 o_ref, uprev_ref, h_ref):
    w = RWKV_WIDTH
    c = p_ref.shape[0]

    @pl.when(pl.program_id(1) == 0)
    def _():
        uprev_ref[...] = jnp.zeros_like(uprev_ref)
        h_ref[...] = jnp.zeros_like(h_ref)

    u = p_ref[...]
    ext = jnp.concatenate([uprev_ref[...], u], axis=0)
    prev = pltpu.roll(ext, 1, 0)[SUBLANES:, :]
    uprev_ref[...] = u[c - SUBLANES:, :]
    um = u + (prev - u) * mu_ref[...]

    r = um[:, 0:w]
    k = um[:, w:2 * w]
    v = um[:, 2 * w:3 * w]
    z = um[:, 3 * w:3 * w + LANES]
    gd = um[:, 3 * w + LANES:]
    w0 = pv_ref[0:1, :]
    a0 = pv_ref[1:2, :]
    k_k = pv_ref[2:3, :]
    k_a = pv_ref[3:4, :]
    r_k = pv_ref[4:5, :]
    ln_w = pv_ref[5:6, :]
    ln_b = pv_ref[6:7, :]

    lane = lax.broadcasted_iota(I32, (c, LANES), 1)
    even = lane < HEAD_DIM
    zz = jnp.where(even, jnp.tanh(z), z)
    lora = _mm(zz, wlo_ref[...], pa=2, pb=2)
    wlog = -_softplus(-(w0 + lora[:, :w])) - 0.5
    ld = -jnp.exp(wlog)
    a = jax.nn.sigmoid(a0 + lora[:, w:])
    g = _mm(jax.nn.sigmoid(gd), gup_ref[...], pa=2, pb=2)

    ri = lax.broadcasted_iota(I32, (LANES, LANES), 0)
    ci = lax.broadcasted_iota(I32, (LANES, LANES), 1)
    same = (ri >> 6) == (ci >> 6)
    strict = same & (ri > ci)
    incl = same & (ri >= ci)
    ones_bd = jnp.where(same, 1.0, 0.0).astype(BF16)

    kk = k * k_k
    kp = k * (1.0 + (a - 1.0) * k_a)
    cum = _mm(tril_ref[...], ld, pb=3)
    cum_c = cum[c - 1:c, :]
    p_in = jnp.exp(cum)
    p_ex = jnp.exp(cum - ld)
    p_inv = jnp.exp(-cum)
    p_hat = jnp.exp(cum_c - cum)
    p_c = jnp.exp(cum_c)

    def stack(x):
        return jnp.concatenate([jnp.where(even, x, 0.0), jnp.where(even, 0.0, x)], axis=0)

    def unstack(x):
        return x[:c, :] + x[c:, :]

    for p in range(w // LANES):
        sl = slice(p * LANES, (p + 1) * LANES)
        kk_p = kk[:, sl]
        ss = _mm(kk_p * kk_p, ones_bd, pa=2)
        kk_p = kk_p / jnp.maximum(jnp.sqrt(ss), 1e-12)
        kp_p = kp[:, sl]
        r_p = r[:, sl]
        v_p = v[:, sl]
        kka = kk_p * a[:, sl]
        bonus = _mm(r_p * kp_p * r_k[:, sl], ones_bd, pa=2) * v_p

        al_s = stack(-kk_p * p_ex[:, sl])
        rt_s = stack(r_p * p_in[:, sl])
        bt_s = stack(kka * p_inv[:, sl])
        kt_s = stack(kp_p * p_inv[:, sl])
        bh_s = stack(kka * p_hat[:, sl])
        kh_s = stack(kp_p * p_hat[:, sl])
        v_s = stack(v_p)

        gram = _mm(jnp.concatenate([al_s, rt_s], axis=0),
                   jnp.concatenate([bt_s, kt_s], axis=0), NT, pa=PG[0], pb=PG[1])
        a_ab = jnp.where(strict, gram[:LANES, :LANES], 0.0)
        a_ak = jnp.where(strict, gram[:LANES, LANES:], 0.0)
        a_rb = jnp.where(incl, gram[LANES:, :LANES], 0.0)
        a_rk = jnp.where(incl, gram[LANES:, LANES:], 0.0)
        t_inv = _unit_lower_inverse(a_ab, ri, ci)

        x1 = _mm(a_ak, v_s, pa=PS[0], pb=PS[1])
        tw = _mm(t_inv, jnp.concatenate([al_s, x1], axis=1), pa=PS[0], pb=PS[1])
        qo = _mm(a_rb, tw, pa=PS[0], pb=PS[1])
        q = unstack(rt_s + qo[:, :LANES])
        o_loc = unstack(qo[:, LANES:] + _mm(a_rk, v_s, pa=PS[0], pb=PS[1]))
        mn = _mm(bh_s, tw, TN, pa=PS[0], pb=PS[1])
        m_full = mn[:, :LANES] + jnp.where(ri == ci, p_c[:, sl], 0.0)
        n_loc = mn[:, LANES:] + _mm(kh_s, v_s, TN, pa=PS[0], pb=PS[1])

        h0 = h_ref[p]
        o = _mm(q, h0, pa=PS[0], pb=PS[1]) + o_loc
        h_ref[p] = _mm(m_full, h0, pa=PS[0], pb=PS[1]) + n_loc

        mean = _mm(o, ones_bd, pa=2) * (1.0 / HEAD_DIM)
        d = o - mean
        var = _mm(d * d, ones_bd, pa=2) * (1.0 / HEAD_DIM)
        on = d * lax.rsqrt(var + GN_EPS) * ln_w[:, sl] + ln_b[:, sl]
        o_ref[:, sl] = ((on + bonus) * g[:, sl]).astype(o_ref.dtype)


def _rwkv7(p_rw, mu, w0, w_up, a0, a_up, g_up, k_k, k_a, r_k, ln_w, ln_b):
    bsz, t, npj = p_rw.shape
    w = RWKV_WIDTH
    pv = jnp.stack([w0, a0, k_k, k_a, r_k.reshape(w), ln_w, ln_b, jnp.zeros((w,), F32)])
    wlo = jnp.zeros((LANES, 2 * w), F32)
    wlo = wlo.at[:DECAY_LORA, :w].set(w_up).at[DECAY_LORA:, w:].set(a_up)
    tril = (jnp.arange(CHUNK)[:, None] >= jnp.arange(CHUNK)[None, :]).astype(BF16)
    const = lambda shape: pl.BlockSpec(shape, lambda b, i: (0, 0))
    return pl.pallas_call(
        _rwkv_kernel,
        grid=(bsz, t // CHUNK),
        in_specs=[
            pl.BlockSpec((None, CHUNK, npj), lambda b, i: (b, i, 0)),
            const((1, npj)), const((SUBLANES, w)), const((LANES, 2 * w)),
            const((GATE_LORA, w)), const((CHUNK, CHUNK)),
        ],
        out_specs=pl.BlockSpec((None, CHUNK, w), lambda b, i: (b, i, 0)),
        out_shape=jax.ShapeDtypeStruct((bsz, t, w), BF16),
        scratch_shapes=[pltpu.VMEM((SUBLANES, npj), F32),
                        pltpu.VMEM((w // LANES, LANES, LANES), F32)],
        compiler_params=_cparams(("arbitrary", "arbitrary")),
        name="rwkv7",
    )(p_rw, mu.reshape(1, npj), pv, wlo, g_up, tril)


def _mix_kernel(x_ref, yl_ref, yr_ref, wo1_ref, wo2_ref, g1_ref, sh_ref, sc_ref, ng_ref,
                wr_ref, br_ref, x1_ref, h2_ref, rid_ref, rw_ref):
    y = (jnp.dot(yl_ref[...], wo1_ref[...], preferred_element_type=F32)
         + jnp.dot(yr_ref[...], wo2_ref[...], preferred_element_type=F32))
    x1 = x_ref[...] + g1_ref[...] * y
    x1_ref[...] = x1
    h2 = _adaln(x1, ng_ref[...], sh_ref[...], sc_ref[...])
    h2_ref[...] = h2
    logits = _mm(h2, wr_ref[...], pa=2, pb=2) + br_ref[...]

    tm = logits.shape[0]
    neg = -jnp.inf
    lane = lax.broadcasted_iota(I32, (tm, LANES), 1)
    lane_f = lane.astype(F32)

    def first_argmax(x):
        m = jnp.max(x, axis=-1, keepdims=True)
        idx = jnp.min(jnp.where(x == m, lane_f, float(LANES)), axis=-1, keepdims=True)
        return m, idx.astype(I32)

    lg = jnp.where(lane < N_GROUPS, logits, neg)
    gm, g_idx = first_argmax(lg)
    g_w = 1.0 / jnp.sum(jnp.exp(lg - gm), axis=-1, keepdims=True)
    lo = N_GROUPS + EXPERTS_PER_GROUP * g_idx
    le = jnp.where((lane >= lo) & (lane < lo + EXPERTS_PER_GROUP), logits, neg)
    v1, i1 = first_argmax(le)
    v2, i2 = first_argmax(jnp.where(lane == i1, neg, le))
    e2 = jnp.exp(v2 - v1)
    w1 = g_w / (1.0 + e2)
    w2 = g_w * e2 / (1.0 + e2)
    rid_ref[...] = jnp.where(lane == 0, i1 - N_GROUPS, jnp.where(lane == 1, i2 - N_GROUPS, 0))
    rw_ref[...] = jnp.where(lane == 0, w1, jnp.where(lane == 1, w2, 0.0))


def _mix(x, y_lru, y_rw, w_out, g1, sh2, sc2, ng, w_grp, b_grp, w_exp, b_exp, tm=256):
    bsz, t, d = x.shape
    wo = w_out.astype(BF16)
    wl = LRU_WIDTH
    wr = jnp.zeros((d, LANES), F32).at[:, :N_GROUPS].set(w_grp)
    wr = wr.at[:, N_GROUPS:N_GROUPS + N_EXPERTS].set(w_exp)
    br = jnp.zeros((1, LANES), F32).at[0, :N_GROUPS].set(b_grp)
    br = br.at[0, N_GROUPS:N_GROUPS + N_EXPERTS].set(b_exp)
    vec = pl.BlockSpec((None, 1, d), lambda b, i: (b, 0, 0))
    const = lambda shape: pl.BlockSpec(shape, lambda b, i: (0, 0))
    row = lambda n: pl.BlockSpec((None, tm, n), lambda b, i: (b, i, 0))
    return pl.pallas_call(
        _mix_kernel,
        grid=(bsz, t // tm),
        in_specs=[row(d), row(wl), row(d - wl), const((wl, d)), const((d - wl, d)),
                  vec, vec, vec, const((1, d)), const((d, LANES)), const((1, LANES))],
        out_specs=[row(d), row(d), row(LANES), row(LANES)],
        out_shape=[jax.ShapeDtypeStruct((bsz, t, d), F32),
                   jax.ShapeDtypeStruct((bsz, t, d), F32),
                   jax.ShapeDtypeStruct((bsz, t, LANES), I32),
                   jax.ShapeDtypeStruct((bsz, t, LANES), F32)],
        compiler_params=_cparams(("arbitrary", "arbitrary")),
        name="outproj_adaln2_router",
    )(x, y_lru, y_rw, wo[:wl], wo[wl:], g1, sh2, sc2, ng.reshape(1, d), wr, br)


def _plan_kernel(rid_ref, tril_ref, pos_ref, te_ref, cnt_ref, base_ref):
    ph = pl.program_id(0)
    i = pl.program_id(1)
    tp = rid_ref.shape[0]
    lane = lax.broadcasted_iota(I32, (tp, LANES), 1)
    rid = rid_ref[...]
    id0 = rid[:, 0:1]
    id1 = rid[:, 1:2]
    oh0 = lane == id0
    oh1 = lane == id1
    oh = jnp.where(oh0 | oh1, 1.0, 0.0)
    colsum = jnp.sum(oh, axis=0, keepdims=True).astype(I32)

    @pl.when((ph == 0) & (i == 0))
    def _():
        cnt_ref[...] = jnp.zeros_like(cnt_ref)

    @pl.when(ph == 0)
    def _():
        cnt_ref[...] = cnt_ref[...] + colsum

    @pl.when((ph == 1) & (i == 0))
    def _():
        cnt = cnt_ref[...]
        shift = EXPERT_TILE.bit_length() - 1
        padded = ((cnt + (EXPERT_TILE - 1)) >> shift) << shift
        l8 = lax.broadcasted_iota(I32, (SUBLANES, LANES), 1)
        end = padded
        s = 1
        while s < LANES:
            end = end + jnp.where(l8 >= s, pltpu.roll(end, s, 1), 0)
            s *= 2
        base_ref[...] = end - padded
        nt = te_ref.shape[0]
        j = lax.broadcasted_iota(I32, (nt, LANES), 0) * EXPERT_TILE
        lt = lax.broadcasted_iota(I32, (nt, LANES), 1)
        done = jnp.where((lt < N_EXPERTS) & (end[0:1, :] <= j), 1.0, 0.0)
        e_of = jnp.minimum(jnp.sum(done, axis=-1, keepdims=True), float(N_EXPERTS - 1))
        total = end[0:1, N_EXPERTS - 1:N_EXPERTS]
        te_ref[...] = jnp.where(lt == 0, e_of.astype(I32), jnp.where(lt == 1, total >> shift, 0))

    @pl.when(ph == 1)
    def _():
        prefix = jnp.dot(tril_ref[...], oh.astype(BF16), preferred_element_type=F32)
        dest = base_ref[0:1, :] + prefix.astype(I32)
        pos0 = jnp.sum(jnp.where(oh0, dest, 0), axis=-1, keepdims=True)
        pos1 = jnp.sum(jnp.where(oh1, dest, 0), axis=-1, keepdims=True)
        pos_ref[...] = jnp.where(lane == 0, pos0, jnp.where(lane == 1, pos1, 0))
        base_ref[...] = base_ref[...] + colsum


def _plan(rid, n_tiles, tp=512):
    n = rid.shape[0]
    tp = min(tp, n)
    nt_pad = -(-n_tiles // SUBLANES) * SUBLANES
    tril = (jnp.arange(tp)[:, None] > jnp.arange(tp)[None, :]).astype(BF16)
    pos, te = pl.pallas_call(
        _plan_kernel,
        grid=(2, n // tp),
        in_specs=[pl.BlockSpec((tp, LANES), lambda ph, i: (i, 0)),
                  pl.BlockSpec((tp, tp), lambda ph, i: (0, 0))],
        out_specs=[pl.BlockSpec((tp, LANES), lambda ph, i: (i * ph, 0)),
                   pl.BlockSpec((nt_pad, LANES), lambda ph, i: (0, 0))],
        out_shape=[jax.ShapeDtypeStruct((n, LANES), I32),
                   jax.ShapeDtypeStruct((nt_pad, LANES), I32)],
        scratch_shapes=[pltpu.VMEM((SUBLANES, LANES), I32), pltpu.VMEM((SUBLANES, LANES), I32)],
        compiler_params=_cparams(("arbitrary", "arbitrary")),
        name="route_plan",
    )(rid, tril)
    return pos, te


def _scatter_kernel(pos_ref, h_ref, xs_in_ref, xs_ref, sem):
    del xs_in_ref
    ts = h_ref.shape[0]

    def row_copy(t, s):
        dst = pos_ref[0, 0, TOP_K * t + s]
        return pltpu.make_async_copy(h_ref.at[pl.ds(t, 1)], xs_ref.at[pl.ds(dst, 1)], sem)

    def issue(t, carry):
        for s in range(TOP_K):
            row_copy(t, s).start()
        return carry

    def drain(t, carry):
        for s in range(TOP_K):
            row_copy(t, s).wait()
        return carry

    lax.fori_loop(0, ts, issue, 0)
    lax.fori_loop(0, ts, drain, 0)


def _scatter_rows(h2, pos3, n_rows, ts):
    n, d = h2.shape
    xs0 = jnp.zeros((n_rows, d), h2.dtype)
    return pl.pallas_call(
        _scatter_kernel,
        grid=(n // ts,),
        in_specs=[pl.BlockSpec((1, 1, TOP_K * ts), lambda i: (i, 0, 0), memory_space=pltpu.SMEM),
                  pl.BlockSpec((ts, d), lambda i: (i, 0)),
                  pl.BlockSpec(memory_space=pl.ANY)],
        out_specs=pl.BlockSpec(memory_space=pl.ANY),
        out_shape=jax.ShapeDtypeStruct((n_rows, d), h2.dtype),
        scratch_shapes=[pltpu.SemaphoreType.DMA],
        input_output_aliases={2: 0},
        compiler_params=_cparams(("arbitrary",)),
        name="scatter_rows",
    )(pos3, h2, xs0)


def _expert_kernel(te_ref, nu_ref, x_ref, w1_ref, w3_ref, w2_ref, o_ref):
    j = pl.program_id(0)

    @pl.when(j < nu_ref[0])
    def _():
        x = x_ref[...].astype(BF16)
        h1 = jnp.dot(x, w1_ref[...], preferred_element_type=F32)
        h3 = jnp.dot(x, w3_ref[...], preferred_element_type=F32)
        hid = (h1 * jax.nn.sigmoid(h1) * h3).astype(BF16)
        o_ref[...] = jnp.dot(hid, w2_ref[...], preferred_element_type=F32)

    @pl.when(j >= nu_ref[0])
    def _():
        o_ref[...] = jnp.zeros_like(o_ref)


def _experts(xs, te, nu, w1, w3, w2):
    n_rows, d = xs.shape
    de = w1.shape[-1]
    nt = n_rows // EXPERT_TILE
    used = lambda j, te, nu: jnp.minimum(j, nu[0] - 1)
    grid_spec = pltpu.PrefetchScalarGridSpec(
        num_scalar_prefetch=2,
        grid=(nt,),
        in_specs=[
            pl.BlockSpec((EXPERT_TILE, d), lambda j, te, nu: (used(j, te, nu), 0)),
            pl.BlockSpec((None, d, de), lambda j, te, nu: (te[used(j, te, nu)], 0, 0)),
            pl.BlockSpec((None, d, de), lambda j, te, nu: (te[used(j, te, nu)], 0, 0)),
            pl.BlockSpec((None, de, d), lambda j, te, nu: (te[used(j, te, nu)], 0, 0)),
        ],
        out_specs=pl.BlockSpec((EXPERT_TILE, d), lambda j, te, nu: (j, 0)),
    )
    return pl.pallas_call(
        _expert_kernel,
        grid_spec=grid_spec,
        out_shape=jax.ShapeDtypeStruct((n_rows, d), F32),
        compiler_params=_cparams(("arbitrary",)),
        name="expert_mlp",
    )(te, nu, xs, w1.astype(BF16), w3.astype(BF16), w2.astype(BF16))


def _combine_kernel(pos_ref, x1_ref, rw_ref, g2_ref, fg_ref, ys_ref, o_ref, buf_ref, sem):
    tc = x1_ref.shape[0]

    def row_copy(t, s):
        src = pos_ref[0, 0, TOP_K * t + s]
        return pltpu.make_async_copy(ys_ref.at[pl.ds(src, 1)], buf_ref.at[s, pl.ds(t, 1)], sem)

    def issue(t, carry):
        for s in range(TOP_K):
            row_copy(t, s).start()
        return carry

    def drain(t, carry):
        for s in range(TOP_K):
            row_copy(t, s).wait()
        return carry

    lax.fori_loop(0, tc, issue, 0)
    lax.fori_loop(0, tc, drain, 0)
    rw = rw_ref[...]
    moe = rw[:, 0:1] * buf_ref[0] + rw[:, 1:2] * buf_ref[1]
    x2 = x1_ref[...] + g2_ref[...] * moe
    ms = jnp.mean(x2 * x2, axis=-1, keepdims=True)
    o_ref[...] = x2 * lax.rsqrt(ms + RMS_EPS) * fg_ref[...]


def _combine(x1, rw, pos3, ys, g2, final_g, tc):
    bsz, t, d = x1.shape
    tpb = t // tc
    return pl.pallas_call(
        _combine_kernel,
        grid=(bsz * tpb,),
        in_specs=[pl.BlockSpec((1, 1, TOP_K * tc), lambda i: (i, 0, 0), memory_space=pltpu.SMEM),
                  pl.BlockSpec((None, tc, d), lambda i: (i // tpb, i % tpb, 0)),
                  pl.BlockSpec((None, tc, LANES), lambda i: (i // tpb, i % tpb, 0)),
                  pl.BlockSpec((None, 1, d), lambda i: (i // tpb, 0, 0)),
                  pl.BlockSpec((1, d), lambda i: (0, 0)),
                  pl.BlockSpec(memory_space=pl.ANY)],
        out_specs=pl.BlockSpec((None, tc, d), lambda i: (i // tpb, i % tpb, 0)),
        out_shape=jax.ShapeDtypeStruct((bsz, t, d), F32),
        scratch_shapes=[pltpu.VMEM((TOP_K, tc, d), F32), pltpu.SemaphoreType.DMA],
        compiler_params=_cparams(("arbitrary",)),
        name="combine_final_norm",
    )(pos3, x1, rw, g2, final_g.reshape(1, d), ys)


def _row_tile(t, want):
    return want if t % want == 0 else t


def kernel(x, c, w_ada, b_ada, norm1_g, w_in, conv_w, conv_b, lru_wa, lru_ba, lru_wi, lru_bi, lru_lam, lru_norm_g, tok_mu, w0, w_up, a0, a_up, g_up, k_k, k_a, r_k, ln_x_w, ln_x_b, w_out, norm2_g, w_grp, b_grp, w_exp, b_exp, w1, w3, w2, final_g):
    bsz, t, d = x.shape
    n = bsz * t
    depth = w_ada.shape[0]
    tile = _row_tile(t, 256)
    n_tiles = (n * TOP_K) // EXPERT_TILE + N_EXPERTS
    out = None
    for l in range(depth):
        mod = _modulation(c, w_ada[l], b_ada[l]).reshape(bsz, 6, 1, d)
        sh1, sc1, g1, sh2, sc2, g2 = (mod[:, i] for i in range(6))
        p_lru, p_rw = _inproj(x, sh1, sc1, norm1_g[l], w_in[l], tm=tile)
        y_lru = _rglru(p_lru, conv_w[l], conv_b[l], lru_wa[l], lru_ba[l], lru_wi[l], lru_bi[l],
                       lru_lam[l], lru_norm_g[l], tt=tile)
        y_rw = _rwkv7(p_rw, tok_mu[l], w0[l], w_up[l], a0[l], a_up[l], g_up[l], k_k[l], k_a[l],
                      r_k[l], ln_x_w[l], ln_x_b[l])
        x1, h2, rid, rw = _mix(x, y_lru, y_rw, w_out[l], g1, sh2, sc2, norm2_g[l],
                               w_grp[l], b_grp[l], w_exp[l], b_exp[l], tm=tile)
        pos, te = _plan(rid.reshape(n, LANES), n_tiles)
        pos3 = pos[:, :TOP_K].reshape(n // tile, 1, TOP_K * tile)
        xs = _scatter_rows(h2.reshape(n, d), pos3, n_tiles * EXPERT_TILE, tile)
        ys = _experts(xs, te[:n_tiles, 0], te[0:1, 1], w1[l], w3[l], w2[l])
        last = l == depth - 1
        fg = final_g if last else jnp.ones((d,), F32)
        x = _combine(x1, rw, pos3, ys, g2, fg, tile)
        assert last, "only DEPTH == 1 is wired"
    return x
```

```python
import functools

import jax
import jax.numpy as jnp
from jax import lax
from jax.experimental import pallas as pl
from jax.experimental.pallas import tpu as pltpu

F32 = jnp.float32
BF16 = jnp.bfloat16
I32 = jnp.int32

LRU_WIDTH = 512
LRU_HEAD_DIM = 64
CONV_WIDTH = 4
LRU_C = 8.0
RWKV_WIDTH = 512
HEAD_DIM = 64
DECAY_LORA = 64
AAA_LORA = 64
GATE_LORA = 128
RWKV_PROJ = 3 * RWKV_WIDTH + DECAY_LORA + AAA_LORA + GATE_LORA
N_GROUPS = 4
EXPERTS_PER_GROUP = 8
N_EXPERTS = N_GROUPS * EXPERTS_PER_GROUP
TOP_K = 2
RMS_EPS = 1e-6
GN_EPS = 64e-5

LANES = 128
SUBLANES = 8
CHUNK = 64
EXPERT_TILE = 256
VMEM_LIMIT = 48 * 1024 * 1024

NN = (((1,), (0,)), ((), ()))
NT = (((1,), (1,)), ((), ()))
TN = (((0,), (0,)), ((), ()))


def _split(x, n):
    if x.dtype == BF16:
        return [x]
    parts = []
    rem = x
    for i in range(n):
        p = rem.astype(BF16)
        parts.append(p)
        if i + 1 < n:
            rem = rem - p.astype(F32)
    return parts


def _mm(a, b, dn=NN, pa=1, pb=1):
    aps = _split(a, pa)
    bps = _split(b, pb)
    order = max(len(aps), len(bps))
    terms = [(i, j) for i in range(len(aps)) for j in range(len(bps)) if i + j < order]
    ka = dn[0][0][0]
    kb = dn[0][1][0]
    if len(terms) > 1 and a.shape[ka] % LANES == 0:
        a_cat = jnp.concatenate([aps[i] for i, _ in terms], axis=ka)
        b_cat = jnp.concatenate([bps[j] for _, j in terms], axis=kb)
        return lax.dot_general(a_cat, b_cat, dn, preferred_element_type=F32)
    out = None
    for i, j in terms:
        t = lax.dot_general(aps[i], bps[j], dn, preferred_element_type=F32)
        out = t if out is None else out + t
    return out


def _softplus(x):
    return jnp.maximum(x, 0.0) + jnp.log1p(jnp.exp(-jnp.abs(x)))


def _cparams(sem):
    return pltpu.CompilerParams(dimension_semantics=sem, vmem_limit_bytes=VMEM_LIMIT)


def _mod_kernel(c_ref, w_ref, b_ref, o_ref):
    c = c_ref[...]
    s = c * jax.nn.sigmoid(c)
    o_ref[...] = _mm(s, w_ref[...], pa=2, pb=2) + b_ref[...]


def _modulation(c, w_ada, b_ada):
    bsz, d = c.shape
    n_out = w_ada.shape[1]
    rows = -(-bsz // SUBLANES) * SUBLANES
    c_pad = jnp.zeros((rows, d), F32).at[:bsz].set(c)
    bn = d
    out = pl.pallas_call(
        _mod_kernel,
        grid=(n_out // bn,),
        in_specs=[
            pl.BlockSpec((rows, d), lambda j: (0, 0)),
            pl.BlockSpec((d, bn), lambda j: (0, j)),
            pl.BlockSpec((1, bn), lambda j: (0, j)),
        ],
        out_specs=pl.BlockSpec((rows, bn), lambda j: (0, j)),
        out_shape=jax.ShapeDtypeStruct((rows, n_out), F32),
        compiler_params=_cparams(("arbitrary",)),
        name="adaln_mod",
    )(c_pad, w_ada, b_ada.reshape(1, n_out))
    return out[:bsz]


def _adaln(x, g, shift, scale):
    ms = jnp.mean(x * x, axis=-1, keepdims=True)
    y = x * lax.rsqrt(ms + RMS_EPS) * g
    return y * (1.0 + scale) + shift


def _inproj_kernel(x_ref, sh_ref, sc_ref, g_ref, wl_ref, wr_ref, ol_ref, or_ref):
    h = _adaln(x_ref[...], g_ref[...], sh_ref[...], sc_ref[...]).astype(BF16)
    ol_ref[...] = jnp.dot(h, wl_ref[...], preferred_element_type=F32)
    or_ref[...] = jnp.dot(h, wr_ref[...], preferred_element_type=F32)


def _inproj(x, sh1, sc1, g, w_in, tm=256):
    bsz, t, d = x.shape
    nl = 2 * LRU_WIDTH
    nr = RWKV_PROJ
    wl = w_in[:, :nl].astype(BF16)
    wr = w_in[:, nl:].astype(BF16)
    vec = pl.BlockSpec((None, 1, d), lambda b, i: (b, 0, 0))
    return pl.pallas_call(
        _inproj_kernel,
        grid=(bsz, t // tm),
        in_specs=[
            pl.BlockSpec((None, tm, d), lambda b, i: (b, i, 0)),
            vec, vec,
            pl.BlockSpec((1, d), lambda b, i: (0, 0)),
            pl.BlockSpec((d, nl), lambda b, i: (0, 0)),
            pl.BlockSpec((d, nr), lambda b, i: (0, 0)),
        ],
        out_specs=[
            pl.BlockSpec((None, tm, nl), lambda b, i: (b, i, 0)),
            pl.BlockSpec((None, tm, nr), lambda b, i: (b, i, 0)),
        ],
        out_shape=[
            jax.ShapeDtypeStruct((bsz, t, nl), F32),
            jax.ShapeDtypeStruct((bsz, t, nr), F32),
        ],
        compiler_params=_cparams(("arbitrary", "arbitrary")),
        name="adaln1_inproj",
    )(x, sh1, sc1, g.reshape(1, d), wl, wr)


def _gelu_tanh(x):
    c = 0.7978845608028654
    return x * (0.5 * (1.0 + jnp.tanh(c * (x + 0.044715 * (x * x * x)))))


def _lru_kernel(p_ref, cw_ref, cb_ref, wab_ref, bab_ref, lam_ref, ng_ref, ones_ref,
                o_ref, xprev_ref, hprev_ref):
    w = LRU_WIDTH
    tt = p_ref.shape[0]

    @pl.when(pl.program_id(1) == 0)
    def _():
        xprev_ref[...] = jnp.zeros_like(xprev_ref)
        hprev_ref[...] = jnp.zeros_like(hprev_ref)

    ux = p_ref[:, :w]
    ug = p_ref[:, w:]
    ext = jnp.concatenate([xprev_ref[...], ux], axis=0)
    xc = cb_ref[...] + cw_ref[CONV_WIDTH - 1:CONV_WIDTH, :] * ux
    for k in range(1, CONV_WIDTH):
        shifted = pltpu.roll(ext, k, 0)[SUBLANES:, :]
        xc = xc + cw_ref[CONV_WIDTH - 1 - k:CONV_WIDTH - k, :] * shifted
    xprev_ref[...] = ux[tt - SUBLANES:, :]

    gates = jnp.dot(xc.astype(BF16), wab_ref[...], preferred_element_type=F32) + bab_ref[...]
    r = jax.nn.sigmoid(gates[:, :w])
    ig = jax.nn.sigmoid(gates[:, w:])
    log_a = (-LRU_C) * r * _softplus(-lam_ref[...])
    a = jnp.exp(log_a)
    th = jnp.tanh(log_a)
    one_minus_a2 = (-2.0 * th) / (1.0 - th)
    b = jnp.sqrt(one_minus_a2) * (ig * xc)

    row = lax.broadcasted_iota(I32, (tt, w), 0)
    acc_a, acc_b = a, b
    s = 1
    while s < tt:
        sh_a = pltpu.roll(acc_a, s, 0)
        sh_b = pltpu.roll(acc_b, s, 0)
        live = row >= s
        acc_b = jnp.where(live, acc_a * sh_b + acc_b, acc_b)
        acc_a = jnp.where(live, acc_a * sh_a, acc_a)
        s *= 2
    h = acc_a * hprev_ref[SUBLANES - 1:SUBLANES, :] + acc_b
    hprev_ref[...] = h[tt - SUBLANES:, :]

    y = h * _gelu_tanh(ug)
    ms = _mm(y * y, ones_ref[...], pa=2) * (1.0 / LRU_HEAD_DIM)
    o_ref[...] = (y * lax.rsqrt(ms + RMS_EPS) * ng_ref[...]).astype(o_ref.dtype)


def _block_diag(w):
    h, n, _ = w.shape
    eye = jnp.eye(h, dtype=w.dtype)
    return (eye[:, None, :, None] * w[:, :, None, :]).reshape(h * n, h * n)


def _head_ones(width, head):
    idx = jnp.arange(width) // head
    return (idx[:, None] == idx[None, :]).astype(BF16)


def _rglru(p_lru, conv_w, conv_b, wa, ba, wi, bi, lam, norm_g, tt=256):
    bsz, t, _ = p_lru.shape
    w = LRU_WIDTH
    wab = jnp.concatenate([_block_diag(wa), _block_diag(wi)], axis=1).astype(BF16)
    bab = jnp.concatenate([ba.reshape(1, w), bi.reshape(1, w)], axis=1)
    const = lambda shape: pl.BlockSpec(shape, lambda b, i: (0, 0))
    return pl.pallas_call(
        _lru_kernel,
        grid=(bsz, t // tt),
        in_specs=[
            pl.BlockSpec((None, tt, 2 * w), lambda b, i: (b, i, 0)),
            const((CONV_WIDTH, w)), const((1, w)), const((w, 2 * w)), const((1, 2 * w)),
            const((1, w)), const((1, w)), const((w, w)),
        ],
        out_specs=pl.BlockSpec((None, tt, w), lambda b, i: (b, i, 0)),
        out_shape=jax.ShapeDtypeStruct((bsz, t, w), BF16),
        scratch_shapes=[pltpu.VMEM((SUBLANES, w), F32), pltpu.VMEM((SUBLANES, w), F32)],
        compiler_params=_cparams(("arbitrary", "arbitrary")),
        name="rglru",
    )(p_lru, conv_w, conv_b.reshape(1, w), wab, bab, lam.reshape(1, w),
      norm_g.reshape(1, w), _head_ones(w, LRU_HEAD_DIM))


PG = (1, 1)
PI = (1, 1)
PS = (1, 1)
PH = (2, 2)


def _unit_lower_inverse(a_list, ri, ci):
    mm = functools.partial(_mm, pa=PI[0], pb=PI[1])
    eye = jnp.where(ri == ci, 1.0, 0.0)
    leaf = (ri >> 3) == (ci >> 3)
    a8 = [jnp.where(leaf, a, 0.0) for a in a_list]
    a8_2 = [mm(x, x) for x in a8]
    a8_4 = [mm(x, x) for x in a8_2]
    t = [mm(eye + x, eye + y) for x, y in zip(a8, a8_2)]
    t = [mm(x, eye + y) for x, y in zip(t, a8_4)]
    zero = jnp.zeros((LANES, LANES), F32)
    for sh in (3, 4, 5):
        s = 1 << sh
        off = ((ri >> (sh + 1)) == (ci >> (sh + 1))) & ((ri >> sh) != (ci >> sh))
        t_lo = [_second_blocks(x, s) for x in t]
        b_lo = [mm(_second_blocks(jnp.where(off, a, 0.0), s), x) for a, x in zip(a_list, t)]
        d_lo = [mm(x, _interleave_blocks(zero, y, s)) for x, y in zip(t_lo, b_lo)]
        t = [_interleave_blocks(x, y + z, s) for x, y, z in zip(t, t_lo, d_lo)]
    return t


def _second_blocks(x, s):
    return jnp.concatenate(
        [x[s * (2 * m + 1):s * (2 * m + 2)] for m in range(x.shape[0] // (2 * s))], axis=0)


def _interleave_blocks(first_src, second, s):
    parts = []
    for m in range(first_src.shape[0] // (2 * s)):
        parts.append(first_src[2 * s * m:2 * s * m + s])
        parts.append(second[s * m:s * (m + 1)])
    return jnp.concatenate(parts, axis=0)


def _rwkv_kernel(p_ref, mu_ref, pv_ref, wlo_ref, gup_ref, tril_ref,
                 o_ref, uprev_ref, h_ref):
    w = RWKV_WIDTH
    tt = p_ref.shape[0]
    c = CHUNK
    n_pairs = w // LANES
    units = [(j, p) for j in range(tt // c) for p in range(n_pairs)]

    @pl.when(pl.program_id(1) == 0)
    def _():
        uprev_ref[...] = jnp.zeros_like(uprev_ref)
        h_ref[...] = jnp.zeros_like(h_ref)

    u = p_ref[...]
    ext = jnp.concatenate([uprev_ref[...], u], axis=0)
    prev = pltpu.roll(ext, 1, 0)[SUBLANES:, :]
    uprev_ref[...] = u[tt - SUBLANES:, :]
    um = u + (prev - u) * mu_ref[...]

    r = um[:, 0:w]
    k = um[:, w:2 * w]
    v = um[:, 2 * w:3 * w]
    z = um[:, 3 * w:3 * w + LANES]
    gd = um[:, 3 * w + LANES:]
    w0 = pv_ref[0:1, :]
    a0 = pv_ref[1:2, :]
    k_k = pv_ref[2:3, :]
    k_a = pv_ref[3:4, :]
    r_k = pv_ref[4:5, :]
    ln_w = pv_ref[5:6, :]
    ln_b = pv_ref[6:7, :]

    lane_t = lax.broadcasted_iota(I32, (tt, LANES), 1)
    zz = jnp.where(lane_t < HEAD_DIM, jnp.tanh(z), z)
    lora = _mm(zz, wlo_ref[...], pa=2, pb=2)
    wlog = -_softplus(-(w0 + lora[:, :w])) - 0.5
    ld = -jnp.exp(wlog)
    a = jax.nn.sigmoid(a0 + lora[:, w:])
    g = _mm(jax.nn.sigmoid(gd), gup_ref[...], pa=2, pb=2)

    ri = lax.broadcasted_iota(I32, (LANES, LANES), 0)
    ci = lax.broadcasted_iota(I32, (LANES, LANES), 1)
    same = (ri >> 6) == (ci >> 6)
    strict = same & (ri > ci)
    incl = same & (ri >= ci)
    ones_bd = jnp.where(same, 1.0, 0.0).astype(BF16)
    even = lax.broadcasted_iota(I32, (c, LANES), 1) < HEAD_DIM

    def head_sum(x, pieces=2):
        return jnp.concatenate(
            [_mm(x[:, p * LANES:(p + 1) * LANES], ones_bd, pa=pieces) for p in range(n_pairs)],
            axis=1)

    kk = k * k_k
    kk = kk / jnp.maximum(jnp.sqrt(head_sum(kk * kk, 1)), 1e-12)
    kp = k * (1.0 + (a - 1.0) * k_a)
    kka = kk * a
    bonus = head_sum(r * kp * r_k) * v
    cum = _mm(tril_ref[...], ld, pb=3)
    cum_c = [cum[j * c + c - 1:j * c + c, :] for j in range(tt // c)]
    p_c = [jnp.exp(x) for x in cum_c]
    p_inv = jnp.exp(-cum)
    p_hat = jnp.exp(jnp.concatenate(
        [cum_c[j] - cum[j * c:(j + 1) * c, :] for j in range(tt // c)], axis=0))
    al = -kk * jnp.exp(cum - ld)
    rt = r * jnp.exp(cum)
    bt = kka * p_inv
    kt = kp * p_inv
    bh = kka * p_hat
    kh = kp * p_hat

    def blk(x):
        return [x[j * c:(j + 1) * c, p * LANES:(p + 1) * LANES] for j, p in units]

    def halves(x):
        xs = blk(x)
        return [jnp.where(even, y, 0.0) for y in xs], [jnp.where(even, 0.0, y) for y in xs]

    def rows(top, bot):
        return [jnp.concatenate([x, y], axis=0) for x, y in zip(top, bot)]

    def unstack(x):
        return x[:c, :] + x[c:, :]

    al_e, al_o = halves(al)
    rt_e, rt_o = halves(rt)
    bt_e, bt_o = halves(bt)
    kt_e, kt_o = halves(kt)
    bh_e, bh_o = halves(bh)
    kh_e, kh_o = halves(kh)
    v_e, v_o = halves(v)
    al_n = rows(al_e, al_o)
    bh_n = rows(bh_e, bh_o)
    v_s = rows(v_o, v_e)
    kh_s = rows(kh_o, kh_e)
    rt_b = blk(rt)
    nu = range(len(units))

    g0 = [_mm(x, y, NT, pa=PG[0], pb=PG[1])
          for x, y in zip(rows(al_e, rt_e), rows(bt_e, kt_e))]
    g1 = [_mm(x, y, NT, pa=PG[0], pb=PG[1])
          for x, y in zip(rows(rt_o, al_o), rows(kt_o, bt_o))]
    top = ri < c
    left = ci < c
    tri_s = (ri & (c - 1)) > (ci & (c - 1))
    tri_i = (ri & (c - 1)) >= (ci & (c - 1))
    diag_q = top == left

    def pick(x0, x1, in_q0, tri):
        return [jnp.where(tri, jnp.where(in_q0, x, y), 0.0) for x, y in zip(x0, x1)]

    a_ab = pick(g0, g1, top, diag_q & tri_s)
    a_rk = pick(g1, g0, top, diag_q & tri_i)
    a_ak = pick(g0, g1, top, (~diag_q) & tri_s)
    a_rb = pick(g1, g0, top, (~diag_q) & tri_i)
    x1 = [_mm(a_ak[n], v_s[n], pa=PS[0], pb=PS[1]) for n in nu]
    akv = [_mm(a_rk[n], v_s[n], pa=PS[0], pb=PS[1]) for n in nu]
    khv = [_mm(kh_s[n], v_s[n], TN, pa=PS[0], pb=PS[1]) for n in nu]
    t_inv = _unit_lower_inverse(a_ab, ri, ci)
    tw = [_mm(t_inv[n], jnp.concatenate([al_n[n], x1[n]], axis=1), pa=PS[0], pb=PS[1])
          for n in nu]
    qo = [_mm(a_rb[n], tw[n], pa=PS[0], pb=PS[1]) for n in nu]
    mn = [_mm(bh_n[n], tw[n], TN, pa=PS[0], pb=PS[1]) for n in nu]
    q = [rt_b[n] + unstack(qo[n][:, :LANES]) for n in nu]
    o_loc = [unstack(qo[n][:, LANES:] + akv[n]) for n in nu]
    m_full = [mn[n][:, :LANES]
              + jnp.where(ri == ci, p_c[j][:, p * LANES:(p + 1) * LANES], 0.0)
              for n, (j, p) in enumerate(units)]
    n_loc = [mn[n][:, LANES:] + khv[n] for n in nu]

    h = [h_ref[p] for p in range(n_pairs)]
    o_rows = []
    for j in range(tt // c):
        o_parts = []
        for p in range(n_pairs):
            n = j * n_pairs + p
            o_parts.append(_mm(q[n], h[p], pa=PH[0], pb=PH[1]) + o_loc[n])
            h[p] = _mm(m_full[n], h[p], pa=PH[0], pb=PH[1]) + n_loc[n]
        o_rows.append(jnp.concatenate(o_parts, axis=1))
    for p in range(n_pairs):
        h_ref[p] = h[p]

    o = jnp.concatenate(o_rows, axis=0)
    mean = head_sum(o) * (1.0 / HEAD_DIM)
    d = o - mean
    var = head_sum(d * d, 1) * (1.0 / HEAD_DIM)
    on = d * lax.rsqrt(var + GN_EPS) * ln_w + ln_b
    o_ref[...] = ((on + bonus) * g).astype(o_ref.dtype)


RWKV_TILE = 256


def _rwkv7(p_rw, mu, w0, w_up, a0, a_up, g_up, k_k, k_a, r_k, ln_w, ln_b):
    bsz, t, npj = p_rw.shape
    w = RWKV_WIDTH
    tt = RWKV_TILE
    pv = jnp.stack([w0, a0, k_k, k_a, r_k.reshape(w), ln_w, ln_b, jnp.zeros((w,), F32)])
    wlo = jnp.zeros((LANES, 2 * w), F32)
    wlo = wlo.at[:DECAY_LORA, :w].set(w_up).at[DECAY_LORA:, w:].set(a_up)
    row = jnp.arange(tt)
    tril = ((row[:, None] >= row[None, :])
            & (row[:, None] // CHUNK == row[None, :] // CHUNK)).astype(BF16)
    const = lambda shape: pl.BlockSpec(shape, lambda b, i: (0, 0))
    return pl.pallas_call(
        _rwkv_kernel,
        grid=(bsz, t // tt),
        in_specs=[
            pl.BlockSpec((None, tt, npj), lambda b, i: (b, i, 0)),
            const((1, npj)), const((SUBLANES, w)), const((LANES, 2 * w)),
            const((GATE_LORA, w)), const((tt, tt)),
        ],
        out_specs=pl.BlockSpec((None, tt, w), lambda b, i: (b, i, 0)),
        out_shape=jax.ShapeDtypeStruct((bsz, t, w), BF16),
        scratch_shapes=[pltpu.VMEM((SUBLANES, npj), F32),
                        pltpu.VMEM((w // LANES, LANES, LANES), F32)],
        compiler_params=_cparams(("arbitrary", "arbitrary")),
        name="rwkv7",
    )(p_rw, mu.reshape(1, npj), pv, wlo, g_up, tril)


def _mix_kernel(x_ref, yl_ref, yr_ref, wo1_ref, wo2_ref, g1_ref, sh_ref, sc_ref, ng_ref,
                wr_ref, br_ref, x1_ref, h2_ref, rid_ref, rw_ref):
    y = (jnp.dot(yl_ref[...], wo1_ref[...], preferred_element_type=F32)
         + jnp.dot(yr_ref[...], wo2_ref[...], preferred_element_type=F32))
    x1 = x_ref[...] + g1_ref[...] * y
    x1_ref[...] = x1
    h2 = _adaln(x1, ng_ref[...], sh_ref[...], sc_ref[...])
    h2_ref[...] = h2
    logits = _mm(h2, wr_ref[...], pa=2, pb=2) + br_ref[...]

    tm = logits.shape[0]
    neg = -jnp.inf
    lane = lax.broadcasted_iota(I32, (tm, LANES), 1)
    lane_f = lane.astype(F32)

    def first_argmax(x):
        m = jnp.max(x, axis=-1, keepdims=True)
        idx = jnp.min(jnp.where(x == m, lane_f, float(LANES)), axis=-1, keepdims=True)
        return m, idx.astype(I32)

    lg = jnp.where(lane < N_GROUPS, logits, neg)
    gm, g_idx = first_argmax(lg)
    g_w = 1.0 / jnp.sum(jnp.exp(lg - gm), axis=-1, keepdims=True)
    lo = N_GROUPS + EXPERTS_PER_GROUP * g_idx
    le = jnp.where((lane >= lo) & (lane < lo + EXPERTS_PER_GROUP), logits, neg)
    v1, i1 = first_argmax(le)
    v2, i2 = first_argmax(jnp.where(lane == i1, neg, le))
    e2 = jnp.exp(v2 - v1)
    w1 = g_w / (1.0 + e2)
    w2 = g_w * e2 / (1.0 + e2)
    rid_ref[...] = jnp.where(lane == 0, i1 - N_GROUPS, jnp.where(lane == 1, i2 - N_GROUPS, 0))
    rw_ref[...] = jnp.where(lane == 0, w1, jnp.where(lane == 1, w2, 0.0))


def _mix(x, y_lru, y_rw, w_out, g1, sh2, sc2, ng, w_grp, b_grp, w_exp, b_exp, tm=256):
    bsz, t, d = x.shape
    wo = w_out.astype(BF16)
    wl = LRU_WIDTH
    wr = jnp.zeros((d, LANES), F32).at[:, :N_GROUPS].set(w_grp)
    wr = wr.at[:, N_GROUPS:N_GROUPS + N_EXPERTS].set(w_exp)
    br = jnp.zeros((1, LANES), F32).at[0, :N_GROUPS].set(b_grp)
    br = br.at[0, N_GROUPS:N_GROUPS + N_EXPERTS].set(b_exp)
    vec = pl.BlockSpec((None, 1, d), lambda b, i: (b, 0, 0))
    const = lambda shape: pl.BlockSpec(shape, lambda b, i: (0, 0))
    row = lambda n: pl.BlockSpec((None, tm, n), lambda b, i: (b, i, 0))
    return pl.pallas_call(
        _mix_kernel,
        grid=(bsz, t // tm),
        in_specs=[row(d), row(wl), row(d - wl), const((wl, d)), const((d - wl, d)),
                  vec, vec, vec, const((1, d)), const((d, LANES)), const((1, LANES))],
        out_specs=[row(d), row(d), row(LANES), row(LANES)],
        out_shape=[jax.ShapeDtypeStruct((bsz, t, d), F32),
                   jax.ShapeDtypeStruct((bsz, t, d), F32),
                   jax.ShapeDtypeStruct((bsz, t, LANES), I32),
                   jax.ShapeDtypeStruct((bsz, t, LANES), F32)],
        compiler_params=_cparams(("arbitrary", "arbitrary")),
        name="outproj_adaln2_router",
    )(x, y_lru, y_rw, wo[:wl], wo[wl:], g1, sh2, sc2, ng.reshape(1, d), wr, br)


def _plan_kernel(rid_ref, tril_ref, pos_ref, te_ref, cnt_ref, base_ref):
    ph = pl.program_id(0)
    i = pl.program_id(1)
    tp = rid_ref.shape[0]
    lane = lax.broadcasted_iota(I32, (tp, LANES), 1)
    rid = rid_ref[...]
    id0 = rid[:, 0:1]
    id1 = rid[:, 1:2]
    oh0 = lane == id0
    oh1 = lane == id1
    oh = jnp.where(oh0 | oh1, 1.0, 0.0)
    colsum = jnp.sum(oh, axis=0, keepdims=True).astype(I32)

    @pl.when((ph == 0) & (i == 0))
    def _():
        cnt_ref[...] = jnp.zeros_like(cnt_ref)

    @pl.when(ph == 0)
    def _():
        cnt_ref[...] = cnt_ref[...] + colsum

    @pl.when((ph == 1) & (i == 0))
    def _():
        cnt = cnt_ref[...]
        shift = EXPERT_TILE.bit_length() - 1
        padded = ((cnt + (EXPERT_TILE - 1)) >> shift) << shift
        l8 = lax.broadcasted_iota(I32, (SUBLANES, LANES), 1)
        end = padded
        s = 1
        while s < LANES:
            end = end + jnp.where(l8 >= s, pltpu.roll(end, s, 1), 0)
            s *= 2
        base_ref[...] = end - padded
        nt = te_ref.shape[0]
        j = lax.broadcasted_iota(I32, (nt, LANES), 0) * EXPERT_TILE
        lt = lax.broadcasted_iota(I32, (nt, LANES), 1)
        done = jnp.where((lt < N_EXPERTS) & (end[0:1, :] <= j), 1.0, 0.0)
        e_of = jnp.minimum(jnp.sum(done, axis=-1, keepdims=True), float(N_EXPERTS - 1))
        total = end[0:1, N_EXPERTS - 1:N_EXPERTS]
        te_ref[...] = jnp.where(lt == 0, e_of.astype(I32), jnp.where(lt == 1, total >> shift, 0))

    @pl.when(ph == 1)
    def _():
        prefix = jnp.dot(tril_ref[...], oh.astype(BF16), preferred_element_type=F32)
        dest = base_ref[0:1, :] + prefix.astype(I32)
        pos0 = jnp.sum(jnp.where(oh0, dest, 0), axis=-1, keepdims=True)
        pos1 = jnp.sum(jnp.where(oh1, dest, 0), axis=-1, keepdims=True)
        pos_ref[...] = jnp.where(lane == 0, pos0, jnp.where(lane == 1, pos1, 0))
        base_ref[...] = base_ref[...] + colsum


def _plan(rid, n_tiles, tp=512):
    n = rid.shape[0]
    tp = min(tp, n)
    nt_pad = -(-n_tiles // SUBLANES) * SUBLANES
    tril = (jnp.arange(tp)[:, None] > jnp.arange(tp)[None, :]).astype(BF16)
    pos, te = pl.pallas_call(
        _plan_kernel,
        grid=(2, n // tp),
        in_specs=[pl.BlockSpec((tp, LANES), lambda ph, i: (i, 0)),
                  pl.BlockSpec((tp, tp), lambda ph, i: (0, 0))],
        out_specs=[pl.BlockSpec((tp, LANES), lambda ph, i: (i * ph, 0)),
                   pl.BlockSpec((nt_pad, LANES), lambda ph, i: (0, 0))],
        out_shape=[jax.ShapeDtypeStruct((n, LANES), I32),
                   jax.ShapeDtypeStruct((nt_pad, LANES), I32)],
        scratch_shapes=[pltpu.VMEM((SUBLANES, LANES), I32), pltpu.VMEM((SUBLANES, LANES), I32)],
        compiler_params=_cparams(("arbitrary", "arbitrary")),
        name="route_plan",
    )(rid, tril)
    return pos, te


def _scatter_kernel(pos_ref, h_ref, xs_in_ref, xs_ref, sem):
    del xs_in_ref
    ts = h_ref.shape[0]

    def row_copy(t, s):
        dst = pos_ref[0, 0, TOP_K * t + s]
        return pltpu.make_async_copy(h_ref.at[pl.ds(t, 1)], xs_ref.at[pl.ds(dst, 1)], sem)

    def issue(t, carry):
        for s in range(TOP_K):
            row_copy(t, s).start()
        return carry

    def drain(t, carry):
        for s in range(TOP_K):
            row_copy(t, s).wait()
        return carry

    lax.fori_loop(0, ts, issue, 0)
    lax.fori_loop(0, ts, drain, 0)


def _scatter_rows(h2, pos3, n_rows, ts):
    n, d = h2.shape
    xs0 = jnp.zeros((n_rows, d), h2.dtype)
    return pl.pallas_call(
        _scatter_kernel,
        grid=(n // ts,),
        in_specs=[pl.BlockSpec((1, 1, TOP_K * ts), lambda i: (i, 0, 0), memory_space=pltpu.SMEM),
                  pl.BlockSpec((ts, d), lambda i: (i, 0)),
                  pl.BlockSpec(memory_space=pl.ANY)],
        out_specs=pl.BlockSpec(memory_space=pl.ANY),
        out_shape=jax.ShapeDtypeStruct((n_rows, d), h2.dtype),
        scratch_shapes=[pltpu.SemaphoreType.DMA],
        input_output_aliases={2: 0},
        compiler_params=_cparams(("arbitrary",)),
        name="scatter_rows",
    )(pos3, h2, xs0)


def _expert_kernel(te_ref, nu_ref, x_ref, w1_ref, w3_ref, w2_ref, o_ref):
    j = pl.program_id(0)

    @pl.when(j < nu_ref[0])
    def _():
        x = x_ref[...].astype(BF16)
        h1 = jnp.dot(x, w1_ref[...], preferred_element_type=F32)
        h3 = jnp.dot(x, w3_ref[...], preferred_element_type=F32)
        hid = (h1 * jax.nn.sigmoid(h1) * h3).astype(BF16)
        o_ref[...] = jnp.dot(hid, w2_ref[...], preferred_element_type=F32)

    @pl.when(j >= nu_ref[0])
    def _():
        o_ref[...] = jnp.zeros_like(o_ref)


def _experts(xs, te, nu, w1, w3, w2):
    n_rows, d = xs.shape
    de = w1.shape[-1]
    nt = n_rows // EXPERT_TILE
    used = lambda j, te, nu: jnp.minimum(j, nu[0] - 1)
    grid_spec = pltpu.PrefetchScalarGridSpec(
        num_scalar_prefetch=2,
        grid=(nt,),
        in_specs=[
            pl.BlockSpec((EXPERT_TILE, d), lambda j, te, nu: (used(j, te, nu), 0)),
            pl.BlockSpec((None, d, de), lambda j, te, nu: (te[used(j, te, nu)], 0, 0)),
            pl.BlockSpec((None, d, de), lambda j, te, nu: (te[used(j, te, nu)], 0, 0)),
            pl.BlockSpec((None, de, d), lambda j, te, nu: (te[used(j, te, nu)], 0, 0)),
        ],
        out_specs=pl.BlockSpec((EXPERT_TILE, d), lambda j, te, nu: (j, 0)),
    )
    return pl.pallas_call(
        _expert_kernel,
        grid_spec=grid_spec,
        out_shape=jax.ShapeDtypeStruct((n_rows, d), F32),
        compiler_params=_cparams(("arbitrary",)),
        name="expert_mlp",
    )(te, nu, xs, w1.astype(BF16), w3.astype(BF16), w2.astype(BF16))


def _combine_kernel(pos_ref, x1_ref, rw_ref, g2_ref, fg_ref, ys_ref, o_ref, buf_ref, sem):
    tc = x1_ref.shape[0]

    def row_copy(t, s):
        src = pos_ref[0, 0, TOP_K * t + s]
        return pltpu.make_async_copy(ys_ref.at[pl.ds(src, 1)], buf_ref.at[s, pl.ds(t, 1)], sem)

    def issue(t, carry):
        for s in range(TOP_K):
            row_copy(t, s).start()
        return carry

    def drain(t, carry):
        for s in range(TOP_K):
            row_copy(t, s).wait()
        return carry

    lax.fori_loop(0, tc, issue, 0)
    lax.fori_loop(0, tc, drain, 0)
    rw = rw_ref[...]
    moe = rw[:, 0:1] * buf_ref[0] + rw[:, 1:2] * buf_ref[1]
    x2 = x1_ref[...] + g2_ref[...] * moe
    ms = jnp.mean(x2 * x2, axis=-1, keepdims=True)
    o_ref[...] = x2 * lax.rsqrt(ms + RMS_EPS) * fg_ref[...]


def _combine(x1, rw, pos3, ys, g2, final_g, tc):
    bsz, t, d = x1.shape
    tpb = t // tc
    return pl.pallas_call(
        _combine_kernel,
        grid=(bsz * tpb,),
        in_specs=[pl.BlockSpec((1, 1, TOP_K * tc), lambda i: (i, 0, 0), memory_space=pltpu.SMEM),
                  pl.BlockSpec((None, tc, d), lambda i: (i // tpb, i % tpb, 0)),
                  pl.BlockSpec((None, tc, LANES), lambda i: (i // tpb, i % tpb, 0)),
                  pl.BlockSpec((None, 1, d), lambda i: (i // tpb, 0, 0)),
                  pl.BlockSpec((1, d), lambda i: (0, 0)),
                  pl.BlockSpec(memory_space=pl.ANY)],
        out_specs=pl.BlockSpec((None, tc, d), lambda i: (i // tpb, i % tpb, 0)),
        out_shape=jax.ShapeDtypeStruct((bsz, t, d), F32),
        scratch_shapes=[pltpu.VMEM((TOP_K, tc, d), F32), pltpu.SemaphoreType.DMA],
        compiler_params=_cparams(("arbitrary",)),
        name="combine_final_norm",
    )(pos3, x1, rw, g2, final_g.reshape(1, d), ys)


def _row_tile(t, want):
    return want if t % want == 0 else t


def kernel(x, c, w_ada, b_ada, norm1_g, w_in, conv_w, conv_b, lru_wa, lru_ba, lru_wi, lru_bi, lru_lam, lru_norm_g, tok_mu, w0, w_up, a0, a_up, g_up, k_k, k_a, r_k, ln_x_w, ln_x_b, w_out, norm2_g, w_grp, b_grp, w_exp, b_exp, w1, w3, w2, final_g):
    bsz, t, d = x.shape
    n = bsz * t
    depth = w_ada.shape[0]
    tile = _row_tile(t, 256)
    n_tiles = (n * TOP_K) // EXPERT_TILE + N_EXPERTS
    out = None
    for l in range(depth):
        mod = _modulation(c, w_ada[l], b_ada[l]).reshape(bsz, 6, 1, d)
        sh1, sc1, g1, sh2, sc2, g2 = (mod[:, i] for i in range(6))
        p_lru, p_rw = _inproj(x, sh1, sc1, norm1_g[l], w_in[l], tm=tile)
        y_lru = _rglru(p_lru, conv_w[l], conv_b[l], lru_wa[l], lru_ba[l], lru_wi[l], lru_bi[l],
                       lru_lam[l], lru_norm_g[l], tt=tile)
        y_rw = _rwkv7(p_rw, tok_mu[l], w0[l], w_up[l], a0[l], a_up[l], g_up[l], k_k[l], k_a[l],
                      r_k[l], ln_x_w[l], ln_x_b[l])
        x1, h2, rid, rw = _mix(x, y_lru, y_rw, w_out[l], g1, sh2, sc2, norm2_g[l],
                               w_grp[l], b_grp[l], w_exp[l], b_exp[l], tm=tile)
        pos, te = _plan(rid.reshape(n, LANES), n_tiles)
        pos3 = pos[:, :TOP_K].reshape(n // tile, 1, TOP_K * tile)
        xs = _scatter_rows(h2.reshape(n, d), pos3, n_tiles * EXPERT_TILE, tile)
        ys = _experts(xs, te[:n_tiles, 0], te[0:1, 1], w1[l], w3[l], w2[l])
        last = l == depth - 1
        fg = final_g if last else jnp.ones((d,), F32)
        x = _combine(x1, rw, pos3, ys, g2, fg, tile)
        assert last, "only DEPTH == 1 is wired"
    return x
```

```python
import functools

import jax
import jax.numpy as jnp
from jax import lax
from jax.experimental import pallas as pl
from jax.experimental.pallas import tpu as pltpu
from jax.experimental.pallas import tpu_sc as plsc

F32 = jnp.float32
BF16 = jnp.bfloat16
I32 = jnp.int32

LRU_WIDTH = 512
LRU_HEAD_DIM = 64
CONV_WIDTH = 4
LRU_C = 8.0
RWKV_WIDTH = 512
HEAD_DIM = 64
DECAY_LORA = 64
AAA_LORA = 64
GATE_LORA = 128
RWKV_PROJ = 3 * RWKV_WIDTH + DECAY_LORA + AAA_LORA + GATE_LORA
N_GROUPS = 4
EXPERTS_PER_GROUP = 8
N_EXPERTS = N_GROUPS * EXPERTS_PER_GROUP
TOP_K = 2
RMS_EPS = 1e-6
GN_EPS = 64e-5

LANES = 128
SUBLANES = 8
CHUNK = 64
EXPERT_TILE = 256
VMEM_LIMIT = 48 * 1024 * 1024

NN = (((1,), (0,)), ((), ()))
NT = (((1,), (1,)), ((), ()))
TN = (((0,), (0,)), ((), ()))


def _split(x, n):
    if x.dtype == BF16:
        return [x]
    parts = []
    rem = x
    for i in range(n):
        p = rem.astype(BF16)
        parts.append(p)
        if i + 1 < n:
            rem = rem - p.astype(F32)
    return parts


def _mm(a, b, dn=NN, pa=1, pb=1):
    aps = _split(a, pa)
    bps = _split(b, pb)
    order = max(len(aps), len(bps))
    terms = [(i, j) for i in range(len(aps)) for j in range(len(bps)) if i + j < order]
    ka = dn[0][0][0]
    kb = dn[0][1][0]
    if len(terms) > 1 and a.shape[ka] % LANES == 0:
        a_cat = jnp.concatenate([aps[i] for i, _ in terms], axis=ka)
        b_cat = jnp.concatenate([bps[j] for _, j in terms], axis=kb)
        return lax.dot_general(a_cat, b_cat, dn, preferred_element_type=F32)
    out = None
    for i, j in terms:
        t = lax.dot_general(aps[i], bps[j], dn, preferred_element_type=F32)
        out = t if out is None else out + t
    return out


def _softplus(x):
    return jnp.maximum(x, 0.0) + jnp.log1p(jnp.exp(-jnp.abs(x)))


def _cparams(sem):
    return pltpu.CompilerParams(dimension_semantics=sem, vmem_limit_bytes=VMEM_LIMIT)


def _mod_kernel(c_ref, w_ref, b_ref, o_ref):
    c = c_ref[...]
    s = c * jax.nn.sigmoid(c)
    o_ref[...] = _mm(s, w_ref[...], pa=2, pb=2) + b_ref[...]


def _modulation(c, w_ada, b_ada):
    bsz, d = c.shape
    n_out = w_ada.shape[1]
    rows = -(-bsz // SUBLANES) * SUBLANES
    c_pad = jnp.zeros((rows, d), F32).at[:bsz].set(c)
    bn = d
    out = pl.pallas_call(
        _mod_kernel,
        grid=(n_out // bn,),
        in_specs=[
            pl.BlockSpec((rows, d), lambda j: (0, 0)),
            pl.BlockSpec((d, bn), lambda j: (0, j)),
            pl.BlockSpec((1, bn), lambda j: (0, j)),
        ],
        out_specs=pl.BlockSpec((rows, bn), lambda j: (0, j)),
        out_shape=jax.ShapeDtypeStruct((rows, n_out), F32),
        compiler_params=_cparams(("arbitrary",)),
        name="adaln_mod",
    )(c_pad, w_ada, b_ada.reshape(1, n_out))
    return out[:bsz]


def _adaln(x, g, shift, scale):
    ms = jnp.mean(x * x, axis=-1, keepdims=True)
    y = x * lax.rsqrt(ms + RMS_EPS) * g
    return y * (1.0 + scale) + shift


def _inproj_kernel(x_ref, sh_ref, sc_ref, g_ref, wl_ref, wr_ref, ol_ref, or_ref):
    h = _adaln(x_ref[...], g_ref[...], sh_ref[...], sc_ref[...]).astype(BF16)
    ol_ref[...] = jnp.dot(h, wl_ref[...], preferred_element_type=F32)
    or_ref[...] = jnp.dot(h, wr_ref[...], preferred_element_type=F32)


def _inproj(x, sh1, sc1, g, w_in, tm=256):
    bsz, t, d = x.shape
    nl = 2 * LRU_WIDTH
    nr = RWKV_PROJ
    wl = w_in[:, :nl].astype(BF16)
    wr = w_in[:, nl:].astype(BF16)
    vec = pl.BlockSpec((None, 1, d), lambda b, i: (b, 0, 0))
    return pl.pallas_call(
        _inproj_kernel,
        grid=(bsz, t // tm),
        in_specs=[
            pl.BlockSpec((None, tm, d), lambda b, i: (b, i, 0)),
            vec, vec,
            pl.BlockSpec((1, d), lambda b, i: (0, 0)),
            pl.BlockSpec((d, nl), lambda b, i: (0, 0)),
            pl.BlockSpec((d, nr), lambda b, i: (0, 0)),
        ],
        out_specs=[
            pl.BlockSpec((None, tm, nl), lambda b, i: (b, i, 0)),
            pl.BlockSpec((None, tm, nr), lambda b, i: (b, i, 0)),
        ],
        out_shape=[
            jax.ShapeDtypeStruct((bsz, t, nl), F32),
            jax.ShapeDtypeStruct((bsz, t, nr), F32),
        ],
        compiler_params=_cparams(("arbitrary", "arbitrary")),
        name="adaln1_inproj",
    )(x, sh1, sc1, g.reshape(1, d), wl, wr)


def _gelu_tanh(x):
    c = 0.7978845608028654
    return x * (0.5 * (1.0 + jnp.tanh(c * (x + 0.044715 * (x * x * x)))))


def _lru_kernel(p_ref, cw_ref, cb_ref, wab_ref, bab_ref, lam_ref, ng_ref, ones_ref,
                o_ref, xprev_ref, hprev_ref):
    w = LRU_WIDTH
    tt = p_ref.shape[0]

    @pl.when(pl.program_id(1) == 0)
    def _():
        xprev_ref[...] = jnp.zeros_like(xprev_ref)
        hprev_ref[...] = jnp.zeros_like(hprev_ref)

    ux = p_ref[:, :w]
    ug = p_ref[:, w:]
    ext = jnp.concatenate([xprev_ref[...], ux], axis=0)
    xc = cb_ref[...] + cw_ref[CONV_WIDTH - 1:CONV_WIDTH, :] * ux
    for k in range(1, CONV_WIDTH):
        shifted = pltpu.roll(ext, k, 0)[SUBLANES:, :]
        xc = xc + cw_ref[CONV_WIDTH - 1 - k:CONV_WIDTH - k, :] * shifted
    xprev_ref[...] = ux[tt - SUBLANES:, :]

    gates = jnp.dot(xc.astype(BF16), wab_ref[...], preferred_element_type=F32) + bab_ref[...]
    r = jax.nn.sigmoid(gates[:, :w])
    ig = jax.nn.sigmoid(gates[:, w:])
    log_a = (-LRU_C) * r * _softplus(-lam_ref[...])
    a = jnp.exp(log_a)
    th = jnp.tanh(log_a)
    one_minus_a2 = (-2.0 * th) / (1.0 - th)
    b = jnp.sqrt(one_minus_a2) * (ig * xc)

    row = lax.broadcasted_iota(I32, (tt, w), 0)
    acc_a, acc_b = a, b
    s = 1
    while s < tt:
        sh_a = pltpu.roll(acc_a, s, 0)
        sh_b = pltpu.roll(acc_b, s, 0)
        live = row >= s
        acc_b = jnp.where(live, acc_a * sh_b + acc_b, acc_b)
        acc_a = jnp.where(live, acc_a * sh_a, acc_a)
        s *= 2
    h = acc_a * hprev_ref[SUBLANES - 1:SUBLANES, :] + acc_b
    hprev_ref[...] = h[tt - SUBLANES:, :]

    y = h * _gelu_tanh(ug)
    ms = _mm(y * y, ones_ref[...], pa=2) * (1.0 / LRU_HEAD_DIM)
    o_ref[...] = (y * lax.rsqrt(ms + RMS_EPS) * ng_ref[...]).astype(o_ref.dtype)


def _block_diag(w):
    h, n, _ = w.shape
    eye = jnp.eye(h, dtype=w.dtype)
    return (eye[:, None, :, None] * w[:, :, None, :]).reshape(h * n, h * n)


def _head_ones(width, head):
    idx = jnp.arange(width) // head
    return (idx[:, None] == idx[None, :]).astype(BF16)


def _rglru(p_lru, conv_w, conv_b, wa, ba, wi, bi, lam, norm_g, tt=256):
    bsz, t, _ = p_lru.shape
    w = LRU_WIDTH
    wab = jnp.concatenate([_block_diag(wa), _block_diag(wi)], axis=1).astype(BF16)
    bab = jnp.concatenate([ba.reshape(1, w), bi.reshape(1, w)], axis=1)
    const = lambda shape: pl.BlockSpec(shape, lambda b, i: (0, 0))
    return pl.pallas_call(
        _lru_kernel,
        grid=(bsz, t // tt),
        in_specs=[
            pl.BlockSpec((None, tt, 2 * w), lambda b, i: (b, i, 0)),
            const((CONV_WIDTH, w)), const((1, w)), const((w, 2 * w)), const((1, 2 * w)),
            const((1, w)), const((1, w)), const((w, w)),
        ],
        out_specs=pl.BlockSpec((None, tt, w), lambda b, i: (b, i, 0)),
        out_shape=jax.ShapeDtypeStruct((bsz, t, w), BF16),
        scratch_shapes=[pltpu.VMEM((SUBLANES, w), F32), pltpu.VMEM((SUBLANES, w), F32)],
        compiler_params=_cparams(("arbitrary", "arbitrary")),
        name="rglru",
    )(p_lru, conv_w, conv_b.reshape(1, w), wab, bab, lam.reshape(1, w),
      norm_g.reshape(1, w), _head_ones(w, LRU_HEAD_DIM))


PG = (1, 1)
PI = (1, 1)
PS = (1, 1)
PH = (2, 2)


def _unit_lower_inverse(a_list, ri, ci):
    mm = functools.partial(_mm, pa=PI[0], pb=PI[1])
    eye = jnp.where(ri == ci, 1.0, 0.0)
    leaf = (ri >> 3) == (ci >> 3)
    a8 = [jnp.where(leaf, a, 0.0) for a in a_list]
    a8_2 = [mm(x, x) for x in a8]
    a8_4 = [mm(x, x) for x in a8_2]
    t = [mm(eye + x, eye + y) for x, y in zip(a8, a8_2)]
    t = [mm(x, eye + y) for x, y in zip(t, a8_4)]
    zero = jnp.zeros((LANES, LANES), F32)
    for sh in (3, 4, 5):
        s = 1 << sh
        off = ((ri >> (sh + 1)) == (ci >> (sh + 1))) & ((ri >> sh) != (ci >> sh))
        t_lo = [_second_blocks(x, s) for x in t]
        b_lo = [mm(_second_blocks(jnp.where(off, a, 0.0), s), x) for a, x in zip(a_list, t)]
        d_lo = [mm(x, _interleave_blocks(zero, y, s)) for x, y in zip(t_lo, b_lo)]
        t = [_interleave_blocks(x, y + z, s) for x, y, z in zip(t, t_lo, d_lo)]
    return t


def _second_blocks(x, s):
    return jnp.concatenate(
        [x[s * (2 * m + 1):s * (2 * m + 2)] for m in range(x.shape[0] // (2 * s))], axis=0)


def _interleave_blocks(first_src, second, s):
    parts = []
    for m in range(first_src.shape[0] // (2 * s)):
        parts.append(first_src[2 * s * m:2 * s * m + s])
        parts.append(second[s * m:s * (m + 1)])
    return jnp.concatenate(parts, axis=0)


def _rwkv_kernel(p_ref, mu_ref, pv_ref, wlo_ref, gup_ref, tril_ref,
                 o_ref, uprev_ref, h_ref):
    w = RWKV_WIDTH
    tt = p_ref.shape[0]
    c = CHUNK
    n_pairs = w // LANES
    units = [(j, p) for j in range(tt // c) for p in range(n_pairs)]

    @pl.when(pl.program_id(1) == 0)
    def _():
        uprev_ref[...] = jnp.zeros_like(uprev_ref)
        h_ref[...] = jnp.zeros_like(h_ref)

    u = p_ref[...]
    ext = jnp.concatenate([uprev_ref[...], u], axis=0)
    prev = pltpu.roll(ext, 1, 0)[SUBLANES:, :]
    uprev_ref[...] = u[tt - SUBLANES:, :]
    um = u + (prev - u) * mu_ref[...]

    r = um[:, 0:w]
    k = um[:, w:2 * w]
    v = um[:, 2 * w:3 * w]
    z = um[:, 3 * w:3 * w + LANES]
    gd = um[:, 3 * w + LANES:]
    w0 = pv_ref[0:1, :]
    a0 = pv_ref[1:2, :]
    k_k = pv_ref[2:3, :]
    k_a = pv_ref[3:4, :]
    r_k = pv_ref[4:5, :]
    ln_w = pv_ref[5:6, :]
    ln_b = pv_ref[6:7, :]

    lane_t = lax.broadcasted_iota(I32, (tt, LANES), 1)
    zz = jnp.where(lane_t < HEAD_DIM, jnp.tanh(z), z)
    lora = _mm(zz, wlo_ref[...], pa=2, pb=2)
    wlog = -_softplus(-(w0 + lora[:, :w])) - 0.5
    ld = -jnp.exp(wlog)
    a = jax.nn.sigmoid(a0 + lora[:, w:])
    g = _mm(jax.nn.sigmoid(gd), gup_ref[...], pa=2, pb=2)

    ri = lax.broadcasted_iota(I32, (LANES, LANES), 0)
    ci = lax.broadcasted_iota(I32, (LANES, LANES), 1)
    same = (ri >> 6) == (ci >> 6)
    strict = same & (ri > ci)
    incl = same & (ri >= ci)
    ones_bd = jnp.where(same, 1.0, 0.0).astype(BF16)
    even = lax.broadcasted_iota(I32, (c, LANES), 1) < HEAD_DIM

    def head_sum(x, pieces=2):
        return jnp.concatenate(
            [_mm(x[:, p * LANES:(p + 1) * LANES], ones_bd, pa=pieces) for p in range(n_pairs)],
            axis=1)

    kk = k * k_k
    kk = kk / jnp.maximum(jnp.sqrt(head_sum(kk * kk, 1)), 1e-12)
    kp = k * (1.0 + (a - 1.0) * k_a)
    kka = kk * a
    bonus = head_sum(r * kp * r_k) * v
    cum = _mm(tril_ref[...], ld, pb=3)
    cum_c = [cum[j * c + c - 1:j * c + c, :] for j in range(tt // c)]
    p_c = [jnp.exp(x) for x in cum_c]
    p_inv = jnp.exp(-cum)
    p_hat = jnp.exp(jnp.concatenate(
        [cum_c[j] - cum[j * c:(j + 1) * c, :] for j in range(tt // c)], axis=0))
    al = -kk * jnp.exp(cum - ld)
    rt = r * jnp.exp(cum)
    bt = kka * p_inv
    kt = kp * p_inv
    bh = kka * p_hat
    kh = kp * p_hat

    def blk(x):
        return [x[j * c:(j + 1) * c, p * LANES:(p + 1) * LANES] for j, p in units]

    def halves(x):
        xs = blk(x)
        return [jnp.where(even, y, 0.0) for y in xs], [jnp.where(even, 0.0, y) for y in xs]

    def rows(top, bot):
        return [jnp.concatenate([x, y], axis=0) for x, y in zip(top, bot)]

    def unstack(x):
        return x[:c, :] + x[c:, :]

    al_e, al_o = halves(al)
    rt_e, rt_o = halves(rt)
    bt_e, bt_o = halves(bt)
    kt_e, kt_o = halves(kt)
    bh_e, bh_o = halves(bh)
    kh_e, kh_o = halves(kh)
    v_e, v_o = halves(v)
    al_n = rows(al_e, al_o)
    bh_n = rows(bh_e, bh_o)
    v_s = rows(v_o, v_e)
    kh_s = rows(kh_o, kh_e)
    rt_b = blk(rt)
    nu = range(len(units))

    g0 = [_mm(x, y, NT, pa=PG[0], pb=PG[1])
          for x, y in zip(rows(al_e, rt_e), rows(bt_e, kt_e))]
    g1 = [_mm(x, y, NT, pa=PG[0], pb=PG[1])
          for x, y in zip(rows(rt_o, al_o), rows(kt_o, bt_o))]
    top = ri < c
    left = ci < c
    tri_s = (ri & (c - 1)) > (ci & (c - 1))
    tri_i = (ri & (c - 1)) >= (ci & (c - 1))
    diag_q = top == left

    def pick(x0, x1, in_q0, tri):
        return [jnp.where(tri, jnp.where(in_q0, x, y), 0.0) for x, y in zip(x0, x1)]

    a_ab = pick(g0, g1, top, diag_q & tri_s)
    a_rk = pick(g1, g0, top, diag_q & tri_i)
    a_ak = pick(g0, g1, top, (~diag_q) & tri_s)
    a_rb = pick(g1, g0, top, (~diag_q) & tri_i)
    x1 = [_mm(a_ak[n], v_s[n], pa=PS[0], pb=PS[1]) for n in nu]
    akv = [_mm(a_rk[n], v_s[n], pa=PS[0], pb=PS[1]) for n in nu]
    khv = [_mm(kh_s[n], v_s[n], TN, pa=PS[0], pb=PS[1]) for n in nu]
    t_inv = _unit_lower_inverse(a_ab, ri, ci)
    tw = [_mm(t_inv[n], jnp.concatenate([al_n[n], x1[n]], axis=1), pa=PS[0], pb=PS[1])
          for n in nu]
    qo = [_mm(a_rb[n], tw[n], pa=PS[0], pb=PS[1]) for n in nu]
    mn = [_mm(bh_n[n], tw[n], TN, pa=PS[0], pb=PS[1]) for n in nu]
    q = [rt_b[n] + unstack(qo[n][:, :LANES]) for n in nu]
    o_loc = [unstack(qo[n][:, LANES:] + akv[n]) for n in nu]
    m_full = [mn[n][:, :LANES]
              + jnp.where(ri == ci, p_c[j][:, p * LANES:(p + 1) * LANES], 0.0)
              for n, (j, p) in enumerate(units)]
    n_loc = [mn[n][:, LANES:] + khv[n] for n in nu]

    h = [h_ref[p] for p in range(n_pairs)]
    o_rows = []
    for j in range(tt // c):
        o_parts = []
        for p in range(n_pairs):
            n = j * n_pairs + p
            o_parts.append(_mm(q[n], h[p], pa=PH[0], pb=PH[1]) + o_loc[n])
            h[p] = _mm(m_full[n], h[p], pa=PH[0], pb=PH[1]) + n_loc[n]
        o_rows.append(jnp.concatenate(o_parts, axis=1))
    for p in range(n_pairs):
        h_ref[p] = h[p]

    o = jnp.concatenate(o_rows, axis=0)
    mean = head_sum(o) * (1.0 / HEAD_DIM)
    d = o - mean
    var = head_sum(d * d, 1) * (1.0 / HEAD_DIM)
    on = d * lax.rsqrt(var + GN_EPS) * ln_w + ln_b
    o_ref[...] = ((on + bonus) * g).astype(o_ref.dtype)


RWKV_TILE = 256


def _rwkv7(p_rw, mu, w0, w_up, a0, a_up, g_up, k_k, k_a, r_k, ln_w, ln_b):
    bsz, t, npj = p_rw.shape
    w = RWKV_WIDTH
    tt = RWKV_TILE
    pv = jnp.stack([w0, a0, k_k, k_a, r_k.reshape(w), ln_w, ln_b, jnp.zeros((w,), F32)])
    wlo = jnp.zeros((LANES, 2 * w), F32)
    wlo = wlo.at[:DECAY_LORA, :w].set(w_up).at[DECAY_LORA:, w:].set(a_up)
    row = jnp.arange(tt)
    tril = ((row[:, None] >= row[None, :])
            & (row[:, None] // CHUNK == row[None, :] // CHUNK)).astype(BF16)
    const = lambda shape: pl.BlockSpec(shape, lambda b, i: (0, 0))
    return pl.pallas_call(
        _rwkv_kernel,
        grid=(bsz, t // tt),
        in_specs=[
            pl.BlockSpec((None, tt, npj), lambda b, i: (b, i, 0)),
            const((1, npj)), const((SUBLANES, w)), const((LANES, 2 * w)),
            const((GATE_LORA, w)), const((tt, tt)),
        ],
        out_specs=pl.BlockSpec((None, tt, w), lambda b, i: (b, i, 0)),
        out_shape=jax.ShapeDtypeStruct((bsz, t, w), BF16),
        scratch_shapes=[pltpu.VMEM((SUBLANES, npj), F32),
                        pltpu.VMEM((w // LANES, LANES, LANES), F32)],
        compiler_params=_cparams(("arbitrary", "arbitrary")),
        name="rwkv7",
    )(p_rw, mu.reshape(1, npj), pv, wlo, g_up, tril)


def _mix_kernel(x_ref, yl_ref, yr_ref, wo1_ref, wo2_ref, g1_ref, sh_ref, sc_ref, ng_ref,
                wr_ref, br_ref, x1_ref, h2_ref, rid_ref, rw_ref):
    y = (jnp.dot(yl_ref[...], wo1_ref[...], preferred_element_type=F32)
         + jnp.dot(yr_ref[...], wo2_ref[...], preferred_element_type=F32))
    x1 = x_ref[...] + g1_ref[...] * y
    x1_ref[...] = x1
    h2 = _adaln(x1, ng_ref[...], sh_ref[...], sc_ref[...])
    h2_ref[...] = h2
    logits = _mm(h2, wr_ref[...], pa=2, pb=2) + br_ref[...]

    tm = logits.shape[0]
    neg = -jnp.inf
    lane = lax.broadcasted_iota(I32, (tm, LANES), 1)
    lane_f = lane.astype(F32)

    def first_argmax(x):
        m = jnp.max(x, axis=-1, keepdims=True)
        idx = jnp.min(jnp.where(x == m, lane_f, float(LANES)), axis=-1, keepdims=True)
        return m, idx.astype(I32)

    lg = jnp.where(lane < N_GROUPS, logits, neg)
    gm, g_idx = first_argmax(lg)
    g_w = 1.0 / jnp.sum(jnp.exp(lg - gm), axis=-1, keepdims=True)
    lo = N_GROUPS + EXPERTS_PER_GROUP * g_idx
    le = jnp.where((lane >= lo) & (lane < lo + EXPERTS_PER_GROUP), logits, neg)
    v1, i1 = first_argmax(le)
    v2, i2 = first_argmax(jnp.where(lane == i1, neg, le))
    e2 = jnp.exp(v2 - v1)
    w1 = g_w / (1.0 + e2)
    w2 = g_w * e2 / (1.0 + e2)
    rid_ref[...] = jnp.where(lane == 0, i1 - N_GROUPS, jnp.where(lane == 1, i2 - N_GROUPS, 0))
    rw_ref[...] = jnp.where(lane == 0, w1, jnp.where(lane == 1, w2, 0.0))


def _mix(x, y_lru, y_rw, w_out, g1, sh2, sc2, ng, w_grp, b_grp, w_exp, b_exp, tm=256):
    bsz, t, d = x.shape
    wo = w_out.astype(BF16)
    wl = LRU_WIDTH
    wr = jnp.zeros((d, LANES), F32).at[:, :N_GROUPS].set(w_grp)
    wr = wr.at[:, N_GROUPS:N_GROUPS + N_EXPERTS].set(w_exp)
    br = jnp.zeros((1, LANES), F32).at[0, :N_GROUPS].set(b_grp)
    br = br.at[0, N_GROUPS:N_GROUPS + N_EXPERTS].set(b_exp)
    vec = pl.BlockSpec((None, 1, d), lambda b, i: (b, 0, 0))
    const = lambda shape: pl.BlockSpec(shape, lambda b, i: (0, 0))
    row = lambda n: pl.BlockSpec((None, tm, n), lambda b, i: (b, i, 0))
    return pl.pallas_call(
        _mix_kernel,
        grid=(bsz, t // tm),
        in_specs=[row(d), row(wl), row(d - wl), const((wl, d)), const((d - wl, d)),
                  vec, vec, vec, const((1, d)), const((d, LANES)), const((1, LANES))],
        out_specs=[row(d), row(d), row(LANES), row(LANES)],
        out_shape=[jax.ShapeDtypeStruct((bsz, t, d), F32),
                   jax.ShapeDtypeStruct((bsz, t, d), F32),
                   jax.ShapeDtypeStruct((bsz, t, LANES), I32),
                   jax.ShapeDtypeStruct((bsz, t, LANES), F32)],
        compiler_params=_cparams(("arbitrary", "arbitrary")),
        name="outproj_adaln2_router",
    )(x, y_lru, y_rw, wo[:wl], wo[wl:], g1, sh2, sc2, ng.reshape(1, d), wr, br)


def _plan_kernel(rid_ref, tril_ref, pos_ref, te_ref, cnt_ref, base_ref):
    ph = pl.program_id(0)
    i = pl.program_id(1)
    tp = rid_ref.shape[0]
    lane = lax.broadcasted_iota(I32, (tp, LANES), 1)
    rid = rid_ref[...]
    id0 = rid[:, 0:1]
    id1 = rid[:, 1:2]
    oh0 = lane == id0
    oh1 = lane == id1
    oh = jnp.where(oh0 | oh1, 1.0, 0.0)
    colsum = jnp.sum(oh, axis=0, keepdims=True).astype(I32)

    @pl.when((ph == 0) & (i == 0))
    def _():
        cnt_ref[...] = jnp.zeros_like(cnt_ref)

    @pl.when(ph == 0)
    def _():
        cnt_ref[...] = cnt_ref[...] + colsum

    @pl.when((ph == 1) & (i == 0))
    def _():
        cnt = cnt_ref[...]
        shift = EXPERT_TILE.bit_length() - 1
        padded = ((cnt + (EXPERT_TILE - 1)) >> shift) << shift
        l8 = lax.broadcasted_iota(I32, (SUBLANES, LANES), 1)
        end = padded
        s = 1
        while s < LANES:
            end = end + jnp.where(l8 >= s, pltpu.roll(end, s, 1), 0)
            s *= 2
        base_ref[...] = end - padded
        nt = te_ref.shape[0]
        j = lax.broadcasted_iota(I32, (nt, LANES), 0) * EXPERT_TILE
        lt = lax.broadcasted_iota(I32, (nt, LANES), 1)
        done = jnp.where((lt < N_EXPERTS) & (end[0:1, :] <= j), 1.0, 0.0)
        e_of = jnp.minimum(jnp.sum(done, axis=-1, keepdims=True), float(N_EXPERTS - 1)).astype(I32)
        total = end[0:1, N_EXPERTS - 1:N_EXPERTS]
        mine = lt == e_of
        cnt_e = jnp.sum(jnp.where(mine, cnt[0:1, :], 0), axis=-1, keepdims=True)
        start_e = jnp.sum(jnp.where(mine, end[0:1, :] - padded[0:1, :], 0), axis=-1, keepdims=True)
        valid = jnp.clip(cnt_e - (j[:, 0:1] - start_e), 0, EXPERT_TILE)
        te_ref[...] = jnp.where(lt == 0, e_of, jnp.where(lt == 1, total >> shift,
                                                        jnp.where(lt == 2, valid, 0)))

    @pl.when(ph == 1)
    def _():
        prefix = jnp.dot(tril_ref[...], oh.astype(BF16), preferred_element_type=F32)
        dest = base_ref[0:1, :] + prefix.astype(I32)
        pos0 = jnp.sum(jnp.where(oh0, dest, 0), axis=-1, keepdims=True)
        pos1 = jnp.sum(jnp.where(oh1, dest, 0), axis=-1, keepdims=True)
        pos_ref[...] = jnp.where(lane == 0, pos0, jnp.where(lane == 1, pos1, 0))
        base_ref[...] = base_ref[...] + colsum


def _plan(rid, n_tiles, tp=512):
    n = rid.shape[0]
    tp = min(tp, n)
    nt_pad = -(-n_tiles // SUBLANES) * SUBLANES
    tril = (jnp.arange(tp)[:, None] > jnp.arange(tp)[None, :]).astype(BF16)
    pos, te = pl.pallas_call(
        _plan_kernel,
        grid=(2, n // tp),
        in_specs=[pl.BlockSpec((tp, LANES), lambda ph, i: (i, 0)),
                  pl.BlockSpec((tp, tp), lambda ph, i: (0, 0))],
        out_specs=[pl.BlockSpec((tp, LANES), lambda ph, i: (i * ph, 0)),
                   pl.BlockSpec((nt_pad, LANES), lambda ph, i: (0, 0))],
        out_shape=[jax.ShapeDtypeStruct((n, LANES), I32),
                   jax.ShapeDtypeStruct((nt_pad, LANES), I32)],
        scratch_shapes=[pltpu.VMEM((SUBLANES, LANES), I32), pltpu.VMEM((SUBLANES, LANES), I32)],
        compiler_params=_cparams(("arbitrary", "arbitrary")),
        name="route_plan",
    )(rid, tril)
    return pos, te


SC_CORES = 2
SC_SUBCORES = 16
SC_WORKERS = SC_CORES * SC_SUBCORES
SC_WINDOW = 32


def _sc_mesh():
    return plsc.VectorSubcoreMesh(core_axis_name="c", subcore_axis_name="s",
                                  num_cores=SC_CORES, num_subcores=SC_SUBCORES)


def _sc_scatter_rows(rows, pos0, pos1, n_out):
    n, d = rows.shape
    per_w = n // SC_WORKERS
    n_win = per_w // SC_WINDOW
    shape3 = (SC_WORKERS, n_win, SC_WINDOW)

    @functools.partial(
        pl.kernel, mesh=_sc_mesh(), out_type=jax.ShapeDtypeStruct((n_out, d), rows.dtype),
        scratch_types=[pltpu.VMEM((n_win, SC_WINDOW), I32), pltpu.VMEM((n_win, SC_WINDOW), I32),
                       pltpu.VMEM((SC_WINDOW, d), rows.dtype)],
        name="sc_scatter_rows")
    def scatter(rows_hbm, p0_hbm, p1_hbm, out_hbm, p0_v, p1_v, buf):
        wid = lax.axis_index("s") * SC_CORES + lax.axis_index("c")
        base = wid * per_w
        pltpu.sync_copy(p0_hbm.at[wid], p0_v)
        pltpu.sync_copy(p1_hbm.at[wid], p1_v)

        @pl.loop(0, n_win)
        def _(j):
            pltpu.sync_copy(rows_hbm.at[pl.ds(base + j * SC_WINDOW, SC_WINDOW)], buf)
            pltpu.sync_copy(buf, out_hbm.at[p0_v.at[j]])
            pltpu.sync_copy(buf, out_hbm.at[p1_v.at[j]])

    return scatter(rows, pos0.reshape(shape3), pos1.reshape(shape3))


def _sc_gather_rows(table, idx):
    m = idx.shape[0]
    d = table.shape[1]
    per_w = m // SC_WORKERS
    n_win = per_w // SC_WINDOW

    @functools.partial(
        pl.kernel, mesh=_sc_mesh(), out_type=jax.ShapeDtypeStruct((m, d), table.dtype),
        scratch_types=[pltpu.VMEM((n_win, SC_WINDOW), I32), pltpu.VMEM((SC_WINDOW, d), table.dtype)],
        name="sc_gather_rows")
    def gather(table_hbm, idx_hbm, out_hbm, idx_v, buf):
        wid = lax.axis_index("s") * SC_CORES + lax.axis_index("c")
        base = wid * per_w
        pltpu.sync_copy(idx_hbm.at[wid], idx_v)

        @pl.loop(0, n_win)
        def _(j):
            pltpu.sync_copy(table_hbm.at[idx_v.at[j]], buf)
            pltpu.sync_copy(buf, out_hbm.at[pl.ds(base + j * SC_WINDOW, SC_WINDOW)])

    return gather(table, idx.reshape(SC_WORKERS, n_win, SC_WINDOW))


def _expert_kernel(te_ref, nu_ref, valid_ref, x_ref, w1_ref, w3_ref, w2_ref, o_ref):
    j = pl.program_id(0)

    @pl.when(j < nu_ref[0])
    def _():
        row = lax.broadcasted_iota(I32, x_ref.shape, 0)
        x = jnp.where(row < valid_ref[j], x_ref[...], 0.0).astype(BF16)
        h1 = jnp.dot(x, w1_ref[...], preferred_element_type=F32)
        h3 = jnp.dot(x, w3_ref[...], preferred_element_type=F32)
        hid = (h1 * jax.nn.sigmoid(h1) * h3).astype(BF16)
        o_ref[...] = jnp.dot(hid, w2_ref[...], preferred_element_type=F32)

    @pl.when(j >= nu_ref[0])
    def _():
        o_ref[...] = jnp.zeros_like(o_ref)


def _experts(xs, te, nu, valid, w1, w3, w2):
    n_rows, d = xs.shape
    de = w1.shape[-1]
    nt = n_rows // EXPERT_TILE
    used = lambda j, nu: jnp.minimum(j, nu[0] - 1)
    grid_spec = pltpu.PrefetchScalarGridSpec(
        num_scalar_prefetch=3,
        grid=(nt,),
        in_specs=[
            pl.BlockSpec((EXPERT_TILE, d), lambda j, te, nu, va: (used(j, nu), 0)),
            pl.BlockSpec((None, d, de), lambda j, te, nu, va: (te[used(j, nu)], 0, 0)),
            pl.BlockSpec((None, d, de), lambda j, te, nu, va: (te[used(j, nu)], 0, 0)),
            pl.BlockSpec((None, de, d), lambda j, te, nu, va: (te[used(j, nu)], 0, 0)),
        ],
        out_specs=pl.BlockSpec((EXPERT_TILE, d), lambda j, te, nu, va: (j, 0)),
    )
    return pl.pallas_call(
        _expert_kernel,
        grid_spec=grid_spec,
        out_shape=jax.ShapeDtypeStruct((n_rows, d), F32),
        compiler_params=_cparams(("arbitrary",)),
        name="expert_mlp",
    )(te, nu, valid, xs, w1.astype(BF16), w3.astype(BF16), w2.astype(BF16))


def _combine_kernel(x1_ref, rw_ref, g2_ref, fg_ref, y0_ref, y1_ref, o_ref):
    rw = rw_ref[...]
    moe = rw[:, 0:1] * y0_ref[...] + rw[:, 1:2] * y1_ref[...]
    x2 = x1_ref[...] + g2_ref[...] * moe
    ms = jnp.mean(x2 * x2, axis=-1, keepdims=True)
    o_ref[...] = x2 * lax.rsqrt(ms + RMS_EPS) * fg_ref[...]


def _combine(x1, rw, yg, g2, final_g, tc):
    bsz, t, d = x1.shape
    row = lambda n: pl.BlockSpec((None, tc, n), lambda b, i: (b, i, 0))
    slot = lambda s: pl.BlockSpec((None, None, tc, d), lambda b, i: (s, b, i, 0))
    return pl.pallas_call(
        _combine_kernel,
        grid=(bsz, t // tc),
        in_specs=[row(d), row(LANES),
                  pl.BlockSpec((None, 1, d), lambda b, i: (b, 0, 0)),
                  pl.BlockSpec((1, d), lambda b, i: (0, 0)),
                  slot(0), slot(1)],
        out_specs=row(d),
        out_shape=jax.ShapeDtypeStruct((bsz, t, d), F32),
        compiler_params=_cparams(("arbitrary", "arbitrary")),
        name="combine_final_norm",
    )(x1, rw, g2, final_g.reshape(1, d), yg, yg)


def _row_tile(t, want):
    return want if t % want == 0 else t


def kernel(x, c, w_ada, b_ada, norm1_g, w_in, conv_w, conv_b, lru_wa, lru_ba, lru_wi, lru_bi, lru_lam, lru_norm_g, tok_mu, w0, w_up, a0, a_up, g_up, k_k, k_a, r_k, ln_x_w, ln_x_b, w_out, norm2_g, w_grp, b_grp, w_exp, b_exp, w1, w3, w2, final_g):
    bsz, t, d = x.shape
    n = bsz * t
    depth = w_ada.shape[0]
    assert depth == 1, "the combine kernel applies the final norm: only DEPTH == 1 is wired"
    tile = _row_tile(t, 256)
    n_tiles = (n * TOP_K) // EXPERT_TILE + N_EXPERTS
    for l in range(depth):
        mod = _modulation(c, w_ada[l], b_ada[l]).reshape(bsz, 6, 1, d)
        sh1, sc1, g1, sh2, sc2, g2 = (mod[:, i] for i in range(6))
        p_lru, p_rw = _inproj(x, sh1, sc1, norm1_g[l], w_in[l], tm=tile)
        y_lru = _rglru(p_lru, conv_w[l], conv_b[l], lru_wa[l], lru_ba[l], lru_wi[l], lru_bi[l],
                       lru_lam[l], lru_norm_g[l], tt=tile)
        y_rw = _rwkv7(p_rw, tok_mu[l], w0[l], w_up[l], a0[l], a_up[l], g_up[l], k_k[l], k_a[l],
                      r_k[l], ln_x_w[l], ln_x_b[l])
        x1, h2, rid, rw = _mix(x, y_lru, y_rw, w_out[l], g1, sh2, sc2, norm2_g[l],
                               w_grp[l], b_grp[l], w_exp[l], b_exp[l], tm=tile)
        pos, te = _plan(rid.reshape(n, LANES), n_tiles)
        pos0, pos1 = pos[:, 0], pos[:, 1]
        xs = _sc_scatter_rows(h2.reshape(n, d), pos0, pos1, n_tiles * EXPERT_TILE)
        ys = _experts(xs, te[:n_tiles, 0], te[0:1, 1], te[:n_tiles, 2], w1[l], w3[l], w2[l])
        yg = _sc_gather_rows(ys, jnp.concatenate([pos0, pos1]))
        x = _combine(x1, rw, yg.reshape(TOP_K, bsz, t, d), g2, final_g, tile)
    return x
```

```python
import functools

import jax
import jax.numpy as jnp
from jax import lax
from jax.experimental import pallas as pl
from jax.experimental.pallas import tpu as pltpu
from jax.experimental.pallas import tpu_sc as plsc

F32 = jnp.float32
BF16 = jnp.bfloat16
I32 = jnp.int32

LRU_WIDTH = 512
LRU_HEAD_DIM = 64
CONV_WIDTH = 4
LRU_C = 8.0
RWKV_WIDTH = 512
HEAD_DIM = 64
DECAY_LORA = 64
AAA_LORA = 64
GATE_LORA = 128
RWKV_PROJ = 3 * RWKV_WIDTH + DECAY_LORA + AAA_LORA + GATE_LORA
N_GROUPS = 4
EXPERTS_PER_GROUP = 8
N_EXPERTS = N_GROUPS * EXPERTS_PER_GROUP
TOP_K = 2
RMS_EPS = 1e-6
GN_EPS = 64e-5

LANES = 128
SUBLANES = 8
CHUNK = 64
EXPERT_TILE = 256
VMEM_LIMIT = 48 * 1024 * 1024

NN = (((1,), (0,)), ((), ()))
NT = (((1,), (1,)), ((), ()))
TN = (((0,), (0,)), ((), ()))


def _split(x, n):
    if x.dtype == BF16:
        return [x]
    parts = []
    rem = x
    for i in range(n):
        p = rem.astype(BF16)
        parts.append(p)
        if i + 1 < n:
            rem = rem - p.astype(F32)
    return parts


def _mm(a, b, dn=NN, pa=1, pb=1):
    aps = _split(a, pa)
    bps = _split(b, pb)
    order = max(len(aps), len(bps))
    terms = [(i, j) for i in range(len(aps)) for j in range(len(bps)) if i + j < order]
    ka = dn[0][0][0]
    kb = dn[0][1][0]
    if len(terms) > 1 and a.shape[ka] % LANES == 0:
        a_cat = jnp.concatenate([aps[i] for i, _ in terms], axis=ka)
        b_cat = jnp.concatenate([bps[j] for _, j in terms], axis=kb)
        return lax.dot_general(a_cat, b_cat, dn, preferred_element_type=F32)
    out = None
    for i, j in terms:
        t = lax.dot_general(aps[i], bps[j], dn, preferred_element_type=F32)
        out = t if out is None else out + t
    return out


def _pack_bf16_halves(x):
    n = x.shape[1] // 2
    bits = lax.bitcast_convert_type(x.astype(BF16).astype(F32), I32)
    return bits[:, n:] | ((bits[:, :n] >> 16) & 0xFFFF)


def _unpack_bf16_halves(p):
    lo = lax.bitcast_convert_type(p << 16, F32)
    hi = lax.bitcast_convert_type(p & (-65536), F32)
    return jnp.concatenate([lo, hi], axis=1)


def _softplus(x):
    return jnp.maximum(x, 0.0) + jnp.log1p(jnp.exp(-jnp.abs(x)))


def _cparams(sem):
    return pltpu.CompilerParams(dimension_semantics=sem, vmem_limit_bytes=VMEM_LIMIT)


def _mod_kernel(c_ref, w_ref, b_ref, o_ref):
    c = c_ref[...]
    s = c * jax.nn.sigmoid(c)
    o_ref[...] = _mm(s, w_ref[...], pa=2, pb=2) + b_ref[...]


def _modulation(c, w_ada, b_ada):
    bsz, d = c.shape
    n_out = w_ada.shape[1]
    rows = -(-bsz // SUBLANES) * SUBLANES
    c_pad = jnp.zeros((rows, d), F32).at[:bsz].set(c)
    bn = d
    out = pl.pallas_call(
        _mod_kernel,
        grid=(n_out // bn,),
        in_specs=[
            pl.BlockSpec((rows, d), lambda j: (0, 0)),
            pl.BlockSpec((d, bn), lambda j: (0, j)),
            pl.BlockSpec((1, bn), lambda j: (0, j)),
        ],
        out_specs=pl.BlockSpec((rows, bn), lambda j: (0, j)),
        out_shape=jax.ShapeDtypeStruct((rows, n_out), F32),
        compiler_params=_cparams(("arbitrary",)),
        name="adaln_mod",
    )(c_pad, w_ada, b_ada.reshape(1, n_out))
    return out[:bsz]


def _adaln(x, g, shift, scale):
    ms = jnp.mean(x * x, axis=-1, keepdims=True)
    y = x * lax.rsqrt(ms + RMS_EPS) * g
    return y * (1.0 + scale) + shift


def _inproj_kernel(x_ref, sh_ref, sc_ref, g_ref, wl_ref, wr_ref, ol_ref, or_ref):
    h = _adaln(x_ref[...], g_ref[...], sh_ref[...], sc_ref[...]).astype(BF16)
    ol_ref[...] = jnp.dot(h, wl_ref[...], preferred_element_type=F32)
    or_ref[...] = jnp.dot(h, wr_ref[...], preferred_element_type=F32)


def _inproj(x, sh1, sc1, g, w_in, tm=256):
    bsz, t, d = x.shape
    nl = 2 * LRU_WIDTH
    nr = RWKV_PROJ
    wl = w_in[:, :nl].astype(BF16)
    wr = w_in[:, nl:].astype(BF16)
    vec = pl.BlockSpec((None, 1, d), lambda b, i: (b, 0, 0))
    return pl.pallas_call(
        _inproj_kernel,
        grid=(bsz, t // tm),
        in_specs=[
            pl.BlockSpec((None, tm, d), lambda b, i: (b, i, 0)),
            vec, vec,
            pl.BlockSpec((1, d), lambda b, i: (0, 0)),
            pl.BlockSpec((d, nl), lambda b, i: (0, 0)),
            pl.BlockSpec((d, nr), lambda b, i: (0, 0)),
        ],
        out_specs=[
            pl.BlockSpec((None, tm, nl), lambda b, i: (b, i, 0)),
            pl.BlockSpec((None, tm, nr), lambda b, i: (b, i, 0)),
        ],
        out_shape=[
            jax.ShapeDtypeStruct((bsz, t, nl), F32),
            jax.ShapeDtypeStruct((bsz, t, nr), F32),
        ],
        compiler_params=_cparams(("arbitrary", "arbitrary")),
        name="adaln1_inproj",
    )(x, sh1, sc1, g.reshape(1, d), wl, wr)


def _gelu_tanh(x):
    c = 0.7978845608028654
    return x * (0.5 * (1.0 + jnp.tanh(c * (x + 0.044715 * (x * x * x)))))


def _lru_kernel(p_ref, cw_ref, cb_ref, wab_ref, bab_ref, lam_ref, ng_ref, ones_ref,
                o_ref, xprev_ref, hprev_ref):
    w = LRU_WIDTH
    tt = p_ref.shape[0]

    @pl.when(pl.program_id(1) == 0)
    def _():
        xprev_ref[...] = jnp.zeros_like(xprev_ref)
        hprev_ref[...] = jnp.zeros_like(hprev_ref)

    ux = p_ref[:, :w]
    ug = p_ref[:, w:]
    ext = jnp.concatenate([xprev_ref[...], ux], axis=0)
    xc = cb_ref[...] + cw_ref[CONV_WIDTH - 1:CONV_WIDTH, :] * ux
    for k in range(1, CONV_WIDTH):
        shifted = pltpu.roll(ext, k, 0)[SUBLANES:, :]
        xc = xc + cw_ref[CONV_WIDTH - 1 - k:CONV_WIDTH - k, :] * shifted
    xprev_ref[...] = ux[tt - SUBLANES:, :]

    gates = jnp.dot(xc.astype(BF16), wab_ref[...], preferred_element_type=F32) + bab_ref[...]
    r = jax.nn.sigmoid(gates[:, :w])
    ig = jax.nn.sigmoid(gates[:, w:])
    log_a = (-LRU_C) * r * _softplus(-lam_ref[...])
    a = jnp.exp(log_a)
    th = jnp.tanh(log_a)
    one_minus_a2 = (-2.0 * th) / (1.0 - th)
    b = jnp.sqrt(one_minus_a2) * (ig * xc)

    row8 = lax.broadcasted_iota(I32, (tt, w), 0) & (SUBLANES - 1)
    acc_a, acc_b = a, b
    for s in (1, 2, 4):
        sh_a = pltpu.roll(acc_a, s, 0)
        sh_b = pltpu.roll(acc_b, s, 0)
        live = row8 >= s
        acc_b = jnp.where(live, acc_a * sh_b + acc_b, acc_b)
        acc_a = jnp.where(live, acc_a * sh_a, acc_a)
    carry = hprev_ref[SUBLANES - 1:SUBLANES, :]
    groups = []
    for gi in range(tt // SUBLANES):
        lo = gi * SUBLANES
        groups.append(acc_a[lo:lo + SUBLANES, :] * carry + acc_b[lo:lo + SUBLANES, :])
        last = lo + SUBLANES - 1
        carry = acc_a[last:last + 1, :] * carry + acc_b[last:last + 1, :]
    h = jnp.concatenate(groups, axis=0)
    hprev_ref[...] = groups[-1]

    y = h * _gelu_tanh(ug)
    ms = _mm(y * y, ones_ref[...], pa=2) * (1.0 / LRU_HEAD_DIM)
    o_ref[...] = (y * lax.rsqrt(ms + RMS_EPS) * ng_ref[...]).astype(o_ref.dtype)


def _block_diag(w):
    h, n, _ = w.shape
    eye = jnp.eye(h, dtype=w.dtype)
    return (eye[:, None, :, None] * w[:, :, None, :]).reshape(h * n, h * n)


def _head_ones(width, head):
    idx = jnp.arange(width) // head
    return (idx[:, None] == idx[None, :]).astype(BF16)


def _rglru(p_lru, conv_w, conv_b, wa, ba, wi, bi, lam, norm_g, tt=256):
    bsz, t, _ = p_lru.shape
    w = LRU_WIDTH
    wab = jnp.concatenate([_block_diag(wa), _block_diag(wi)], axis=1).astype(BF16)
    bab = jnp.concatenate([ba.reshape(1, w), bi.reshape(1, w)], axis=1)
    const = lambda shape: pl.BlockSpec(shape, lambda b, i: (0, 0))
    return pl.pallas_call(
        _lru_kernel,
        grid=(bsz, t // tt),
        in_specs=[
            pl.BlockSpec((None, tt, 2 * w), lambda b, i: (b, i, 0)),
            const((CONV_WIDTH, w)), const((1, w)), const((w, 2 * w)), const((1, 2 * w)),
            const((1, w)), const((1, w)), const((w, w)),
        ],
        out_specs=pl.BlockSpec((None, tt, w), lambda b, i: (b, i, 0)),
        out_shape=jax.ShapeDtypeStruct((bsz, t, w), BF16),
        scratch_shapes=[pltpu.VMEM((SUBLANES, w), F32), pltpu.VMEM((SUBLANES, w), F32)],
        compiler_params=_cparams(("arbitrary", "arbitrary")),
        name="rglru",
    )(p_lru, conv_w, conv_b.reshape(1, w), wab, bab, lam.reshape(1, w),
      norm_g.reshape(1, w), _head_ones(w, LRU_HEAD_DIM))


PG = (1, 1)
PI = (1, 1)
PS = (1, 1)
PH = (1, 2)


def _unit_lower_inverse(a_list, ri, ci):
    mm = functools.partial(_mm, pa=PI[0], pb=PI[1])
    eye = jnp.where(ri == ci, 1.0, 0.0)
    leaf = (ri >> 3) == (ci >> 3)
    a8 = [jnp.where(leaf, a, 0.0) for a in a_list]
    a8_2 = [mm(x, x) for x in a8]
    a8_4 = [mm(x, x) for x in a8_2]
    t = [mm(eye + x, eye + y) for x, y in zip(a8, a8_2)]
    t = [mm(x, eye + y) for x, y in zip(t, a8_4)]
    zero = jnp.zeros((LANES, LANES), F32)
    for sh in (3, 4, 5):
        s = 1 << sh
        off = ((ri >> (sh + 1)) == (ci >> (sh + 1))) & ((ri >> sh) != (ci >> sh))
        t_lo = [_second_blocks(x, s) for x in t]
        b_lo = [mm(_second_blocks(jnp.where(off, a, 0.0), s), x) for a, x in zip(a_list, t)]
        d_lo = [mm(x, _interleave_blocks(zero, y, s)) for x, y in zip(t_lo, b_lo)]
        t = [_interleave_blocks(x, y + z, s) for x, y, z in zip(t, t_lo, d_lo)]
    return t


def _second_blocks(x, s):
    return jnp.concatenate(
        [x[s * (2 * m + 1):s * (2 * m + 2)] for m in range(x.shape[0] // (2 * s))], axis=0)


def _interleave_blocks(first_src, second, s):
    parts = []
    for m in range(first_src.shape[0] // (2 * s)):
        parts.append(first_src[2 * s * m:2 * s * m + s])
        parts.append(second[s * m:s * (m + 1)])
    return jnp.concatenate(parts, axis=0)


def _rwkv_kernel(p_ref, mu_ref, pv_ref, wlo_ref, gup_ref, o_ref, uprev_ref, h_ref):
    w = RWKV_WIDTH
    tt = p_ref.shape[0]
    c = CHUNK
    n_pairs = w // LANES
    units = [(j, p) for j in range(tt // c) for p in range(n_pairs)]

    @pl.when(pl.program_id(1) == 0)
    def _():
        uprev_ref[...] = jnp.zeros_like(uprev_ref)
        h_ref[...] = jnp.zeros_like(h_ref)

    u = p_ref[...]
    ext = jnp.concatenate([uprev_ref[...], u], axis=0)
    prev = pltpu.roll(ext, 1, 0)[SUBLANES:, :]
    uprev_ref[...] = u[tt - SUBLANES:, :]
    um = u + (prev - u) * mu_ref[...]

    r = um[:, 0:w]
    k = um[:, w:2 * w]
    v = um[:, 2 * w:3 * w]
    z = um[:, 3 * w:3 * w + LANES]
    gd = um[:, 3 * w + LANES:]
    w0 = pv_ref[0:1, :]
    a0 = pv_ref[1:2, :]
    k_k = pv_ref[2:3, :]
    k_a = pv_ref[3:4, :]
    r_k = pv_ref[4:5, :]
    ln_w = pv_ref[5:6, :]
    ln_b = pv_ref[6:7, :]

    lane_t = lax.broadcasted_iota(I32, (tt, LANES), 1)
    zz = jnp.where(lane_t < HEAD_DIM, jnp.tanh(z), z)
    lora = _mm(zz, wlo_ref[...], pa=2, pb=2)
    wlog = -_softplus(-(w0 + lora[:, :w])) - 0.5
    ld = -jnp.exp(wlog)
    a = jax.nn.sigmoid(a0 + lora[:, w:])
    g = _mm(jax.nn.sigmoid(gd), gup_ref[...], pa=2, pb=2)

    ri = lax.broadcasted_iota(I32, (LANES, LANES), 0)
    ci = lax.broadcasted_iota(I32, (LANES, LANES), 1)
    even = lax.broadcasted_iota(I32, (c, LANES), 1) < HEAD_DIM
    wide = 2 * LANES
    ones_bd = jnp.where(
        (lax.broadcasted_iota(I32, (wide, wide), 0) >> 6)
        == (lax.broadcasted_iota(I32, (wide, wide), 1) >> 6), 1.0, 0.0).astype(BF16)

    def head_sum(x):
        return jnp.concatenate(
            [_mm(x[:, q * wide:(q + 1) * wide], ones_bd) for q in range(w // wide)], axis=1)

    kk = k * k_k
    kk = kk / jnp.maximum(jnp.sqrt(head_sum(kk * kk)), 1e-12)
    kp = k * (1.0 + (a - 1.0) * k_a)
    kka = kk * a
    bonus = head_sum(r * kp * r_k) * v
    row_in_chunk = lax.broadcasted_iota(I32, (tt, w), 0) & (c - 1)
    cum = ld
    s = 1
    while s < c:
        cum = cum + jnp.where(row_in_chunk >= s, pltpu.roll(cum, s, 0), 0.0)
        s *= 2
    cum_c = [cum[j * c + c - 1:j * c + c, :] for j in range(tt // c)]
    p_c = [jnp.exp(x) for x in cum_c]
    p_inv = jnp.exp(-cum)
    p_hat = jnp.exp(jnp.concatenate(
        [cum_c[j] - cum[j * c:(j + 1) * c, :] for j in range(tt // c)], axis=0))
    al = -kk * jnp.exp(cum - ld)
    rt = r * jnp.exp(cum)
    bt = kka * p_inv
    kt = kp * p_inv
    bh = kka * p_hat
    kh = kp * p_hat

    def blk(x):
        return [x[j * c:(j + 1) * c, p * LANES:(p + 1) * LANES] for j, p in units]

    def halves(x):
        xs = blk(x)
        return [jnp.where(even, y, 0.0) for y in xs], [jnp.where(even, 0.0, y) for y in xs]

    def rows(top, bot):
        return [jnp.concatenate([x, y], axis=0) for x, y in zip(top, bot)]

    def unstack(x):
        return x[:c, :] + x[c:, :]

    al_e, al_o = halves(al)
    rt_e, rt_o = halves(rt)
    bt_e, bt_o = halves(bt)
    kt_e, kt_o = halves(kt)
    bh_e, bh_o = halves(bh)
    kh_e, kh_o = halves(kh)
    v_e, v_o = halves(v)
    al_n = rows(al_e, al_o)
    bh_n = rows(bh_e, bh_o)
    v_s = rows(v_o, v_e)
    kh_s = rows(kh_o, kh_e)
    rt_b = blk(rt)
    nu = range(len(units))

    g0 = [_mm(x, y, NT, pa=PG[0], pb=PG[1])
          for x, y in zip(rows(al_e, rt_e), rows(bt_e, kt_e))]
    g1 = [_mm(x, y, NT, pa=PG[0], pb=PG[1])
          for x, y in zip(rows(rt_o, al_o), rows(kt_o, bt_o))]
    top = ri < c
    left = ci < c
    tri_s = (ri & (c - 1)) > (ci & (c - 1))
    tri_i = (ri & (c - 1)) >= (ci & (c - 1))
    diag_q = top == left

    def pick(x0, x1, in_q0, tri):
        return [jnp.where(tri, jnp.where(in_q0, x, y), 0.0) for x, y in zip(x0, x1)]

    a_ab = pick(g0, g1, top, diag_q & tri_s)
    a_rk = pick(g1, g0, top, diag_q & tri_i)
    a_ak = pick(g0, g1, top, (~diag_q) & tri_s)
    a_rb = pick(g1, g0, top, (~diag_q) & tri_i)
    x1 = [_mm(a_ak[n], v_s[n], pa=PS[0], pb=PS[1]) for n in nu]
    akv = [_mm(a_rk[n], v_s[n], pa=PS[0], pb=PS[1]) for n in nu]
    khv = [_mm(kh_s[n], v_s[n], TN, pa=PS[0], pb=PS[1]) for n in nu]
    t_inv = _unit_lower_inverse(a_ab, ri, ci)
    tw = [_mm(t_inv[n], jnp.concatenate([al_n[n], x1[n]], axis=1), pa=PS[0], pb=PS[1])
          for n in nu]
    qo = [_mm(a_rb[n], tw[n], pa=PS[0], pb=PS[1]) for n in nu]
    mn = [_mm(bh_n[n], tw[n], TN, pa=PS[0], pb=PS[1]) for n in nu]
    q = [rt_b[n] + unstack(qo[n][:, :LANES]) for n in nu]
    o_loc = [unstack(qo[n][:, LANES:] + akv[n]) for n in nu]
    m_full = [mn[n][:, :LANES]
              + jnp.where(ri == ci, p_c[j][:, p * LANES:(p + 1) * LANES], 0.0)
              for n, (j, p) in enumerate(units)]
    n_loc = [mn[n][:, LANES:] + khv[n] for n in nu]

    h = [h_ref[p] for p in range(n_pairs)]
    o_rows = []
    for j in range(tt // c):
        o_parts = []
        for p in range(n_pairs):
            n = j * n_pairs + p
            o_parts.append(_mm(q[n], h[p], pa=PH[0], pb=PH[1]) + o_loc[n])
            h[p] = _mm(m_full[n], h[p], pa=PH[0], pb=PH[1]) + n_loc[n]
        o_rows.append(jnp.concatenate(o_parts, axis=1))
    for p in range(n_pairs):
        h_ref[p] = h[p]

    o = jnp.concatenate(o_rows, axis=0)
    mean = head_sum(o) * (1.0 / HEAD_DIM)
    d = o - mean
    var = head_sum(d * d) * (1.0 / HEAD_DIM)
    on = d * lax.rsqrt(var + GN_EPS) * ln_w + ln_b
    o_ref[...] = ((on + bonus) * g).astype(o_ref.dtype)


RWKV_TILE = 256


def _rwkv7(p_rw, mu, w0, w_up, a0, a_up, g_up, k_k, k_a, r_k, ln_w, ln_b):
    bsz, t, npj = p_rw.shape
    w = RWKV_WIDTH
    tt = RWKV_TILE
    pv = jnp.stack([w0, a0, k_k, k_a, r_k.reshape(w), ln_w, ln_b, jnp.zeros((w,), F32)])
    wlo = jnp.zeros((LANES, 2 * w), F32)
    wlo = wlo.at[:DECAY_LORA, :w].set(w_up).at[DECAY_LORA:, w:].set(a_up)
    const = lambda shape: pl.BlockSpec(shape, lambda b, i: (0, 0))
    return pl.pallas_call(
        _rwkv_kernel,
        grid=(bsz, t // tt),
        in_specs=[
            pl.BlockSpec((None, tt, npj), lambda b, i: (b, i, 0)),
            const((1, npj)), const((SUBLANES, w)), const((LANES, 2 * w)),
            const((GATE_LORA, w)),
        ],
        out_specs=pl.BlockSpec((None, tt, w), lambda b, i: (b, i, 0)),
        out_shape=jax.ShapeDtypeStruct((bsz, t, w), BF16),
        scratch_shapes=[pltpu.VMEM((SUBLANES, npj), F32),
                        pltpu.VMEM((w // LANES, LANES, LANES), F32)],
        compiler_params=_cparams(("arbitrary", "arbitrary")),
        name="rwkv7",
    )(p_rw, mu.reshape(1, npj), pv, wlo, g_up)


def _mix_kernel(x_ref, yl_ref, yr_ref, wo1_ref, wo2_ref, g1_ref, sh_ref, sc_ref, ng_ref,
                wr_ref, br_ref, x1_ref, h2_ref, rid_ref, rw_ref):
    y = (jnp.dot(yl_ref[...], wo1_ref[...], preferred_element_type=F32)
         + jnp.dot(yr_ref[...], wo2_ref[...], preferred_element_type=F32))
    x1 = x_ref[...] + g1_ref[...] * y
    x1_ref[...] = x1
    h2 = _adaln(x1, ng_ref[...], sh_ref[...], sc_ref[...])
    h2_ref[...] = _pack_bf16_halves(h2)
    logits = _mm(h2, wr_ref[...], pa=2, pb=2) + br_ref[...]

    tm = logits.shape[0]
    neg = -jnp.inf
    lane = lax.broadcasted_iota(I32, (tm, LANES), 1)
    lane_f = lane.astype(F32)

    def first_argmax(x):
        m = jnp.max(x, axis=-1, keepdims=True)
        idx = jnp.min(jnp.where(x == m, lane_f, float(LANES)), axis=-1, keepdims=True)
        return m, idx.astype(I32)

    lg = jnp.where(lane < N_GROUPS, logits, neg)
    gm, g_idx = first_argmax(lg)
    g_w = 1.0 / jnp.sum(jnp.exp(lg - gm), axis=-1, keepdims=True)
    lo = N_GROUPS + EXPERTS_PER_GROUP * g_idx
    le = jnp.where((lane >= lo) & (lane < lo + EXPERTS_PER_GROUP), logits, neg)
    v1, i1 = first_argmax(le)
    v2, i2 = first_argmax(jnp.where(lane == i1, neg, le))
    e2 = jnp.exp(v2 - v1)
    w1 = g_w / (1.0 + e2)
    w2 = g_w * e2 / (1.0 + e2)
    rid_ref[...] = jnp.where(lane == 0, i1 - N_GROUPS, jnp.where(lane == 1, i2 - N_GROUPS, 0))
    rw_ref[...] = jnp.where(lane == 0, w1, jnp.where(lane == 1, w2, 0.0))


def _mix(x, y_lru, y_rw, w_out, g1, sh2, sc2, ng, w_grp, b_grp, w_exp, b_exp, tm=256):
    bsz, t, d = x.shape
    wo = w_out.astype(BF16)
    wl = LRU_WIDTH
    wr = jnp.zeros((d, LANES), F32).at[:, :N_GROUPS].set(w_grp)
    wr = wr.at[:, N_GROUPS:N_GROUPS + N_EXPERTS].set(w_exp)
    br = jnp.zeros((1, LANES), F32).at[0, :N_GROUPS].set(b_grp)
    br = br.at[0, N_GROUPS:N_GROUPS + N_EXPERTS].set(b_exp)
    vec = pl.BlockSpec((None, 1, d), lambda b, i: (b, 0, 0))
    const = lambda shape: pl.BlockSpec(shape, lambda b, i: (0, 0))
    row = lambda n: pl.BlockSpec((None, tm, n), lambda b, i: (b, i, 0))
    return pl.pallas_call(
        _mix_kernel,
        grid=(bsz, t // tm),
        in_specs=[row(d), row(wl), row(d - wl), const((wl, d)), const((d - wl, d)),
                  vec, vec, vec, const((1, d)), const((d, LANES)), const((1, LANES))],
        out_specs=[row(d), row(d // 2), row(LANES), row(LANES)],
        out_shape=[jax.ShapeDtypeStruct((bsz, t, d), F32),
                   jax.ShapeDtypeStruct((bsz, t, d // 2), I32),
                   jax.ShapeDtypeStruct((bsz, t, LANES), I32),
                   jax.ShapeDtypeStruct((bsz, t, LANES), F32)],
        compiler_params=_cparams(("arbitrary", "arbitrary")),
        name="outproj_adaln2_router",
    )(x, y_lru, y_rw, wo[:wl], wo[wl:], g1, sh2, sc2, ng.reshape(1, d), wr, br)


def _plan_kernel(rid_ref, tril_ref, pos_ref, te_ref, cnt_ref, base_ref):
    ph = pl.program_id(0)
    i = pl.program_id(1)
    tp = rid_ref.shape[0]
    lane = lax.broadcasted_iota(I32, (tp, LANES), 1)
    rid = rid_ref[...]
    id0 = rid[:, 0:1]
    id1 = rid[:, 1:2]
    oh0 = lane == id0
    oh1 = lane == id1
    oh = jnp.where(oh0 | oh1, 1.0, 0.0)
    colsum = jnp.sum(oh, axis=0, keepdims=True).astype(I32)

    @pl.when((ph == 0) & (i == 0))
    def _():
        cnt_ref[...] = jnp.zeros_like(cnt_ref)

    @pl.when(ph == 0)
    def _():
        cnt_ref[...] = cnt_ref[...] + colsum

    @pl.when((ph == 1) & (i == 0))
    def _():
        cnt = cnt_ref[...]
        shift = EXPERT_TILE.bit_length() - 1
        padded = ((cnt + (EXPERT_TILE - 1)) >> shift) << shift
        l8 = lax.broadcasted_iota(I32, (SUBLANES, LANES), 1)
        end = padded
        s = 1
        while s < LANES:
            end = end + jnp.where(l8 >= s, pltpu.roll(end, s, 1), 0)
            s *= 2
        base_ref[...] = end - padded
        nt = te_ref.shape[0]
        j = lax.broadcasted_iota(I32, (nt, LANES), 0) * EXPERT_TILE
        lt = lax.broadcasted_iota(I32, (nt, LANES), 1)
        done = jnp.where((lt < N_EXPERTS) & (end[0:1, :] <= j), 1.0, 0.0)
        e_of = jnp.minimum(jnp.sum(done, axis=-1, keepdims=True), float(N_EXPERTS - 1)).astype(I32)
        total = end[0:1, N_EXPERTS - 1:N_EXPERTS]
        mine = lt == e_of
        cnt_e = jnp.sum(jnp.where(mine, cnt[0:1, :], 0), axis=-1, keepdims=True)
        start_e = jnp.sum(jnp.where(mine, end[0:1, :] - padded[0:1, :], 0), axis=-1, keepdims=True)
        valid = jnp.clip(cnt_e - (j[:, 0:1] - start_e), 0, EXPERT_TILE)
        te_ref[...] = jnp.where(lt == 0, e_of, jnp.where(lt == 1, total >> shift,
                                                        jnp.where(lt == 2, valid, 0)))

    @pl.when(ph == 1)
    def _():
        prefix = jnp.dot(tril_ref[...], oh.astype(BF16), preferred_element_type=F32)
        dest = base_ref[0:1, :] + prefix.astype(I32)
        pos0 = jnp.sum(jnp.where(oh0, dest, 0), axis=-1, keepdims=True)
        pos1 = jnp.sum(jnp.where(oh1, dest, 0), axis=-1, keepdims=True)
        pos_ref[...] = jnp.where(lane == 0, pos0, jnp.where(lane == 1, pos1, 0))
        base_ref[...] = base_ref[...] + colsum


def _plan(rid, n_tiles, tp=512):
    n = rid.shape[0]
    tp = min(tp, n)
    nt_pad = -(-n_tiles // SUBLANES) * SUBLANES
    tril = (jnp.arange(tp)[:, None] > jnp.arange(tp)[None, :]).astype(BF16)
    pos, te = pl.pallas_call(
        _plan_kernel,
        grid=(2, n // tp),
        in_specs=[pl.BlockSpec((tp, LANES), lambda ph, i: (i, 0)),
                  pl.BlockSpec((tp, tp), lambda ph, i: (0, 0))],
        out_specs=[pl.BlockSpec((tp, LANES), lambda ph, i: (i * ph, 0)),
                   pl.BlockSpec((nt_pad, LANES), lambda ph, i: (0, 0))],
        out_shape=[jax.ShapeDtypeStruct((n, LANES), I32),
                   jax.ShapeDtypeStruct((nt_pad, LANES), I32)],
        scratch_shapes=[pltpu.VMEM((SUBLANES, LANES), I32), pltpu.VMEM((SUBLANES, LANES), I32)],
        compiler_params=_cparams(("arbitrary", "arbitrary")),
        name="route_plan",
    )(rid, tril)
    return pos, te


SC_CORES = 2
SC_SUBCORES = 16
SC_WORKERS = SC_CORES * SC_SUBCORES
SC_WINDOW = 64


def _sc_mesh():
    return plsc.VectorSubcoreMesh(core_axis_name="c", subcore_axis_name="s",
                                  num_cores=SC_CORES, num_subcores=SC_SUBCORES)


def _sc_two_buffer_loop(n_win, fill, drain):
    assert n_win % 2 == 0

    def start(copies):
        for cp in copies:
            cp.start()

    def wait(copies):
        for cp in copies:
            cp.wait()

    start(fill(0, 0))

    @pl.loop(0, n_win, step=2)
    def _(j):
        for b in range(2):
            jj = j + b

            @pl.when(jj >= 1)
            def _():
                wait(drain(jj - 1, 1 - b))

            @pl.when(jj + 1 < n_win)
            def _():
                start(fill(jj + 1, 1 - b))

            wait(fill(jj, b))
            start(drain(jj, b))

    wait(drain(n_win - 1, 1))


def _sc_scatter_rows(rows, pos0, pos1, n_out):
    n, d = rows.shape
    per_w = n // SC_WORKERS
    n_win = per_w // SC_WINDOW
    shape3 = (SC_WORKERS, n_win, SC_WINDOW)

    @functools.partial(
        pl.kernel, mesh=_sc_mesh(), out_type=jax.ShapeDtypeStruct((n_out, d), rows.dtype),
        scratch_types=[pltpu.VMEM((n_win, SC_WINDOW), I32), pltpu.VMEM((n_win, SC_WINDOW), I32),
                       pltpu.VMEM((2, SC_WINDOW, d), rows.dtype),
                       pltpu.SemaphoreType.DMA((2,)), pltpu.SemaphoreType.DMA((2,))],
        name="sc_scatter_rows")
    def scatter(rows_hbm, p0_hbm, p1_hbm, out_hbm, p0_v, p1_v, buf, in_sem, out_sem):
        wid = lax.axis_index("s") * SC_CORES + lax.axis_index("c")
        base = wid * per_w
        pltpu.sync_copy(p0_hbm.at[wid], p0_v)
        pltpu.sync_copy(p1_hbm.at[wid], p1_v)

        def fill(j, b):
            src = rows_hbm.at[pl.ds(base + j * SC_WINDOW, SC_WINDOW)]
            return [pltpu.make_async_copy(src, buf.at[b], in_sem.at[b])]

        def drain(j, b):
            return [pltpu.make_async_copy(buf.at[b], out_hbm.at[p.at[j]], out_sem.at[b])
                    for p in (p0_v, p1_v)]

        _sc_two_buffer_loop(n_win, fill, drain)

    return scatter(rows, pos0.reshape(shape3), pos1.reshape(shape3))


def _sc_gather_rows(table, idx):
    m = idx.shape[0]
    d = table.shape[1]
    per_w = m // SC_WORKERS
    n_win = per_w // SC_WINDOW

    @functools.partial(
        pl.kernel, mesh=_sc_mesh(), out_type=jax.ShapeDtypeStruct((m, d), table.dtype),
        scratch_types=[pltpu.VMEM((n_win, SC_WINDOW), I32),
                       pltpu.VMEM((2, SC_WINDOW, d), table.dtype),
                       pltpu.SemaphoreType.DMA((2,)), pltpu.SemaphoreType.DMA((2,))],
        name="sc_gather_rows")
    def gather(table_hbm, idx_hbm, out_hbm, idx_v, buf, in_sem, out_sem):
        wid = lax.axis_index("s") * SC_CORES + lax.axis_index("c")
        base = wid * per_w
        pltpu.sync_copy(idx_hbm.at[wid], idx_v)

        def fill(j, b):
            return [pltpu.make_async_copy(table_hbm.at[idx_v.at[j]], buf.at[b], in_sem.at[b])]

        def drain(j, b):
            dst = out_hbm.at[pl.ds(base + j * SC_WINDOW, SC_WINDOW)]
            return [pltpu.make_async_copy(buf.at[b], dst, out_sem.at[b])]

        _sc_two_buffer_loop(n_win, fill, drain)

    return gather(table, idx.reshape(SC_WORKERS, n_win, SC_WINDOW))


def _expert_kernel(te_ref, nu_ref, valid_ref, x_ref, w1_ref, w3_ref, w2_ref, o_ref,
                   w1b_ref, w3b_ref, w2b_ref):
    j = pl.program_id(0)
    active = j < nu_ref[0]

    @pl.when(active & ((j == 0) | (te_ref[j] != te_ref[jnp.maximum(j - 1, 0)])))
    def _():
        w1b_ref[...] = w1_ref[...].astype(BF16)
        w3b_ref[...] = w3_ref[...].astype(BF16)
        w2b_ref[...] = w2_ref[...].astype(BF16)

    @pl.when(active)
    def _():
        row = lax.broadcasted_iota(I32, x_ref.shape, 0)
        x = _unpack_bf16_halves(jnp.where(row < valid_ref[j], x_ref[...], 0)).astype(BF16)
        h1 = jnp.dot(x, w1b_ref[...], preferred_element_type=F32)
        h3 = jnp.dot(x, w3b_ref[...], preferred_element_type=F32)
        hid = (h1 * jax.nn.sigmoid(h1) * h3).astype(BF16)
        o_ref[...] = _pack_bf16_halves(jnp.dot(hid, w2b_ref[...], preferred_element_type=F32))

    @pl.when(j >= nu_ref[0])
    def _():
        o_ref[...] = jnp.zeros_like(o_ref)


def _experts(xs, te, nu, valid, w1, w3, w2):
    n_rows = xs.shape[0]
    d, de = w1.shape[-2:]
    nt = n_rows // EXPERT_TILE
    used = lambda j, nu: jnp.minimum(j, nu[0] - 1)
    grid_spec = pltpu.PrefetchScalarGridSpec(
        num_scalar_prefetch=3,
        grid=(nt,),
        in_specs=[
            pl.BlockSpec((EXPERT_TILE, d // 2), lambda j, te, nu, va: (used(j, nu), 0)),
            pl.BlockSpec((None, d, de), lambda j, te, nu, va: (te[used(j, nu)], 0, 0)),
            pl.BlockSpec((None, d, de), lambda j, te, nu, va: (te[used(j, nu)], 0, 0)),
            pl.BlockSpec((None, de, d), lambda j, te, nu, va: (te[used(j, nu)], 0, 0)),
        ],
        out_specs=pl.BlockSpec((EXPERT_TILE, d // 2), lambda j, te, nu, va: (j, 0)),
        scratch_shapes=[pltpu.VMEM((d, de), BF16), pltpu.VMEM((d, de), BF16),
                        pltpu.VMEM((de, d), BF16)],
    )
    return pl.pallas_call(
        _expert_kernel,
        grid_spec=grid_spec,
        out_shape=jax.ShapeDtypeStruct((n_rows, d // 2), I32),
        compiler_params=_cparams(("arbitrary",)),
        name="expert_mlp",
    )(te, nu, valid, xs, w1, w3, w2)


def _combine_kernel(x1_ref, rw_ref, g2_ref, fg_ref, y0_ref, y1_ref, o_ref):
    rw = rw_ref[...]
    moe = (rw[:, 0:1] * _unpack_bf16_halves(y0_ref[...])
           + rw[:, 1:2] * _unpack_bf16_halves(y1_ref[...]))
    x2 = x1_ref[...] + g2_ref[...] * moe
    ms = jnp.mean(x2 * x2, axis=-1, keepdims=True)
    o_ref[...] = x2 * lax.rsqrt(ms + RMS_EPS) * fg_ref[...]


def _combine(x1, rw, yg, g2, final_g, tc):
    bsz, t, d = x1.shape
    row = lambda n: pl.BlockSpec((None, tc, n), lambda b, i: (b, i, 0))
    slot = lambda s: pl.BlockSpec((None, None, tc, d // 2), lambda b, i: (s, b, i, 0))
    return pl.pallas_call(
        _combine_kernel,
        grid=(bsz, t // tc),
        in_specs=[row(d), row(LANES),
                  pl.BlockSpec((None, 1, d), lambda b, i: (b, 0, 0)),
                  pl.BlockSpec((1, d), lambda b, i: (0, 0)),
                  slot(0), slot(1)],
        out_specs=row(d),
        out_shape=jax.ShapeDtypeStruct((bsz, t, d), F32),
        compiler_params=_cparams(("arbitrary", "arbitrary")),
        name="combine_final_norm",
    )(x1, rw, g2, final_g.reshape(1, d), yg, yg)


def _row_tile(t, want):
    return want if t % want == 0 else t


def kernel(x, c, w_ada, b_ada, norm1_g, w_in, conv_w, conv_b, lru_wa, lru_ba, lru_wi, lru_bi, lru_lam, lru_norm_g, tok_mu, w0, w_up, a0, a_up, g_up, k_k, k_a, r_k, ln_x_w, ln_x_b, w_out, norm2_g, w_grp, b_grp, w_exp, b_exp, w1, w3, w2, final_g):
    bsz, t, d = x.shape
    n = bsz * t
    depth = w_ada.shape[0]
    assert depth == 1, "the combine kernel applies the final norm: only DEPTH == 1 is wired"
    tile = _row_tile(t, 256)
    n_tiles = (n * TOP_K) // EXPERT_TILE + N_EXPERTS
    for l in range(depth):
        mod = _modulation(c, w_ada[l], b_ada[l]).reshape(bsz, 6, 1, d)
        sh1, sc1, g1, sh2, sc2, g2 = (mod[:, i] for i in range(6))
        p_lru, p_rw = _inproj(x, sh1, sc1, norm1_g[l], w_in[l], tm=tile)
        y_lru = _rglru(p_lru, conv_w[l], conv_b[l], lru_wa[l], lru_ba[l], lru_wi[l], lru_bi[l],
                       lru_lam[l], lru_norm_g[l], tt=tile)
        y_rw = _rwkv7(p_rw, tok_mu[l], w0[l], w_up[l], a0[l], a_up[l], g_up[l], k_k[l], k_a[l],
                      r_k[l], ln_x_w[l], ln_x_b[l])
        x1, h2, rid, rw = _mix(x, y_lru, y_rw, w_out[l], g1, sh2, sc2, norm2_g[l],
                               w_grp[l], b_grp[l], w_exp[l], b_exp[l], tm=tile)
        pos, te = _plan(rid.reshape(n, LANES), n_tiles)
        pos0, pos1 = pos[:, 0], pos[:, 1]
        xs = _sc_scatter_rows(h2.reshape(n, d // 2), pos0, pos1, n_tiles * EXPERT_TILE)
        ys = _experts(xs, te[:n_tiles, 0], te[0:1, 1], te[:n_tiles, 2], w1[l], w3[l], w2[l])
        yg = _sc_gather_rows(ys, jnp.concatenate([pos0, pos1]))
        x = _combine(x1, rw, yg.reshape(TOP_K, bsz, t, d // 2), g2, final_g, tile)
    return x
```

```python
import functools

import jax
import jax.numpy as jnp
from jax import lax
from jax.experimental import pallas as pl
from jax.experimental.pallas import tpu as pltpu
from jax.experimental.pallas import tpu_sc as plsc

F32 = jnp.float32
BF16 = jnp.bfloat16
I32 = jnp.int32

LRU_WIDTH = 512
LRU_HEAD_DIM = 64
CONV_WIDTH = 4
LRU_C = 8.0
RWKV_WIDTH = 512
HEAD_DIM = 64
DECAY_LORA = 64
AAA_LORA = 64
GATE_LORA = 128
RWKV_PROJ = 3 * RWKV_WIDTH + DECAY_LORA + AAA_LORA + GATE_LORA
N_GROUPS = 4
EXPERTS_PER_GROUP = 8
N_EXPERTS = N_GROUPS * EXPERTS_PER_GROUP
TOP_K = 2
RMS_EPS = 1e-6
GN_EPS = 64e-5

LANES = 128
SUBLANES = 8
CHUNK = 64
EXPERT_TILE = 512
VMEM_LIMIT = 48 * 1024 * 1024

NN = (((1,), (0,)), ((), ()))
NT = (((1,), (1,)), ((), ()))
TN = (((0,), (0,)), ((), ()))


def _split(x, n):
    if x.dtype == BF16:
        return [x]
    parts = []
    rem = x
    for i in range(n):
        p = rem.astype(BF16)
        parts.append(p)
        if i + 1 < n:
            rem = rem - p.astype(F32)
    return parts


def _mm(a, b, dn=NN, pa=1, pb=1):
    aps = _split(a, pa)
    bps = _split(b, pb)
    order = max(len(aps), len(bps))
    terms = [(i, j) for i in range(len(aps)) for j in range(len(bps)) if i + j < order]
    ka = dn[0][0][0]
    kb = dn[0][1][0]
    if len(terms) > 1 and a.shape[ka] % LANES == 0:
        a_cat = jnp.concatenate([aps[i] for i, _ in terms], axis=ka)
        b_cat = jnp.concatenate([bps[j] for _, j in terms], axis=kb)
        return lax.dot_general(a_cat, b_cat, dn, preferred_element_type=F32)
    out = None
    for i, j in terms:
        t = lax.dot_general(aps[i], bps[j], dn, preferred_element_type=F32)
        out = t if out is None else out + t
    return out


def _pack_bf16_halves(x):
    n = x.shape[1] // 2
    bits = lax.bitcast_convert_type(x.astype(BF16).astype(F32), I32)
    return bits[:, n:] | ((bits[:, :n] >> 16) & 0xFFFF)


def _unpack_bf16_halves(p):
    lo = lax.bitcast_convert_type(p << 16, F32)
    hi = lax.bitcast_convert_type(p & (-65536), F32)
    return jnp.concatenate([lo, hi], axis=1)


def _softplus(x):
    return jnp.maximum(x, 0.0) + jnp.log1p(jnp.exp(-jnp.abs(x)))


def _cparams(sem):
    return pltpu.CompilerParams(dimension_semantics=sem, vmem_limit_bytes=VMEM_LIMIT)


def _mod_kernel(c_ref, w_ref, b_ref, o_ref):
    c = c_ref[...]
    s = c * jax.nn.sigmoid(c)
    o_ref[...] = _mm(s, w_ref[...], pa=2, pb=2) + b_ref[...]


def _modulation(c, w_ada, b_ada):
    bsz, d = c.shape
    n_out = w_ada.shape[1]
    rows = -(-bsz // SUBLANES) * SUBLANES
    c_pad = jnp.zeros((rows, d), F32).at[:bsz].set(c)
    bn = d
    out = pl.pallas_call(
        _mod_kernel,
        grid=(n_out // bn,),
        in_specs=[
            pl.BlockSpec((rows, d), lambda j: (0, 0)),
            pl.BlockSpec((d, bn), lambda j: (0, j)),
            pl.BlockSpec((1, bn), lambda j: (0, j)),
        ],
        out_specs=pl.BlockSpec((rows, bn), lambda j: (0, j)),
        out_shape=jax.ShapeDtypeStruct((rows, n_out), F32),
        compiler_params=_cparams(("arbitrary",)),
        name="adaln_mod",
    )(c_pad, w_ada, b_ada.reshape(1, n_out))
    return out[:bsz]


def _adaln(x, g, shift, scale):
    ms = jnp.mean(x * x, axis=-1, keepdims=True)
    y = x * lax.rsqrt(ms + RMS_EPS) * g
    return y * (1.0 + scale) + shift


def _inproj_lru_kernel(x_ref, sh_ref, sc_ref, g_ref, wl_ref, wr_ref,
                       cw_ref, cb_ref, wab_ref, bab_ref, lam_ref, ng_ref, ones_ref,
                       yl_ref, pr_ref, xprev_ref, hprev_ref):
    @pl.when(pl.program_id(1) == 0)
    def _():
        xprev_ref[...] = jnp.zeros_like(xprev_ref)
        hprev_ref[...] = jnp.zeros_like(hprev_ref)

    h = _adaln(x_ref[...], g_ref[...], sh_ref[...], sc_ref[...]).astype(BF16)
    p_lru = jnp.dot(h, wl_ref[...], preferred_element_type=F32)
    pr_ref[...] = jnp.dot(h, wr_ref[...], preferred_element_type=F32)

    y = _lru_tile(p_lru, cw_ref, cb_ref, wab_ref, bab_ref, lam_ref, ng_ref, ones_ref,
                  xprev_ref, hprev_ref)
    yl_ref[...] = y.astype(yl_ref.dtype)


def _inproj_lru(x, sh1, sc1, g, w_in, conv_w, conv_b, wa, ba, wi, bi, lam, norm_g, tm=256):
    bsz, t, d = x.shape
    w = LRU_WIDTH
    nl = 2 * w
    nr = RWKV_PROJ
    wl = w_in[:, :nl].astype(BF16)
    wr = w_in[:, nl:].astype(BF16)
    wab = jnp.concatenate([_block_diag(wa), _block_diag(wi)], axis=1).astype(BF16)
    bab = jnp.concatenate([ba.reshape(1, w), bi.reshape(1, w)], axis=1)
    vec = pl.BlockSpec((None, 1, d), lambda b, i: (b, 0, 0))
    const = lambda shape: pl.BlockSpec(shape, lambda b, i: (0, 0))
    return pl.pallas_call(
        _inproj_lru_kernel,
        grid=(bsz, t // tm),
        in_specs=[
            pl.BlockSpec((None, tm, d), lambda b, i: (b, i, 0)),
            vec, vec, const((1, d)), const((d, nl)), const((d, nr)),
            const((CONV_WIDTH, w)), const((1, w)), const((w, 2 * w)), const((1, 2 * w)),
            const((1, w)), const((1, w)), const((w, w)),
        ],
        out_specs=[
            pl.BlockSpec((None, tm, w), lambda b, i: (b, i, 0)),
            pl.BlockSpec((None, tm, nr), lambda b, i: (b, i, 0)),
        ],
        out_shape=[
            jax.ShapeDtypeStruct((bsz, t, w), BF16),
            jax.ShapeDtypeStruct((bsz, t, nr), F32),
        ],
        scratch_shapes=[pltpu.VMEM((SUBLANES, w), F32), pltpu.VMEM((SUBLANES, w), F32)],
        compiler_params=_cparams(("arbitrary", "arbitrary")),
        name="adaln1_inproj_rglru",
    )(x, sh1, sc1, g.reshape(1, d), wl, wr, conv_w, conv_b.reshape(1, w), wab, bab,
      lam.reshape(1, w), norm_g.reshape(1, w), _head_ones(w, LRU_HEAD_DIM))


def _gelu_tanh(x):
    c = 0.7978845608028654
    return x * (0.5 * (1.0 + jnp.tanh(c * (x + 0.044715 * (x * x * x)))))


def _sigmoid(x):
    return 0.5 * jnp.tanh(0.5 * x) + 0.5


def _lru_tile(p, cw_ref, cb_ref, wab_ref, bab_ref, lam_ref, ng_ref, ones_ref,
              xprev_ref, hprev_ref):
    w = LRU_WIDTH
    tt = p.shape[0]
    ux = p[:, :w]
    ug = p[:, w:]
    ext = jnp.concatenate([xprev_ref[...], ux], axis=0)
    xc = cb_ref[...] + cw_ref[CONV_WIDTH - 1:CONV_WIDTH, :] * ux
    for k in range(1, CONV_WIDTH):
        shifted = pltpu.roll(ext, k, 0)[SUBLANES:, :]
        xc = xc + cw_ref[CONV_WIDTH - 1 - k:CONV_WIDTH - k, :] * shifted
    xprev_ref[...] = ux[tt - SUBLANES:, :]

    gates = jnp.dot(xc.astype(BF16), wab_ref[...], preferred_element_type=F32) + bab_ref[...]
    r = _sigmoid(gates[:, :w])
    ig = _sigmoid(gates[:, w:])
    log_a = (-LRU_C) * r * _softplus(-lam_ref[...])
    a = jnp.exp(log_a)
    th = jnp.tanh(log_a)
    q = (-2.0 * th) / (1.0 - th)
    root_q = jnp.where(q > 0.0, q * lax.rsqrt(q), 0.0)
    b = root_q * (ig * xc)

    row8 = lax.broadcasted_iota(I32, (tt, w), 0) & (SUBLANES - 1)
    acc_a, acc_b = a, b
    for s in (1, 2, 4):
        sh_a = pltpu.roll(acc_a, s, 0)
        sh_b = pltpu.roll(acc_b, s, 0)
        live = row8 >= s
        acc_b = jnp.where(live, acc_a * sh_b + acc_b, acc_b)
        acc_a = jnp.where(live, acc_a * sh_a, acc_a)
    carry = hprev_ref[SUBLANES - 1:SUBLANES, :]
    groups = []
    for gi in range(tt // SUBLANES):
        lo = gi * SUBLANES
        groups.append(acc_a[lo:lo + SUBLANES, :] * carry + acc_b[lo:lo + SUBLANES, :])
        last = lo + SUBLANES - 1
        carry = acc_a[last:last + 1, :] * carry + acc_b[last:last + 1, :]
    h = jnp.concatenate(groups, axis=0)
    hprev_ref[...] = groups[-1]

    y = h * _gelu_tanh(ug)
    ms = _mm(y * y, ones_ref[...]) * (1.0 / LRU_HEAD_DIM)
    return y * lax.rsqrt(ms + RMS_EPS) * ng_ref[...]


def _block_diag(w):
    h, n, _ = w.shape
    eye = jnp.eye(h, dtype=w.dtype)
    return (eye[:, None, :, None] * w[:, :, None, :]).reshape(h * n, h * n)


def _head_ones(width, head):
    idx = jnp.arange(width) // head
    return (idx[:, None] == idx[None, :]).astype(BF16)


PG = (1, 1)
PI = (1, 1)
PS = (1, 1)
PH = (1, 2)


def _unit_lower_inverse(a_list, ri, ci):
    mm = functools.partial(_mm, pa=PI[0], pb=PI[1])
    eye = jnp.where(ri == ci, 1.0, 0.0)
    leaf = (ri >> 3) == (ci >> 3)
    a8 = [jnp.where(leaf, a, 0.0) for a in a_list]
    a8_2 = [mm(x, x) for x in a8]
    a8_4 = [mm(x, x) for x in a8_2]
    t = [mm(eye + x, eye + y) for x, y in zip(a8, a8_2)]
    t = [mm(x, eye + y) for x, y in zip(t, a8_4)]
    zero = jnp.zeros((LANES, LANES), F32)
    for sh in (3, 4, 5):
        s = 1 << sh
        off = ((ri >> (sh + 1)) == (ci >> (sh + 1))) & ((ri >> sh) != (ci >> sh))
        t_lo = [_second_blocks(x, s) for x in t]
        b_lo = [mm(_second_blocks(jnp.where(off, a, 0.0), s), x) for a, x in zip(a_list, t)]
        d_lo = [mm(x, _interleave_blocks(zero, y, s)) for x, y in zip(t_lo, b_lo)]
        t = [_interleave_blocks(x, y + z, s) for x, y, z in zip(t, t_lo, d_lo)]
    return t


def _second_blocks(x, s):
    return jnp.concatenate(
        [x[s * (2 * m + 1):s * (2 * m + 2)] for m in range(x.shape[0] // (2 * s))], axis=0)


def _interleave_blocks(first_src, second, s):
    parts = []
    for m in range(first_src.shape[0] // (2 * s)):
        parts.append(first_src[2 * s * m:2 * s * m + s])
        parts.append(second[s * m:s * (m + 1)])
    return jnp.concatenate(parts, axis=0)


def _rwkv_kernel(p_ref, mu_ref, pv_ref, wlo_ref, gup_ref, o_ref, uprev_ref, h_ref):
    w = RWKV_WIDTH
    tt = p_ref.shape[0]
    c = CHUNK
    n_pairs = w // LANES
    units = [(j, p) for j in range(tt // c) for p in range(n_pairs)]

    @pl.when(pl.program_id(1) == 0)
    def _():
        uprev_ref[...] = jnp.zeros_like(uprev_ref)
        h_ref[...] = jnp.zeros_like(h_ref)

    u = p_ref[...]
    ext = jnp.concatenate([uprev_ref[...], u], axis=0)
    prev = pltpu.roll(ext, 1, 0)[SUBLANES:, :]
    uprev_ref[...] = u[tt - SUBLANES:, :]
    um = u + (prev - u) * mu_ref[...]

    r = um[:, 0:w]
    k = um[:, w:2 * w]
    v = um[:, 2 * w:3 * w]
    z = um[:, 3 * w:3 * w + LANES]
    gd = um[:, 3 * w + LANES:]
    w0 = pv_ref[0:1, :]
    a0 = pv_ref[1:2, :]
    k_k = pv_ref[2:3, :]
    k_a = pv_ref[3:4, :]
    r_k = pv_ref[4:5, :]
    ln_w = pv_ref[5:6, :]
    ln_b = pv_ref[6:7, :]

    lane_t = lax.broadcasted_iota(I32, (tt, LANES), 1)
    zz = jnp.where(lane_t < HEAD_DIM, jnp.tanh(z), z)
    lora = _mm(zz, wlo_ref[...], pa=2, pb=2)
    wlog = -_softplus(-(w0 + lora[:, :w])) - 0.5
    ld = -jnp.exp(wlog)
    a = jax.nn.sigmoid(a0 + lora[:, w:])
    g = _mm(jax.nn.sigmoid(gd), gup_ref[...], pa=2, pb=2)

    ri = lax.broadcasted_iota(I32, (LANES, LANES), 0)
    ci = lax.broadcasted_iota(I32, (LANES, LANES), 1)
    even = lax.broadcasted_iota(I32, (c, LANES), 1) < HEAD_DIM
    wide = 2 * LANES
    ones_bd = jnp.where(
        (lax.broadcasted_iota(I32, (wide, wide), 0) >> 6)
        == (lax.broadcasted_iota(I32, (wide, wide), 1) >> 6), 1.0, 0.0).astype(BF16)

    def head_sum(x):
        return jnp.concatenate(
            [_mm(x[:, q * wide:(q + 1) * wide], ones_bd) for q in range(w // wide)], axis=1)

    kk = k * k_k
    kk = kk / jnp.maximum(jnp.sqrt(head_sum(kk * kk)), 1e-12)
    kp = k * (1.0 + (a - 1.0) * k_a)
    kka = kk * a
    bonus = head_sum(r * kp * r_k) * v
    row_in_chunk = lax.broadcasted_iota(I32, (tt, w), 0) & (c - 1)
    cum = ld
    s = 1
    while s < c:
        cum = cum + jnp.where(row_in_chunk >= s, pltpu.roll(cum, s, 0), 0.0)
        s *= 2
    cum_c = [cum[j * c + c - 1:j * c + c, :] for j in range(tt // c)]
    p_c = [jnp.exp(x) for x in cum_c]
    p_inv = jnp.exp(-cum)
    p_hat = jnp.exp(jnp.concatenate(
        [cum_c[j] - cum[j * c:(j + 1) * c, :] for j in range(tt // c)], axis=0))
    al = -kk * jnp.exp(cum - ld)
    rt = r * jnp.exp(cum)
    bt = kka * p_inv
    kt = kp * p_inv
    bh = kka * p_hat
    kh = kp * p_hat

    def blk(x):
        return [x[j * c:(j + 1) * c, p * LANES:(p + 1) * LANES] for j, p in units]

    def halves(x):
        xs = blk(x)
        return [jnp.where(even, y, 0.0) for y in xs], [jnp.where(even, 0.0, y) for y in xs]

    def rows(top, bot):
        return [jnp.concatenate([x, y], axis=0) for x, y in zip(top, bot)]

    def unstack(x):
        return x[:c, :] + x[c:, :]

    al_e, al_o = halves(al)
    rt_e, rt_o = halves(rt)
    bt_e, bt_o = halves(bt)
    kt_e, kt_o = halves(kt)
    bh_e, bh_o = halves(bh)
    kh_e, kh_o = halves(kh)
    v_e, v_o = halves(v)
    al_n = rows(al_e, al_o)
    bh_n = rows(bh_e, bh_o)
    v_s = rows(v_o, v_e)
    kh_s = rows(kh_o, kh_e)
    rt_b = blk(rt)
    nu = range(len(units))

    g0 = [_mm(x, y, NT, pa=PG[0], pb=PG[1])
          for x, y in zip(rows(al_e, rt_e), rows(bt_e, kt_e))]
    g1 = [_mm(x, y, NT, pa=PG[0], pb=PG[1])
          for x, y in zip(rows(rt_o, al_o), rows(kt_o, bt_o))]
    top = ri < c
    left = ci < c
    tri_s = (ri & (c - 1)) > (ci & (c - 1))
    tri_i = (ri & (c - 1)) >= (ci & (c - 1))
    diag_q = top == left

    def pick(x0, x1, in_q0, tri):
        return [jnp.where(tri, jnp.where(in_q0, x, y), 0.0) for x, y in zip(x0, x1)]

    a_ab = pick(g0, g1, top, diag_q & tri_s)
    a_rk = pick(g1, g0, top, diag_q & tri_i)
    a_ak = pick(g0, g1, top, (~diag_q) & tri_s)
    a_rb = pick(g1, g0, top, (~diag_q) & tri_i)
    x1 = [_mm(a_ak[n], v_s[n], pa=PS[0], pb=PS[1]) for n in nu]
    akv = [_mm(a_rk[n], v_s[n], pa=PS[0], pb=PS[1]) for n in nu]
    khv = [_mm(kh_s[n], v_s[n], TN, pa=PS[0], pb=PS[1]) for n in nu]
    t_inv = _unit_lower_inverse(a_ab, ri, ci)
    tw = [_mm(t_inv[n], jnp.concatenate([al_n[n], x1[n]], axis=1), pa=PS[0], pb=PS[1])
          for n in nu]
    qo = [_mm(a_rb[n], tw[n], pa=PS[0], pb=PS[1]) for n in nu]
    mn = [_mm(bh_n[n], tw[n], TN, pa=PS[0], pb=PS[1]) for n in nu]
    q = [rt_b[n] + unstack(qo[n][:, :LANES]) for n in nu]
    o_loc = [unstack(qo[n][:, LANES:] + akv[n]) for n in nu]
    m_full = [mn[n][:, :LANES]
              + jnp.where(ri == ci, p_c[j][:, p * LANES:(p + 1) * LANES], 0.0)
              for n, (j, p) in enumerate(units)]
    n_loc = [mn[n][:, LANES:] + khv[n] for n in nu]

    h = [h_ref[p] for p in range(n_pairs)]
    o_rows = []
    for j in range(tt // c):
        o_parts = []
        for p in range(n_pairs):
            n = j * n_pairs + p
            o_parts.append(_mm(q[n], h[p], pa=PH[0], pb=PH[1]) + o_loc[n])
            h[p] = _mm(m_full[n], h[p], pa=PH[0], pb=PH[1]) + n_loc[n]
        o_rows.append(jnp.concatenate(o_parts, axis=1))
    for p in range(n_pairs):
        h_ref[p] = h[p]

    o = jnp.concatenate(o_rows, axis=0)
    mean = head_sum(o) * (1.0 / HEAD_DIM)
    d = o - mean
    var = head_sum(d * d) * (1.0 / HEAD_DIM)
    on = d * lax.rsqrt(var + GN_EPS) * ln_w + ln_b
    o_ref[...] = ((on + bonus) * g).astype(o_ref.dtype)


RWKV_TILE = 256


def _rwkv7(p_rw, mu, w0, w_up, a0, a_up, g_up, k_k, k_a, r_k, ln_w, ln_b):
    bsz, t, npj = p_rw.shape
    w = RWKV_WIDTH
    tt = RWKV_TILE
    pv = jnp.stack([w0, a0, k_k, k_a, r_k.reshape(w), ln_w, ln_b, jnp.zeros((w,), F32)])
    wlo = jnp.zeros((LANES, 2 * w), F32)
    wlo = wlo.at[:DECAY_LORA, :w].set(w_up).at[DECAY_LORA:, w:].set(a_up)
    const = lambda shape: pl.BlockSpec(shape, lambda b, i: (0, 0))
    return pl.pallas_call(
        _rwkv_kernel,
        grid=(bsz, t // tt),
        in_specs=[
            pl.BlockSpec((None, tt, npj), lambda b, i: (b, i, 0)),
            const((1, npj)), const((SUBLANES, w)), const((LANES, 2 * w)),
            const((GATE_LORA, w)),
        ],
        out_specs=pl.BlockSpec((None, tt, w), lambda b, i: (b, i, 0)),
        out_shape=jax.ShapeDtypeStruct((bsz, t, w), BF16),
        scratch_shapes=[pltpu.VMEM((SUBLANES, npj), F32),
                        pltpu.VMEM((w // LANES, LANES, LANES), F32)],
        compiler_params=_cparams(("arbitrary", "arbitrary")),
        name="rwkv7",
    )(p_rw, mu.reshape(1, npj), pv, wlo, g_up)


def _mix_kernel(x_ref, yl_ref, yr_ref, wo1_ref, wo2_ref, g1_ref, sh_ref, sc_ref, ng_ref,
                wr_ref, br_ref, x1_ref, h2_ref, rid_ref, rw_ref):
    y = (jnp.dot(yl_ref[...], wo1_ref[...], preferred_element_type=F32)
         + jnp.dot(yr_ref[...], wo2_ref[...], preferred_element_type=F32))
    x1 = x_ref[...] + g1_ref[...] * y
    x1_ref[...] = x1
    h2 = _adaln(x1, ng_ref[...], sh_ref[...], sc_ref[...]).astype(BF16)
    h2_ref[...] = _pack_bf16_halves(h2)
    logits = jnp.dot(jnp.concatenate([h2, h2], axis=1), wr_ref[...],
                     preferred_element_type=F32) + br_ref[...]

    tm = logits.shape[0]
    nblk = 2
    rows = tm // nblk
    neg = -jnp.inf
    lane = lax.broadcasted_iota(I32, (rows, LANES), 1)
    lane_f = lane.astype(F32)
    lg_all = [logits[i * rows:(i + 1) * rows, :] for i in range(nblk)]

    def first_argmax(xs):
        m = [jnp.max(x, axis=-1, keepdims=True) for x in xs]
        idx = [jnp.min(jnp.where(x == mi, lane_f, float(LANES)), axis=-1, keepdims=True)
               for x, mi in zip(xs, m)]
        return m, [i.astype(I32) for i in idx]

    lg = [jnp.where(lane < N_GROUPS, x, neg) for x in lg_all]
    gm, g_idx = first_argmax(lg)
    g_w = [1.0 / jnp.sum(jnp.exp(x - m), axis=-1, keepdims=True)
           for x, m in zip(lg, gm)]
    lo = [N_GROUPS + EXPERTS_PER_GROUP * g for g in g_idx]
    le = [jnp.where((lane >= l) & (lane < l + EXPERTS_PER_GROUP), x, neg)
          for x, l in zip(lg_all, lo)]
    v1, i1 = first_argmax(le)
    v2, i2 = first_argmax([jnp.where(lane == i, neg, x) for x, i in zip(le, i1)])
    for blk in range(nblk):
        e2 = jnp.exp(v2[blk] - v1[blk])
        w1 = g_w[blk] / (1.0 + e2)
        w2 = g_w[blk] * e2 / (1.0 + e2)
        sl = slice(blk * rows, (blk + 1) * rows)
        rid_ref[sl, :] = jnp.where(lane == 0, i1[blk] - N_GROUPS,
                                   jnp.where(lane == 1, i2[blk] - N_GROUPS, 0))
        rw_ref[sl, :] = jnp.where(lane == 0, w1, jnp.where(lane == 1, w2, 0.0))


def _mix(x, y_lru, y_rw, w_out, g1, sh2, sc2, ng, w_grp, b_grp, w_exp, b_exp, tm=256):
    bsz, t, d = x.shape
    wo = w_out.astype(BF16)
    wl = LRU_WIDTH
    wr = jnp.zeros((d, LANES), F32).at[:, :N_GROUPS].set(w_grp)
    wr = wr.at[:, N_GROUPS:N_GROUPS + N_EXPERTS].set(w_exp)
    wr = jnp.concatenate(_split(wr, 2), axis=0)
    br = jnp.zeros((1, LANES), F32).at[0, :N_GROUPS].set(b_grp)
    br = br.at[0, N_GROUPS:N_GROUPS + N_EXPERTS].set(b_exp)
    vec = pl.BlockSpec((None, 1, d), lambda b, i: (b, 0, 0))
    const = lambda shape: pl.BlockSpec(shape, lambda b, i: (0, 0))
    row = lambda n: pl.BlockSpec((None, tm, n), lambda b, i: (b, i, 0))
    return pl.pallas_call(
        _mix_kernel,
        grid=(bsz, t // tm),
        in_specs=[row(d), row(wl), row(d - wl), const((wl, d)), const((d - wl, d)),
                  vec, vec, vec, const((1, d)), const((2 * d, LANES)), const((1, LANES))],
        out_specs=[row(d), row(d // 2), row(LANES), row(LANES)],
        out_shape=[jax.ShapeDtypeStruct((bsz, t, d), F32),
                   jax.ShapeDtypeStruct((bsz, t, d // 2), I32),
                   jax.ShapeDtypeStruct((bsz, t, LANES), I32),
                   jax.ShapeDtypeStruct((bsz, t, LANES), F32)],
        compiler_params=_cparams(("arbitrary", "arbitrary")),
        name="outproj_adaln2_router",
    )(x, y_lru, y_rw, wo[:wl], wo[wl:], g1, sh2, sc2, ng.reshape(1, d), wr, br)


def _plan_kernel(rid_ref, tril_ref, pos_ref, te_ref, cnt_ref, base_ref):
    ph = pl.program_id(0)
    i = pl.program_id(1)
    tp = rid_ref.shape[0]
    lane = lax.broadcasted_iota(I32, (tp, LANES), 1)
    rid = rid_ref[...]
    id0 = rid[:, 0:1]
    id1 = rid[:, 1:2]
    oh0 = lane == id0
    oh1 = lane == id1
    oh = jnp.where(oh0 | oh1, 1.0, 0.0)
    colsum = jnp.sum(oh, axis=0, keepdims=True).astype(I32)

    @pl.when((ph == 0) & (i == 0))
    def _():
        cnt_ref[...] = jnp.zeros_like(cnt_ref)

    @pl.when(ph == 0)
    def _():
        cnt_ref[...] = cnt_ref[...] + colsum

    @pl.when((ph == 1) & (i == 0))
    def _():
        cnt = cnt_ref[...]
        shift = EXPERT_TILE.bit_length() - 1
        padded = ((cnt + (EXPERT_TILE - 1)) >> shift) << shift
        l8 = lax.broadcasted_iota(I32, (SUBLANES, LANES), 1)
        end = padded
        s = 1
        while s < LANES:
            end = end + jnp.where(l8 >= s, pltpu.roll(end, s, 1), 0)
            s *= 2
        base_ref[...] = end - padded
        nt = te_ref.shape[0]
        j = lax.broadcasted_iota(I32, (nt, LANES), 0) * EXPERT_TILE
        lt = lax.broadcasted_iota(I32, (nt, LANES), 1)
        done = jnp.where((lt < N_EXPERTS) & (end[0:1, :] <= j), 1.0, 0.0)
        e_of = jnp.minimum(jnp.sum(done, axis=-1, keepdims=True), float(N_EXPERTS - 1)).astype(I32)
        total = end[0:1, N_EXPERTS - 1:N_EXPERTS]
        mine = lt == e_of
        cnt_e = jnp.sum(jnp.where(mine, cnt[0:1, :], 0), axis=-1, keepdims=True)
        start_e = jnp.sum(jnp.where(mine, end[0:1, :] - padded[0:1, :], 0), axis=-1, keepdims=True)
        valid = jnp.clip(cnt_e - (j[:, 0:1] - start_e), 0, EXPERT_TILE)
        te_ref[...] = jnp.where(lt == 0, e_of, jnp.where(lt == 1, total >> shift,
                                                        jnp.where(lt == 2, valid, 0)))

    @pl.when(ph == 1)
    def _():
        prefix = jnp.dot(tril_ref[...], oh.astype(BF16), preferred_element_type=F32)
        dest = base_ref[0:1, :] + prefix.astype(I32)
        pos0 = jnp.sum(jnp.where(oh0, dest, 0), axis=-1, keepdims=True)
        pos1 = jnp.sum(jnp.where(oh1, dest, 0), axis=-1, keepdims=True)
        pos_ref[...] = jnp.where(lane == 0, pos0, jnp.where(lane == 1, pos1, 0))
        base_ref[...] = base_ref[...] + colsum


def _plan(rid, n_tiles, tp=512):
    n = rid.shape[0]
    tp = min(tp, n)
    nt_pad = -(-n_tiles // SUBLANES) * SUBLANES
    tril = (jnp.arange(tp)[:, None] > jnp.arange(tp)[None, :]).astype(BF16)
    pos, te = pl.pallas_call(
        _plan_kernel,
        grid=(2, n // tp),
        in_specs=[pl.BlockSpec((tp, LANES), lambda ph, i: (i, 0)),
                  pl.BlockSpec((tp, tp), lambda ph, i: (0, 0))],
        out_specs=[pl.BlockSpec((tp, LANES), lambda ph, i: (i * ph, 0)),
                   pl.BlockSpec((nt_pad, LANES), lambda ph, i: (0, 0))],
        out_shape=[jax.ShapeDtypeStruct((n, LANES), I32),
                   jax.ShapeDtypeStruct((nt_pad, LANES), I32)],
        scratch_shapes=[pltpu.VMEM((SUBLANES, LANES), I32), pltpu.VMEM((SUBLANES, LANES), I32)],
        compiler_params=_cparams(("arbitrary", "arbitrary")),
        name="route_plan",
    )(rid, tril)
    return pos, te


SC_CORES = 2
SC_SUBCORES = 16
SC_WORKERS = SC_CORES * SC_SUBCORES
SC_WINDOW = 64


def _sc_mesh():
    return plsc.VectorSubcoreMesh(core_axis_name="c", subcore_axis_name="s",
                                  num_cores=SC_CORES, num_subcores=SC_SUBCORES)


def _sc_two_buffer_loop(n_win, fill, drain):
    assert n_win % 2 == 0

    def start(copies):
        for cp in copies:
            cp.start()

    def wait(copies):
        for cp in copies:
            cp.wait()

    start(fill(0, 0))

    @pl.loop(0, n_win, step=2)
    def _(j):
        for b in range(2):
            jj = j + b

            @pl.when(jj >= 1)
            def _():
                wait(drain(jj - 1, 1 - b))

            @pl.when(jj + 1 < n_win)
            def _():
                start(fill(jj + 1, 1 - b))

            wait(fill(jj, b))
            start(drain(jj, b))

    wait(drain(n_win - 1, 1))


def _sc_scatter_rows(rows, pos0, pos1, n_out):
    n, d = rows.shape
    per_w = n // SC_WORKERS
    n_win = per_w // SC_WINDOW
    shape3 = (SC_WORKERS, n_win, SC_WINDOW)

    @functools.partial(
        pl.kernel, mesh=_sc_mesh(), out_type=jax.ShapeDtypeStruct((n_out, d), rows.dtype),
        scratch_types=[pltpu.VMEM((n_win, SC_WINDOW), I32), pltpu.VMEM((n_win, SC_WINDOW), I32),
                       pltpu.VMEM((2, SC_WINDOW, d), rows.dtype),
                       pltpu.SemaphoreType.DMA((2,)), pltpu.SemaphoreType.DMA((2,))],
        name="sc_scatter_rows")
    def scatter(rows_hbm, p0_hbm, p1_hbm, out_hbm, p0_v, p1_v, buf, in_sem, out_sem):
        wid = lax.axis_index("s") * SC_CORES + lax.axis_index("c")
        base = wid * per_w
        pltpu.sync_copy(p0_hbm.at[wid], p0_v)
        pltpu.sync_copy(p1_hbm.at[wid], p1_v)

        def fill(j, b):
            src = rows_hbm.at[pl.ds(base + j * SC_WINDOW, SC_WINDOW)]
            return [pltpu.make_async_copy(src, buf.at[b], in_sem.at[b])]

        def drain(j, b):
            return [pltpu.make_async_copy(buf.at[b], out_hbm.at[p.at[j]], out_sem.at[b])
                    for p in (p0_v, p1_v)]

        _sc_two_buffer_loop(n_win, fill, drain)

    return scatter(rows, pos0.reshape(shape3), pos1.reshape(shape3))


def _sc_gather_rows(table, idx):
    m = idx.shape[0]
    d = table.shape[1]
    per_w = m // SC_WORKERS
    n_win = per_w // SC_WINDOW

    @functools.partial(
        pl.kernel, mesh=_sc_mesh(), out_type=jax.ShapeDtypeStruct((m, d), table.dtype),
        scratch_types=[pltpu.VMEM((n_win, SC_WINDOW), I32),
                       pltpu.VMEM((2, SC_WINDOW, d), table.dtype),
                       pltpu.SemaphoreType.DMA((2,)), pltpu.SemaphoreType.DMA((2,))],
        name="sc_gather_rows")
    def gather(table_hbm, idx_hbm, out_hbm, idx_v, buf, in_sem, out_sem):
        wid = lax.axis_index("s") * SC_CORES + lax.axis_index("c")
        base = wid * per_w
        pltpu.sync_copy(idx_hbm.at[wid], idx_v)

        def fill(j, b):
            return [pltpu.make_async_copy(table_hbm.at[idx_v.at[j]], buf.at[b], in_sem.at[b])]

        def drain(j, b):
            dst = out_hbm.at[pl.ds(base + j * SC_WINDOW, SC_WINDOW)]
            return [pltpu.make_async_copy(buf.at[b], dst, out_sem.at[b])]

        _sc_two_buffer_loop(n_win, fill, drain)

    return gather(table, idx.reshape(SC_WORKERS, n_win, SC_WINDOW))


def _expert_kernel(te_ref, nu_ref, valid_ref, x_ref, w1_ref, w3_ref, w2_ref, o_ref,
                   w1b_ref, w3b_ref, w2b_ref):
    j = pl.program_id(0)
    active = j < nu_ref[0]

    @pl.when(active & ((j == 0) | (te_ref[j] != te_ref[jnp.maximum(j - 1, 0)])))
    def _():
        w1b_ref[...] = w1_ref[...].astype(BF16)
        w3b_ref[...] = w3_ref[...].astype(BF16)
        w2b_ref[...] = w2_ref[...].astype(BF16)

    @pl.when(active)
    def _():
        row = lax.broadcasted_iota(I32, x_ref.shape, 0)
        x = _unpack_bf16_halves(jnp.where(row < valid_ref[j], x_ref[...], 0)).astype(BF16)
        h1 = jnp.dot(x, w1b_ref[...], preferred_element_type=F32)
        h3 = jnp.dot(x, w3b_ref[...], preferred_element_type=F32)
        hid = (h1 * jax.nn.sigmoid(h1) * h3).astype(BF16)
        o_ref[...] = _pack_bf16_halves(jnp.dot(hid, w2b_ref[...], preferred_element_type=F32))

    @pl.when(j >= nu_ref[0])
    def _():
        o_ref[...] = jnp.zeros_like(o_ref)


def _experts(xs, te, nu, valid, w1, w3, w2):
    n_rows = xs.shape[0]
    d, de = w1.shape[-2:]
    nt = n_rows // EXPERT_TILE
    used = lambda j, nu: jnp.minimum(j, nu[0] - 1)
    grid_spec = pltpu.PrefetchScalarGridSpec(
        num_scalar_prefetch=3,
        grid=(nt,),
        in_specs=[
            pl.BlockSpec((EXPERT_TILE, d // 2), lambda j, te, nu, va: (used(j, nu), 0)),
            pl.BlockSpec((None, d, de), lambda j, te, nu, va: (te[used(j, nu)], 0, 0)),
            pl.BlockSpec((None, d, de), lambda j, te, nu, va: (te[used(j, nu)], 0, 0)),
            pl.BlockSpec((None, de, d), lambda j, te, nu, va: (te[used(j, nu)], 0, 0)),
        ],
        out_specs=pl.BlockSpec((EXPERT_TILE, d // 2), lambda j, te, nu, va: (j, 0)),
        scratch_shapes=[pltpu.VMEM((d, de), BF16), pltpu.VMEM((d, de), BF16),
                        pltpu.VMEM((de, d), BF16)],
    )
    return pl.pallas_call(
        _expert_kernel,
        grid_spec=grid_spec,
        out_shape=jax.ShapeDtypeStruct((n_rows, d // 2), I32),
        compiler_params=_cparams(("arbitrary",)),
        name="expert_mlp",
    )(te, nu, valid, xs, w1, w3, w2)


def _combine_kernel(x1_ref, rw_ref, g2_ref, fg_ref, y0_ref, y1_ref, o_ref):
    rw = rw_ref[...]
    moe = (rw[:, 0:1] * _unpack_bf16_halves(y0_ref[...])
           + rw[:, 1:2] * _unpack_bf16_halves(y1_ref[...]))
    x2 = x1_ref[...] + g2_ref[...] * moe
    ms = jnp.mean(x2 * x2, axis=-1, keepdims=True)
    o_ref[...] = x2 * lax.rsqrt(ms + RMS_EPS) * fg_ref[...]


def _combine(x1, rw, yg, g2, final_g, tc):
    bsz, t, d = x1.shape
    row = lambda n: pl.BlockSpec((None, tc, n), lambda b, i: (b, i, 0))
    slot = lambda s: pl.BlockSpec((None, None, tc, d // 2), lambda b, i: (s, b, i, 0))
    return pl.pallas_call(
        _combine_kernel,
        grid=(bsz, t // tc),
        in_specs=[row(d), row(LANES),
                  pl.BlockSpec((None, 1, d), lambda b, i: (b, 0, 0)),
                  pl.BlockSpec((1, d), lambda b, i: (0, 0)),
                  slot(0), slot(1)],
        out_specs=row(d),
        out_shape=jax.ShapeDtypeStruct((bsz, t, d), F32),
        compiler_params=_cparams(("arbitrary", "arbitrary")),
        name="combine_final_norm",
    )(x1, rw, g2, final_g.reshape(1, d), yg, yg)


def _row_tile(t, want):
    return want if t % want == 0 else t


def kernel(x, c, w_ada, b_ada, norm1_g, w_in, conv_w, conv_b, lru_wa, lru_ba, lru_wi, lru_bi, lru_lam, lru_norm_g, tok_mu, w0, w_up, a0, a_up, g_up, k_k, k_a, r_k, ln_x_w, ln_x_b, w_out, norm2_g, w_grp, b_grp, w_exp, b_exp, w1, w3, w2, final_g):
    bsz, t, d = x.shape
    n = bsz * t
    depth = w_ada.shape[0]
    assert depth == 1, "the combine kernel applies the final norm: only DEPTH == 1 is wired"
    tile = _row_tile(t, 256)
    n_tiles = (n * TOP_K) // EXPERT_TILE + N_EXPERTS
    for l in range(depth):
        mod = _modulation(c, w_ada[l], b_ada[l]).reshape(bsz, 6, 1, d)
        sh1, sc1, g1, sh2, sc2, g2 = (mod[:, i] for i in range(6))
        y_lru, p_rw = _inproj_lru(x, sh1, sc1, norm1_g[l], w_in[l], conv_w[l], conv_b[l],
                                  lru_wa[l], lru_ba[l], lru_wi[l], lru_bi[l], lru_lam[l],
                                  lru_norm_g[l], tm=tile)
        y_rw = _rwkv7(p_rw, tok_mu[l], w0[l], w_up[l], a0[l], a_up[l], g_up[l], k_k[l], k_a[l],
                      r_k[l], ln_x_w[l], ln_x_b[l])
        x1, h2, rid, rw = _mix(x, y_lru, y_rw, w_out[l], g1, sh2, sc2, norm2_g[l],
                               w_grp[l], b_grp[l], w_exp[l], b_exp[l], tm=tile)
        pos, te = _plan(rid.reshape(n, LANES), n_tiles)
        pos0, pos1 = pos[:, 0], pos[:, 1]
        xs = _sc_scatter_rows(h2.reshape(n, d // 2), pos0, pos1, n_tiles * EXPERT_TILE)
        ys = _experts(xs, te[:n_tiles, 0], te[0:1, 1], te[:n_tiles, 2], w1[l], w3[l], w2[l])
        yg = _sc_gather_rows(ys, jnp.concatenate([pos0, pos1]))
        x = _combine(x1, rw, yg.reshape(TOP_K, bsz, t, d // 2), g2, final_g, tile)
    return x
```

```python
import functools

import jax
import jax.numpy as jnp
from jax import lax
from jax.experimental import pallas as pl
from jax.experimental.pallas import tpu as pltpu
from jax.experimental.pallas import tpu_sc as plsc

F32 = jnp.float32
BF16 = jnp.bfloat16
I32 = jnp.int32

LRU_WIDTH = 512
LRU_HEAD_DIM = 64
CONV_WIDTH = 4
LRU_C = 8.0
RWKV_WIDTH = 512
HEAD_DIM = 64
DECAY_LORA = 64
AAA_LORA = 64
GATE_LORA = 128
RWKV_PROJ = 3 * RWKV_WIDTH + DECAY_LORA + AAA_LORA + GATE_LORA
N_GROUPS = 4
EXPERTS_PER_GROUP = 8
N_EXPERTS = N_GROUPS * EXPERTS_PER_GROUP
TOP_K = 2
RMS_EPS = 1e-6
GN_EPS = 64e-5

LANES = 128
SUBLANES = 8
CHUNK = 64
EXPERT_TILE = 512
ROW_TILE = 256
COMBINE_TILE = 512
VMEM_LIMIT = 48 * 1024 * 1024

NN = (((1,), (0,)), ((), ()))
NT = (((1,), (1,)), ((), ()))
TN = (((0,), (0,)), ((), ()))


def _split(x, n):
    if x.dtype == BF16:
        return [x]
    parts = []
    rem = x
    for i in range(n):
        p = rem.astype(BF16)
        parts.append(p)
        if i + 1 < n:
            rem = rem - p.astype(F32)
    return parts


def _mm(a, b, dn=NN, pa=1, pb=1):
    aps = _split(a, pa)
    bps = _split(b, pb)
    order = max(len(aps), len(bps))
    terms = [(i, j) for i in range(len(aps)) for j in range(len(bps)) if i + j < order]
    ka = dn[0][0][0]
    kb = dn[0][1][0]
    if len(terms) > 1 and a.shape[ka] % LANES == 0:
        a_cat = jnp.concatenate([aps[i] for i, _ in terms], axis=ka)
        b_cat = jnp.concatenate([bps[j] for _, j in terms], axis=kb)
        return lax.dot_general(a_cat, b_cat, dn, preferred_element_type=F32)
    out = None
    for i, j in terms:
        t = lax.dot_general(aps[i], bps[j], dn, preferred_element_type=F32)
        out = t if out is None else out + t
    return out


def _pack_bf16_halves(x):
    n = x.shape[1] // 2
    bits = lax.bitcast_convert_type(x.astype(BF16).astype(F32), I32)
    return bits[:, n:] | ((bits[:, :n] >> 16) & 0xFFFF)


def _unpack_bf16_halves(p):
    lo = lax.bitcast_convert_type(p << 16, F32)
    hi = lax.bitcast_convert_type(p & (-65536), F32)
    return jnp.concatenate([lo, hi], axis=1)


def _softplus(x):
    return jnp.maximum(x, 0.0) + jnp.log1p(jnp.exp(-jnp.abs(x)))


def _cparams(sem):
    return pltpu.CompilerParams(dimension_semantics=sem, vmem_limit_bytes=VMEM_LIMIT)


def _mod_kernel(c_ref, w_ref, b_ref, o_ref):
    c = c_ref[...]
    s = c * jax.nn.sigmoid(c)
    o_ref[...] = _mm(s, w_ref[...], pa=2, pb=2) + b_ref[...]


def _modulation(c, w_ada, b_ada):
    bsz, d = c.shape
    n_out = w_ada.shape[1]
    rows = -(-bsz // SUBLANES) * SUBLANES
    c_pad = jnp.zeros((rows, d), F32).at[:bsz].set(c)
    bn = d
    out = pl.pallas_call(
        _mod_kernel,
        grid=(n_out // bn,),
        in_specs=[
            pl.BlockSpec((rows, d), lambda j: (0, 0)),
            pl.BlockSpec((d, bn), lambda j: (0, j)),
            pl.BlockSpec((1, bn), lambda j: (0, j)),
        ],
        out_specs=pl.BlockSpec((rows, bn), lambda j: (0, j)),
        out_shape=jax.ShapeDtypeStruct((rows, n_out), F32),
        compiler_params=_cparams(("arbitrary",)),
        name="adaln_mod",
    )(c_pad, w_ada, b_ada.reshape(1, n_out))
    return out[:bsz]


def _adaln(x, g, shift, scale):
    ms = jnp.mean(x * x, axis=-1, keepdims=True)
    y = x * lax.rsqrt(ms + RMS_EPS) * g
    return y * (1.0 + scale) + shift


def _inproj_lru_kernel(x_ref, sh_ref, sc_ref, g_ref, wl_ref, wr_ref,
                       cw_ref, cb_ref, wab_ref, bab_ref, lam_ref, ng_ref, ones_ref,
                       yl_ref, pr_ref, xprev_ref, hprev_ref):
    @pl.when(pl.program_id(1) == 0)
    def _():
        xprev_ref[...] = jnp.zeros_like(xprev_ref)
        hprev_ref[...] = jnp.zeros_like(hprev_ref)

    w = LRU_WIDTH
    h = _adaln(x_ref[...], g_ref[...], sh_ref[...], sc_ref[...]).astype(BF16)
    ux = jnp.dot(h, wl_ref[:, :w], preferred_element_type=F32)

    def remaining_columns():
        pr_ref[...] = jnp.dot(h, wr_ref[...], preferred_element_type=F32)
        return jnp.dot(h, wl_ref[:, w:], preferred_element_type=F32)

    y = _lru_tile(ux, remaining_columns, cw_ref, cb_ref, wab_ref, bab_ref, lam_ref, ng_ref,
                  ones_ref, xprev_ref, hprev_ref)
    yl_ref[...] = y.astype(yl_ref.dtype)


def _inproj_lru(x, sh1, sc1, g, w_in, conv_w, conv_b, wa, ba, wi, bi, lam, norm_g, tm=256):
    bsz, t, d = x.shape
    w = LRU_WIDTH
    nl = 2 * w
    nr = RWKV_PROJ
    wl = w_in[:, :nl].astype(BF16)
    wr = w_in[:, nl:].astype(BF16)
    wab = jnp.concatenate([_block_diag(wa), _block_diag(wi)], axis=1).astype(BF16)
    bab = jnp.concatenate([ba.reshape(1, w), bi.reshape(1, w)], axis=1)
    vec = pl.BlockSpec((None, 1, d), lambda b, i: (b, 0, 0))
    const = lambda shape: pl.BlockSpec(shape, lambda b, i: (0, 0))
    return pl.pallas_call(
        _inproj_lru_kernel,
        grid=(bsz, t // tm),
        in_specs=[
            pl.BlockSpec((None, tm, d), lambda b, i: (b, i, 0)),
            vec, vec, const((1, d)), const((d, nl)), const((d, nr)),
            const((CONV_WIDTH, w)), const((1, w)), const((w, 2 * w)), const((1, 2 * w)),
            const((1, w)), const((1, w)), const((w, w)),
        ],
        out_specs=[
            pl.BlockSpec((None, tm, w), lambda b, i: (b, i, 0)),
            pl.BlockSpec((None, tm, nr), lambda b, i: (b, i, 0)),
        ],
        out_shape=[
            jax.ShapeDtypeStruct((bsz, t, w), BF16),
            jax.ShapeDtypeStruct((bsz, t, nr), F32),
        ],
        scratch_shapes=[pltpu.VMEM((SUBLANES, w), F32), pltpu.VMEM((SUBLANES, w), F32)],
        compiler_params=_cparams(("arbitrary", "arbitrary")),
        name="adaln1_inproj_rglru",
    )(x, sh1, sc1, g.reshape(1, d), wl, wr, conv_w, conv_b.reshape(1, w), wab, bab,
      lam.reshape(1, w), norm_g.reshape(1, w), _head_ones(w, LRU_HEAD_DIM))


def _gelu_tanh(x):
    c = 0.7978845608028654
    return x * (0.5 * (1.0 + jnp.tanh(c * (x + 0.044715 * (x * x * x)))))


def _sigmoid(x):
    return 0.5 * jnp.tanh(0.5 * x) + 0.5


def _lru_tile(ux, gate_branch, cw_ref, cb_ref, wab_ref, bab_ref, lam_ref, ng_ref, ones_ref,
              xprev_ref, hprev_ref):
    w = LRU_WIDTH
    tt = ux.shape[0]
    ext = jnp.concatenate([xprev_ref[...], ux], axis=0)
    xc = cb_ref[...] + cw_ref[CONV_WIDTH - 1:CONV_WIDTH, :] * ux
    for k in range(1, CONV_WIDTH):
        shifted = pltpu.roll(ext, k, 0)[SUBLANES:, :]
        xc = xc + cw_ref[CONV_WIDTH - 1 - k:CONV_WIDTH - k, :] * shifted
    xprev_ref[...] = ux[tt - SUBLANES:, :]

    gates = jnp.dot(xc.astype(BF16), wab_ref[...], preferred_element_type=F32) + bab_ref[...]
    ug = gate_branch()
    r = _sigmoid(gates[:, :w])
    ig = _sigmoid(gates[:, w:])
    log_a = (-LRU_C) * r * _softplus(-lam_ref[...])
    a = jnp.exp(log_a)
    th = jnp.tanh(log_a)
    q = (-2.0 * th) / (1.0 - th)
    root_q = jnp.where(q > 0.0, q * lax.rsqrt(q), 0.0)
    b = root_q * (ig * xc)

    row8 = lax.broadcasted_iota(I32, (tt, w), 0) & (SUBLANES - 1)
    acc_a, acc_b = a, b
    for s in (1, 2, 4):
        sh_a = pltpu.roll(acc_a, s, 0)
        sh_b = pltpu.roll(acc_b, s, 0)
        live = row8 >= s
        acc_b = jnp.where(live, acc_a * sh_b + acc_b, acc_b)
        acc_a = jnp.where(live, acc_a * sh_a, acc_a)
    carry = hprev_ref[SUBLANES - 1:SUBLANES, :]
    groups = []
    for gi in range(tt // SUBLANES):
        lo = gi * SUBLANES
        groups.append(acc_a[lo:lo + SUBLANES, :] * carry + acc_b[lo:lo + SUBLANES, :])
        last = lo + SUBLANES - 1
        carry = acc_a[last:last + 1, :] * carry + acc_b[last:last + 1, :]
    h = jnp.concatenate(groups, axis=0)
    hprev_ref[...] = groups[-1]

    y = h * _gelu_tanh(ug)
    ms = _mm(y * y, ones_ref[...]) * (1.0 / LRU_HEAD_DIM)
    return y * lax.rsqrt(ms + RMS_EPS) * ng_ref[...]


def _block_diag(w):
    h, n, _ = w.shape
    eye = jnp.eye(h, dtype=w.dtype)
    return (eye[:, None, :, None] * w[:, :, None, :]).reshape(h * n, h * n)


def _head_ones(width, head):
    idx = jnp.arange(width) // head
    return (idx[:, None] == idx[None, :]).astype(BF16)


PG = (1, 1)
PI = (1, 1)
PS = (1, 1)
PH = (1, 2)


def _unit_lower_inverse(a_list, ri, ci):
    mm = functools.partial(_mm, pa=PI[0], pb=PI[1])
    eye = jnp.where(ri == ci, 1.0, 0.0)
    leaf = (ri >> 3) == (ci >> 3)
    a8 = [jnp.where(leaf, a, 0.0) for a in a_list]
    a8_2 = [mm(x, x) for x in a8]
    a8_4 = [mm(x, x) for x in a8_2]
    t = [mm(eye + x, eye + y) for x, y in zip(a8, a8_2)]
    t = [mm(x, eye + y) for x, y in zip(t, a8_4)]
    zero = jnp.zeros((LANES, LANES), F32)
    for sh in (3, 4, 5):
        s = 1 << sh
        off = ((ri >> (sh + 1)) == (ci >> (sh + 1))) & ((ri >> sh) != (ci >> sh))
        t_lo = [_second_blocks(x, s) for x in t]
        b_lo = [mm(_second_blocks(jnp.where(off, a, 0.0), s), x) for a, x in zip(a_list, t)]
        d_lo = [mm(x, _interleave_blocks(zero, y, s)) for x, y in zip(t_lo, b_lo)]
        t = [_interleave_blocks(x, y + z, s) for x, y, z in zip(t, t_lo, d_lo)]
    return t


def _second_blocks(x, s):
    return jnp.concatenate(
        [x[s * (2 * m + 1):s * (2 * m + 2)] for m in range(x.shape[0] // (2 * s))], axis=0)


def _interleave_blocks(first_src, second, s):
    parts = []
    for m in range(first_src.shape[0] // (2 * s)):
        parts.append(first_src[2 * s * m:2 * s * m + s])
        parts.append(second[s * m:s * (m + 1)])
    return jnp.concatenate(parts, axis=0)


def _rwkv_kernel(p_ref, mu_ref, pv_ref, wlo_ref, gup_ref, o_ref, uprev_ref, h_ref):
    w = RWKV_WIDTH
    tt = p_ref.shape[0]
    c = CHUNK
    n_pairs = w // LANES
    units = [(j, p) for j in range(tt // c) for p in range(n_pairs)]

    @pl.when(pl.program_id(1) == 0)
    def _():
        uprev_ref[...] = jnp.zeros_like(uprev_ref)
        h_ref[...] = jnp.zeros_like(h_ref)

    u = p_ref[...]
    ext = jnp.concatenate([uprev_ref[...], u], axis=0)
    prev = pltpu.roll(ext, 1, 0)[SUBLANES:, :]
    uprev_ref[...] = u[tt - SUBLANES:, :]
    um = u + (prev - u) * mu_ref[...]

    r = um[:, 0:w]
    k = um[:, w:2 * w]
    v = um[:, 2 * w:3 * w]
    z = um[:, 3 * w:3 * w + LANES]
    gd = um[:, 3 * w + LANES:]
    w0 = pv_ref[0:1, :]
    a0 = pv_ref[1:2, :]
    k_k = pv_ref[2:3, :]
    k_a = pv_ref[3:4, :]
    r_k = pv_ref[4:5, :]
    ln_w = pv_ref[5:6, :]
    ln_b = pv_ref[6:7, :]

    lane_t = lax.broadcasted_iota(I32, (tt, LANES), 1)
    zz = jnp.where(lane_t < HEAD_DIM, jnp.tanh(z), z)
    lora = _mm(zz, wlo_ref[...], pa=2, pb=2)
    wlog = -_softplus(-(w0 + lora[:, :w])) - 0.5
    ld = -jnp.exp(wlog)
    a = jax.nn.sigmoid(a0 + lora[:, w:])
    g = _mm(jax.nn.sigmoid(gd), gup_ref[...], pa=2, pb=2)

    ri = lax.broadcasted_iota(I32, (LANES, LANES), 0)
    ci = lax.broadcasted_iota(I32, (LANES, LANES), 1)
    even = lax.broadcasted_iota(I32, (c, LANES), 1) < HEAD_DIM
    wide = 2 * LANES
    ones_bd = jnp.where(
        (lax.broadcasted_iota(I32, (wide, wide), 0) >> 6)
        == (lax.broadcasted_iota(I32, (wide, wide), 1) >> 6), 1.0, 0.0).astype(BF16)

    def head_sum(x):
        return jnp.concatenate(
            [_mm(x[:, q * wide:(q + 1) * wide], ones_bd) for q in range(w // wide)], axis=1)

    kk = k * k_k
    kk = kk / jnp.maximum(jnp.sqrt(head_sum(kk * kk)), 1e-12)
    kp = k * (1.0 + (a - 1.0) * k_a)
    kka = kk * a
    bonus = head_sum(r * kp * r_k) * v
    row_in_chunk = lax.broadcasted_iota(I32, (tt, w), 0) & (c - 1)
    cum = ld
    s = 1
    while s < c:
        cum = cum + jnp.where(row_in_chunk >= s, pltpu.roll(cum, s, 0), 0.0)
        s *= 2
    cum_c = [cum[j * c + c - 1:j * c + c, :] for j in range(tt // c)]
    p_c = [jnp.exp(x) for x in cum_c]
    p_inv = jnp.exp(-cum)
    p_hat = jnp.exp(jnp.concatenate(
        [cum_c[j] - cum[j * c:(j + 1) * c, :] for j in range(tt // c)], axis=0))
    al = -kk * jnp.exp(cum - ld)
    rt = r * jnp.exp(cum)
    bt = kka * p_inv
    kt = kp * p_inv
    bh = kka * p_hat
    kh = kp * p_hat

    def blk(x):
        return [x[j * c:(j + 1) * c, p * LANES:(p + 1) * LANES] for j, p in units]

    def halves(x):
        xs = blk(x)
        return [jnp.where(even, y, 0.0) for y in xs], [jnp.where(even, 0.0, y) for y in xs]

    def rows(top, bot):
        return [jnp.concatenate([x, y], axis=0) for x, y in zip(top, bot)]

    def unstack(x):
        return x[:c, :] + x[c:, :]

    al_e, al_o = halves(al)
    rt_e, rt_o = halves(rt)
    bt_e, bt_o = halves(bt)
    kt_e, kt_o = halves(kt)
    bh_e, bh_o = halves(bh)
    kh_e, kh_o = halves(kh)
    v_e, v_o = halves(v)
    al_n = rows(al_e, al_o)
    bh_n = rows(bh_e, bh_o)
    v_s = rows(v_o, v_e)
    kh_s = rows(kh_o, kh_e)
    rt_b = blk(rt)
    nu = range(len(units))

    g0 = [_mm(x, y, NT, pa=PG[0], pb=PG[1])
          for x, y in zip(rows(al_e, rt_e), rows(bt_e, kt_e))]
    g1 = [_mm(x, y, NT, pa=PG[0], pb=PG[1])
          for x, y in zip(rows(rt_o, al_o), rows(kt_o, bt_o))]
    top = ri < c
    left = ci < c
    tri_s = (ri & (c - 1)) > (ci & (c - 1))
    tri_i = (ri & (c - 1)) >= (ci & (c - 1))
    diag_q = top == left

    def pick(x0, x1, in_q0, tri):
        return [jnp.where(tri, jnp.where(in_q0, x, y), 0.0) for x, y in zip(x0, x1)]

    a_ab = pick(g0, g1, top, diag_q & tri_s)
    a_rk = pick(g1, g0, top, diag_q & tri_i)
    a_ak = pick(g0, g1, top, (~diag_q) & tri_s)
    a_rb = pick(g1, g0, top, (~diag_q) & tri_i)
    x1 = [_mm(a_ak[n], v_s[n], pa=PS[0], pb=PS[1]) for n in nu]
    akv = [_mm(a_rk[n], v_s[n], pa=PS[0], pb=PS[1]) for n in nu]
    khv = [_mm(kh_s[n], v_s[n], TN, pa=PS[0], pb=PS[1]) for n in nu]
    t_inv = _unit_lower_inverse(a_ab, ri, ci)
    tw = [_mm(t_inv[n], jnp.concatenate([al_n[n], x1[n]], axis=1), pa=PS[0], pb=PS[1])
          for n in nu]
    qo = [_mm(a_rb[n], tw[n], pa=PS[0], pb=PS[1]) for n in nu]
    mn = [_mm(bh_n[n], tw[n], TN, pa=PS[0], pb=PS[1]) for n in nu]
    q = [rt_b[n] + unstack(qo[n][:, :LANES]) for n in nu]
    o_loc = [unstack(qo[n][:, LANES:] + akv[n]) for n in nu]
    m_full = [mn[n][:, :LANES]
              + jnp.where(ri == ci, p_c[j][:, p * LANES:(p + 1) * LANES], 0.0)
              for n, (j, p) in enumerate(units)]
    n_loc = [mn[n][:, LANES:] + khv[n] for n in nu]

    h = [h_ref[p] for p in range(n_pairs)]
    o_rows = []
    for j in range(tt // c):
        o_parts = []
        for p in range(n_pairs):
            n = j * n_pairs + p
            o_parts.append(_mm(q[n], h[p], pa=PH[0], pb=PH[1]) + o_loc[n])
            h[p] = _mm(m_full[n], h[p], pa=PH[0], pb=PH[1]) + n_loc[n]
        o_rows.append(jnp.concatenate(o_parts, axis=1))
    for p in range(n_pairs):
        h_ref[p] = h[p]

    o = jnp.concatenate(o_rows, axis=0)
    mean = head_sum(o) * (1.0 / HEAD_DIM)
    d = o - mean
    var = head_sum(d * d) * (1.0 / HEAD_DIM)
    on = d * lax.rsqrt(var + GN_EPS) * ln_w + ln_b
    o_ref[...] = ((on + bonus) * g).astype(o_ref.dtype)


RWKV_TILE = 256


def _rwkv7(p_rw, mu, w0, w_up, a0, a_up, g_up, k_k, k_a, r_k, ln_w, ln_b):
    bsz, t, npj = p_rw.shape
    w = RWKV_WIDTH
    tt = RWKV_TILE
    pv = jnp.stack([w0, a0, k_k, k_a, r_k.reshape(w), ln_w, ln_b, jnp.zeros((w,), F32)])
    wlo = jnp.zeros((LANES, 2 * w), F32)
    wlo = wlo.at[:DECAY_LORA, :w].set(w_up).at[DECAY_LORA:, w:].set(a_up)
    const = lambda shape: pl.BlockSpec(shape, lambda b, i: (0, 0))
    return pl.pallas_call(
        _rwkv_kernel,
        grid=(bsz, t // tt),
        in_specs=[
            pl.BlockSpec((None, tt, npj), lambda b, i: (b, i, 0)),
            const((1, npj)), const((SUBLANES, w)), const((LANES, 2 * w)),
            const((GATE_LORA, w)),
        ],
        out_specs=pl.BlockSpec((None, tt, w), lambda b, i: (b, i, 0)),
        out_shape=jax.ShapeDtypeStruct((bsz, t, w), BF16),
        scratch_shapes=[pltpu.VMEM((SUBLANES, npj), F32),
                        pltpu.VMEM((w // LANES, LANES, LANES), F32)],
        compiler_params=_cparams(("arbitrary", "arbitrary")),
        name="rwkv7",
    )(p_rw, mu.reshape(1, npj), pv, wlo, g_up)


def _mix_kernel(x_ref, yl_ref, yr_ref, wo1_ref, wo2_ref, g1_ref, sh_ref, sc_ref, ng_ref,
                wr_ref, br_ref, x1_ref, h2_ref, rid_ref, rw_ref, cnt_ref):
    y = (jnp.dot(yl_ref[...], wo1_ref[...], preferred_element_type=F32)
         + jnp.dot(yr_ref[...], wo2_ref[...], preferred_element_type=F32))
    x1 = x_ref[...] + g1_ref[...] * y
    x1_ref[...] = x1
    h2 = _adaln(x1, ng_ref[...], sh_ref[...], sc_ref[...]).astype(BF16)
    h2_ref[...] = _pack_bf16_halves(h2)
    logits = jnp.dot(jnp.concatenate([h2, h2], axis=1), wr_ref[...],
                     preferred_element_type=F32) + br_ref[...]

    tm = logits.shape[0]
    nblk = 2
    rows = tm // nblk
    neg = -jnp.inf
    lane = lax.broadcasted_iota(I32, (rows, LANES), 1)
    lane_f = lane.astype(F32)
    lg_all = [logits[i * rows:(i + 1) * rows, :] for i in range(nblk)]

    def first_argmax(xs):
        m = [jnp.max(x, axis=-1, keepdims=True) for x in xs]
        idx = [jnp.min(jnp.where(x == mi, lane_f, float(LANES)), axis=-1, keepdims=True)
               for x, mi in zip(xs, m)]
        return m, [i.astype(I32) for i in idx]

    lg = [jnp.where(lane < N_GROUPS, x, neg) for x in lg_all]
    gm, g_idx = first_argmax(lg)
    g_w = [1.0 / jnp.sum(jnp.exp(x - m), axis=-1, keepdims=True)
           for x, m in zip(lg, gm)]
    lo = [N_GROUPS + EXPERTS_PER_GROUP * g for g in g_idx]
    le = [jnp.where((lane >= l) & (lane < l + EXPERTS_PER_GROUP), x, neg)
          for x, l in zip(lg_all, lo)]
    v1, i1 = first_argmax(le)
    v2, i2 = first_argmax([jnp.where(lane == i, neg, x) for x, i in zip(le, i1)])

    @pl.when((pl.program_id(0) == 0) & (pl.program_id(1) == 0))
    def _():
        cnt_ref[...] = jnp.zeros_like(cnt_ref)

    e_lane = lane + N_GROUPS
    hits = sum(jnp.sum(jnp.where((e_lane == a) | (e_lane == b), 1.0, 0.0), axis=0, keepdims=True)
               for a, b in zip(i1, i2))
    cnt_ref[...] = cnt_ref[...] + hits.astype(I32)

    for blk in range(nblk):
        e2 = jnp.exp(v2[blk] - v1[blk])
        w1 = g_w[blk] / (1.0 + e2)
        w2 = g_w[blk] * e2 / (1.0 + e2)
        sl = slice(blk * rows, (blk + 1) * rows)
        rid_ref[sl, :] = jnp.where(lane == 0, i1[blk] - N_GROUPS,
                                   jnp.where(lane == 1, i2[blk] - N_GROUPS, 0))
        rw_ref[sl, :] = jnp.where(lane == 0, w1, jnp.where(lane == 1, w2, 0.0))


def _mix(x, y_lru, y_rw, w_out, g1, sh2, sc2, ng, w_grp, b_grp, w_exp, b_exp, tm=256):
    bsz, t, d = x.shape
    wo = w_out.astype(BF16)
    wl = LRU_WIDTH
    wr = jnp.zeros((d, LANES), F32).at[:, :N_GROUPS].set(w_grp)
    wr = wr.at[:, N_GROUPS:N_GROUPS + N_EXPERTS].set(w_exp)
    wr = jnp.concatenate(_split(wr, 2), axis=0)
    br = jnp.zeros((1, LANES), F32).at[0, :N_GROUPS].set(b_grp)
    br = br.at[0, N_GROUPS:N_GROUPS + N_EXPERTS].set(b_exp)
    vec = pl.BlockSpec((None, 1, d), lambda b, i: (b, 0, 0))
    const = lambda shape: pl.BlockSpec(shape, lambda b, i: (0, 0))
    row = lambda n: pl.BlockSpec((None, tm, n), lambda b, i: (b, i, 0))
    return pl.pallas_call(
        _mix_kernel,
        grid=(bsz, t // tm),
        in_specs=[row(d), row(wl), row(d - wl), const((wl, d)), const((d - wl, d)),
                  vec, vec, vec, const((1, d)), const((2 * d, LANES)), const((1, LANES))],
        out_specs=[row(d), row(d // 2), row(LANES), row(LANES), const((SUBLANES, LANES))],
        out_shape=[jax.ShapeDtypeStruct((bsz, t, d), F32),
                   jax.ShapeDtypeStruct((bsz, t, d // 2), I32),
                   jax.ShapeDtypeStruct((bsz, t, LANES), I32),
                   jax.ShapeDtypeStruct((bsz, t, LANES), F32),
                   jax.ShapeDtypeStruct((SUBLANES, LANES), I32)],
        compiler_params=_cparams(("arbitrary", "arbitrary")),
        name="outproj_adaln2_router",
    )(x, y_lru, y_rw, wo[:wl], wo[wl:], g1, sh2, sc2, ng.reshape(1, d), wr, br)


def _plan_kernel(rid_ref, cnt_ref, tril_ref, pos_ref, te_ref, base_ref):
    i = pl.program_id(0)
    tp = rid_ref.shape[0]
    lane = lax.broadcasted_iota(I32, (tp, LANES), 1)
    rid = rid_ref[...]
    oh0 = lane == rid[:, 0:1]
    oh1 = lane == rid[:, 1:2]
    oh = jnp.where(oh0 | oh1, 1.0, 0.0)

    @pl.when(i == 0)
    def _():
        cnt = cnt_ref[...]
        shift = EXPERT_TILE.bit_length() - 1
        padded = ((cnt + (EXPERT_TILE - 1)) >> shift) << shift
        l8 = lax.broadcasted_iota(I32, (SUBLANES, LANES), 1)
        end = padded
        s = 1
        while s < LANES:
            end = end + jnp.where(l8 >= s, pltpu.roll(end, s, 1), 0)
            s *= 2
        base_ref[...] = end - padded
        nt = te_ref.shape[0]
        j = lax.broadcasted_iota(I32, (nt, LANES), 0) * EXPERT_TILE
        lt = lax.broadcasted_iota(I32, (nt, LANES), 1)
        done = jnp.where((lt < N_EXPERTS) & (end[0:1, :] <= j), 1.0, 0.0)
        e_of = jnp.minimum(jnp.sum(done, axis=-1, keepdims=True), float(N_EXPERTS - 1)).astype(I32)
        total = end[0:1, N_EXPERTS - 1:N_EXPERTS]
        mine = lt == e_of
        cnt_e = jnp.sum(jnp.where(mine, cnt[0:1, :], 0), axis=-1, keepdims=True)
        start_e = jnp.sum(jnp.where(mine, end[0:1, :] - padded[0:1, :], 0), axis=-1, keepdims=True)
        valid = jnp.clip(cnt_e - (j[:, 0:1] - start_e), 0, EXPERT_TILE)
        te_ref[...] = jnp.where(lt == 0, e_of, jnp.where(lt == 1, total >> shift,
                                                        jnp.where(lt == 2, valid, 0)))

    prefix = jnp.dot(tril_ref[...], oh.astype(BF16), preferred_element_type=F32)
    dest = base_ref[0:1, :] + prefix.astype(I32)
    pos0 = jnp.sum(jnp.where(oh0, dest, 0), axis=-1, keepdims=True)
    pos1 = jnp.sum(jnp.where(oh1, dest, 0), axis=-1, keepdims=True)
    pos_ref[...] = jnp.where(lane == 0, pos0, jnp.where(lane == 1, pos1, 0))
    base_ref[...] = base_ref[...] + jnp.sum(oh, axis=0, keepdims=True).astype(I32)


def _plan(rid, cnt, n_tiles, tp=512):
    n = rid.shape[0]
    tp = min(tp, n)
    nt_pad = -(-n_tiles // SUBLANES) * SUBLANES
    tril = (jnp.arange(tp)[:, None] > jnp.arange(tp)[None, :]).astype(BF16)
    pos, te = pl.pallas_call(
        _plan_kernel,
        grid=(n // tp,),
        in_specs=[pl.BlockSpec((tp, LANES), lambda i: (i, 0)),
                  pl.BlockSpec((SUBLANES, LANES), lambda i: (0, 0)),
                  pl.BlockSpec((tp, tp), lambda i: (0, 0))],
        out_specs=[pl.BlockSpec((tp, LANES), lambda i: (i, 0)),
                   pl.BlockSpec((nt_pad, LANES), lambda i: (0, 0))],
        out_shape=[jax.ShapeDtypeStruct((n, LANES), I32),
                   jax.ShapeDtypeStruct((nt_pad, LANES), I32)],
        scratch_shapes=[pltpu.VMEM((SUBLANES, LANES), I32)],
        compiler_params=_cparams(("arbitrary",)),
        name="route_plan",
    )(rid, cnt, tril)
    return pos, te


SC_CORES = 2
SC_SUBCORES = 16
SC_WORKERS = SC_CORES * SC_SUBCORES
SC_WINDOW = 64


def _sc_mesh():
    return plsc.VectorSubcoreMesh(core_axis_name="c", subcore_axis_name="s",
                                  num_cores=SC_CORES, num_subcores=SC_SUBCORES)


def _sc_two_buffer_loop(n_win, fill, drain):
    assert n_win % 2 == 0

    def start(copies):
        for cp in copies:
            cp.start()

    def wait(copies):
        for cp in copies:
            cp.wait()

    start(fill(0, 0))

    @pl.loop(0, n_win, step=2)
    def _(j):
        for b in range(2):
            jj = j + b

            @pl.when(jj >= 1)
            def _():
                wait(drain(jj - 1, 1 - b))

            @pl.when(jj + 1 < n_win)
            def _():
                start(fill(jj + 1, 1 - b))

            wait(fill(jj, b))
            start(drain(jj, b))

    wait(drain(n_win - 1, 1))


def _sc_scatter_rows(rows, pos0, pos1, n_out):
    n, d = rows.shape
    per_w = n // SC_WORKERS
    n_win = per_w // SC_WINDOW
    shape3 = (SC_WORKERS, n_win, SC_WINDOW)

    @functools.partial(
        pl.kernel, mesh=_sc_mesh(), out_type=jax.ShapeDtypeStruct((n_out, d), rows.dtype),
        scratch_types=[pltpu.VMEM((n_win, SC_WINDOW), I32), pltpu.VMEM((n_win, SC_WINDOW), I32),
                       pltpu.VMEM((2, SC_WINDOW, d), rows.dtype),
                       pltpu.SemaphoreType.DMA((2,)), pltpu.SemaphoreType.DMA((2,))],
        name="sc_scatter_rows")
    def scatter(rows_hbm, p0_hbm, p1_hbm, out_hbm, p0_v, p1_v, buf, in_sem, out_sem):
        wid = lax.axis_index("s") * SC_CORES + lax.axis_index("c")
        base = wid * per_w
        pltpu.sync_copy(p0_hbm.at[wid], p0_v)
        pltpu.sync_copy(p1_hbm.at[wid], p1_v)

        def fill(j, b):
            src = rows_hbm.at[pl.ds(base + j * SC_WINDOW, SC_WINDOW)]
            return [pltpu.make_async_copy(src, buf.at[b], in_sem.at[b])]

        def drain(j, b):
            return [pltpu.make_async_copy(buf.at[b], out_hbm.at[p.at[j]], out_sem.at[b])
                    for p in (p0_v, p1_v)]

        _sc_two_buffer_loop(n_win, fill, drain)

    return scatter(rows, pos0.reshape(shape3), pos1.reshape(shape3))


def _sc_gather_rows(table, idx):
    m = idx.shape[0]
    d = table.shape[1]
    per_w = m // SC_WORKERS
    n_win = per_w // SC_WINDOW

    @functools.partial(
        pl.kernel, mesh=_sc_mesh(), out_type=jax.ShapeDtypeStruct((m, d), table.dtype),
        scratch_types=[pltpu.VMEM((n_win, SC_WINDOW), I32),
                       pltpu.VMEM((2, SC_WINDOW, d), table.dtype),
                       pltpu.SemaphoreType.DMA((2,)), pltpu.SemaphoreType.DMA((2,))],
        name="sc_gather_rows")
    def gather(table_hbm, idx_hbm, out_hbm, idx_v, buf, in_sem, out_sem):
        wid = lax.axis_index("s") * SC_CORES + lax.axis_index("c")
        base = wid * per_w
        pltpu.sync_copy(idx_hbm.at[wid], idx_v)

        def fill(j, b):
            return [pltpu.make_async_copy(table_hbm.at[idx_v.at[j]], buf.at[b], in_sem.at[b])]

        def drain(j, b):
            dst = out_hbm.at[pl.ds(base + j * SC_WINDOW, SC_WINDOW)]
            return [pltpu.make_async_copy(buf.at[b], dst, out_sem.at[b])]

        _sc_two_buffer_loop(n_win, fill, drain)

    return gather(table, idx.reshape(SC_WORKERS, n_win, SC_WINDOW))


def _expert_kernel(te_ref, nu_ref, valid_ref, x_ref, w1_ref, w3_ref, w2_ref, o_ref,
                   w1b_ref, w3b_ref, w2b_ref):
    j = pl.program_id(0)
    active = j < nu_ref[0]

    @pl.when(active & ((j == 0) | (te_ref[j] != te_ref[jnp.maximum(j - 1, 0)])))
    def _():
        w1b_ref[...] = w1_ref[...].astype(BF16)
        w3b_ref[...] = w3_ref[...].astype(BF16)
        w2b_ref[...] = w2_ref[...].astype(BF16)

    @pl.when(active)
    def _():
        row = lax.broadcasted_iota(I32, x_ref.shape, 0)
        x = _unpack_bf16_halves(jnp.where(row < valid_ref[j], x_ref[...], 0)).astype(BF16)
        h1 = jnp.dot(x, w1b_ref[...], preferred_element_type=F32)
        h3 = jnp.dot(x, w3b_ref[...], preferred_element_type=F32)
        hid = (h1 * jax.nn.sigmoid(h1) * h3).astype(BF16)
        o_ref[...] = _pack_bf16_halves(jnp.dot(hid, w2b_ref[...], preferred_element_type=F32))

    @pl.when(j >= nu_ref[0])
    def _():
        o_ref[...] = jnp.zeros_like(o_ref)


def _experts(xs, te, nu, valid, w1, w3, w2):
    n_rows = xs.shape[0]
    d, de = w1.shape[-2:]
    nt = n_rows // EXPERT_TILE
    used = lambda j, nu: jnp.minimum(j, nu[0] - 1)
    grid_spec = pltpu.PrefetchScalarGridSpec(
        num_scalar_prefetch=3,
        grid=(nt,),
        in_specs=[
            pl.BlockSpec((EXPERT_TILE, d // 2), lambda j, te, nu, va: (used(j, nu), 0)),
            pl.BlockSpec((None, d, de), lambda j, te, nu, va: (te[used(j, nu)], 0, 0)),
            pl.BlockSpec((None, d, de), lambda j, te, nu, va: (te[used(j, nu)], 0, 0)),
            pl.BlockSpec((None, de, d), lambda j, te, nu, va: (te[used(j, nu)], 0, 0)),
        ],
        out_specs=pl.BlockSpec((EXPERT_TILE, d // 2), lambda j, te, nu, va: (j, 0)),
        scratch_shapes=[pltpu.VMEM((d, de), BF16), pltpu.VMEM((d, de), BF16),
                        pltpu.VMEM((de, d), BF16)],
    )
    return pl.pallas_call(
        _expert_kernel,
        grid_spec=grid_spec,
        out_shape=jax.ShapeDtypeStruct((n_rows, d // 2), I32),
        compiler_params=_cparams(("arbitrary",)),
        name="expert_mlp",
    )(te, nu, valid, xs, w1, w3, w2)


def _combine_kernel(x1_ref, rw_ref, g2_ref, fg_ref, y0_ref, y1_ref, o_ref):
    rw = rw_ref[...]
    moe = (rw[:, 0:1] * _unpack_bf16_halves(y0_ref[...])
           + rw[:, 1:2] * _unpack_bf16_halves(y1_ref[...]))
    x2 = x1_ref[...] + g2_ref[...] * moe
    ms = jnp.mean(x2 * x2, axis=-1, keepdims=True)
    o_ref[...] = x2 * lax.rsqrt(ms + RMS_EPS) * fg_ref[...]


def _combine(x1, rw, yg, g2, final_g, tc):
    bsz, t, d = x1.shape
    row = lambda n: pl.BlockSpec((None, tc, n), lambda b, i: (b, i, 0))
    slot = lambda s: pl.BlockSpec((None, None, tc, d // 2), lambda b, i: (s, b, i, 0))
    return pl.pallas_call(
        _combine_kernel,
        grid=(bsz, t // tc),
        in_specs=[row(d), row(LANES),
                  pl.BlockSpec((None, 1, d), lambda b, i: (b, 0, 0)),
                  pl.BlockSpec((1, d), lambda b, i: (0, 0)),
                  slot(0), slot(1)],
        out_specs=row(d),
        out_shape=jax.ShapeDtypeStruct((bsz, t, d), F32),
        compiler_params=_cparams(("arbitrary", "arbitrary")),
        name="combine_final_norm",
    )(x1, rw, g2, final_g.reshape(1, d), yg, yg)


def _row_tile(t, want):
    return want if t % want == 0 else t


def kernel(x, c, w_ada, b_ada, norm1_g, w_in, conv_w, conv_b, lru_wa, lru_ba, lru_wi, lru_bi, lru_lam, lru_norm_g, tok_mu, w0, w_up, a0, a_up, g_up, k_k, k_a, r_k, ln_x_w, ln_x_b, w_out, norm2_g, w_grp, b_grp, w_exp, b_exp, w1, w3, w2, final_g):
    bsz, t, d = x.shape
    n = bsz * t
    depth = w_ada.shape[0]
    assert depth == 1, "the combine kernel applies the final norm: only DEPTH == 1 is wired"
    tile = _row_tile(t, ROW_TILE)
    n_tiles = (n * TOP_K) // EXPERT_TILE + N_EXPERTS
    for l in range(depth):
        mod = _modulation(c, w_ada[l], b_ada[l]).reshape(bsz, 6, 1, d)
        sh1, sc1, g1, sh2, sc2, g2 = (mod[:, i] for i in range(6))
        y_lru, p_rw = _inproj_lru(x, sh1, sc1, norm1_g[l], w_in[l], conv_w[l], conv_b[l],
                                  lru_wa[l], lru_ba[l], lru_wi[l], lru_bi[l], lru_lam[l],
                                  lru_norm_g[l], tm=tile)
        y_rw = _rwkv7(p_rw, tok_mu[l], w0[l], w_up[l], a0[l], a_up[l], g_up[l], k_k[l], k_a[l],
                      r_k[l], ln_x_w[l], ln_x_b[l])
        x1, h2, rid, rw, cnt = _mix(x, y_lru, y_rw, w_out[l], g1, sh2, sc2, norm2_g[l],
                                    w_grp[l], b_grp[l], w_exp[l], b_exp[l], tm=tile)
        pos, te = _plan(rid.reshape(n, LANES), cnt, n_tiles)
        pos0, pos1 = pos[:, 0], pos[:, 1]
        xs = _sc_scatter_rows(h2.reshape(n, d // 2), pos0, pos1, n_tiles * EXPERT_TILE)
        ys = _experts(xs, te[:n_tiles, 0], te[0:1, 1], te[:n_tiles, 2], w1[l], w3[l], w2[l])
        yg = _sc_gather_rows(ys, jnp.concatenate([pos0, pos1]))
        x = _combine(x1, rw, yg.reshape(TOP_K, bsz, t, d // 2), g2, final_g,
                     _row_tile(t, COMBINE_TILE))
    return x
```

```python
import functools

import jax
import jax.numpy as jnp
from jax import lax
from jax.experimental import pallas as pl
from jax.experimental.pallas import tpu as pltpu
from jax.experimental.pallas import tpu_sc as plsc

F32 = jnp.float32
BF16 = jnp.bfloat16
I32 = jnp.int32

LRU_WIDTH = 512
LRU_HEAD_DIM = 64
CONV_WIDTH = 4
LRU_C = 8.0
RWKV_WIDTH = 512
HEAD_DIM = 64
DECAY_LORA = 64
AAA_LORA = 64
GATE_LORA = 128
RWKV_PROJ = 3 * RWKV_WIDTH + DECAY_LORA + AAA_LORA + GATE_LORA
N_GROUPS = 4
EXPERTS_PER_GROUP = 8
N_EXPERTS = N_GROUPS * EXPERTS_PER_GROUP
TOP_K = 2
RMS_EPS = 1e-6
GN_EPS = 64e-5

LANES = 128
SUBLANES = 8
CHUNK = 64
EXPERT_TILE = 512
ROW_TILE = 256
COMBINE_TILE = 512
VMEM_LIMIT = 48 * 1024 * 1024

NN = (((1,), (0,)), ((), ()))
NT = (((1,), (1,)), ((), ()))
TN = (((0,), (0,)), ((), ()))


def _split(x, n):
    if x.dtype == BF16:
        return [x]
    parts = []
    rem = x
    for i in range(n):
        p = rem.astype(BF16)
        parts.append(p)
        if i + 1 < n:
            rem = rem - p.astype(F32)
    return parts


def _mm(a, b, dn=NN, pa=1, pb=1):
    aps = _split(a, pa)
    bps = _split(b, pb)
    order = max(len(aps), len(bps))
    terms = [(i, j) for i in range(len(aps)) for j in range(len(bps)) if i + j < order]
    ka = dn[0][0][0]
    kb = dn[0][1][0]
    if len(terms) > 1 and a.shape[ka] % LANES == 0:
        a_cat = jnp.concatenate([aps[i] for i, _ in terms], axis=ka)
        b_cat = jnp.concatenate([bps[j] for _, j in terms], axis=kb)
        return lax.dot_general(a_cat, b_cat, dn, preferred_element_type=F32)
    out = None
    for i, j in terms:
        t = lax.dot_general(aps[i], bps[j], dn, preferred_element_type=F32)
        out = t if out is None else out + t
    return out


def _pack_bf16_halves(x):
    n = x.shape[1] // 2
    bits = lax.bitcast_convert_type(x.astype(BF16).astype(F32), I32)
    return bits[:, n:] | ((bits[:, :n] >> 16) & 0xFFFF)


def _unpack_bf16_halves(p):
    lo = lax.bitcast_convert_type(p << 16, F32)
    hi = lax.bitcast_convert_type(p & (-65536), F32)
    return jnp.concatenate([lo, hi], axis=1)


def _softplus(x):
    return jnp.maximum(x, 0.0) + jnp.log1p(jnp.exp(-jnp.abs(x)))


def _cparams(sem):
    return pltpu.CompilerParams(dimension_semantics=sem, vmem_limit_bytes=VMEM_LIMIT)


def _mod_kernel(c_ref, w_ref, b_ref, o_ref):
    c = c_ref[...]
    s = c * jax.nn.sigmoid(c)
    o_ref[...] = _mm(s, w_ref[...], pa=2, pb=2) + b_ref[...]


def _modulation(c, w_ada, b_ada):
    bsz, d = c.shape
    n_out = w_ada.shape[1]
    rows = -(-bsz // SUBLANES) * SUBLANES
    c_pad = jnp.zeros((rows, d), F32).at[:bsz].set(c)
    bn = d
    out = pl.pallas_call(
        _mod_kernel,
        grid=(n_out // bn,),
        in_specs=[
            pl.BlockSpec((rows, d), lambda j: (0, 0)),
            pl.BlockSpec((d, bn), lambda j: (0, j)),
            pl.BlockSpec((1, bn), lambda j: (0, j)),
        ],
        out_specs=pl.BlockSpec((rows, bn), lambda j: (0, j)),
        out_shape=jax.ShapeDtypeStruct((rows, n_out), F32),
        compiler_params=_cparams(("arbitrary",)),
        name="adaln_mod",
    )(c_pad, w_ada, b_ada.reshape(1, n_out))
    return out[:bsz]


def _adaln(x, g, shift, scale):
    ms = jnp.mean(x * x, axis=-1, keepdims=True)
    y = x * lax.rsqrt(ms + RMS_EPS) * g
    return y * (1.0 + scale) + shift


def _inproj_lru_kernel(x_ref, sh_ref, sc_ref, g_ref, wl_ref, wr_ref,
                       cw_ref, cb_ref, wab_ref, bab_ref, lam_ref, ng_ref, ones_ref,
                       yl_ref, pr_ref, xprev_ref, hprev_ref):
    @pl.when(pl.program_id(1) == 0)
    def _():
        xprev_ref[...] = jnp.zeros_like(xprev_ref)
        hprev_ref[...] = jnp.zeros_like(hprev_ref)

    w = LRU_WIDTH
    tm = x_ref.shape[0]
    nblk = 2
    rows = tm // nblk
    blocks = [slice(i * rows, (i + 1) * rows) for i in range(nblk)]
    h = [_adaln(x_ref[sl, :], g_ref[...], sh_ref[...], sc_ref[...]).astype(BF16)
         for sl in blocks]
    ux = [jnp.dot(hb, wl_ref[:, :w], preferred_element_type=F32) for hb in h]

    def remaining_columns():
        for sl, hb in zip(blocks, h):
            pr_ref[sl, :] = jnp.dot(hb, wr_ref[...], preferred_element_type=F32)
        return [jnp.dot(hb, wl_ref[:, w:], preferred_element_type=F32) for hb in h]

    y = _lru_tile(ux, remaining_columns, cw_ref, cb_ref, wab_ref, bab_ref, lam_ref, ng_ref,
                  ones_ref, xprev_ref, hprev_ref)
    for sl, yb in zip(blocks, y):
        yl_ref[sl, :] = yb.astype(yl_ref.dtype)


def _inproj_lru(x, sh1, sc1, g, w_in, conv_w, conv_b, wa, ba, wi, bi, lam, norm_g, tm=256):
    bsz, t, d = x.shape
    w = LRU_WIDTH
    nl = 2 * w
    nr = RWKV_PROJ
    wl = w_in[:, :nl].astype(BF16)
    wr = w_in[:, nl:].astype(BF16)
    wab = jnp.concatenate([_block_diag(wa), _block_diag(wi)], axis=1).astype(BF16)
    bab = jnp.concatenate([ba.reshape(1, w), bi.reshape(1, w)], axis=1)
    vec = pl.BlockSpec((None, 1, d), lambda b, i: (b, 0, 0))
    const = lambda shape: pl.BlockSpec(shape, lambda b, i: (0, 0))
    return pl.pallas_call(
        _inproj_lru_kernel,
        grid=(bsz, t // tm),
        in_specs=[
            pl.BlockSpec((None, tm, d), lambda b, i: (b, i, 0)),
            vec, vec, const((1, d)), const((d, nl)), const((d, nr)),
            const((CONV_WIDTH, w)), const((1, w)), const((w, 2 * w)), const((1, 2 * w)),
            const((1, w)), const((1, w)), const((w, w)),
        ],
        out_specs=[
            pl.BlockSpec((None, tm, w), lambda b, i: (b, i, 0)),
            pl.BlockSpec((None, tm, nr), lambda b, i: (b, i, 0)),
        ],
        out_shape=[
            jax.ShapeDtypeStruct((bsz, t, w), BF16),
            jax.ShapeDtypeStruct((bsz, t, nr), F32),
        ],
        scratch_shapes=[pltpu.VMEM((SUBLANES, w), F32), pltpu.VMEM((SUBLANES, w), F32)],
        compiler_params=_cparams(("arbitrary", "arbitrary")),
        name="adaln1_inproj_rglru",
    )(x, sh1, sc1, g.reshape(1, d), wl, wr, conv_w, conv_b.reshape(1, w), wab, bab,
      lam.reshape(1, w), norm_g.reshape(1, w), _head_ones(w, LRU_HEAD_DIM))


def _gelu_tanh(x):
    c = 0.7978845608028654
    return x * (0.5 * (1.0 + jnp.tanh(c * (x + 0.044715 * (x * x * x)))))


def _sigmoid(x):
    return 0.5 * jnp.tanh(0.5 * x) + 0.5


def _lru_tile(ux, gate_branch, cw_ref, cb_ref, wab_ref, bab_ref, lam_ref, ng_ref, ones_ref,
              xprev_ref, hprev_ref):
    w = LRU_WIDTH
    rows = ux[0].shape[0]
    prev8 = [xprev_ref[...]] + [u[rows - SUBLANES:, :] for u in ux[:-1]]
    ext = [jnp.concatenate([p, u], axis=0) for p, u in zip(prev8, ux)]
    xc = [cb_ref[...] + cw_ref[CONV_WIDTH - 1:CONV_WIDTH, :] * u for u in ux]
    for k in range(1, CONV_WIDTH):
        tap = cw_ref[CONV_WIDTH - 1 - k:CONV_WIDTH - k, :]
        xc = [x + tap * pltpu.roll(e, k, 0)[SUBLANES:, :] for x, e in zip(xc, ext)]
    xprev_ref[...] = ux[-1][rows - SUBLANES:, :]

    gates = [jnp.dot(x.astype(BF16), wab_ref[...], preferred_element_type=F32) + bab_ref[...]
             for x in xc]
    ug = gate_branch()
    r = [_sigmoid(g[:, :w]) for g in gates]
    ig = [_sigmoid(g[:, w:]) for g in gates]
    sp = _softplus(-lam_ref[...])
    log_a = [(-LRU_C) * x * sp for x in r]
    a = [jnp.exp(x) for x in log_a]
    th = [jnp.tanh(x) for x in log_a]
    q = [(-2.0 * x) / (1.0 - x) for x in th]
    root_q = [jnp.where(x > 0.0, x * lax.rsqrt(x), 0.0) for x in q]
    b = [s * (i * x) for s, i, x in zip(root_q, ig, xc)]

    row8 = lax.broadcasted_iota(I32, (rows, w), 0) & (SUBLANES - 1)
    acc_a, acc_b = a, b
    for s in (1, 2, 4):
        live = row8 >= s
        sh_a = [pltpu.roll(x, s, 0) for x in acc_a]
        sh_b = [pltpu.roll(x, s, 0) for x in acc_b]
        acc_b = [jnp.where(live, x * sb + y, y) for x, sb, y in zip(acc_a, sh_b, acc_b)]
        acc_a = [jnp.where(live, x * sa, x) for x, sa in zip(acc_a, sh_a)]
    carry = hprev_ref[SUBLANES - 1:SUBLANES, :]
    h = []
    for xa, xb in zip(acc_a, acc_b):
        groups = []
        for gi in range(rows // SUBLANES):
            lo = gi * SUBLANES
            groups.append(xa[lo:lo + SUBLANES, :] * carry + xb[lo:lo + SUBLANES, :])
            last = lo + SUBLANES - 1
            carry = xa[last:last + 1, :] * carry + xb[last:last + 1, :]
        h.append(jnp.concatenate(groups, axis=0))
    hprev_ref[...] = h[-1][rows - SUBLANES:, :]

    y = [x * _gelu_tanh(g) for x, g in zip(h, ug)]
    ms = [_mm(x * x, ones_ref[...]) * (1.0 / LRU_HEAD_DIM) for x in y]
    return [x * lax.rsqrt(m + RMS_EPS) * ng_ref[...] for x, m in zip(y, ms)]


def _block_diag(w):
    h, n, _ = w.shape
    eye = jnp.eye(h, dtype=w.dtype)
    return (eye[:, None, :, None] * w[:, :, None, :]).reshape(h * n, h * n)


def _head_ones(width, head):
    idx = jnp.arange(width) // head
    return (idx[:, None] == idx[None, :]).astype(BF16)


PG = (1, 1)
PI = (1, 1)
PS = (1, 1)
PH = (1, 2)


def _unit_lower_inverse(a_list, ri, ci):
    mm = functools.partial(_mm, pa=PI[0], pb=PI[1])
    eye = jnp.where(ri == ci, 1.0, 0.0)
    leaf = (ri >> 3) == (ci >> 3)
    a8 = [jnp.where(leaf, a, 0.0) for a in a_list]
    a8_2 = [mm(x, x) for x in a8]
    a8_4 = [mm(x, x) for x in a8_2]
    t = [mm(eye + x, eye + y) for x, y in zip(a8, a8_2)]
    t = [mm(x, eye + y) for x, y in zip(t, a8_4)]
    zero = jnp.zeros((LANES, LANES), F32)
    for sh in (3, 4, 5):
        s = 1 << sh
        off = ((ri >> (sh + 1)) == (ci >> (sh + 1))) & ((ri >> sh) != (ci >> sh))
        t_lo = [_second_blocks(x, s) for x in t]
        b_lo = [mm(_second_blocks(jnp.where(off, a, 0.0), s), x) for a, x in zip(a_list, t)]
        d_lo = [mm(x, _interleave_blocks(zero, y, s)) for x, y in zip(t_lo, b_lo)]
        t = [_interleave_blocks(x, y + z, s) for x, y, z in zip(t, t_lo, d_lo)]
    return t


def _second_blocks(x, s):
    return jnp.concatenate(
        [x[s * (2 * m + 1):s * (2 * m + 2)] for m in range(x.shape[0] // (2 * s))], axis=0)


def _interleave_blocks(first_src, second, s):
    parts = []
    for m in range(first_src.shape[0] // (2 * s)):
        parts.append(first_src[2 * s * m:2 * s * m + s])
        parts.append(second[s * m:s * (m + 1)])
    return jnp.concatenate(parts, axis=0)


def _rwkv_kernel(p_ref, mu_ref, pv_ref, wlo_ref, gup_ref, o_ref, uprev_ref, h_ref):
    w = RWKV_WIDTH
    tt = p_ref.shape[0]
    c = CHUNK
    n_pairs = w // LANES
    units = [(j, p) for j in range(tt // c) for p in range(n_pairs)]

    @pl.when(pl.program_id(1) == 0)
    def _():
        uprev_ref[...] = jnp.zeros_like(uprev_ref)
        h_ref[...] = jnp.zeros_like(h_ref)

    u = p_ref[...]
    ext = jnp.concatenate([uprev_ref[...], u], axis=0)
    prev = pltpu.roll(ext, 1, 0)[SUBLANES:, :]
    uprev_ref[...] = u[tt - SUBLANES:, :]
    um = u + (prev - u) * mu_ref[...]

    r = um[:, 0:w]
    k = um[:, w:2 * w]
    v = um[:, 2 * w:3 * w]
    z = um[:, 3 * w:3 * w + LANES]
    gd = um[:, 3 * w + LANES:]
    w0 = pv_ref[0:1, :]
    a0 = pv_ref[1:2, :]
    k_k = pv_ref[2:3, :]
    k_a = pv_ref[3:4, :]
    r_k = pv_ref[4:5, :]
    ln_w = pv_ref[5:6, :]
    ln_b = pv_ref[6:7, :]

    lane_t = lax.broadcasted_iota(I32, (tt, LANES), 1)
    zz = jnp.where(lane_t < HEAD_DIM, jnp.tanh(z), z)
    lora = _mm(zz, wlo_ref[...], pa=2, pb=2)
    wlog = -_softplus(-(w0 + lora[:, :w])) - 0.5
    ld = -jnp.exp(wlog)
    a = jax.nn.sigmoid(a0 + lora[:, w:])
    g = _mm(jax.nn.sigmoid(gd), gup_ref[...], pa=2, pb=2)

    ri = lax.broadcasted_iota(I32, (LANES, LANES), 0)
    ci = lax.broadcasted_iota(I32, (LANES, LANES), 1)
    even = lax.broadcasted_iota(I32, (c, LANES), 1) < HEAD_DIM
    wide = 2 * LANES
    ones_bd = jnp.where(
        (lax.broadcasted_iota(I32, (wide, wide), 0) >> 6)
        == (lax.broadcasted_iota(I32, (wide, wide), 1) >> 6), 1.0, 0.0).astype(BF16)

    def head_sum(x):
        return jnp.concatenate(
            [_mm(x[:, q * wide:(q + 1) * wide], ones_bd) for q in range(w // wide)], axis=1)

    kk = k * k_k
    kk = kk / jnp.maximum(jnp.sqrt(head_sum(kk * kk)), 1e-12)
    kp = k * (1.0 + (a - 1.0) * k_a)
    kka = kk * a
    bonus = head_sum(r * kp * r_k) * v
    row_in_chunk = lax.broadcasted_iota(I32, (tt, w), 0) & (c - 1)
    cum = ld
    s = 1
    while s < c:
        cum = cum + jnp.where(row_in_chunk >= s, pltpu.roll(cum, s, 0), 0.0)
        s *= 2
    cum_c = [cum[j * c + c - 1:j * c + c, :] for j in range(tt // c)]
    p_c = [jnp.exp(x) for x in cum_c]
    p_inv = jnp.exp(-cum)
    p_hat = jnp.exp(jnp.concatenate(
        [cum_c[j] - cum[j * c:(j + 1) * c, :] for j in range(tt // c)], axis=0))
    al = -kk * jnp.exp(cum - ld)
    rt = r * jnp.exp(cum)
    bt = kka * p_inv
    kt = kp * p_inv
    bh = kka * p_hat
    kh = kp * p_hat

    def blk(x):
        return [x[j * c:(j + 1) * c, p * LANES:(p + 1) * LANES] for j, p in units]

    def halves(x):
        xs = blk(x)
        return [jnp.where(even, y, 0.0) for y in xs], [jnp.where(even, 0.0, y) for y in xs]

    def rows(top, bot):
        return [jnp.concatenate([x, y], axis=0) for x, y in zip(top, bot)]

    def unstack(x):
        return x[:c, :] + x[c:, :]

    al_e, al_o = halves(al)
    rt_e, rt_o = halves(rt)
    bt_e, bt_o = halves(bt)
    kt_e, kt_o = halves(kt)
    bh_e, bh_o = halves(bh)
    kh_e, kh_o = halves(kh)
    v_e, v_o = halves(v)
    al_n = rows(al_e, al_o)
    bh_n = rows(bh_e, bh_o)
    v_s = rows(v_o, v_e)
    kh_s = rows(kh_o, kh_e)
    rt_b = blk(rt)
    nu = range(len(units))

    g0 = [_mm(x, y, NT, pa=PG[0], pb=PG[1])
          for x, y in zip(rows(al_e, rt_e), rows(bt_e, kt_e))]
    g1 = [_mm(x, y, NT, pa=PG[0], pb=PG[1])
          for x, y in zip(rows(rt_o, al_o), rows(kt_o, bt_o))]
    top = ri < c
    left = ci < c
    tri_s = (ri & (c - 1)) > (ci & (c - 1))
    tri_i = (ri & (c - 1)) >= (ci & (c - 1))
    diag_q = top == left

    def pick(x0, x1, in_q0, tri):
        return [jnp.where(tri, jnp.where(in_q0, x, y), 0.0) for x, y in zip(x0, x1)]

    a_ab = pick(g0, g1, top, diag_q & tri_s)
    a_rk = pick(g1, g0, top, diag_q & tri_i)
    a_ak = pick(g0, g1, top, (~diag_q) & tri_s)
    a_rb = pick(g1, g0, top, (~diag_q) & tri_i)
    x1 = [_mm(a_ak[n], v_s[n], pa=PS[0], pb=PS[1]) for n in nu]
    akv = [_mm(a_rk[n], v_s[n], pa=PS[0], pb=PS[1]) for n in nu]
    khv = [_mm(kh_s[n], v_s[n], TN, pa=PS[0], pb=PS[1]) for n in nu]
    t_inv = _unit_lower_inverse(a_ab, ri, ci)
    tw = [_mm(t_inv[n], jnp.concatenate([al_n[n], x1[n]], axis=1), pa=PS[0], pb=PS[1])
          for n in nu]
    qo = [_mm(a_rb[n], tw[n], pa=PS[0], pb=PS[1]) for n in nu]
    mn = [_mm(bh_n[n], tw[n], TN, pa=PS[0], pb=PS[1]) for n in nu]
    q = [rt_b[n] + unstack(qo[n][:, :LANES]) for n in nu]
    o_loc = [unstack(qo[n][:, LANES:] + akv[n]) for n in nu]
    m_full = [mn[n][:, :LANES]
              + jnp.where(ri == ci, p_c[j][:, p * LANES:(p + 1) * LANES], 0.0)
              for n, (j, p) in enumerate(units)]
    n_loc = [mn[n][:, LANES:] + khv[n] for n in nu]

    h = [h_ref[p] for p in range(n_pairs)]
    o_rows = []
    for j in range(tt // c):
        o_parts = []
        for p in range(n_pairs):
            n = j * n_pairs + p
            o_parts.append(_mm(q[n], h[p], pa=PH[0], pb=PH[1]) + o_loc[n])
            h[p] = _mm(m_full[n], h[p], pa=PH[0], pb=PH[1]) + n_loc[n]
        o_rows.append(jnp.concatenate(o_parts, axis=1))
    for p in range(n_pairs):
        h_ref[p] = h[p]

    o = jnp.concatenate(o_rows, axis=0)
    mean = head_sum(o) * (1.0 / HEAD_DIM)
    d = o - mean
    var = head_sum(d * d) * (1.0 / HEAD_DIM)
    on = d * lax.rsqrt(var + GN_EPS) * ln_w + ln_b
    o_ref[...] = ((on + bonus) * g).astype(o_ref.dtype)


RWKV_TILE = 256


def _rwkv7(p_rw, mu, w0, w_up, a0, a_up, g_up, k_k, k_a, r_k, ln_w, ln_b):
    bsz, t, npj = p_rw.shape
    w = RWKV_WIDTH
    tt = RWKV_TILE
    pv = jnp.stack([w0, a0, k_k, k_a, r_k.reshape(w), ln_w, ln_b, jnp.zeros((w,), F32)])
    wlo = jnp.zeros((LANES, 2 * w), F32)
    wlo = wlo.at[:DECAY_LORA, :w].set(w_up).at[DECAY_LORA:, w:].set(a_up)
    const = lambda shape: pl.BlockSpec(shape, lambda b, i: (0, 0))
    return pl.pallas_call(
        _rwkv_kernel,
        grid=(bsz, t // tt),
        in_specs=[
            pl.BlockSpec((None, tt, npj), lambda b, i: (b, i, 0)),
            const((1, npj)), const((SUBLANES, w)), const((LANES, 2 * w)),
            const((GATE_LORA, w)),
        ],
        out_specs=pl.BlockSpec((None, tt, w), lambda b, i: (b, i, 0)),
        out_shape=jax.ShapeDtypeStruct((bsz, t, w), BF16),
        scratch_shapes=[pltpu.VMEM((SUBLANES, npj), F32),
                        pltpu.VMEM((w // LANES, LANES, LANES), F32)],
        compiler_params=_cparams(("arbitrary", "arbitrary")),
        name="rwkv7",
    )(p_rw, mu.reshape(1, npj), pv, wlo, g_up)


def _mix_kernel(x_ref, yl_ref, yr_ref, wo1_ref, wo2_ref, g1_ref, sh_ref, sc_ref, ng_ref,
                wr_ref, br_ref, x1_ref, h2_ref, rid_ref, rw_ref, cnt_ref):
    tm = x_ref.shape[0]
    nblk = 2
    rows = tm // nblk
    blocks = [slice(i * rows, (i + 1) * rows) for i in range(nblk)]
    y = [jnp.dot(yl_ref[sl, :], wo1_ref[...], preferred_element_type=F32)
         + jnp.dot(yr_ref[sl, :], wo2_ref[...], preferred_element_type=F32) for sl in blocks]
    x1 = [x_ref[sl, :] + g1_ref[...] * yb for sl, yb in zip(blocks, y)]
    h2 = [_adaln(xb, ng_ref[...], sh_ref[...], sc_ref[...]).astype(BF16) for xb in x1]
    lg_all = [jnp.dot(jnp.concatenate([hb, hb], axis=1), wr_ref[...],
                      preferred_element_type=F32) + br_ref[...] for hb in h2]
    for sl, xb, hb in zip(blocks, x1, h2):
        x1_ref[sl, :] = xb
        h2_ref[sl, :] = _pack_bf16_halves(hb)

    neg = -jnp.inf
    lane = lax.broadcasted_iota(I32, (rows, LANES), 1)
    lane_f = lane.astype(F32)

    def first_argmax(xs):
        m = [jnp.max(x, axis=-1, keepdims=True) for x in xs]
        idx = [jnp.min(jnp.where(x == mi, lane_f, float(LANES)), axis=-1, keepdims=True)
               for x, mi in zip(xs, m)]
        return m, [i.astype(I32) for i in idx]

    lg = [jnp.where(lane < N_GROUPS, x, neg) for x in lg_all]
    gm, g_idx = first_argmax(lg)
    g_w = [1.0 / jnp.sum(jnp.exp(x - m), axis=-1, keepdims=True)
           for x, m in zip(lg, gm)]
    lo = [N_GROUPS + EXPERTS_PER_GROUP * g for g in g_idx]
    le = [jnp.where((lane >= l) & (lane < l + EXPERTS_PER_GROUP), x, neg)
          for x, l in zip(lg_all, lo)]
    v1, i1 = first_argmax(le)
    v2, i2 = first_argmax([jnp.where(lane == i, neg, x) for x, i in zip(le, i1)])

    @pl.when((pl.program_id(0) == 0) & (pl.program_id(1) == 0))
    def _():
        cnt_ref[...] = jnp.zeros_like(cnt_ref)

    e_lane = lane + N_GROUPS
    hits = sum(jnp.sum(jnp.where((e_lane == a) | (e_lane == b), 1.0, 0.0), axis=0, keepdims=True)
               for a, b in zip(i1, i2))
    cnt_ref[...] = cnt_ref[...] + hits.astype(I32)

    for blk in range(nblk):
        e2 = jnp.exp(v2[blk] - v1[blk])
        w1 = g_w[blk] / (1.0 + e2)
        w2 = g_w[blk] * e2 / (1.0 + e2)
        sl = blocks[blk]
        rid_ref[sl, :] = jnp.where(lane == 0, i1[blk] - N_GROUPS,
                                   jnp.where(lane == 1, i2[blk] - N_GROUPS, 0))
        rw_ref[sl, :] = jnp.where(lane == 0, w1, jnp.where(lane == 1, w2, 0.0))


def _mix(x, y_lru, y_rw, w_out, g1, sh2, sc2, ng, w_grp, b_grp, w_exp, b_exp, tm=256):
    bsz, t, d = x.shape
    wo = w_out.astype(BF16)
    wl = LRU_WIDTH
    wr = jnp.zeros((d, LANES), F32).at[:, :N_GROUPS].set(w_grp)
    wr = wr.at[:, N_GROUPS:N_GROUPS + N_EXPERTS].set(w_exp)
    wr = jnp.concatenate(_split(wr, 2), axis=0)
    br = jnp.zeros((1, LANES), F32).at[0, :N_GROUPS].set(b_grp)
    br = br.at[0, N_GROUPS:N_GROUPS + N_EXPERTS].set(b_exp)
    vec = pl.BlockSpec((None, 1, d), lambda b, i: (b, 0, 0))
    const = lambda shape: pl.BlockSpec(shape, lambda b, i: (0, 0))
    row = lambda n: pl.BlockSpec((None, tm, n), lambda b, i: (b, i, 0))
    return pl.pallas_call(
        _mix_kernel,
        grid=(bsz, t // tm),
        in_specs=[row(d), row(wl), row(d - wl), const((wl, d)), const((d - wl, d)),
                  vec, vec, vec, const((1, d)), const((2 * d, LANES)), const((1, LANES))],
        out_specs=[row(d), row(d // 2), row(LANES), row(LANES), const((SUBLANES, LANES))],
        out_shape=[jax.ShapeDtypeStruct((bsz, t, d), F32),
                   jax.ShapeDtypeStruct((bsz, t, d // 2), I32),
                   jax.ShapeDtypeStruct((bsz, t, LANES), I32),
                   jax.ShapeDtypeStruct((bsz, t, LANES), F32),
                   jax.ShapeDtypeStruct((SUBLANES, LANES), I32)],
        compiler_params=_cparams(("arbitrary", "arbitrary")),
        name="outproj_adaln2_router",
    )(x, y_lru, y_rw, wo[:wl], wo[wl:], g1, sh2, sc2, ng.reshape(1, d), wr, br)


def _plan_kernel(rid_ref, cnt_ref, tril_ref, pos_ref, te_ref, base_ref):
    i = pl.program_id(0)
    tp = rid_ref.shape[0]
    lane = lax.broadcasted_iota(I32, (tp, LANES), 1)
    rid = rid_ref[...]
    oh0 = lane == rid[:, 0:1]
    oh1 = lane == rid[:, 1:2]
    oh = jnp.where(oh0 | oh1, 1.0, 0.0)

    @pl.when(i == 0)
    def _():
        cnt = cnt_ref[...]
        shift = EXPERT_TILE.bit_length() - 1
        padded = ((cnt + (EXPERT_TILE - 1)) >> shift) << shift
        l8 = lax.broadcasted_iota(I32, (SUBLANES, LANES), 1)
        end = padded
        s = 1
        while s < LANES:
            end = end + jnp.where(l8 >= s, pltpu.roll(end, s, 1), 0)
            s *= 2
        base_ref[...] = end - padded
        nt = te_ref.shape[0]
        j = lax.broadcasted_iota(I32, (nt, LANES), 0) * EXPERT_TILE
        lt = lax.broadcasted_iota(I32, (nt, LANES), 1)
        done = jnp.where((lt < N_EXPERTS) & (end[0:1, :] <= j), 1.0, 0.0)
        e_of = jnp.minimum(jnp.sum(done, axis=-1, keepdims=True), float(N_EXPERTS - 1)).astype(I32)
        total = end[0:1, N_EXPERTS - 1:N_EXPERTS]
        mine = lt == e_of
        cnt_e = jnp.sum(jnp.where(mine, cnt[0:1, :], 0), axis=-1, keepdims=True)
        start_e = jnp.sum(jnp.where(mine, end[0:1, :] - padded[0:1, :], 0), axis=-1, keepdims=True)
        valid = jnp.clip(cnt_e - (j[:, 0:1] - start_e), 0, EXPERT_TILE)
        te_ref[...] = jnp.where(lt == 0, e_of, jnp.where(lt == 1, total >> shift,
                                                        jnp.where(lt == 2, valid, 0)))

    prefix = jnp.dot(tril_ref[...], oh.astype(BF16), preferred_element_type=F32)
    dest = base_ref[0:1, :] + prefix.astype(I32)
    pos0 = jnp.sum(jnp.where(oh0, dest, 0), axis=-1, keepdims=True)
    pos1 = jnp.sum(jnp.where(oh1, dest, 0), axis=-1, keepdims=True)
    pos_ref[...] = jnp.where(lane == 0, pos0, jnp.where(lane == 1, pos1, 0))
    base_ref[...] = base_ref[...] + jnp.sum(oh, axis=0, keepdims=True).astype(I32)


def _plan(rid, cnt, n_tiles, tp=512):
    n = rid.shape[0]
    tp = min(tp, n)
    nt_pad = -(-n_tiles // SUBLANES) * SUBLANES
    tril = (jnp.arange(tp)[:, None] > jnp.arange(tp)[None, :]).astype(BF16)
    pos, te = pl.pallas_call(
        _plan_kernel,
        grid=(n // tp,),
        in_specs=[pl.BlockSpec((tp, LANES), lambda i: (i, 0)),
                  pl.BlockSpec((SUBLANES, LANES), lambda i: (0, 0)),
                  pl.BlockSpec((tp, tp), lambda i: (0, 0))],
        out_specs=[pl.BlockSpec((tp, LANES), lambda i: (i, 0)),
                   pl.BlockSpec((nt_pad, LANES), lambda i: (0, 0))],
        out_shape=[jax.ShapeDtypeStruct((n, LANES), I32),
                   jax.ShapeDtypeStruct((nt_pad, LANES), I32)],
        scratch_shapes=[pltpu.VMEM((SUBLANES, LANES), I32)],
        compiler_params=_cparams(("arbitrary",)),
        name="route_plan",
    )(rid, cnt, tril)
    return pos, te


SC_CORES = 2
SC_SUBCORES = 16
SC_WORKERS = SC_CORES * SC_SUBCORES
SC_WINDOW = 64


def _sc_mesh():
    return plsc.VectorSubcoreMesh(core_axis_name="c", subcore_axis_name="s",
                                  num_cores=SC_CORES, num_subcores=SC_SUBCORES)


def _sc_two_buffer_loop(n_win, fill, drain):
    assert n_win % 2 == 0

    def start(copies):
        for cp in copies:
            cp.start()

    def wait(copies):
        for cp in copies:
            cp.wait()

    start(fill(0, 0))

    @pl.loop(0, n_win, step=2)
    def _(j):
        for b in range(2):
            jj = j + b

            @pl.when(jj >= 1)
            def _():
                wait(drain(jj - 1, 1 - b))

            @pl.when(jj + 1 < n_win)
            def _():
                start(fill(jj + 1, 1 - b))

            wait(fill(jj, b))
            start(drain(jj, b))

    wait(drain(n_win - 1, 1))


def _sc_scatter_rows(rows, pos0, pos1, n_out):
    n, d = rows.shape
    per_w = n // SC_WORKERS
    n_win = per_w // SC_WINDOW
    shape3 = (SC_WORKERS, n_win, SC_WINDOW)

    @functools.partial(
        pl.kernel, mesh=_sc_mesh(), out_type=jax.ShapeDtypeStruct((n_out, d), rows.dtype),
        scratch_types=[pltpu.VMEM((n_win, SC_WINDOW), I32), pltpu.VMEM((n_win, SC_WINDOW), I32),
                       pltpu.VMEM((2, SC_WINDOW, d), rows.dtype),
                       pltpu.SemaphoreType.DMA((2,)), pltpu.SemaphoreType.DMA((2,))],
        name="sc_scatter_rows")
    def scatter(rows_hbm, p0_hbm, p1_hbm, out_hbm, p0_v, p1_v, buf, in_sem, out_sem):
        wid = lax.axis_index("s") * SC_CORES + lax.axis_index("c")
        base = wid * per_w
        pltpu.sync_copy(p0_hbm.at[wid], p0_v)
        pltpu.sync_copy(p1_hbm.at[wid], p1_v)

        def fill(j, b):
            src = rows_hbm.at[pl.ds(base + j * SC_WINDOW, SC_WINDOW)]
            return [pltpu.make_async_copy(src, buf.at[b], in_sem.at[b])]

        def drain(j, b):
            return [pltpu.make_async_copy(buf.at[b], out_hbm.at[p.at[j]], out_sem.at[b])
                    for p in (p0_v, p1_v)]

        _sc_two_buffer_loop(n_win, fill, drain)

    return scatter(rows, pos0.reshape(shape3), pos1.reshape(shape3))


def _sc_gather_rows(table, idx):
    m = idx.shape[0]
    d = table.shape[1]
    per_w = m // SC_WORKERS
    n_win = per_w // SC_WINDOW

    @functools.partial(
        pl.kernel, mesh=_sc_mesh(), out_type=jax.ShapeDtypeStruct((m, d), table.dtype),
        scratch_types=[pltpu.VMEM((n_win, SC_WINDOW), I32),
                       pltpu.VMEM((2, SC_WINDOW, d), table.dtype),
                       pltpu.SemaphoreType.DMA((2,)), pltpu.SemaphoreType.DMA((2,))],
        name="sc_gather_rows")
    def gather(table_hbm, idx_hbm, out_hbm, idx_v, buf, in_sem, out_sem):
        wid = lax.axis_index("s") * SC_CORES + lax.axis_index("c")
        base = wid * per_w
        pltpu.sync_copy(idx_hbm.at[wid], idx_v)

        def fill(j, b):
            return [pltpu.make_async_copy(table_hbm.at[idx_v.at[j]], buf.at[b], in_sem.at[b])]

        def drain(j, b):
            dst = out_hbm.at[pl.ds(base + j * SC_WINDOW, SC_WINDOW)]
            return [pltpu.make_async_copy(buf.at[b], dst, out_sem.at[b])]

        _sc_two_buffer_loop(n_win, fill, drain)

    return gather(table, idx.reshape(SC_WORKERS, n_win, SC_WINDOW))


def _expert_kernel(te_ref, nu_ref, valid_ref, x_ref, w1_ref, w3_ref, w2_ref, o_ref,
                   w1b_ref, w3b_ref, w2b_ref):
    j = pl.program_id(0)
    active = j < nu_ref[0]

    @pl.when(active & ((j == 0) | (te_ref[j] != te_ref[jnp.maximum(j - 1, 0)])))
    def _():
        w1b_ref[...] = w1_ref[...].astype(BF16)
        w3b_ref[...] = w3_ref[...].astype(BF16)
        w2b_ref[...] = w2_ref[...].astype(BF16)

    @pl.when(active)
    def _():
        nblk = 2
        rows = x_ref.shape[0] // nblk
        blocks = [slice(i * rows, (i + 1) * rows) for i in range(nblk)]
        row = lax.broadcasted_iota(I32, (rows, x_ref.shape[1]), 0)
        xs = [_unpack_bf16_halves(
            jnp.where(row + i * rows < valid_ref[j], x_ref[sl, :], 0)).astype(BF16)
            for i, sl in enumerate(blocks)]
        h1 = [jnp.dot(x, w1b_ref[...], preferred_element_type=F32) for x in xs]
        h3 = [jnp.dot(x, w3b_ref[...], preferred_element_type=F32) for x in xs]
        hid = [(a * jax.nn.sigmoid(a) * b).astype(BF16) for a, b in zip(h1, h3)]
        for sl, hb in zip(blocks, hid):
            o_ref[sl, :] = _pack_bf16_halves(
                jnp.dot(hb, w2b_ref[...], preferred_element_type=F32))

    @pl.when(j >= nu_ref[0])
    def _():
        o_ref[...] = jnp.zeros_like(o_ref)


def _experts(xs, te, nu, valid, w1, w3, w2):
    n_rows = xs.shape[0]
    d, de = w1.shape[-2:]
    nt = n_rows // EXPERT_TILE
    used = lambda j, nu: jnp.minimum(j, nu[0] - 1)
    grid_spec = pltpu.PrefetchScalarGridSpec(
        num_scalar_prefetch=3,
        grid=(nt,),
        in_specs=[
            pl.BlockSpec((EXPERT_TILE, d // 2), lambda j, te, nu, va: (used(j, nu), 0)),
            pl.BlockSpec((None, d, de), lambda j, te, nu, va: (te[used(j, nu)], 0, 0)),
            pl.BlockSpec((None, d, de), lambda j, te, nu, va: (te[used(j, nu)], 0, 0)),
            pl.BlockSpec((None, de, d), lambda j, te, nu, va: (te[used(j, nu)], 0, 0)),
        ],
        out_specs=pl.BlockSpec((EXPERT_TILE, d // 2), lambda j, te, nu, va: (j, 0)),
        scratch_shapes=[pltpu.VMEM((d, de), BF16), pltpu.VMEM((d, de), BF16),
                        pltpu.VMEM((de, d), BF16)],
    )
    return pl.pallas_call(
        _expert_kernel,
        grid_spec=grid_spec,
        out_shape=jax.ShapeDtypeStruct((n_rows, d // 2), I32),
        compiler_params=_cparams(("arbitrary",)),
        name="expert_mlp",
    )(te, nu, valid, xs, w1, w3, w2)


def _combine_kernel(x1_ref, rw_ref, g2_ref, fg_ref, y0_ref, y1_ref, o_ref):
    rw = rw_ref[...]
    moe = (rw[:, 0:1] * _unpack_bf16_halves(y0_ref[...])
           + rw[:, 1:2] * _unpack_bf16_halves(y1_ref[...]))
    x2 = x1_ref[...] + g2_ref[...] * moe
    ms = jnp.mean(x2 * x2, axis=-1, keepdims=True)
    o_ref[...] = x2 * lax.rsqrt(ms + RMS_EPS) * fg_ref[...]


def _combine(x1, rw, yg, g2, final_g, tc):
    bsz, t, d = x1.shape
    row = lambda n: pl.BlockSpec((None, tc, n), lambda b, i: (b, i, 0))
    slot = lambda s: pl.BlockSpec((None, None, tc, d // 2), lambda b, i: (s, b, i, 0))
    return pl.pallas_call(
        _combine_kernel,
        grid=(bsz, t // tc),
        in_specs=[row(d), row(LANES),
                  pl.BlockSpec((None, 1, d), lambda b, i: (b, 0, 0)),
                  pl.BlockSpec((1, d), lambda b, i: (0, 0)),
                  slot(0), slot(1)],
        out_specs=row(d),
        out_shape=jax.ShapeDtypeStruct((bsz, t, d), F32),
        compiler_params=_cparams(("arbitrary", "arbitrary")),
        name="combine_final_norm",
    )(x1, rw, g2, final_g.reshape(1, d), yg, yg)


def _row_tile(t, want):
    return want if t % want == 0 else t


def kernel(x, c, w_ada, b_ada, norm1_g, w_in, conv_w, conv_b, lru_wa, lru_ba, lru_wi, lru_bi, lru_lam, lru_norm_g, tok_mu, w0, w_up, a0, a_up, g_up, k_k, k_a, r_k, ln_x_w, ln_x_b, w_out, norm2_g, w_grp, b_grp, w_exp, b_exp, w1, w3, w2, final_g):
    bsz, t, d = x.shape
    n = bsz * t
    depth = w_ada.shape[0]
    assert depth == 1, "the combine kernel applies the final norm: only DEPTH == 1 is wired"
    tile = _row_tile(t, ROW_TILE)
    n_tiles = (n * TOP_K) // EXPERT_TILE + N_EXPERTS
    for l in range(depth):
        mod = _modulation(c, w_ada[l], b_ada[l]).reshape(bsz, 6, 1, d)
        sh1, sc1, g1, sh2, sc2, g2 = (mod[:, i] for i in range(6))
        y_lru, p_rw = _inproj_lru(x, sh1, sc1, norm1_g[l], w_in[l], conv_w[l], conv_b[l],
                                  lru_wa[l], lru_ba[l], lru_wi[l], lru_bi[l], lru_lam[l],
                                  lru_norm_g[l], tm=tile)
        y_rw = _rwkv7(p_rw, tok_mu[l], w0[l], w_up[l], a0[l], a_up[l], g_up[l], k_k[l], k_a[l],
                      r_k[l], ln_x_w[l], ln_x_b[l])
        x1, h2, rid, rw, cnt = _mix(x, y_lru, y_rw, w_out[l], g1, sh2, sc2, norm2_g[l],
                                    w_grp[l], b_grp[l], w_exp[l], b_exp[l], tm=tile)
        pos, te = _plan(rid.reshape(n, LANES), cnt, n_tiles)
        pos0, pos1 = pos[:, 0], pos[:, 1]
        xs = _sc_scatter_rows(h2.reshape(n, d // 2), pos0, pos1, n_tiles * EXPERT_TILE)
        ys = _experts(xs, te[:n_tiles, 0], te[0:1, 1], te[:n_tiles, 2], w1[l], w3[l], w2[l])
        yg = _sc_gather_rows(ys, jnp.concatenate([pos0, pos1]))
        x = _combine(x1, rw, yg.reshape(TOP_K, bsz, t, d // 2), g2, final_g,
                     _row_tile(t, COMBINE_TILE))
    return x
```

```python
import functools

import jax
import jax.numpy as jnp
from jax import lax
from jax.experimental import pallas as pl
from jax.experimental.pallas import tpu as pltpu
from jax.experimental.pallas import tpu_sc as plsc

F32 = jnp.float32
BF16 = jnp.bfloat16
I32 = jnp.int32

LRU_WIDTH = 512
LRU_HEAD_DIM = 64
CONV_WIDTH = 4
LRU_C = 8.0
RWKV_WIDTH = 512
HEAD_DIM = 64
DECAY_LORA = 64
AAA_LORA = 64
GATE_LORA = 128
RWKV_PROJ = 3 * RWKV_WIDTH + DECAY_LORA + AAA_LORA + GATE_LORA
N_GROUPS = 4
EXPERTS_PER_GROUP = 8
N_EXPERTS = N_GROUPS * EXPERTS_PER_GROUP
TOP_K = 2
RMS_EPS = 1e-6
GN_EPS = 64e-5

LANES = 128
SUBLANES = 8
CHUNK = 64
EXPERT_TILE = 512
ROW_TILE = 256
COMBINE_TILE = 512
MOE_PARTS = 2
VMEM_LIMIT = 48 * 1024 * 1024

NN = (((1,), (0,)), ((), ()))
NT = (((1,), (1,)), ((), ()))
TN = (((0,), (0,)), ((), ()))


def _split(x, n):
    if x.dtype == BF16:
        return [x]
    parts = []
    rem = x
    for i in range(n):
        p = rem.astype(BF16)
        parts.append(p)
        if i + 1 < n:
            rem = rem - p.astype(F32)
    return parts


def _mm(a, b, dn=NN, pa=1, pb=1):
    aps = _split(a, pa)
    bps = _split(b, pb)
    order = max(len(aps), len(bps))
    terms = [(i, j) for i in range(len(aps)) for j in range(len(bps)) if i + j < order]
    ka = dn[0][0][0]
    kb = dn[0][1][0]
    if len(terms) > 1 and a.shape[ka] % LANES == 0:
        a_cat = jnp.concatenate([aps[i] for i, _ in terms], axis=ka)
        b_cat = jnp.concatenate([bps[j] for _, j in terms], axis=kb)
        return lax.dot_general(a_cat, b_cat, dn, preferred_element_type=F32)
    out = None
    for i, j in terms:
        t = lax.dot_general(aps[i], bps[j], dn, preferred_element_type=F32)
        out = t if out is None else out + t
    return out


def _pack_bf16_halves(x):
    n = x.shape[1] // 2
    bits = lax.bitcast_convert_type(x.astype(BF16).astype(F32), I32)
    return bits[:, n:] | ((bits[:, :n] >> 16) & 0xFFFF)


def _unpack_bf16_halves(p):
    lo = lax.bitcast_convert_type(p << 16, F32)
    hi = lax.bitcast_convert_type(p & (-65536), F32)
    return jnp.concatenate([lo, hi], axis=1)


def _softplus(x):
    return jnp.maximum(x, 0.0) + jnp.log1p(jnp.exp(-jnp.abs(x)))


def _cparams(sem):
    return pltpu.CompilerParams(dimension_semantics=sem, vmem_limit_bytes=VMEM_LIMIT)


def _mod_kernel(c_ref, w_ref, b_ref, o_ref):
    c = c_ref[...]
    s = c * jax.nn.sigmoid(c)
    o_ref[...] = _mm(s, w_ref[...], pa=2, pb=2) + b_ref[...]


def _modulation(c, w_ada, b_ada):
    bsz, d = c.shape
    n_out = w_ada.shape[1]
    rows = -(-bsz // SUBLANES) * SUBLANES
    c_pad = jnp.zeros((rows, d), F32).at[:bsz].set(c)
    bn = d
    out = pl.pallas_call(
        _mod_kernel,
        grid=(n_out // bn,),
        in_specs=[
            pl.BlockSpec((rows, d), lambda j: (0, 0)),
            pl.BlockSpec((d, bn), lambda j: (0, j)),
            pl.BlockSpec((1, bn), lambda j: (0, j)),
        ],
        out_specs=pl.BlockSpec((rows, bn), lambda j: (0, j)),
        out_shape=jax.ShapeDtypeStruct((rows, n_out), F32),
        compiler_params=_cparams(("arbitrary",)),
        name="adaln_mod",
    )(c_pad, w_ada, b_ada.reshape(1, n_out))
    return out[:bsz]


def _adaln(x, g, shift, scale):
    ms = jnp.mean(x * x, axis=-1, keepdims=True)
    y = x * lax.rsqrt(ms + RMS_EPS) * g
    return y * (1.0 + scale) + shift


def _inproj_lru_kernel(x_ref, sh_ref, sc_ref, g_ref, wl_ref, wr_ref,
                       cw_ref, cb_ref, wab_ref, bab_ref, lam_ref, ng_ref, ones_ref,
                       yl_ref, pr_ref, xprev_ref, hprev_ref):
    @pl.when(pl.program_id(1) == 0)
    def _():
        xprev_ref[...] = jnp.zeros_like(xprev_ref)
        hprev_ref[...] = jnp.zeros_like(hprev_ref)

    w = LRU_WIDTH
    tm = x_ref.shape[0]
    nblk = 2
    rows = tm // nblk
    blocks = [slice(i * rows, (i + 1) * rows) for i in range(nblk)]
    h = [_adaln(x_ref[sl, :], g_ref[...], sh_ref[...], sc_ref[...]).astype(BF16)
         for sl in blocks]
    ux = [jnp.dot(hb, wl_ref[:, :w], preferred_element_type=F32) for hb in h]

    def remaining_columns():
        for sl, hb in zip(blocks, h):
            pr_ref[sl, :] = jnp.dot(hb, wr_ref[...], preferred_element_type=F32)
        return [jnp.dot(hb, wl_ref[:, w:], preferred_element_type=F32) for hb in h]

    y = _lru_tile(ux, remaining_columns, cw_ref, cb_ref, wab_ref, bab_ref, lam_ref, ng_ref,
                  ones_ref, xprev_ref, hprev_ref)
    for sl, yb in zip(blocks, y):
        yl_ref[sl, :] = yb.astype(yl_ref.dtype)


def _inproj_lru(x, sh1, sc1, g, w_in, conv_w, conv_b, wa, ba, wi, bi, lam, norm_g, tm=256):
    bsz, t, d = x.shape
    w = LRU_WIDTH
    nl = 2 * w
    nr = RWKV_PROJ
    wl = w_in[:, :nl].astype(BF16)
    wr = w_in[:, nl:].astype(BF16)
    wab = jnp.concatenate([_block_diag(wa), _block_diag(wi)], axis=1).astype(BF16)
    bab = jnp.concatenate([ba.reshape(1, w), bi.reshape(1, w)], axis=1)
    vec = pl.BlockSpec((None, 1, d), lambda b, i: (b, 0, 0))
    const = lambda shape: pl.BlockSpec(shape, lambda b, i: (0, 0))
    return pl.pallas_call(
        _inproj_lru_kernel,
        grid=(bsz, t // tm),
        in_specs=[
            pl.BlockSpec((None, tm, d), lambda b, i: (b, i, 0)),
            vec, vec, const((1, d)), const((d, nl)), const((d, nr)),
            const((CONV_WIDTH, w)), const((1, w)), const((w, 2 * w)), const((1, 2 * w)),
            const((1, w)), const((1, w)), const((w, w)),
        ],
        out_specs=[
            pl.BlockSpec((None, tm, w), lambda b, i: (b, i, 0)),
            pl.BlockSpec((None, tm, nr), lambda b, i: (b, i, 0)),
        ],
        out_shape=[
            jax.ShapeDtypeStruct((bsz, t, w), BF16),
            jax.ShapeDtypeStruct((bsz, t, nr), F32),
        ],
        scratch_shapes=[pltpu.VMEM((SUBLANES, w), F32), pltpu.VMEM((SUBLANES, w), F32)],
        compiler_params=_cparams(("arbitrary", "arbitrary")),
        name="adaln1_inproj_rglru",
    )(x, sh1, sc1, g.reshape(1, d), wl, wr, conv_w, conv_b.reshape(1, w), wab, bab,
      lam.reshape(1, w), norm_g.reshape(1, w), _head_ones(w, LRU_HEAD_DIM))


def _gelu_tanh(x):
    c = 0.7978845608028654
    return x * (0.5 * (1.0 + jnp.tanh(c * (x + 0.044715 * (x * x * x)))))


def _sigmoid(x):
    return 0.5 * jnp.tanh(0.5 * x) + 0.5


def _lru_tile(ux, gate_branch, cw_ref, cb_ref, wab_ref, bab_ref, lam_ref, ng_ref, ones_ref,
              xprev_ref, hprev_ref):
    w = LRU_WIDTH
    rows = ux[0].shape[0]
    prev8 = [xprev_ref[...]] + [u[rows - SUBLANES:, :] for u in ux[:-1]]
    ext = [jnp.concatenate([p, u], axis=0) for p, u in zip(prev8, ux)]
    xc = [cb_ref[...] + cw_ref[CONV_WIDTH - 1:CONV_WIDTH, :] * u for u in ux]
    for k in range(1, CONV_WIDTH):
        tap = cw_ref[CONV_WIDTH - 1 - k:CONV_WIDTH - k, :]
        xc = [x + tap * pltpu.roll(e, k, 0)[SUBLANES:, :] for x, e in zip(xc, ext)]
    xprev_ref[...] = ux[-1][rows - SUBLANES:, :]

    gates = [jnp.dot(x.astype(BF16), wab_ref[...], preferred_element_type=F32) + bab_ref[...]
             for x in xc]
    ug = gate_branch()
    r = [_sigmoid(g[:, :w]) for g in gates]
    ig = [_sigmoid(g[:, w:]) for g in gates]
    sp = _softplus(-lam_ref[...])
    log_a = [(-LRU_C) * x * sp for x in r]
    a = [jnp.exp(x) for x in log_a]
    th = [jnp.tanh(x) for x in log_a]
    q = [(-2.0 * x) / (1.0 - x) for x in th]
    root_q = [jnp.where(x > 0.0, x * lax.rsqrt(x), 0.0) for x in q]
    b = [s * (i * x) for s, i, x in zip(root_q, ig, xc)]

    row8 = lax.broadcasted_iota(I32, (rows, w), 0) & (SUBLANES - 1)
    acc_a, acc_b = a, b
    for s in (1, 2, 4):
        live = row8 >= s
        sh_a = [pltpu.roll(x, s, 0) for x in acc_a]
        sh_b = [pltpu.roll(x, s, 0) for x in acc_b]
        acc_b = [jnp.where(live, x * sb + y, y) for x, sb, y in zip(acc_a, sh_b, acc_b)]
        acc_a = [jnp.where(live, x * sa, x) for x, sa in zip(acc_a, sh_a)]
    carry = hprev_ref[SUBLANES - 1:SUBLANES, :]
    h = []
    for xa, xb in zip(acc_a, acc_b):
        groups = []
        for gi in range(rows // SUBLANES):
            lo = gi * SUBLANES
            groups.append(xa[lo:lo + SUBLANES, :] * carry + xb[lo:lo + SUBLANES, :])
            last = lo + SUBLANES - 1
            carry = xa[last:last + 1, :] * carry + xb[last:last + 1, :]
        h.append(jnp.concatenate(groups, axis=0))
    hprev_ref[...] = h[-1][rows - SUBLANES:, :]

    y = [x * _gelu_tanh(g) for x, g in zip(h, ug)]
    ms = [_mm(x * x, ones_ref[...]) * (1.0 / LRU_HEAD_DIM) for x in y]
    return [x * lax.rsqrt(m + RMS_EPS) * ng_ref[...] for x, m in zip(y, ms)]


def _block_diag(w):
    h, n, _ = w.shape
    eye = jnp.eye(h, dtype=w.dtype)
    return (eye[:, None, :, None] * w[:, :, None, :]).reshape(h * n, h * n)


def _head_ones(width, head):
    idx = jnp.arange(width) // head
    return (idx[:, None] == idx[None, :]).astype(BF16)


PG = (1, 1)
PI = (1, 1)
PS = (1, 1)
PH = (1, 2)


def _unit_lower_inverse(a_list, ri, ci):
    mm = functools.partial(_mm, pa=PI[0], pb=PI[1])
    eye = jnp.where(ri == ci, 1.0, 0.0)
    leaf = (ri >> 3) == (ci >> 3)
    a8 = [jnp.where(leaf, a, 0.0) for a in a_list]
    a8_2 = [mm(x, x) for x in a8]
    a8_4 = [mm(x, x) for x in a8_2]
    t = [mm(eye + x, eye + y) for x, y in zip(a8, a8_2)]
    t = [mm(x, eye + y) for x, y in zip(t, a8_4)]
    zero = jnp.zeros((LANES, LANES), F32)
    for sh in (3, 4, 5):
        s = 1 << sh
        off = ((ri >> (sh + 1)) == (ci >> (sh + 1))) & ((ri >> sh) != (ci >> sh))
        t_lo = [_second_blocks(x, s) for x in t]
        b_lo = [mm(_second_blocks(jnp.where(off, a, 0.0), s), x) for a, x in zip(a_list, t)]
        d_lo = [mm(x, _interleave_blocks(zero, y, s)) for x, y in zip(t_lo, b_lo)]
        t = [_interleave_blocks(x, y + z, s) for x, y, z in zip(t, t_lo, d_lo)]
    return t


def _second_blocks(x, s):
    return jnp.concatenate(
        [x[s * (2 * m + 1):s * (2 * m + 2)] for m in range(x.shape[0] // (2 * s))], axis=0)


def _interleave_blocks(first_src, second, s):
    parts = []
    for m in range(first_src.shape[0] // (2 * s)):
        parts.append(first_src[2 * s * m:2 * s * m + s])
        parts.append(second[s * m:s * (m + 1)])
    return jnp.concatenate(parts, axis=0)


def _rwkv_kernel(p_ref, mu_ref, pv_ref, wlo_ref, gup_ref, o_ref, uprev_ref, h_ref):
    w = RWKV_WIDTH
    tt = p_ref.shape[0]
    c = CHUNK
    n_pairs = w // LANES
    units = [(j, p) for j in range(tt // c) for p in range(n_pairs)]

    @pl.when(pl.program_id(1) == 0)
    def _():
        uprev_ref[...] = jnp.zeros_like(uprev_ref)
        h_ref[...] = jnp.zeros_like(h_ref)

    u = p_ref[...]
    ext = jnp.concatenate([uprev_ref[...], u], axis=0)
    prev = pltpu.roll(ext, 1, 0)[SUBLANES:, :]
    uprev_ref[...] = u[tt - SUBLANES:, :]
    um = u + (prev - u) * mu_ref[...]

    r = um[:, 0:w]
    k = um[:, w:2 * w]
    v = um[:, 2 * w:3 * w]
    z = um[:, 3 * w:3 * w + LANES]
    gd = um[:, 3 * w + LANES:]
    w0 = pv_ref[0:1, :]
    a0 = pv_ref[1:2, :]
    k_k = pv_ref[2:3, :]
    k_a = pv_ref[3:4, :]
    r_k = pv_ref[4:5, :]
    ln_w = pv_ref[5:6, :]
    ln_b = pv_ref[6:7, :]

    lane_t = lax.broadcasted_iota(I32, (tt, LANES), 1)
    zz = jnp.where(lane_t < HEAD_DIM, jnp.tanh(z), z)
    lora = _mm(zz, wlo_ref[...], pa=2, pb=2)
    wlog = -_softplus(-(w0 + lora[:, :w])) - 0.5
    ld = -jnp.exp(wlog)
    a = jax.nn.sigmoid(a0 + lora[:, w:])
    g = _mm(jax.nn.sigmoid(gd), gup_ref[...], pa=2, pb=2)

    ri = lax.broadcasted_iota(I32, (LANES, LANES), 0)
    ci = lax.broadcasted_iota(I32, (LANES, LANES), 1)
    even = lax.broadcasted_iota(I32, (c, LANES), 1) < HEAD_DIM
    wide = 2 * LANES
    ones_bd = jnp.where(
        (lax.broadcasted_iota(I32, (wide, wide), 0) >> 6)
        == (lax.broadcasted_iota(I32, (wide, wide), 1) >> 6), 1.0, 0.0).astype(BF16)

    def head_sum(x):
        return jnp.concatenate(
            [_mm(x[:, q * wide:(q + 1) * wide], ones_bd) for q in range(w // wide)], axis=1)

    kk = k * k_k
    kk = kk / jnp.maximum(jnp.sqrt(head_sum(kk * kk)), 1e-12)
    kp = k * (1.0 + (a - 1.0) * k_a)
    kka = kk * a
    bonus = head_sum(r * kp * r_k) * v
    row_in_chunk = lax.broadcasted_iota(I32, (tt, w), 0) & (c - 1)
    cum = ld
    s = 1
    while s < c:
        cum = cum + jnp.where(row_in_chunk >= s, pltpu.roll(cum, s, 0), 0.0)
        s *= 2
    cum_c = [cum[j * c + c - 1:j * c + c, :] for j in range(tt // c)]
    p_c = [jnp.exp(x) for x in cum_c]
    p_inv = jnp.exp(-cum)
    p_hat = jnp.exp(jnp.concatenate(
        [cum_c[j] - cum[j * c:(j + 1) * c, :] for j in range(tt // c)], axis=0))
    al = -kk * jnp.exp(cum - ld)
    rt = r * jnp.exp(cum)
    bt = kka * p_inv
    kt = kp * p_inv
    bh = kka * p_hat
    kh = kp * p_hat

    def blk(x):
        return [x[j * c:(j + 1) * c, p * LANES:(p + 1) * LANES] for j, p in units]

    def halves(x):
        xs = blk(x)
        return [jnp.where(even, y, 0.0) for y in xs], [jnp.where(even, 0.0, y) for y in xs]

    def rows(top, bot):
        return [jnp.concatenate([x, y], axis=0) for x, y in zip(top, bot)]

    def unstack(x):
        return x[:c, :] + x[c:, :]

    al_e, al_o = halves(al)
    rt_e, rt_o = halves(rt)
    bt_e, bt_o = halves(bt)
    kt_e, kt_o = halves(kt)
    bh_e, bh_o = halves(bh)
    kh_e, kh_o = halves(kh)
    v_e, v_o = halves(v)
    al_n = rows(al_e, al_o)
    bh_n = rows(bh_e, bh_o)
    v_s = rows(v_o, v_e)
    kh_s = rows(kh_o, kh_e)
    rt_b = blk(rt)
    nu = range(len(units))

    g0 = [_mm(x, y, NT, pa=PG[0], pb=PG[1])
          for x, y in zip(rows(al_e, rt_e), rows(bt_e, kt_e))]
    g1 = [_mm(x, y, NT, pa=PG[0], pb=PG[1])
          for x, y in zip(rows(rt_o, al_o), rows(kt_o, bt_o))]
    top = ri < c
    left = ci < c
    tri_s = (ri & (c - 1)) > (ci & (c - 1))
    tri_i = (ri & (c - 1)) >= (ci & (c - 1))
    diag_q = top == left

    def pick(x0, x1, in_q0, tri):
        return [jnp.where(tri, jnp.where(in_q0, x, y), 0.0) for x, y in zip(x0, x1)]

    a_ab = pick(g0, g1, top, diag_q & tri_s)
    a_rk = pick(g1, g0, top, diag_q & tri_i)
    a_ak = pick(g0, g1, top, (~diag_q) & tri_s)
    a_rb = pick(g1, g0, top, (~diag_q) & tri_i)
    x1 = [_mm(a_ak[n], v_s[n], pa=PS[0], pb=PS[1]) for n in nu]
    akv = [_mm(a_rk[n], v_s[n], pa=PS[0], pb=PS[1]) for n in nu]
    khv = [_mm(kh_s[n], v_s[n], TN, pa=PS[0], pb=PS[1]) for n in nu]
    t_inv = _unit_lower_inverse(a_ab, ri, ci)
    tw = [_mm(t_inv[n], jnp.concatenate([al_n[n], x1[n]], axis=1), pa=PS[0], pb=PS[1])
          for n in nu]
    qo = [_mm(a_rb[n], tw[n], pa=PS[0], pb=PS[1]) for n in nu]
    mn = [_mm(bh_n[n], tw[n], TN, pa=PS[0], pb=PS[1]) for n in nu]
    q = [rt_b[n] + unstack(qo[n][:, :LANES]) for n in nu]
    o_loc = [unstack(qo[n][:, LANES:] + akv[n]) for n in nu]
    m_full = [mn[n][:, :LANES]
              + jnp.where(ri == ci, p_c[j][:, p * LANES:(p + 1) * LANES], 0.0)
              for n, (j, p) in enumerate(units)]
    n_loc = [mn[n][:, LANES:] + khv[n] for n in nu]

    h = [h_ref[p] for p in range(n_pairs)]
    o_rows = []
    for j in range(tt // c):
        o_parts = []
        for p in range(n_pairs):
            n = j * n_pairs + p
            o_parts.append(_mm(q[n], h[p], pa=PH[0], pb=PH[1]) + o_loc[n])
            h[p] = _mm(m_full[n], h[p], pa=PH[0], pb=PH[1]) + n_loc[n]
        o_rows.append(jnp.concatenate(o_parts, axis=1))
    for p in range(n_pairs):
        h_ref[p] = h[p]

    o = jnp.concatenate(o_rows, axis=0)
    mean = head_sum(o) * (1.0 / HEAD_DIM)
    d = o - mean
    var = head_sum(d * d) * (1.0 / HEAD_DIM)
    on = d * lax.rsqrt(var + GN_EPS) * ln_w + ln_b
    o_ref[...] = ((on + bonus) * g).astype(o_ref.dtype)


RWKV_TILE = 256


def _rwkv7(p_rw, mu, w0, w_up, a0, a_up, g_up, k_k, k_a, r_k, ln_w, ln_b):
    bsz, t, npj = p_rw.shape
    w = RWKV_WIDTH
    tt = RWKV_TILE
    pv = jnp.stack([w0, a0, k_k, k_a, r_k.reshape(w), ln_w, ln_b, jnp.zeros((w,), F32)])
    wlo = jnp.zeros((LANES, 2 * w), F32)
    wlo = wlo.at[:DECAY_LORA, :w].set(w_up).at[DECAY_LORA:, w:].set(a_up)
    const = lambda shape: pl.BlockSpec(shape, lambda b, i: (0, 0))
    return pl.pallas_call(
        _rwkv_kernel,
        grid=(bsz, t // tt),
        in_specs=[
            pl.BlockSpec((None, tt, npj), lambda b, i: (b, i, 0)),
            const((1, npj)), const((SUBLANES, w)), const((LANES, 2 * w)),
            const((GATE_LORA, w)),
        ],
        out_specs=pl.BlockSpec((None, tt, w), lambda b, i: (b, i, 0)),
        out_shape=jax.ShapeDtypeStruct((bsz, t, w), BF16),
        scratch_shapes=[pltpu.VMEM((SUBLANES, npj), F32),
                        pltpu.VMEM((w // LANES, LANES, LANES), F32)],
        compiler_params=_cparams(("arbitrary", "arbitrary")),
        name="rwkv7",
    )(p_rw, mu.reshape(1, npj), pv, wlo, g_up)


def _mix_kernel(batches_per_part, x_ref, yl_ref, yr_ref, wo1_ref, wo2_ref, g1_ref, sh_ref,
                sc_ref, ng_ref, wr_ref, br_ref, x1_ref, h2_ref, rid_ref, rw_ref, cnt_ref):
    tm = x_ref.shape[0]
    nblk = 2
    rows = tm // nblk
    blocks = [slice(i * rows, (i + 1) * rows) for i in range(nblk)]
    y = [jnp.dot(yl_ref[sl, :], wo1_ref[...], preferred_element_type=F32)
         + jnp.dot(yr_ref[sl, :], wo2_ref[...], preferred_element_type=F32) for sl in blocks]
    x1 = [x_ref[sl, :] + g1_ref[...] * yb for sl, yb in zip(blocks, y)]
    h2 = [_adaln(xb, ng_ref[...], sh_ref[...], sc_ref[...]).astype(BF16) for xb in x1]
    lg_all = [jnp.dot(jnp.concatenate([hb, hb], axis=1), wr_ref[...],
                      preferred_element_type=F32) + br_ref[...] for hb in h2]
    for sl, xb, hb in zip(blocks, x1, h2):
        x1_ref[sl, :] = xb
        h2_ref[sl, :] = _pack_bf16_halves(hb)

    neg = -jnp.inf
    lane = lax.broadcasted_iota(I32, (rows, LANES), 1)
    lane_f = lane.astype(F32)

    def first_argmax(xs):
        m = [jnp.max(x, axis=-1, keepdims=True) for x in xs]
        idx = [jnp.min(jnp.where(x == mi, lane_f, float(LANES)), axis=-1, keepdims=True)
               for x, mi in zip(xs, m)]
        return m, [i.astype(I32) for i in idx]

    lg = [jnp.where(lane < N_GROUPS, x, neg) for x in lg_all]
    gm, g_idx = first_argmax(lg)
    g_w = [1.0 / jnp.sum(jnp.exp(x - m), axis=-1, keepdims=True)
           for x, m in zip(lg, gm)]
    lo = [N_GROUPS + EXPERTS_PER_GROUP * g for g in g_idx]
    le = [jnp.where((lane >= l) & (lane < l + EXPERTS_PER_GROUP), x, neg)
          for x, l in zip(lg_all, lo)]
    v1, i1 = first_argmax(le)
    v2, i2 = first_argmax([jnp.where(lane == i, neg, x) for x, i in zip(le, i1)])

    @pl.when((pl.program_id(0) % batches_per_part == 0) & (pl.program_id(1) == 0))
    def _():
        cnt_ref[...] = jnp.zeros_like(cnt_ref)

    e_lane = lane + N_GROUPS
    hits = sum(jnp.sum(jnp.where((e_lane == a) | (e_lane == b), 1.0, 0.0), axis=0, keepdims=True)
               for a, b in zip(i1, i2))
    cnt_ref[...] = cnt_ref[...] + hits.astype(I32)

    for blk in range(nblk):
        e2 = jnp.exp(v2[blk] - v1[blk])
        w1 = g_w[blk] / (1.0 + e2)
        w2 = g_w[blk] * e2 / (1.0 + e2)
        sl = blocks[blk]
        rid_ref[sl, :] = jnp.where(lane == 0, i1[blk] - N_GROUPS,
                                   jnp.where(lane == 1, i2[blk] - N_GROUPS, 0))
        rw_ref[sl, :] = jnp.where(lane == 0, w1, jnp.where(lane == 1, w2, 0.0))


def _mix(x, y_lru, y_rw, w_out, g1, sh2, sc2, ng, w_grp, b_grp, w_exp, b_exp, tm, parts):
    bsz, t, d = x.shape
    bpp = bsz // parts
    wo = w_out.astype(BF16)
    wl = LRU_WIDTH
    wr = jnp.zeros((d, LANES), F32).at[:, :N_GROUPS].set(w_grp)
    wr = wr.at[:, N_GROUPS:N_GROUPS + N_EXPERTS].set(w_exp)
    wr = jnp.concatenate(_split(wr, 2), axis=0)
    br = jnp.zeros((1, LANES), F32).at[0, :N_GROUPS].set(b_grp)
    br = br.at[0, N_GROUPS:N_GROUPS + N_EXPERTS].set(b_exp)
    vec = pl.BlockSpec((None, 1, d), lambda b, i: (b, 0, 0))
    const = lambda shape: pl.BlockSpec(shape, lambda b, i: (0, 0))
    row = lambda n: pl.BlockSpec((None, tm, n), lambda b, i: (b, i, 0))
    part_cnt = pl.BlockSpec((None, SUBLANES, LANES), lambda b, i: (b // bpp, 0, 0))
    return pl.pallas_call(
        functools.partial(_mix_kernel, bpp),
        grid=(bsz, t // tm),
        in_specs=[row(d), row(wl), row(d - wl), const((wl, d)), const((d - wl, d)),
                  vec, vec, vec, const((1, d)), const((2 * d, LANES)), const((1, LANES))],
        out_specs=[row(d), row(d // 2), row(LANES), row(LANES), part_cnt],
        out_shape=[jax.ShapeDtypeStruct((bsz, t, d), F32),
                   jax.ShapeDtypeStruct((bsz, t, d // 2), I32),
                   jax.ShapeDtypeStruct((bsz, t, LANES), I32),
                   jax.ShapeDtypeStruct((bsz, t, LANES), F32),
                   jax.ShapeDtypeStruct((parts, SUBLANES, LANES), I32)],
        compiler_params=_cparams(("arbitrary", "arbitrary")),
        name="outproj_adaln2_router",
    )(x, y_lru, y_rw, wo[:wl], wo[wl:], g1, sh2, sc2, ng.reshape(1, d), wr, br)


def _plan_kernel(rid_ref, cnt_ref, tril_ref, pos_ref, te_ref, base_ref):
    i = pl.program_id(0)
    tp = rid_ref.shape[0]
    lane = lax.broadcasted_iota(I32, (tp, LANES), 1)
    rid = rid_ref[...]
    oh0 = lane == rid[:, 0:1]
    oh1 = lane == rid[:, 1:2]
    oh = jnp.where(oh0 | oh1, 1.0, 0.0)

    @pl.when(i == 0)
    def _():
        cnt = cnt_ref[...]
        shift = EXPERT_TILE.bit_length() - 1
        padded = ((cnt + (EXPERT_TILE - 1)) >> shift) << shift
        l8 = lax.broadcasted_iota(I32, (SUBLANES, LANES), 1)
        end = padded
        s = 1
        while s < LANES:
            end = end + jnp.where(l8 >= s, pltpu.roll(end, s, 1), 0)
            s *= 2
        base_ref[...] = end - padded
        nt = te_ref.shape[0]
        j = lax.broadcasted_iota(I32, (nt, LANES), 0) * EXPERT_TILE
        lt = lax.broadcasted_iota(I32, (nt, LANES), 1)
        done = jnp.where((lt < N_EXPERTS) & (end[0:1, :] <= j), 1.0, 0.0)
        e_of = jnp.minimum(jnp.sum(done, axis=-1, keepdims=True), float(N_EXPERTS - 1)).astype(I32)
        total = end[0:1, N_EXPERTS - 1:N_EXPERTS]
        mine = lt == e_of
        cnt_e = jnp.sum(jnp.where(mine, cnt[0:1, :], 0), axis=-1, keepdims=True)
        start_e = jnp.sum(jnp.where(mine, end[0:1, :] - padded[0:1, :], 0), axis=-1, keepdims=True)
        valid = jnp.clip(cnt_e - (j[:, 0:1] - start_e), 0, EXPERT_TILE)
        te_ref[...] = jnp.where(lt == 0, e_of, jnp.where(lt == 1, total >> shift,
                                                        jnp.where(lt == 2, valid, 0)))

    prefix = jnp.dot(tril_ref[...], oh.astype(BF16), preferred_element_type=F32)
    dest = base_ref[0:1, :] + prefix.astype(I32)
    pos0 = jnp.sum(jnp.where(oh0, dest, 0), axis=-1, keepdims=True)
    pos1 = jnp.sum(jnp.where(oh1, dest, 0), axis=-1, keepdims=True)
    pos_ref[...] = jnp.where(lane == 0, pos0, jnp.where(lane == 1, pos1, 0))
    base_ref[...] = base_ref[...] + jnp.sum(oh, axis=0, keepdims=True).astype(I32)


def _plan(rid, cnt, n_tiles, row0, n, tp=512):
    tp = min(tp, n)
    nt_pad = -(-n_tiles // SUBLANES) * SUBLANES
    tril = (jnp.arange(tp)[:, None] > jnp.arange(tp)[None, :]).astype(BF16)
    pos, te = pl.pallas_call(
        _plan_kernel,
        grid=(n // tp,),
        in_specs=[pl.BlockSpec((tp, LANES), lambda i: (i + row0 // tp, 0)),
                  pl.BlockSpec((SUBLANES, LANES), lambda i: (0, 0)),
                  pl.BlockSpec((tp, tp), lambda i: (0, 0))],
        out_specs=[pl.BlockSpec((tp, LANES), lambda i: (i, 0)),
                   pl.BlockSpec((nt_pad, LANES), lambda i: (0, 0))],
        out_shape=[jax.ShapeDtypeStruct((n, LANES), I32),
                   jax.ShapeDtypeStruct((nt_pad, LANES), I32)],
        scratch_shapes=[pltpu.VMEM((SUBLANES, LANES), I32)],
        compiler_params=_cparams(("arbitrary",)),
        name="route_plan",
    )(rid, cnt, tril)
    return pos, te


SC_CORES = 2
SC_SUBCORES = 16
SC_WORKERS = SC_CORES * SC_SUBCORES
SC_WINDOW = 64


def _sc_mesh():
    return plsc.VectorSubcoreMesh(core_axis_name="c", subcore_axis_name="s",
                                  num_cores=SC_CORES, num_subcores=SC_SUBCORES)


def _sc_two_buffer_loop(n_win, fill, drain):
    assert n_win % 2 == 0

    def start(copies):
        for cp in copies:
            cp.start()

    def wait(copies):
        for cp in copies:
            cp.wait()

    start(fill(0, 0))

    @pl.loop(0, n_win, step=2)
    def _(j):
        for b in range(2):
            jj = j + b

            @pl.when(jj >= 1)
            def _():
                wait(drain(jj - 1, 1 - b))

            @pl.when(jj + 1 < n_win)
            def _():
                start(fill(jj + 1, 1 - b))

            wait(fill(jj, b))
            start(drain(jj, b))

    wait(drain(n_win - 1, 1))


def _sc_scatter_rows(rows, pos0, pos1, n_out, row0):
    n = pos0.shape[0]
    d = rows.shape[1]
    per_w = n // SC_WORKERS
    n_win = per_w // SC_WINDOW
    shape3 = (SC_WORKERS, n_win, SC_WINDOW)

    @functools.partial(
        pl.kernel, mesh=_sc_mesh(), out_type=jax.ShapeDtypeStruct((n_out, d), rows.dtype),
        scratch_types=[pltpu.VMEM((n_win, SC_WINDOW), I32), pltpu.VMEM((n_win, SC_WINDOW), I32),
                       pltpu.VMEM((2, SC_WINDOW, d), rows.dtype),
                       pltpu.SemaphoreType.DMA((2,)), pltpu.SemaphoreType.DMA((2,))],
        name="sc_scatter_rows")
    def scatter(rows_hbm, p0_hbm, p1_hbm, out_hbm, p0_v, p1_v, buf, in_sem, out_sem):
        wid = lax.axis_index("s") * SC_CORES + lax.axis_index("c")
        base = row0 + wid * per_w
        pltpu.sync_copy(p0_hbm.at[wid], p0_v)
        pltpu.sync_copy(p1_hbm.at[wid], p1_v)

        def fill(j, b):
            src = rows_hbm.at[pl.ds(base + j * SC_WINDOW, SC_WINDOW)]
            return [pltpu.make_async_copy(src, buf.at[b], in_sem.at[b])]

        def drain(j, b):
            return [pltpu.make_async_copy(buf.at[b], out_hbm.at[p.at[j]], out_sem.at[b])
                    for p in (p0_v, p1_v)]

        _sc_two_buffer_loop(n_win, fill, drain)

    return scatter(rows, pos0.reshape(shape3), pos1.reshape(shape3))


def _sc_gather_rows(table, idx):
    m = idx.shape[0]
    d = table.shape[1]
    per_w = m // SC_WORKERS
    n_win = per_w // SC_WINDOW

    @functools.partial(
        pl.kernel, mesh=_sc_mesh(), out_type=jax.ShapeDtypeStruct((m, d), table.dtype),
        scratch_types=[pltpu.VMEM((n_win, SC_WINDOW), I32),
                       pltpu.VMEM((2, SC_WINDOW, d), table.dtype),
                       pltpu.SemaphoreType.DMA((2,)), pltpu.SemaphoreType.DMA((2,))],
        name="sc_gather_rows")
    def gather(table_hbm, idx_hbm, out_hbm, idx_v, buf, in_sem, out_sem):
        wid = lax.axis_index("s") * SC_CORES + lax.axis_index("c")
        base = wid * per_w
        pltpu.sync_copy(idx_hbm.at[wid], idx_v)

        def fill(j, b):
            return [pltpu.make_async_copy(table_hbm.at[idx_v.at[j]], buf.at[b], in_sem.at[b])]

        def drain(j, b):
            dst = out_hbm.at[pl.ds(base + j * SC_WINDOW, SC_WINDOW)]
            return [pltpu.make_async_copy(buf.at[b], dst, out_sem.at[b])]

        _sc_two_buffer_loop(n_win, fill, drain)

    return gather(table, idx.reshape(SC_WORKERS, n_win, SC_WINDOW))


def _expert_kernel(te_ref, nu_ref, valid_ref, x_ref, w1_ref, w3_ref, w2_ref, o_ref,
                   w1b_ref, w3b_ref, w2b_ref):
    j = pl.program_id(0)
    active = j < nu_ref[0]

    @pl.when(active & ((j == 0) | (te_ref[j] != te_ref[jnp.maximum(j - 1, 0)])))
    def _():
        w1b_ref[...] = w1_ref[...].astype(BF16)
        w3b_ref[...] = w3_ref[...].astype(BF16)
        w2b_ref[...] = w2_ref[...].astype(BF16)

    @pl.when(active)
    def _():
        nblk = 2
        rows = x_ref.shape[0] // nblk
        blocks = [slice(i * rows, (i + 1) * rows) for i in range(nblk)]
        row = lax.broadcasted_iota(I32, (rows, x_ref.shape[1]), 0)
        xs = [_unpack_bf16_halves(
            jnp.where(row + i * rows < valid_ref[j], x_ref[sl, :], 0)).astype(BF16)
            for i, sl in enumerate(blocks)]
        h1 = [jnp.dot(x, w1b_ref[...], preferred_element_type=F32) for x in xs]
        h3 = [jnp.dot(x, w3b_ref[...], preferred_element_type=F32) for x in xs]
        hid = [(a * jax.nn.sigmoid(a) * b).astype(BF16) for a, b in zip(h1, h3)]
        for sl, hb in zip(blocks, hid):
            o_ref[sl, :] = _pack_bf16_halves(
                jnp.dot(hb, w2b_ref[...], preferred_element_type=F32))

    @pl.when(j >= nu_ref[0])
    def _():
        o_ref[...] = jnp.zeros_like(o_ref)


def _experts(xs, te, nu, valid, w1, w3, w2):
    n_rows = xs.shape[0]
    d, de = w1.shape[-2:]
    nt = n_rows // EXPERT_TILE
    used = lambda j, nu: jnp.minimum(j, nu[0] - 1)
    grid_spec = pltpu.PrefetchScalarGridSpec(
        num_scalar_prefetch=3,
        grid=(nt,),
        in_specs=[
            pl.BlockSpec((EXPERT_TILE, d // 2), lambda j, te, nu, va: (used(j, nu), 0)),
            pl.BlockSpec((None, d, de), lambda j, te, nu, va: (te[used(j, nu)], 0, 0)),
            pl.BlockSpec((None, d, de), lambda j, te, nu, va: (te[used(j, nu)], 0, 0)),
            pl.BlockSpec((None, de, d), lambda j, te, nu, va: (te[used(j, nu)], 0, 0)),
        ],
        out_specs=pl.BlockSpec((EXPERT_TILE, d // 2), lambda j, te, nu, va: (j, 0)),
        scratch_shapes=[pltpu.VMEM((d, de), BF16), pltpu.VMEM((d, de), BF16),
                        pltpu.VMEM((de, d), BF16)],
    )
    return pl.pallas_call(
        _expert_kernel,
        grid_spec=grid_spec,
        out_shape=jax.ShapeDtypeStruct((n_rows, d // 2), I32),
        compiler_params=_cparams(("arbitrary",)),
        name="expert_mlp",
    )(te, nu, valid, xs, w1, w3, w2)


def _combine_kernel(x1_ref, rw_ref, g2_ref, fg_ref, y0_ref, y1_ref, *rest):
    o_ref = rest[-1]
    rw = rw_ref[...]
    moe = (rw[:, 0:1] * _unpack_bf16_halves(y0_ref[...])
           + rw[:, 1:2] * _unpack_bf16_halves(y1_ref[...]))
    x2 = x1_ref[...] + g2_ref[...] * moe
    ms = jnp.mean(x2 * x2, axis=-1, keepdims=True)
    o_ref[...] = x2 * lax.rsqrt(ms + RMS_EPS) * fg_ref[...]


def _combine(x1, rw, yg, g2, final_g, tc, b0, out):
    bsz, t, d = x1.shape
    bpp = yg.shape[1]
    row = lambda n: pl.BlockSpec((None, tc, n), lambda b, i: (b + b0, i, 0))
    slot = lambda s: pl.BlockSpec((None, None, tc, d // 2), lambda b, i: (s, b, i, 0))
    operands = [x1, rw, g2, final_g.reshape(1, d), yg, yg]
    in_specs = [row(d), row(LANES),
                pl.BlockSpec((None, 1, d), lambda b, i: (b + b0, 0, 0)),
                pl.BlockSpec((1, d), lambda b, i: (0, 0)),
                slot(0), slot(1)]
    aliases = {}
    if out is not None:
        operands.append(out)
        in_specs.append(pl.BlockSpec(memory_space=pl.ANY))
        aliases = {len(operands) - 1: 0}
    return pl.pallas_call(
        _combine_kernel,
        grid=(bpp, t // tc),
        in_specs=in_specs,
        out_specs=row(d),
        out_shape=jax.ShapeDtypeStruct((bsz, t, d), F32),
        input_output_aliases=aliases,
        compiler_params=_cparams(("arbitrary", "arbitrary")),
        name="combine_final_norm",
    )(*operands)


def _row_tile(t, want):
    return want if t % want == 0 else t


def kernel(x, c, w_ada, b_ada, norm1_g, w_in, conv_w, conv_b, lru_wa, lru_ba, lru_wi, lru_bi, lru_lam, lru_norm_g, tok_mu, w0, w_up, a0, a_up, g_up, k_k, k_a, r_k, ln_x_w, ln_x_b, w_out, norm2_g, w_grp, b_grp, w_exp, b_exp, w1, w3, w2, final_g):
    bsz, t, d = x.shape
    n = bsz * t
    depth = w_ada.shape[0]
    assert depth == 1, "the combine kernel applies the final norm: only DEPTH == 1 is wired"
    tile = _row_tile(t, ROW_TILE)
    parts = MOE_PARTS if bsz % MOE_PARTS == 0 else 1
    bpp = bsz // parts
    n_part = bpp * t
    n_tiles = (n_part * TOP_K) // EXPERT_TILE + N_EXPERTS
    for l in range(depth):
        mod = _modulation(c, w_ada[l], b_ada[l]).reshape(bsz, 6, 1, d)
        sh1, sc1, g1, sh2, sc2, g2 = (mod[:, i] for i in range(6))
        y_lru, p_rw = _inproj_lru(x, sh1, sc1, norm1_g[l], w_in[l], conv_w[l], conv_b[l],
                                  lru_wa[l], lru_ba[l], lru_wi[l], lru_bi[l], lru_lam[l],
                                  lru_norm_g[l], tm=tile)
        y_rw = _rwkv7(p_rw, tok_mu[l], w0[l], w_up[l], a0[l], a_up[l], g_up[l], k_k[l], k_a[l],
                      r_k[l], ln_x_w[l], ln_x_b[l])
        x1, h2, rid, rw, cnt = _mix(x, y_lru, y_rw, w_out[l], g1, sh2, sc2, norm2_g[l],
                                    w_grp[l], b_grp[l], w_exp[l], b_exp[l], tile, parts)
        rid = rid.reshape(n, LANES)
        h2 = h2.reshape(n, d // 2)
        plans = [_plan(rid, cnt[p], n_tiles, p * n_part, n_part) for p in range(parts)]
        out = None
        for p, (pos, te) in enumerate(plans):
            pos0, pos1 = pos[:, 0], pos[:, 1]
            xs = _sc_scatter_rows(h2, pos0, pos1, n_tiles * EXPERT_TILE, p * n_part)
            ys = _experts(xs, te[:n_tiles, 0], te[0:1, 1], te[:n_tiles, 2], w1[l], w3[l], w2[l])
            yg = _sc_gather_rows(ys, jnp.concatenate([pos0, pos1]))
            out = _combine(x1, rw, yg.reshape(TOP_K, bpp, t, d // 2), g2, final_g,
                           _row_tile(t, COMBINE_TILE), p * bpp, out)
    return out
```

```python
import functools

import jax
import jax.numpy as jnp
from jax import lax
from jax.experimental import pallas as pl
from jax.experimental.pallas import tpu as pltpu
from jax.experimental.pallas import tpu_sc as plsc

F32 = jnp.float32
BF16 = jnp.bfloat16
I32 = jnp.int32

LRU_WIDTH = 512
LRU_HEAD_DIM = 64
CONV_WIDTH = 4
LRU_C = 8.0
RWKV_WIDTH = 512
HEAD_DIM = 64
DECAY_LORA = 64
AAA_LORA = 64
GATE_LORA = 128
RWKV_PROJ = 3 * RWKV_WIDTH + DECAY_LORA + AAA_LORA + GATE_LORA
N_GROUPS = 4
EXPERTS_PER_GROUP = 8
N_EXPERTS = N_GROUPS * EXPERTS_PER_GROUP
TOP_K = 2
RMS_EPS = 1e-6
GN_EPS = 64e-5

LANES = 128
SUBLANES = 8
CHUNK = 64
EXPERT_TILE = 512
ROW_TILE = 256
COMBINE_TILE = 512
VMEM_LIMIT = 48 * 1024 * 1024

NN = (((1,), (0,)), ((), ()))
NT = (((1,), (1,)), ((), ()))
TN = (((0,), (0,)), ((), ()))


def _split(x, n):
    if x.dtype == BF16:
        return [x]
    parts = []
    rem = x
    for i in range(n):
        p = rem.astype(BF16)
        parts.append(p)
        if i + 1 < n:
            rem = rem - p.astype(F32)
    return parts


def _mm(a, b, dn=NN, pa=1, pb=1):
    aps = _split(a, pa)
    bps = _split(b, pb)
    order = max(len(aps), len(bps))
    terms = [(i, j) for i in range(len(aps)) for j in range(len(bps)) if i + j < order]
    ka = dn[0][0][0]
    kb = dn[0][1][0]
    if len(terms) > 1 and a.shape[ka] % LANES == 0:
        a_cat = jnp.concatenate([aps[i] for i, _ in terms], axis=ka)
        b_cat = jnp.concatenate([bps[j] for _, j in terms], axis=kb)
        return lax.dot_general(a_cat, b_cat, dn, preferred_element_type=F32)
    out = None
    for i, j in terms:
        t = lax.dot_general(aps[i], bps[j], dn, preferred_element_type=F32)
        out = t if out is None else out + t
    return out


def _presplit(b):
    hi, lo = _split(b, 2)
    return jnp.concatenate([hi, lo, hi], axis=0)


def _mm_presplit(a, b3):
    a_hi, a_lo = _split(a, 2)
    return jnp.dot(jnp.concatenate([a_hi, a_hi, a_lo], axis=1), b3, preferred_element_type=F32)


def _pack_bf16_halves(x):
    n = x.shape[1] // 2
    bits = lax.bitcast_convert_type(x.astype(BF16).astype(F32), I32)
    return bits[:, n:] | ((bits[:, :n] >> 16) & 0xFFFF)


def _unpack_bf16_halves(p):
    lo = lax.bitcast_convert_type(p << 16, F32)
    hi = lax.bitcast_convert_type(p & (-65536), F32)
    return jnp.concatenate([lo, hi], axis=1)


def _softplus(x):
    return jnp.maximum(x, 0.0) + jnp.log1p(jnp.exp(-jnp.abs(x)))


def _cparams(sem):
    return pltpu.CompilerParams(dimension_semantics=sem, vmem_limit_bytes=VMEM_LIMIT)


def _mod_kernel(c_ref, w_ref, b_ref, o_ref):
    c = c_ref[...]
    s = c * jax.nn.sigmoid(c)
    o_ref[...] = _mm(s, w_ref[...], pa=2, pb=2) + b_ref[...]


def _modulation(c, w_ada, b_ada):
    bsz, d = c.shape
    n_out = w_ada.shape[1]
    rows = -(-bsz // SUBLANES) * SUBLANES
    c_pad = jnp.zeros((rows, d), F32).at[:bsz].set(c)
    bn = d
    out = pl.pallas_call(
        _mod_kernel,
        grid=(n_out // bn,),
        in_specs=[
            pl.BlockSpec((rows, d), lambda j: (0, 0)),
            pl.BlockSpec((d, bn), lambda j: (0, j)),
            pl.BlockSpec((1, bn), lambda j: (0, j)),
        ],
        out_specs=pl.BlockSpec((rows, bn), lambda j: (0, j)),
        out_shape=jax.ShapeDtypeStruct((rows, n_out), F32),
        compiler_params=_cparams(("arbitrary",)),
        name="adaln_mod",
    )(c_pad, w_ada, b_ada.reshape(1, n_out))
    return out[:bsz]


def _adaln(x, g, shift, scale):
    ms = jnp.mean(x * x, axis=-1, keepdims=True)
    y = x * lax.rsqrt(ms + RMS_EPS) * g
    return y * (1.0 + scale) + shift


def _inproj_lru_kernel(x_ref, sh_ref, sc_ref, g_ref, wl_ref, wr_ref,
                       cw_ref, cb_ref, wab_ref, bab_ref, lam_ref, ng_ref, ones_ref,
                       yl_ref, pr_ref, xprev_ref, hprev_ref):
    @pl.when(pl.program_id(1) == 0)
    def _():
        xprev_ref[...] = jnp.zeros_like(xprev_ref)
        hprev_ref[...] = jnp.zeros_like(hprev_ref)

    w = LRU_WIDTH
    tm = x_ref.shape[0]
    nblk = 2
    rows = tm // nblk
    blocks = [slice(i * rows, (i + 1) * rows) for i in range(nblk)]
    h = [_adaln(x_ref[sl, :], g_ref[...], sh_ref[...], sc_ref[...]).astype(BF16)
         for sl in blocks]
    ux = [jnp.dot(hb, wl_ref[:, :w], preferred_element_type=F32) for hb in h]

    def remaining_columns():
        for sl, hb in zip(blocks, h):
            pr_ref[sl, :] = jnp.dot(hb, wr_ref[...], preferred_element_type=F32)
        return [jnp.dot(hb, wl_ref[:, w:], preferred_element_type=F32) for hb in h]

    y = _lru_tile(ux, remaining_columns, cw_ref, cb_ref, wab_ref, bab_ref, lam_ref, ng_ref,
                  ones_ref, xprev_ref, hprev_ref)
    for sl, yb in zip(blocks, y):
        yl_ref[sl, :] = yb.astype(yl_ref.dtype)


def _inproj_lru(x, sh1, sc1, g, w_in, conv_w, conv_b, wa, ba, wi, bi, lam, norm_g, tm=256):
    bsz, t, d = x.shape
    w = LRU_WIDTH
    nl = 2 * w
    nr = RWKV_PROJ
    wl = w_in[:, :nl].astype(BF16)
    wr = w_in[:, nl:].astype(BF16)
    wab = jnp.concatenate([_block_diag(wa), _block_diag(wi)], axis=1).astype(BF16)
    bab = jnp.concatenate([ba.reshape(1, w), bi.reshape(1, w)], axis=1)
    vec = pl.BlockSpec((None, 1, d), lambda b, i: (b, 0, 0))
    const = lambda shape: pl.BlockSpec(shape, lambda b, i: (0, 0))
    return pl.pallas_call(
        _inproj_lru_kernel,
        grid=(bsz, t // tm),
        in_specs=[
            pl.BlockSpec((None, tm, d), lambda b, i: (b, i, 0)),
            vec, vec, const((1, d)), const((d, nl)), const((d, nr)),
            const((CONV_WIDTH, w)), const((1, w)), const((w, 2 * w)), const((1, 2 * w)),
            const((1, w)), const((1, w)), const((w, w)),
        ],
        out_specs=[
            pl.BlockSpec((None, tm, w), lambda b, i: (b, i, 0)),
            pl.BlockSpec((None, tm, nr), lambda b, i: (b, i, 0)),
        ],
        out_shape=[
            jax.ShapeDtypeStruct((bsz, t, w), BF16),
            jax.ShapeDtypeStruct((bsz, t, nr), F32),
        ],
        scratch_shapes=[pltpu.VMEM((SUBLANES, w), F32), pltpu.VMEM((SUBLANES, w), F32)],
        compiler_params=_cparams(("arbitrary", "arbitrary")),
        name="adaln1_inproj_rglru",
    )(x, sh1, sc1, g.reshape(1, d), wl, wr, conv_w, conv_b.reshape(1, w), wab, bab,
      lam.reshape(1, w), norm_g.reshape(1, w), _head_ones(w, LRU_HEAD_DIM))


def _gelu_tanh(x):
    c = 0.7978845608028654
    return x * (0.5 * (1.0 + jnp.tanh(c * (x + 0.044715 * (x * x * x)))))


def _sigmoid(x):
    return 0.5 * jnp.tanh(0.5 * x) + 0.5


def _lru_tile(ux, gate_branch, cw_ref, cb_ref, wab_ref, bab_ref, lam_ref, ng_ref, ones_ref,
              xprev_ref, hprev_ref):
    w = LRU_WIDTH
    rows = ux[0].shape[0]
    prev8 = [xprev_ref[...]] + [u[rows - SUBLANES:, :] for u in ux[:-1]]
    ext = [jnp.concatenate([p, u], axis=0) for p, u in zip(prev8, ux)]
    xc = [cb_ref[...] + cw_ref[CONV_WIDTH - 1:CONV_WIDTH, :] * u for u in ux]
    for k in range(1, CONV_WIDTH):
        tap = cw_ref[CONV_WIDTH - 1 - k:CONV_WIDTH - k, :]
        xc = [x + tap * pltpu.roll(e, k, 0)[SUBLANES:, :] for x, e in zip(xc, ext)]
    xprev_ref[...] = ux[-1][rows - SUBLANES:, :]

    gates = [jnp.dot(x.astype(BF16), wab_ref[...], preferred_element_type=F32) + bab_ref[...]
             for x in xc]
    ug = gate_branch()
    r = [_sigmoid(g[:, :w]) for g in gates]
    ig = [_sigmoid(g[:, w:]) for g in gates]
    sp = _softplus(-lam_ref[...])
    log_a = [(-LRU_C) * x * sp for x in r]
    a = [jnp.exp(x) for x in log_a]
    th = [jnp.tanh(x) for x in log_a]
    q = [(-2.0 * x) / (1.0 - x) for x in th]
    root_q = [jnp.where(x > 0.0, x * lax.rsqrt(x), 0.0) for x in q]
    b = [s * (i * x) for s, i, x in zip(root_q, ig, xc)]

    row8 = lax.broadcasted_iota(I32, (rows, w), 0) & (SUBLANES - 1)
    acc_a, acc_b = a, b
    for s in (1, 2, 4):
        live = row8 >= s
        sh_a = [pltpu.roll(x, s, 0) for x in acc_a]
        sh_b = [pltpu.roll(x, s, 0) for x in acc_b]
        acc_b = [jnp.where(live, x * sb + y, y) for x, sb, y in zip(acc_a, sh_b, acc_b)]
        acc_a = [jnp.where(live, x * sa, x) for x, sa in zip(acc_a, sh_a)]
    carry = hprev_ref[SUBLANES - 1:SUBLANES, :]
    h = []
    for xa, xb in zip(acc_a, acc_b):
        groups = []
        for gi in range(rows // SUBLANES):
            lo = gi * SUBLANES
            groups.append(xa[lo:lo + SUBLANES, :] * carry + xb[lo:lo + SUBLANES, :])
            last = lo + SUBLANES - 1
            carry = xa[last:last + 1, :] * carry + xb[last:last + 1, :]
        h.append(jnp.concatenate(groups, axis=0))
    hprev_ref[...] = h[-1][rows - SUBLANES:, :]

    y = [x * _gelu_tanh(g) for x, g in zip(h, ug)]
    ms = [_mm(x * x, ones_ref[...]) * (1.0 / LRU_HEAD_DIM) for x in y]
    return [x * lax.rsqrt(m + RMS_EPS) * ng_ref[...] for x, m in zip(y, ms)]


def _block_diag(w):
    h, n, _ = w.shape
    eye = jnp.eye(h, dtype=w.dtype)
    return (eye[:, None, :, None] * w[:, :, None, :]).reshape(h * n, h * n)


def _head_ones(width, head):
    idx = jnp.arange(width) // head
    return (idx[:, None] == idx[None, :]).astype(BF16)


PG = (1, 1)
PI = (1, 1)
PS = (1, 1)
PH = (1, 2)


def _unit_lower_inverse(a_list, ri, ci):
    mm = functools.partial(_mm, pa=PI[0], pb=PI[1])
    eye = jnp.where(ri == ci, 1.0, 0.0)
    leaf = (ri >> 3) == (ci >> 3)
    a8 = [jnp.where(leaf, a, 0.0) for a in a_list]
    a8_2 = [mm(x, x) for x in a8]
    a8_4 = [mm(x, x) for x in a8_2]
    t = [mm(eye + x, eye + y) for x, y in zip(a8, a8_2)]
    t = [mm(x, eye + y) for x, y in zip(t, a8_4)]
    zero = jnp.zeros((LANES, LANES), F32)
    for sh in (3, 4, 5):
        s = 1 << sh
        off = ((ri >> (sh + 1)) == (ci >> (sh + 1))) & ((ri >> sh) != (ci >> sh))
        t_lo = [_second_blocks(x, s) for x in t]
        b_lo = [mm(_second_blocks(jnp.where(off, a, 0.0), s), x) for a, x in zip(a_list, t)]
        d_lo = [mm(x, _interleave_blocks(zero, y, s)) for x, y in zip(t_lo, b_lo)]
        t = [_interleave_blocks(x, y + z, s) for x, y, z in zip(t, t_lo, d_lo)]
    return t


def _second_blocks(x, s):
    return jnp.concatenate(
        [x[s * (2 * m + 1):s * (2 * m + 2)] for m in range(x.shape[0] // (2 * s))], axis=0)


def _interleave_blocks(first_src, second, s):
    parts = []
    for m in range(first_src.shape[0] // (2 * s)):
        parts.append(first_src[2 * s * m:2 * s * m + s])
        parts.append(second[s * m:s * (m + 1)])
    return jnp.concatenate(parts, axis=0)


def _rwkv_kernel(p_ref, mu_ref, pv_ref, wlo_ref, gup_ref, tril_ref, o_ref, uprev_ref, h_ref):
    w = RWKV_WIDTH
    tt = p_ref.shape[0]
    c = CHUNK
    n_pairs = w // LANES
    units = [(j, p) for j in range(tt // c) for p in range(n_pairs)]

    @pl.when(pl.program_id(1) == 0)
    def _():
        uprev_ref[...] = jnp.zeros_like(uprev_ref)
        h_ref[...] = jnp.zeros_like(h_ref)

    u = p_ref[...]
    ext = jnp.concatenate([uprev_ref[...], u], axis=0)
    prev = pltpu.roll(ext, 1, 0)[SUBLANES:, :]
    uprev_ref[...] = u[tt - SUBLANES:, :]
    um = u + (prev - u) * mu_ref[...]

    r = um[:, 0:w]
    k = um[:, w:2 * w]
    v = um[:, 2 * w:3 * w]
    z = um[:, 3 * w:3 * w + LANES]
    gd = um[:, 3 * w + LANES:]
    w0 = pv_ref[0:1, :]
    a0 = pv_ref[1:2, :]
    k_k = pv_ref[2:3, :]
    k_a = pv_ref[3:4, :]
    r_k = pv_ref[4:5, :]
    ln_w = pv_ref[5:6, :]
    ln_b = pv_ref[6:7, :]

    lane_t = lax.broadcasted_iota(I32, (tt, LANES), 1)
    zz = jnp.where(lane_t < HEAD_DIM, jnp.tanh(z), z)
    lora = _mm_presplit(zz, wlo_ref[...])
    wlog = -_softplus(-(w0 + lora[:, :w])) - 0.5
    ld = -jnp.exp(wlog)
    a = _sigmoid(a0 + lora[:, w:])
    g = _mm_presplit(_sigmoid(gd), gup_ref[...])

    ri = lax.broadcasted_iota(I32, (LANES, LANES), 0)
    ci = lax.broadcasted_iota(I32, (LANES, LANES), 1)
    even = lax.broadcasted_iota(I32, (c, LANES), 1) < HEAD_DIM
    wide = 2 * LANES
    ones_bd = jnp.where(
        (lax.broadcasted_iota(I32, (wide, wide), 0) >> 6)
        == (lax.broadcasted_iota(I32, (wide, wide), 1) >> 6), 1.0, 0.0).astype(BF16)

    def head_sum(x):
        return jnp.concatenate(
            [_mm(x[:, q * wide:(q + 1) * wide], ones_bd) for q in range(w // wide)], axis=1)

    kk = k * k_k
    kk = kk * lax.rsqrt(jnp.maximum(head_sum(kk * kk), 1e-24))
    kp = k * (1.0 + (a - 1.0) * k_a)
    kka = kk * a
    bonus = head_sum(r * kp * r_k) * v
    cum = _mm(tril_ref[...], ld, pb=3)
    cum_c = [cum[j * c + c - 1:j * c + c, :] for j in range(tt // c)]
    p_c = [jnp.exp(x) for x in cum_c]
    p_inv = jnp.exp(-cum)
    p_hat = jnp.concatenate(
        [p_c[j] * p_inv[j * c:(j + 1) * c, :] for j in range(tt // c)], axis=0)
    al = -kk * jnp.exp(cum - ld)
    rt = r * jnp.exp(cum)
    bt = kka * p_inv
    kt = kp * p_inv
    bh = kka * p_hat
    kh = kp * p_hat

    def blk(x):
        return [x[j * c:(j + 1) * c, p * LANES:(p + 1) * LANES] for j, p in units]

    def halves(x):
        xs = blk(x)
        return [jnp.where(even, y, 0.0) for y in xs], [jnp.where(even, 0.0, y) for y in xs]

    def rows(top, bot):
        return [jnp.concatenate([x, y], axis=0) for x, y in zip(top, bot)]

    def unstack(x):
        return x[:c, :] + x[c:, :]

    al_e, al_o = halves(al)
    rt_e, rt_o = halves(rt)
    bt_b, kt_b = blk(bt), blk(kt)
    bh_e, bh_o = halves(bh)
    kh_e, kh_o = halves(kh)
    v_e, v_o = halves(v)
    al_n = rows(al_e, al_o)
    bh_n = rows(bh_e, bh_o)
    v_s = rows(v_o, v_e)
    kh_s = rows(kh_o, kh_e)
    rt_b = blk(rt)
    nu = range(len(units))

    g0 = [_mm(x, y, NT, pa=PG[0], pb=PG[1])
          for x, y in zip(rows(al_e, rt_e), rows(bt_b, kt_b))]
    g1 = [_mm(x, y, NT, pa=PG[0], pb=PG[1])
          for x, y in zip(rows(rt_o, al_o), rows(kt_b, bt_b))]
    top = ri < c
    left = ci < c
    tri_s = (ri & (c - 1)) > (ci & (c - 1))
    tri_i = (ri & (c - 1)) >= (ci & (c - 1))
    diag_q = top == left

    def pick(x0, x1, in_q0, tri):
        return [jnp.where(tri, jnp.where(in_q0, x, y), 0.0) for x, y in zip(x0, x1)]

    a_ab = pick(g0, g1, top, diag_q & tri_s)
    a_rk = pick(g1, g0, top, diag_q & tri_i)
    a_ak = pick(g0, g1, top, (~diag_q) & tri_s)
    a_rb = pick(g1, g0, top, (~diag_q) & tri_i)
    x1 = [_mm(a_ak[n], v_s[n], pa=PS[0], pb=PS[1]) for n in nu]
    akv = [_mm(a_rk[n], v_s[n], pa=PS[0], pb=PS[1]) for n in nu]
    khv = [_mm(kh_s[n], v_s[n], TN, pa=PS[0], pb=PS[1]) for n in nu]
    t_inv = _unit_lower_inverse(a_ab, ri, ci)
    tw = [_mm(t_inv[n], jnp.concatenate([al_n[n], x1[n]], axis=1), pa=PS[0], pb=PS[1])
          for n in nu]
    qo = [_mm(a_rb[n], tw[n], pa=PS[0], pb=PS[1]) for n in nu]
    mn = [_mm(bh_n[n], tw[n], TN, pa=PS[0], pb=PS[1]) for n in nu]
    q = [rt_b[n] + unstack(qo[n][:, :LANES]) for n in nu]
    o_loc = [unstack(qo[n][:, LANES:] + akv[n]) for n in nu]
    m_full = [mn[n][:, :LANES]
              + jnp.where(ri == ci, p_c[j][:, p * LANES:(p + 1) * LANES], 0.0)
              for n, (j, p) in enumerate(units)]
    n_loc = [mn[n][:, LANES:] + khv[n] for n in nu]

    h = [h_ref[p] for p in range(n_pairs)]
    o_rows = []
    for j in range(tt // c):
        o_parts = []
        for p in range(n_pairs):
            n = j * n_pairs + p
            o_parts.append(_mm(q[n], h[p], pa=PH[0], pb=PH[1]) + o_loc[n])
            h[p] = _mm(m_full[n], h[p], pa=PH[0], pb=PH[1]) + n_loc[n]
        o_rows.append(jnp.concatenate(o_parts, axis=1))
    for p in range(n_pairs):
        h_ref[p] = h[p]

    o = jnp.concatenate(o_rows, axis=0)
    mean = head_sum(o) * (1.0 / HEAD_DIM)
    d = o - mean
    var = head_sum(d * d) * (1.0 / HEAD_DIM)
    on = d * lax.rsqrt(var + GN_EPS) * ln_w + ln_b
    o_ref[...] = ((on + bonus) * g).astype(o_ref.dtype)


RWKV_TILE = 256


def _rwkv7(p_rw, mu, w0, w_up, a0, a_up, g_up, k_k, k_a, r_k, ln_w, ln_b):
    bsz, t, npj = p_rw.shape
    w = RWKV_WIDTH
    tt = RWKV_TILE
    pv = jnp.stack([w0, a0, k_k, k_a, r_k.reshape(w), ln_w, ln_b, jnp.zeros((w,), F32)])
    wlo = jnp.zeros((LANES, 2 * w), F32)
    wlo = wlo.at[:DECAY_LORA, :w].set(w_up).at[DECAY_LORA:, w:].set(a_up)
    row = jnp.arange(tt)
    tril = ((row[:, None] >= row[None, :])
            & (row[:, None] // CHUNK == row[None, :] // CHUNK)).astype(BF16)
    const = lambda shape: pl.BlockSpec(shape, lambda b, i: (0, 0))
    return pl.pallas_call(
        _rwkv_kernel,
        grid=(bsz, t // tt),
        in_specs=[
            pl.BlockSpec((None, tt, npj), lambda b, i: (b, i, 0)),
            const((1, npj)), const((SUBLANES, w)), const((3 * LANES, 2 * w)),
            const((3 * GATE_LORA, w)), const((tt, tt)),
        ],
        out_specs=pl.BlockSpec((None, tt, w), lambda b, i: (b, i, 0)),
        out_shape=jax.ShapeDtypeStruct((bsz, t, w), BF16),
        scratch_shapes=[pltpu.VMEM((SUBLANES, npj), F32),
                        pltpu.VMEM((w // LANES, LANES, LANES), F32)],
        compiler_params=_cparams(("arbitrary", "arbitrary")),
        name="rwkv7",
    )(p_rw, mu.reshape(1, npj), pv, _presplit(wlo), _presplit(g_up), tril)


def _mix_kernel(x_ref, yl_ref, yr_ref, wo1_ref, wo2_ref, g1_ref, sh_ref, sc_ref, ng_ref,
                wr_ref, br_ref, x1_ref, h2_ref, rid_ref, rw_ref, cnt_ref):
    tm = x_ref.shape[0]
    nblk = 2
    rows = tm // nblk
    blocks = [slice(i * rows, (i + 1) * rows) for i in range(nblk)]
    y = [jnp.dot(yl_ref[sl, :], wo1_ref[...], preferred_element_type=F32)
         + jnp.dot(yr_ref[sl, :], wo2_ref[...], preferred_element_type=F32) for sl in blocks]
    x1 = [x_ref[sl, :] + g1_ref[...] * yb for sl, yb in zip(blocks, y)]
    h2 = [_adaln(xb, ng_ref[...], sh_ref[...], sc_ref[...]).astype(BF16) for xb in x1]
    lg_all = [jnp.dot(jnp.concatenate([hb, hb], axis=1), wr_ref[...],
                      preferred_element_type=F32) + br_ref[...] for hb in h2]
    for sl, xb, hb in zip(blocks, x1, h2):
        x1_ref[sl, :] = xb
        h2_ref[sl, :] = _pack_bf16_halves(hb)

    neg = -jnp.inf
    lane = lax.broadcasted_iota(I32, (rows, LANES), 1)
    lane_f = lane.astype(F32)

    def first_argmax(xs):
        m = [jnp.max(x, axis=-1, keepdims=True) for x in xs]
        idx = [jnp.min(jnp.where(x == mi, lane_f, float(LANES)), axis=-1, keepdims=True)
               for x, mi in zip(xs, m)]
        return m, [i.astype(I32) for i in idx]

    lg = [jnp.where(lane < N_GROUPS, x, neg) for x in lg_all]
    gm, g_idx = first_argmax(lg)
    g_w = [1.0 / jnp.sum(jnp.exp(x - m), axis=-1, keepdims=True)
           for x, m in zip(lg, gm)]
    lo = [N_GROUPS + EXPERTS_PER_GROUP * g for g in g_idx]
    le = [jnp.where((lane >= l) & (lane < l + EXPERTS_PER_GROUP), x, neg)
          for x, l in zip(lg_all, lo)]
    v1, i1 = first_argmax(le)
    v2, i2 = first_argmax([jnp.where(lane == i, neg, x) for x, i in zip(le, i1)])

    @pl.when((pl.program_id(0) == 0) & (pl.program_id(1) == 0))
    def _():
        cnt_ref[...] = jnp.zeros_like(cnt_ref)

    e_lane = lane + N_GROUPS
    hits = sum(jnp.sum(jnp.where((e_lane == a) | (e_lane == b), 1.0, 0.0), axis=0, keepdims=True)
               for a, b in zip(i1, i2))
    cnt_ref[...] = cnt_ref[...] + hits.astype(I32)

    for blk in range(nblk):
        e2 = jnp.exp(v2[blk] - v1[blk])
        w1 = g_w[blk] / (1.0 + e2)
        w2 = g_w[blk] * e2 / (1.0 + e2)
        sl = blocks[blk]
        rid_ref[sl, :] = jnp.where(lane == 0, i1[blk] - N_GROUPS,
                                   jnp.where(lane == 1, i2[blk] - N_GROUPS, 0))
        rw_ref[sl, :] = jnp.where(lane == 0, w1, jnp.where(lane == 1, w2, 0.0))


def _mix(x, y_lru, y_rw, w_out, g1, sh2, sc2, ng, w_grp, b_grp, w_exp, b_exp, tm=256):
    bsz, t, d = x.shape
    wo = w_out.astype(BF16)
    wl = LRU_WIDTH
    wr = jnp.zeros((d, LANES), F32).at[:, :N_GROUPS].set(w_grp)
    wr = wr.at[:, N_GROUPS:N_GROUPS + N_EXPERTS].set(w_exp)
    wr = jnp.concatenate(_split(wr, 2), axis=0)
    br = jnp.zeros((1, LANES), F32).at[0, :N_GROUPS].set(b_grp)
    br = br.at[0, N_GROUPS:N_GROUPS + N_EXPERTS].set(b_exp)
    vec = pl.BlockSpec((None, 1, d), lambda b, i: (b, 0, 0))
    const = lambda shape: pl.BlockSpec(shape, lambda b, i: (0, 0))
    row = lambda n: pl.BlockSpec((None, tm, n), lambda b, i: (b, i, 0))
    return pl.pallas_call(
        _mix_kernel,
        grid=(bsz, t // tm),
        in_specs=[row(d), row(wl), row(d - wl), const((wl, d)), const((d - wl, d)),
                  vec, vec, vec, const((1, d)), const((2 * d, LANES)), const((1, LANES))],
        out_specs=[row(d), row(d // 2), row(LANES), row(LANES), const((SUBLANES, LANES))],
        out_shape=[jax.ShapeDtypeStruct((bsz, t, d), F32),
                   jax.ShapeDtypeStruct((bsz, t, d // 2), I32),
                   jax.ShapeDtypeStruct((bsz, t, LANES), I32),
                   jax.ShapeDtypeStruct((bsz, t, LANES), F32),
                   jax.ShapeDtypeStruct((SUBLANES, LANES), I32)],
        compiler_params=_cparams(("arbitrary", "arbitrary")),
        name="outproj_adaln2_router",
    )(x, y_lru, y_rw, wo[:wl], wo[wl:], g1, sh2, sc2, ng.reshape(1, d), wr, br)


def _plan_kernel(rid_ref, cnt_ref, tril_ref, pos_ref, te_ref, base_ref):
    i = pl.program_id(0)
    tp = rid_ref.shape[0]
    lane = lax.broadcasted_iota(I32, (tp, LANES), 1)
    rid = rid_ref[...]
    oh0 = lane == rid[:, 0:1]
    oh1 = lane == rid[:, 1:2]
    oh = jnp.where(oh0 | oh1, 1.0, 0.0)

    @pl.when(i == 0)
    def _():
        cnt = cnt_ref[...]
        shift = EXPERT_TILE.bit_length() - 1
        padded = ((cnt + (EXPERT_TILE - 1)) >> shift) << shift
        l8 = lax.broadcasted_iota(I32, (SUBLANES, LANES), 1)
        end = padded
        s = 1
        while s < LANES:
            end = end + jnp.where(l8 >= s, pltpu.roll(end, s, 1), 0)
            s *= 2
        base_ref[...] = end - padded
        nt = te_ref.shape[0]
        j = lax.broadcasted_iota(I32, (nt, LANES), 0) * EXPERT_TILE
        lt = lax.broadcasted_iota(I32, (nt, LANES), 1)
        done = jnp.where((lt < N_EXPERTS) & (end[0:1, :] <= j), 1.0, 0.0)
        e_of = jnp.minimum(jnp.sum(done, axis=-1, keepdims=True), float(N_EXPERTS - 1)).astype(I32)
        total = end[0:1, N_EXPERTS - 1:N_EXPERTS]
        mine = lt == e_of
        cnt_e = jnp.sum(jnp.where(mine, cnt[0:1, :], 0), axis=-1, keepdims=True)
        start_e = jnp.sum(jnp.where(mine, end[0:1, :] - padded[0:1, :], 0), axis=-1, keepdims=True)
        valid = jnp.clip(cnt_e - (j[:, 0:1] - start_e), 0, EXPERT_TILE)
        te_ref[...] = jnp.where(lt == 0, e_of, jnp.where(lt == 1, total >> shift,
                                                        jnp.where(lt == 2, valid, 0)))

    prefix = jnp.dot(tril_ref[...], oh.astype(BF16), preferred_element_type=F32)
    dest = base_ref[0:1, :] + prefix.astype(I32)
    pos0 = jnp.sum(jnp.where(oh0, dest, 0), axis=-1, keepdims=True)
    pos1 = jnp.sum(jnp.where(oh1, dest, 0), axis=-1, keepdims=True)
    both = jnp.where(lane == 0, pos0, jnp.where(lane == 1, pos1, 0))
    pos_ref[...] = both.T[:SUBLANES, :]
    base_ref[...] = base_ref[...] + jnp.sum(oh, axis=0, keepdims=True).astype(I32)


def _plan(rid, cnt, n_tiles, tp=512):
    n = rid.shape[0]
    tp = min(tp, n)
    nt_pad = -(-n_tiles // SUBLANES) * SUBLANES
    tril = (jnp.arange(tp)[:, None] > jnp.arange(tp)[None, :]).astype(BF16)
    pos, te = pl.pallas_call(
        _plan_kernel,
        grid=(n // tp,),
        in_specs=[pl.BlockSpec((tp, LANES), lambda i: (i, 0)),
                  pl.BlockSpec((SUBLANES, LANES), lambda i: (0, 0)),
                  pl.BlockSpec((tp, tp), lambda i: (0, 0))],
        out_specs=[pl.BlockSpec((None, SUBLANES, tp), lambda i: (i, 0, 0)),
                   pl.BlockSpec((nt_pad, LANES), lambda i: (0, 0))],
        out_shape=[jax.ShapeDtypeStruct((n // tp, SUBLANES, tp), I32),
                   jax.ShapeDtypeStruct((nt_pad, LANES), I32)],
        scratch_shapes=[pltpu.VMEM((SUBLANES, LANES), I32)],
        compiler_params=_cparams(("arbitrary",)),
        name="route_plan",
    )(rid, cnt, tril)
    return pos, te


SC_CORES = 2
SC_SUBCORES = 16
SC_WORKERS = SC_CORES * SC_SUBCORES
SC_WINDOW = 64


def _sc_mesh():
    return plsc.VectorSubcoreMesh(core_axis_name="c", subcore_axis_name="s",
                                  num_cores=SC_CORES, num_subcores=SC_SUBCORES)


def _sc_two_buffer_loop(n_win, fill, drain):
    assert n_win % 2 == 0

    def start(copies):
        for cp in copies:
            cp.start()

    def wait(copies):
        for cp in copies:
            cp.wait()

    start(fill(0, 0))

    @pl.loop(0, n_win, step=2)
    def _(j):
        for b in range(2):
            jj = j + b

            @pl.when(jj >= 1)
            def _():
                wait(drain(jj - 1, 1 - b))

            @pl.when(jj + 1 < n_win)
            def _():
                start(fill(jj + 1, 1 - b))

            wait(fill(jj, b))
            start(drain(jj, b))

    wait(drain(n_win - 1, 1))


def _sc_scatter_rows(rows, pos0, pos1, n_out):
    n, d = rows.shape
    per_w = n // SC_WORKERS
    n_win = per_w // SC_WINDOW
    shape3 = (SC_WORKERS, n_win, SC_WINDOW)

    @functools.partial(
        pl.kernel, mesh=_sc_mesh(), out_type=jax.ShapeDtypeStruct((n_out, d), rows.dtype),
        scratch_types=[pltpu.VMEM((n_win, SC_WINDOW), I32), pltpu.VMEM((n_win, SC_WINDOW), I32),
                       pltpu.VMEM((2, SC_WINDOW, d), rows.dtype),
                       pltpu.SemaphoreType.DMA((2,)), pltpu.SemaphoreType.DMA((2,))],
        name="sc_scatter_rows")
    def scatter(rows_hbm, p0_hbm, p1_hbm, out_hbm, p0_v, p1_v, buf, in_sem, out_sem):
        wid = lax.axis_index("s") * SC_CORES + lax.axis_index("c")
        base = wid * per_w
        pltpu.sync_copy(p0_hbm.at[wid], p0_v)
        pltpu.sync_copy(p1_hbm.at[wid], p1_v)

        def fill(j, b):
            src = rows_hbm.at[pl.ds(base + j * SC_WINDOW, SC_WINDOW)]
            return [pltpu.make_async_copy(src, buf.at[b], in_sem.at[b])]

        def drain(j, b):
            return [pltpu.make_async_copy(buf.at[b], out_hbm.at[p.at[j]], out_sem.at[b])
                    for p in (p0_v, p1_v)]

        _sc_two_buffer_loop(n_win, fill, drain)

    return scatter(rows, pos0.reshape(shape3), pos1.reshape(shape3))


def _sc_gather_rows(table, idx):
    m = idx.shape[0]
    d = table.shape[1]
    per_w = m // SC_WORKERS
    n_win = per_w // SC_WINDOW

    @functools.partial(
        pl.kernel, mesh=_sc_mesh(), out_type=jax.ShapeDtypeStruct((m, d), table.dtype),
        scratch_types=[pltpu.VMEM((n_win, SC_WINDOW), I32),
                       pltpu.VMEM((2, SC_WINDOW, d), table.dtype),
                       pltpu.SemaphoreType.DMA((2,)), pltpu.SemaphoreType.DMA((2,))],
        name="sc_gather_rows")
    def gather(table_hbm, idx_hbm, out_hbm, idx_v, buf, in_sem, out_sem):
        wid = lax.axis_index("s") * SC_CORES + lax.axis_index("c")
        base = wid * per_w
        pltpu.sync_copy(idx_hbm.at[wid], idx_v)

        def fill(j, b):
            return [pltpu.make_async_copy(table_hbm.at[idx_v.at[j]], buf.at[b], in_sem.at[b])]

        def drain(j, b):
            dst = out_hbm.at[pl.ds(base + j * SC_WINDOW, SC_WINDOW)]
            return [pltpu.make_async_copy(buf.at[b], dst, out_sem.at[b])]

        _sc_two_buffer_loop(n_win, fill, drain)

    return gather(table, idx.reshape(SC_WORKERS, n_win, SC_WINDOW))


def _expert_kernel(te_ref, nu_ref, valid_ref, x_ref, w1_ref, w3_ref, w2_ref, o_ref,
                   w1b_ref, w3b_ref, w2b_ref):
    j = pl.program_id(0)
    active = j < nu_ref[0]

    @pl.when(active & ((j == 0) | (te_ref[j] != te_ref[jnp.maximum(j - 1, 0)])))
    def _():
        w1b_ref[...] = w1_ref[...].astype(BF16)
        w3b_ref[...] = w3_ref[...].astype(BF16)
        w2b_ref[...] = w2_ref[...].astype(BF16)

    @pl.when(active)
    def _():
        nblk = 2
        rows = x_ref.shape[0] // nblk
        blocks = [slice(i * rows, (i + 1) * rows) for i in range(nblk)]
        row = lax.broadcasted_iota(I32, (rows, x_ref.shape[1]), 0)
        xs = [_unpack_bf16_halves(
            jnp.where(row + i * rows < valid_ref[j], x_ref[sl, :], 0)).astype(BF16)
            for i, sl in enumerate(blocks)]
        h1 = [jnp.dot(x, w1b_ref[...], preferred_element_type=F32) for x in xs]
        h3 = [jnp.dot(x, w3b_ref[...], preferred_element_type=F32) for x in xs]
        hid = [(a * jax.nn.sigmoid(a) * b).astype(BF16) for a, b in zip(h1, h3)]
        for sl, hb in zip(blocks, hid):
            o_ref[sl, :] = _pack_bf16_halves(
                jnp.dot(hb, w2b_ref[...], preferred_element_type=F32))

    @pl.when(j >= nu_ref[0])
    def _():
        o_ref[...] = jnp.zeros_like(o_ref)


def _experts(xs, te, nu, valid, w1, w3, w2):
    n_rows = xs.shape[0]
    d, de = w1.shape[-2:]
    nt = n_rows // EXPERT_TILE
    used = lambda j, nu: jnp.minimum(j, nu[0] - 1)
    grid_spec = pltpu.PrefetchScalarGridSpec(
        num_scalar_prefetch=3,
        grid=(nt,),
        in_specs=[
            pl.BlockSpec((EXPERT_TILE, d // 2), lambda j, te, nu, va: (used(j, nu), 0)),
            pl.BlockSpec((None, d, de), lambda j, te, nu, va: (te[used(j, nu)], 0, 0)),
            pl.BlockSpec((None, d, de), lambda j, te, nu, va: (te[used(j, nu)], 0, 0)),
            pl.BlockSpec((None, de, d), lambda j, te, nu, va: (te[used(j, nu)], 0, 0)),
        ],
        out_specs=pl.BlockSpec((EXPERT_TILE, d // 2), lambda j, te, nu, va: (j, 0)),
        scratch_shapes=[pltpu.VMEM((d, de), BF16), pltpu.VMEM((d, de), BF16),
                        pltpu.VMEM((de, d), BF16)],
    )
    return pl.pallas_call(
        _expert_kernel,
        grid_spec=grid_spec,
        out_shape=jax.ShapeDtypeStruct((n_rows, d // 2), I32),
        compiler_params=_cparams(("arbitrary",)),
        name="expert_mlp",
    )(te, nu, valid, xs, w1, w3, w2)


def _combine_kernel(x1_ref, rw_ref, g2_ref, fg_ref, y0_ref, y1_ref, o_ref):
    rw = rw_ref[...]
    moe = (rw[:, 0:1] * _unpack_bf16_halves(y0_ref[...])
           + rw[:, 1:2] * _unpack_bf16_halves(y1_ref[...]))
    x2 = x1_ref[...] + g2_ref[...] * moe
    ms = jnp.mean(x2 * x2, axis=-1, keepdims=True)
    o_ref[...] = x2 * lax.rsqrt(ms + RMS_EPS) * fg_ref[...]


def _combine(x1, rw, yg, g2, final_g, tc):
    bsz, t, d = x1.shape
    row = lambda n: pl.BlockSpec((None, tc, n), lambda b, i: (b, i, 0))
    slot = lambda s: pl.BlockSpec((None, None, tc, d // 2), lambda b, i: (s, b, i, 0))
    return pl.pallas_call(
        _combine_kernel,
        grid=(bsz, t // tc),
        in_specs=[row(d), row(LANES),
                  pl.BlockSpec((None, 1, d), lambda b, i: (b, 0, 0)),
                  pl.BlockSpec((1, d), lambda b, i: (0, 0)),
                  slot(0), slot(1)],
        out_specs=row(d),
        out_shape=jax.ShapeDtypeStruct((bsz, t, d), F32),
        compiler_params=_cparams(("arbitrary", "arbitrary")),
        name="combine_final_norm",
    )(x1, rw, g2, final_g.reshape(1, d), yg, yg)


def _row_tile(t, want):
    return want if t % want == 0 else t


def kernel(x, c, w_ada, b_ada, norm1_g, w_in, conv_w, conv_b, lru_wa, lru_ba, lru_wi, lru_bi, lru_lam, lru_norm_g, tok_mu, w0, w_up, a0, a_up, g_up, k_k, k_a, r_k, ln_x_w, ln_x_b, w_out, norm2_g, w_grp, b_grp, w_exp, b_exp, w1, w3, w2, final_g):
    bsz, t, d = x.shape
    n = bsz * t
    depth = w_ada.shape[0]
    assert depth == 1, "the combine kernel applies the final norm: only DEPTH == 1 is wired"
    tile = _row_tile(t, ROW_TILE)
    n_tiles = (n * TOP_K) // EXPERT_TILE + N_EXPERTS
    for l in range(depth):
        mod = _modulation(c, w_ada[l], b_ada[l]).reshape(bsz, 6, 1, d)
        sh1, sc1, g1, sh2, sc2, g2 = (mod[:, i] for i in range(6))
        y_lru, p_rw = _inproj_lru(x, sh1, sc1, norm1_g[l], w_in[l], conv_w[l], conv_b[l],
                                  lru_wa[l], lru_ba[l], lru_wi[l], lru_bi[l], lru_lam[l],
                                  lru_norm_g[l], tm=tile)
        y_rw = _rwkv7(p_rw, tok_mu[l], w0[l], w_up[l], a0[l], a_up[l], g_up[l], k_k[l], k_a[l],
                      r_k[l], ln_x_w[l], ln_x_b[l])
        x1, h2, rid, rw, cnt = _mix(x, y_lru, y_rw, w_out[l], g1, sh2, sc2, norm2_g[l],
                                    w_grp[l], b_grp[l], w_exp[l], b_exp[l], tm=tile)
        pos, te = _plan(rid.reshape(n, LANES), cnt, n_tiles)
        pos0, pos1 = pos[:, 0, :].reshape(n), pos[:, 1, :].reshape(n)
        xs = _sc_scatter_rows(h2.reshape(n, d // 2), pos0, pos1, n_tiles * EXPERT_TILE)
        ys = _experts(xs, te[:n_tiles, 0], te[0:1, 1], te[:n_tiles, 2], w1[l], w3[l], w2[l])
        yg = _sc_gather_rows(ys, jnp.concatenate([pos0, pos1]))
        x = _combine(x1, rw, yg.reshape(TOP_K, bsz, t, d // 2), g2, final_g,
                     _row_tile(t, COMBINE_TILE))
    return x
```

```python
import functools

import jax
import jax.numpy as jnp
from jax import lax
from jax.experimental import pallas as pl
from jax.experimental.pallas import tpu as pltpu
from jax.experimental.pallas import tpu_sc as plsc

F32 = jnp.float32
BF16 = jnp.bfloat16
I32 = jnp.int32

LRU_WIDTH = 512
LRU_HEAD_DIM = 64
CONV_WIDTH = 4
LRU_C = 8.0
RWKV_WIDTH = 512
HEAD_DIM = 64
DECAY_LORA = 64
AAA_LORA = 64
GATE_LORA = 128
RWKV_PROJ = 3 * RWKV_WIDTH + DECAY_LORA + AAA_LORA + GATE_LORA
N_GROUPS = 4
EXPERTS_PER_GROUP = 8
N_EXPERTS = N_GROUPS * EXPERTS_PER_GROUP
TOP_K = 2
RMS_EPS = 1e-6
GN_EPS = 64e-5

LANES = 128
SUBLANES = 8
CHUNK = 64
EXPERT_TILE = 512
ROW_TILE = 256
COMBINE_TILE = 512
VMEM_LIMIT = 48 * 1024 * 1024

NN = (((1,), (0,)), ((), ()))
NT = (((1,), (1,)), ((), ()))
TN = (((0,), (0,)), ((), ()))


def _split(x, n):
    if x.dtype == BF16:
        return [x]
    parts = []
    rem = x
    for i in range(n):
        p = rem.astype(BF16)
        parts.append(p)
        if i + 1 < n:
            rem = rem - p.astype(F32)
    return parts


def _mm(a, b, dn=NN, pa=1, pb=1):
    aps = _split(a, pa)
    bps = _split(b, pb)
    order = max(len(aps), len(bps))
    terms = [(i, j) for i in range(len(aps)) for j in range(len(bps)) if i + j < order]
    ka = dn[0][0][0]
    kb = dn[0][1][0]
    if len(terms) > 1 and a.shape[ka] % LANES == 0:
        a_cat = jnp.concatenate([aps[i] for i, _ in terms], axis=ka)
        b_cat = jnp.concatenate([bps[j] for _, j in terms], axis=kb)
        return lax.dot_general(a_cat, b_cat, dn, preferred_element_type=F32)
    out = None
    for i, j in terms:
        t = lax.dot_general(aps[i], bps[j], dn, preferred_element_type=F32)
        out = t if out is None else out + t
    return out


def _presplit(b):
    hi, lo = _split(b, 2)
    return jnp.concatenate([hi, lo, hi], axis=0)


def _mm_presplit(a, b3):
    a_hi, a_lo = _split(a, 2)
    return jnp.dot(jnp.concatenate([a_hi, a_hi, a_lo], axis=1), b3, preferred_element_type=F32)


def _pack_bf16_halves(x):
    n = x.shape[1] // 2
    bits = lax.bitcast_convert_type(x.astype(BF16).astype(F32), I32)
    return bits[:, n:] | ((bits[:, :n] >> 16) & 0xFFFF)


def _unpack_bf16_halves(p):
    lo = lax.bitcast_convert_type(p << 16, F32)
    hi = lax.bitcast_convert_type(p & (-65536), F32)
    return jnp.concatenate([lo, hi], axis=1)


def _softplus(x):
    return jnp.maximum(x, 0.0) + jnp.log1p(jnp.exp(-jnp.abs(x)))


def _cparams(sem):
    return pltpu.CompilerParams(dimension_semantics=sem, vmem_limit_bytes=VMEM_LIMIT)


def _mod_kernel(c_ref, w_ref, b_ref, o_ref):
    c = c_ref[...]
    s = c * jax.nn.sigmoid(c)
    o_ref[...] = _mm(s, w_ref[...], pa=2, pb=2) + b_ref[...]


def _modulation(c, w_ada, b_ada):
    bsz, d = c.shape
    n_out = w_ada.shape[1]
    rows = -(-bsz // SUBLANES) * SUBLANES
    c_pad = jnp.zeros((rows, d), F32).at[:bsz].set(c)
    bn = d
    out = pl.pallas_call(
        _mod_kernel,
        grid=(n_out // bn,),
        in_specs=[
            pl.BlockSpec((rows, d), lambda j: (0, 0)),
            pl.BlockSpec((d, bn), lambda j: (0, j)),
            pl.BlockSpec((1, bn), lambda j: (0, j)),
        ],
        out_specs=pl.BlockSpec((rows, bn), lambda j: (0, j)),
        out_shape=jax.ShapeDtypeStruct((rows, n_out), F32),
        compiler_params=_cparams(("arbitrary",)),
        name="adaln_mod",
    )(c_pad, w_ada, b_ada.reshape(1, n_out))
    return out[:bsz]


def _adaln(x, g, shift, scale):
    ms = jnp.mean(x * x, axis=-1, keepdims=True)
    return (x * lax.rsqrt(ms + RMS_EPS)) * (g * (1.0 + scale)) + shift


def _inproj_lru_kernel(x_ref, sh_ref, sc_ref, g_ref, wl_ref, wr_ref,
                       cw_ref, cb_ref, wab_ref, bab_ref, lam_ref, ng_ref, ones_ref,
                       yl_ref, pr_ref, xprev_ref, hprev_ref):
    @pl.when(pl.program_id(1) == 0)
    def _():
        xprev_ref[...] = jnp.zeros_like(xprev_ref)
        hprev_ref[...] = jnp.zeros_like(hprev_ref)

    w = LRU_WIDTH
    tm = x_ref.shape[0]
    nblk = 2
    rows = tm // nblk
    blocks = [slice(i * rows, (i + 1) * rows) for i in range(nblk)]
    h = [_adaln(x_ref[sl, :], g_ref[...], sh_ref[...], sc_ref[...]).astype(BF16)
         for sl in blocks]
    ux = [jnp.dot(hb, wl_ref[:, :w], preferred_element_type=F32) for hb in h]

    def remaining_columns():
        for sl, hb in zip(blocks, h):
            pr_ref[sl, :] = jnp.dot(hb, wr_ref[...], preferred_element_type=F32)
        return [jnp.dot(hb, wl_ref[:, w:], preferred_element_type=F32) for hb in h]

    y = _lru_tile(ux, remaining_columns, cw_ref, cb_ref, wab_ref, bab_ref, lam_ref, ng_ref,
                  ones_ref, xprev_ref, hprev_ref)
    for sl, yb in zip(blocks, y):
        yl_ref[sl, :] = yb.astype(yl_ref.dtype)


def _inproj_lru(x, sh1, sc1, g, w_in, conv_w, conv_b, wa, ba, wi, bi, lam, norm_g, tm=256):
    bsz, t, d = x.shape
    w = LRU_WIDTH
    nl = 2 * w
    nr = RWKV_PROJ
    wl = w_in[:, :nl].astype(BF16)
    wr = w_in[:, nl:].astype(BF16)
    wab = jnp.concatenate([_block_diag(wa), _block_diag(wi)], axis=1).astype(BF16)
    bab = jnp.concatenate([ba.reshape(1, w), bi.reshape(1, w)], axis=1)
    vec = pl.BlockSpec((None, 1, d), lambda b, i: (b, 0, 0))
    const = lambda shape: pl.BlockSpec(shape, lambda b, i: (0, 0))
    return pl.pallas_call(
        _inproj_lru_kernel,
        grid=(bsz, t // tm),
        in_specs=[
            pl.BlockSpec((None, tm, d), lambda b, i: (b, i, 0)),
            vec, vec, const((1, d)), const((d, nl)), const((d, nr)),
            const((CONV_WIDTH, w)), const((1, w)), const((w, 2 * w)), const((1, 2 * w)),
            const((1, w)), const((1, w)), const((w, w)),
        ],
        out_specs=[
            pl.BlockSpec((None, tm, w), lambda b, i: (b, i, 0)),
            pl.BlockSpec((None, tm, nr), lambda b, i: (b, i, 0)),
        ],
        out_shape=[
            jax.ShapeDtypeStruct((bsz, t, w), BF16),
            jax.ShapeDtypeStruct((bsz, t, nr), F32),
        ],
        scratch_shapes=[pltpu.VMEM((SUBLANES, w), F32), pltpu.VMEM((SUBLANES, w), F32)],
        compiler_params=_cparams(("arbitrary", "arbitrary")),
        name="adaln1_inproj_rglru",
    )(x, sh1, sc1, g.reshape(1, d), wl, wr, conv_w, conv_b.reshape(1, w), wab, bab,
      lam.reshape(1, w), norm_g.reshape(1, w), _head_ones(w, LRU_HEAD_DIM))


def _gelu_tanh(x):
    c = 0.7978845608028654
    return x * (0.5 * (1.0 + jnp.tanh(c * (x + 0.044715 * (x * x * x)))))


def _sigmoid(x):
    return 0.5 * jnp.tanh(0.5 * x) + 0.5


def _lru_tile(ux, gate_branch, cw_ref, cb_ref, wab_ref, bab_ref, lam_ref, ng_ref, ones_ref,
              xprev_ref, hprev_ref):
    w = LRU_WIDTH
    rows = ux[0].shape[0]
    prev8 = [xprev_ref[...]] + [u[rows - SUBLANES:, :] for u in ux[:-1]]
    ext = [jnp.concatenate([p, u], axis=0) for p, u in zip(prev8, ux)]
    xc = [cb_ref[...] + cw_ref[CONV_WIDTH - 1:CONV_WIDTH, :] * u for u in ux]
    for k in range(1, CONV_WIDTH):
        tap = cw_ref[CONV_WIDTH - 1 - k:CONV_WIDTH - k, :]
        xc = [x + tap * pltpu.roll(e, k, 0)[SUBLANES:, :] for x, e in zip(xc, ext)]
    xprev_ref[...] = ux[-1][rows - SUBLANES:, :]

    gates = [jnp.dot(x.astype(BF16), wab_ref[...], preferred_element_type=F32) + bab_ref[...]
             for x in xc]
    ug = gate_branch()
    r = [_sigmoid(g[:, :w]) for g in gates]
    ig = [_sigmoid(g[:, w:]) for g in gates]
    sp = _softplus(-lam_ref[...])
    log_a = [(-LRU_C) * x * sp for x in r]
    a = [jnp.exp(x) for x in log_a]
    th = [jnp.tanh(x) for x in log_a]
    q = [(-2.0 * x) / (1.0 - x) for x in th]
    root_q = [jnp.where(x > 0.0, x * lax.rsqrt(x), 0.0) for x in q]
    b = [s * (i * x) for s, i, x in zip(root_q, ig, xc)]

    row8 = lax.broadcasted_iota(I32, (rows, w), 0) & (SUBLANES - 1)
    acc_a, acc_b = a, b
    for s in (1, 2, 4):
        live = row8 >= s
        sh_a = [pltpu.roll(x, s, 0) for x in acc_a]
        sh_b = [pltpu.roll(x, s, 0) for x in acc_b]
        acc_b = [jnp.where(live, x * sb + y, y) for x, sb, y in zip(acc_a, sh_b, acc_b)]
        acc_a = [jnp.where(live, x * sa, x) for x, sa in zip(acc_a, sh_a)]
    carry = hprev_ref[SUBLANES - 1:SUBLANES, :]
    h = []
    for xa, xb in zip(acc_a, acc_b):
        groups = []
        for gi in range(rows // SUBLANES):
            lo = gi * SUBLANES
            groups.append(xa[lo:lo + SUBLANES, :] * carry + xb[lo:lo + SUBLANES, :])
            last = lo + SUBLANES - 1
            carry = xa[last:last + 1, :] * carry + xb[last:last + 1, :]
        h.append(jnp.concatenate(groups, axis=0))
    hprev_ref[...] = h[-1][rows - SUBLANES:, :]

    y = [x * _gelu_tanh(g) for x, g in zip(h, ug)]
    ms = [_mm(x * x, ones_ref[...]) * (1.0 / LRU_HEAD_DIM) for x in y]
    return [x * lax.rsqrt(m + RMS_EPS) * ng_ref[...] for x, m in zip(y, ms)]


def _block_diag(w):
    h, n, _ = w.shape
    eye = jnp.eye(h, dtype=w.dtype)
    return (eye[:, None, :, None] * w[:, :, None, :]).reshape(h * n, h * n)


def _head_ones(width, head):
    idx = jnp.arange(width) // head
    return (idx[:, None] == idx[None, :]).astype(BF16)


PG = (1, 1)
PI = (1, 1)
PS = (1, 1)
PH = (1, 2)


def _unit_lower_inverse(a_list, ri, ci):
    mm = functools.partial(_mm, pa=PI[0], pb=PI[1])
    eye = jnp.where(ri == ci, 1.0, 0.0)
    leaf = (ri >> 3) == (ci >> 3)
    a8 = [jnp.where(leaf, a, 0.0) for a in a_list]
    a8_2 = [mm(x, x) for x in a8]
    a8_4 = [mm(x, x) for x in a8_2]
    t = [mm(eye + x, eye + y) for x, y in zip(a8, a8_2)]
    t = [mm(x, eye + y) for x, y in zip(t, a8_4)]
    zero = jnp.zeros((LANES, LANES), F32)
    for sh in (3, 4, 5):
        s = 1 << sh
        off = ((ri >> (sh + 1)) == (ci >> (sh + 1))) & ((ri >> sh) != (ci >> sh))
        t_lo = [_second_blocks(x, s) for x in t]
        b_lo = [mm(_second_blocks(jnp.where(off, a, 0.0), s), x) for a, x in zip(a_list, t)]
        d_lo = [mm(x, _interleave_blocks(zero, y, s)) for x, y in zip(t_lo, b_lo)]
        t = [_interleave_blocks(x, y + z, s) for x, y, z in zip(t, t_lo, d_lo)]
    return t


def _second_blocks(x, s):
    return jnp.concatenate(
        [x[s * (2 * m + 1):s * (2 * m + 2)] for m in range(x.shape[0] // (2 * s))], axis=0)


def _interleave_blocks(first_src, second, s):
    parts = []
    for m in range(first_src.shape[0] // (2 * s)):
        parts.append(first_src[2 * s * m:2 * s * m + s])
        parts.append(second[s * m:s * (m + 1)])
    return jnp.concatenate(parts, axis=0)


def _rwkv_kernel(p_ref, mu_ref, pv_ref, wlo_ref, gup_ref, tril_ref, o_ref, uprev_ref, h_ref):
    w = RWKV_WIDTH
    tt = p_ref.shape[0]
    c = CHUNK
    n_pairs = w // LANES
    units = [(j, p) for j in range(tt // c) for p in range(n_pairs)]

    @pl.when(pl.program_id(1) == 0)
    def _():
        uprev_ref[...] = jnp.zeros_like(uprev_ref)
        h_ref[...] = jnp.zeros_like(h_ref)

    u = p_ref[...]
    ext = jnp.concatenate([uprev_ref[...], u], axis=0)
    prev = pltpu.roll(ext, 1, 0)[SUBLANES:, :]
    uprev_ref[...] = u[tt - SUBLANES:, :]
    um = u + (prev - u) * mu_ref[...]

    r = um[:, 0:w]
    k = um[:, w:2 * w]
    v = um[:, 2 * w:3 * w]
    z = um[:, 3 * w:3 * w + LANES]
    gd = um[:, 3 * w + LANES:]
    w0 = pv_ref[0:1, :]
    a0 = pv_ref[1:2, :]
    k_k = pv_ref[2:3, :]
    k_a = pv_ref[3:4, :]
    r_k = pv_ref[4:5, :]
    ln_w = pv_ref[5:6, :]
    ln_b = pv_ref[6:7, :]

    lane_t = lax.broadcasted_iota(I32, (tt, LANES), 1)
    zz = jnp.where(lane_t < HEAD_DIM, jnp.tanh(z), z)
    lora = _mm_presplit(zz, wlo_ref[...])
    ld = (-0.6065306597126334) * jax.nn.sigmoid(w0 + lora[:, :w])
    a = _sigmoid(a0 + lora[:, w:])
    g = _mm_presplit(_sigmoid(gd), gup_ref[...])

    ri = lax.broadcasted_iota(I32, (LANES, LANES), 0)
    ci = lax.broadcasted_iota(I32, (LANES, LANES), 1)
    even = lax.broadcasted_iota(I32, (c, LANES), 1) < HEAD_DIM
    wide = 2 * LANES
    ones_bd = jnp.where(
        (lax.broadcasted_iota(I32, (wide, wide), 0) >> 6)
        == (lax.broadcasted_iota(I32, (wide, wide), 1) >> 6), 1.0, 0.0).astype(BF16)

    def head_sum(x):
        return jnp.concatenate(
            [_mm(x[:, q * wide:(q + 1) * wide], ones_bd) for q in range(w // wide)], axis=1)

    kk = k * k_k
    kk = kk * lax.rsqrt(jnp.maximum(head_sum(kk * kk), 1e-24))
    kp = k * (1.0 + (a - 1.0) * k_a)
    kka = kk * a
    bonus = head_sum(r * kp * r_k) * v
    cum = _mm(tril_ref[...], ld, pb=3)
    cum_c = [cum[j * c + c - 1:j * c + c, :] for j in range(tt // c)]
    p_c = [jnp.exp(x) for x in cum_c]
    p_inv = jnp.exp(-cum)
    al = -kk * jnp.exp(cum - ld)
    rt = r * jnp.exp(cum)
    bt = kka * p_inv
    kt = kp * p_inv

    def to_chunk_end(x):
        return jnp.concatenate(
            [x[j * c:(j + 1) * c, :] * p_c[j] for j in range(tt // c)], axis=0)

    bh = to_chunk_end(bt)
    kh = to_chunk_end(kt)

    def blk(x):
        return [x[j * c:(j + 1) * c, p * LANES:(p + 1) * LANES] for j, p in units]

    def halves(x):
        xs = blk(x)
        return [jnp.where(even, y, 0.0) for y in xs], [jnp.where(even, 0.0, y) for y in xs]

    def rows(top, bot):
        return [jnp.concatenate([x, y], axis=0) for x, y in zip(top, bot)]

    def unstack(x):
        return x[:c, :] + x[c:, :]

    al_e, al_o = halves(al)
    rt_e, rt_o = halves(rt)
    bt_b, kt_b = blk(bt), blk(kt)
    bh_e, bh_o = halves(bh)
    kh_e, kh_o = halves(kh)
    v_e, v_o = halves(v)
    al_n = rows(al_e, al_o)
    bh_n = rows(bh_e, bh_o)
    v_s = rows(v_o, v_e)
    kh_s = rows(kh_o, kh_e)
    rt_b = blk(rt)
    nu = range(len(units))

    g0 = [_mm(x, y, NT, pa=PG[0], pb=PG[1])
          for x, y in zip(rows(al_e, rt_e), rows(bt_b, kt_b))]
    g1 = [_mm(x, y, NT, pa=PG[0], pb=PG[1])
          for x, y in zip(rows(rt_o, al_o), rows(kt_b, bt_b))]
    top = ri < c
    left = ci < c
    tri_s = (ri & (c - 1)) > (ci & (c - 1))
    tri_i = (ri & (c - 1)) >= (ci & (c - 1))
    diag_q = top == left

    def pick(x0, x1, in_q0, tri):
        return [jnp.where(tri, jnp.where(in_q0, x, y), 0.0) for x, y in zip(x0, x1)]

    a_ab = pick(g0, g1, top, diag_q & tri_s)
    a_rk = pick(g1, g0, top, diag_q & tri_i)
    a_ak = pick(g0, g1, top, (~diag_q) & tri_s)
    a_rb = pick(g1, g0, top, (~diag_q) & tri_i)
    x1 = [_mm(a_ak[n], v_s[n], pa=PS[0], pb=PS[1]) for n in nu]
    akv = [_mm(a_rk[n], v_s[n], pa=PS[0], pb=PS[1]) for n in nu]
    khv = [_mm(kh_s[n], v_s[n], TN, pa=PS[0], pb=PS[1]) for n in nu]
    t_inv = _unit_lower_inverse(a_ab, ri, ci)
    tw = [_mm(t_inv[n], jnp.concatenate([al_n[n], x1[n]], axis=1), pa=PS[0], pb=PS[1])
          for n in nu]
    qo = [_mm(a_rb[n], tw[n], pa=PS[0], pb=PS[1]) for n in nu]
    mn = [_mm(bh_n[n], tw[n], TN, pa=PS[0], pb=PS[1]) for n in nu]
    q = [rt_b[n] + unstack(qo[n][:, :LANES]) for n in nu]
    o_loc = [unstack(qo[n][:, LANES:] + akv[n]) for n in nu]
    m_full = [mn[n][:, :LANES]
              + jnp.where(ri == ci, p_c[j][:, p * LANES:(p + 1) * LANES], 0.0)
              for n, (j, p) in enumerate(units)]
    n_loc = [mn[n][:, LANES:] + khv[n] for n in nu]

    h = [h_ref[p] for p in range(n_pairs)]
    o_rows = []
    for j in range(tt // c):
        o_parts = []
        for p in range(n_pairs):
            n = j * n_pairs + p
            o_parts.append(_mm(q[n], h[p], pa=PH[0], pb=PH[1]) + o_loc[n])
            h[p] = _mm(m_full[n], h[p], pa=PH[0], pb=PH[1]) + n_loc[n]
        o_rows.append(jnp.concatenate(o_parts, axis=1))
    for p in range(n_pairs):
        h_ref[p] = h[p]

    o = jnp.concatenate(o_rows, axis=0)
    mean = head_sum(o) * (1.0 / HEAD_DIM)
    d = o - mean
    var = head_sum(d * d) * (1.0 / HEAD_DIM)
    on = d * lax.rsqrt(var + GN_EPS) * ln_w + ln_b
    o_ref[...] = ((on + bonus) * g).astype(o_ref.dtype)


RWKV_TILE = 256


def _rwkv7(p_rw, mu, w0, w_up, a0, a_up, g_up, k_k, k_a, r_k, ln_w, ln_b):
    bsz, t, npj = p_rw.shape
    w = RWKV_WIDTH
    tt = RWKV_TILE
    pv = jnp.stack([w0, a0, k_k, k_a, r_k.reshape(w), ln_w, ln_b, jnp.zeros((w,), F32)])
    wlo = jnp.zeros((LANES, 2 * w), F32)
    wlo = wlo.at[:DECAY_LORA, :w].set(w_up).at[DECAY_LORA:, w:].set(a_up)
    row = jnp.arange(tt)
    tril = ((row[:, None] >= row[None, :])
            & (row[:, None] // CHUNK == row[None, :] // CHUNK)).astype(BF16)
    const = lambda shape: pl.BlockSpec(shape, lambda b, i: (0, 0))
    return pl.pallas_call(
        _rwkv_kernel,
        grid=(bsz, t // tt),
        in_specs=[
            pl.BlockSpec((None, tt, npj), lambda b, i: (b, i, 0)),
            const((1, npj)), const((SUBLANES, w)), const((3 * LANES, 2 * w)),
            const((3 * GATE_LORA, w)), const((tt, tt)),
        ],
        out_specs=pl.BlockSpec((None, tt, w), lambda b, i: (b, i, 0)),
        out_shape=jax.ShapeDtypeStruct((bsz, t, w), BF16),
        scratch_shapes=[pltpu.VMEM((SUBLANES, npj), F32),
                        pltpu.VMEM((w // LANES, LANES, LANES), F32)],
        compiler_params=_cparams(("arbitrary", "arbitrary")),
        name="rwkv7",
    )(p_rw, mu.reshape(1, npj), pv, _presplit(wlo), _presplit(g_up), tril)


def _mix_kernel(x_ref, yl_ref, yr_ref, wo1_ref, wo2_ref, g1_ref, sh_ref, sc_ref, ng_ref,
                wr_ref, br_ref, x1_ref, h2_ref, rid_ref, rw_ref, cnt_ref):
    tm = x_ref.shape[0]
    nblk = 2
    rows = tm // nblk
    blocks = [slice(i * rows, (i + 1) * rows) for i in range(nblk)]
    y = [jnp.dot(yl_ref[sl, :], wo1_ref[...], preferred_element_type=F32)
         + jnp.dot(yr_ref[sl, :], wo2_ref[...], preferred_element_type=F32) for sl in blocks]
    x1 = [x_ref[sl, :] + g1_ref[...] * yb for sl, yb in zip(blocks, y)]
    h2 = [_adaln(xb, ng_ref[...], sh_ref[...], sc_ref[...]).astype(BF16) for xb in x1]
    lg_all = [jnp.dot(jnp.concatenate([hb, hb], axis=1), wr_ref[...],
                      preferred_element_type=F32) + br_ref[...] for hb in h2]
    for sl, xb, hb in zip(blocks, x1, h2):
        x1_ref[sl, :] = xb
        h2_ref[sl, :] = _pack_bf16_halves(hb)

    neg = -jnp.inf
    lane = lax.broadcasted_iota(I32, (rows, LANES), 1)
    lane_f = lane.astype(F32)

    def first_argmax(xs):
        m = [jnp.max(x, axis=-1, keepdims=True) for x in xs]
        idx = [jnp.min(jnp.where(x == mi, lane_f, float(LANES)), axis=-1, keepdims=True)
               for x, mi in zip(xs, m)]
        return m, [i.astype(I32) for i in idx]

    lg = [jnp.where(lane < N_GROUPS, x, neg) for x in lg_all]
    gm, g_idx = first_argmax(lg)
    g_w = [1.0 / jnp.sum(jnp.exp(x - m), axis=-1, keepdims=True)
           for x, m in zip(lg, gm)]
    lo = [N_GROUPS + EXPERTS_PER_GROUP * g for g in g_idx]
    le = [jnp.where((lane >= l) & (lane < l + EXPERTS_PER_GROUP), x, neg)
          for x, l in zip(lg_all, lo)]
    v1, i1 = first_argmax(le)
    v2, i2 = first_argmax([jnp.where(lane == i, neg, x) for x, i in zip(le, i1)])

    @pl.when((pl.program_id(0) == 0) & (pl.program_id(1) == 0))
    def _():
        cnt_ref[...] = jnp.zeros_like(cnt_ref)

    e_lane = lane + N_GROUPS
    hits = sum(jnp.sum(jnp.where((e_lane == a) | (e_lane == b), 1.0, 0.0), axis=0, keepdims=True)
               for a, b in zip(i1, i2))
    cnt_ref[...] = cnt_ref[...] + hits.astype(I32)

    for blk in range(nblk):
        e2 = jnp.exp(v2[blk] - v1[blk])
        w1 = g_w[blk] / (1.0 + e2)
        w2 = g_w[blk] * e2 / (1.0 + e2)
        sl = blocks[blk]
        rid_ref[sl, :] = jnp.where(lane == 0, i1[blk] - N_GROUPS,
                                   jnp.where(lane == 1, i2[blk] - N_GROUPS, 0))
        rw_ref[sl, :] = jnp.where(lane == 0, w1, jnp.where(lane == 1, w2, 0.0))


def _mix(x, y_lru, y_rw, w_out, g1, sh2, sc2, ng, w_grp, b_grp, w_exp, b_exp, tm=256):
    bsz, t, d = x.shape
    wo = w_out.astype(BF16)
    wl = LRU_WIDTH
    wr = jnp.zeros((d, LANES), F32).at[:, :N_GROUPS].set(w_grp)
    wr = wr.at[:, N_GROUPS:N_GROUPS + N_EXPERTS].set(w_exp)
    wr = jnp.concatenate(_split(wr, 2), axis=0)
    br = jnp.zeros((1, LANES), F32).at[0, :N_GROUPS].set(b_grp)
    br = br.at[0, N_GROUPS:N_GROUPS + N_EXPERTS].set(b_exp)
    vec = pl.BlockSpec((None, 1, d), lambda b, i: (b, 0, 0))
    const = lambda shape: pl.BlockSpec(shape, lambda b, i: (0, 0))
    row = lambda n: pl.BlockSpec((None, tm, n), lambda b, i: (b, i, 0))
    return pl.pallas_call(
        _mix_kernel,
        grid=(bsz, t // tm),
        in_specs=[row(d), row(wl), row(d - wl), const((wl, d)), const((d - wl, d)),
                  vec, vec, vec, const((1, d)), const((2 * d, LANES)), const((1, LANES))],
        out_specs=[row(d), row(d // 2), row(LANES), row(LANES), const((SUBLANES, LANES))],
        out_shape=[jax.ShapeDtypeStruct((bsz, t, d), F32),
                   jax.ShapeDtypeStruct((bsz, t, d // 2), I32),
                   jax.ShapeDtypeStruct((bsz, t, LANES), I32),
                   jax.ShapeDtypeStruct((bsz, t, LANES), F32),
                   jax.ShapeDtypeStruct((SUBLANES, LANES), I32)],
        compiler_params=_cparams(("arbitrary", "arbitrary")),
        name="outproj_adaln2_router",
    )(x, y_lru, y_rw, wo[:wl], wo[wl:], g1, sh2, sc2, ng.reshape(1, d), wr, br)


def _plan_kernel(rid_ref, cnt_ref, tril_ref, pos_ref, te_ref, base_ref):
    i = pl.program_id(0)
    tp = rid_ref.shape[0]
    lane = lax.broadcasted_iota(I32, (tp, LANES), 1)
    rid = rid_ref[...]
    oh0 = lane == rid[:, 0:1]
    oh1 = lane == rid[:, 1:2]
    oh = jnp.where(oh0 | oh1, 1.0, 0.0)

    @pl.when(i == 0)
    def _():
        cnt = cnt_ref[...]
        shift = EXPERT_TILE.bit_length() - 1
        padded = ((cnt + (EXPERT_TILE - 1)) >> shift) << shift
        l8 = lax.broadcasted_iota(I32, (SUBLANES, LANES), 1)
        end = padded
        s = 1
        while s < LANES:
            end = end + jnp.where(l8 >= s, pltpu.roll(end, s, 1), 0)
            s *= 2
        base_ref[...] = end - padded
        nt = te_ref.shape[0]
        j = lax.broadcasted_iota(I32, (nt, LANES), 0) * EXPERT_TILE
        lt = lax.broadcasted_iota(I32, (nt, LANES), 1)
        done = jnp.where((lt < N_EXPERTS) & (end[0:1, :] <= j), 1.0, 0.0)
        e_of = jnp.minimum(jnp.sum(done, axis=-1, keepdims=True), float(N_EXPERTS - 1)).astype(I32)
        total = end[0:1, N_EXPERTS - 1:N_EXPERTS]
        mine = lt == e_of
        cnt_e = jnp.sum(jnp.where(mine, cnt[0:1, :], 0), axis=-1, keepdims=True)
        start_e = jnp.sum(jnp.where(mine, end[0:1, :] - padded[0:1, :], 0), axis=-1, keepdims=True)
        valid = jnp.clip(cnt_e - (j[:, 0:1] - start_e), 0, EXPERT_TILE)
        te_ref[...] = jnp.where(lt == 0, e_of, jnp.where(lt == 1, total >> shift,
                                                        jnp.where(lt == 2, valid, 0)))

    prefix = jnp.dot(tril_ref[...], oh.astype(BF16), preferred_element_type=F32)
    dest = base_ref[0:1, :] + prefix.astype(I32)
    pos0 = jnp.sum(jnp.where(oh0, dest, 0), axis=-1, keepdims=True)
    pos1 = jnp.sum(jnp.where(oh1, dest, 0), axis=-1, keepdims=True)
    both = jnp.where(lane == 0, pos0, jnp.where(lane == 1, pos1, 0))
    pos_ref[...] = both.T[:SUBLANES, :]
    base_ref[...] = base_ref[...] + jnp.sum(oh, axis=0, keepdims=True).astype(I32)


def _plan(rid, cnt, n_tiles, tp=512):
    n = rid.shape[0]
    tp = min(tp, n)
    nt_pad = -(-n_tiles // SUBLANES) * SUBLANES
    tril = (jnp.arange(tp)[:, None] > jnp.arange(tp)[None, :]).astype(BF16)
    pos, te = pl.pallas_call(
        _plan_kernel,
        grid=(n // tp,),
        in_specs=[pl.BlockSpec((tp, LANES), lambda i: (i, 0)),
                  pl.BlockSpec((SUBLANES, LANES), lambda i: (0, 0)),
                  pl.BlockSpec((tp, tp), lambda i: (0, 0))],
        out_specs=[pl.BlockSpec((None, SUBLANES, tp), lambda i: (i, 0, 0)),
                   pl.BlockSpec((nt_pad, LANES), lambda i: (0, 0))],
        out_shape=[jax.ShapeDtypeStruct((n // tp, SUBLANES, tp), I32),
                   jax.ShapeDtypeStruct((nt_pad, LANES), I32)],
        scratch_shapes=[pltpu.VMEM((SUBLANES, LANES), I32)],
        compiler_params=_cparams(("arbitrary",)),
        name="route_plan",
    )(rid, cnt, tril)
    return pos, te


SC_CORES = 2
SC_SUBCORES = 16
SC_WORKERS = SC_CORES * SC_SUBCORES
SC_WINDOW = 64


def _sc_mesh():
    return plsc.VectorSubcoreMesh(core_axis_name="c", subcore_axis_name="s",
                                  num_cores=SC_CORES, num_subcores=SC_SUBCORES)


def _sc_two_buffer_loop(n_win, fill, drain):
    assert n_win % 2 == 0

    def start(copies):
        for cp in copies:
            cp.start()

    def wait(copies):
        for cp in copies:
            cp.wait()

    start(fill(0, 0))

    @pl.loop(0, n_win, step=2)
    def _(j):
        for b in range(2):
            jj = j + b

            @pl.when(jj >= 1)
            def _():
                wait(drain(jj - 1, 1 - b))

            @pl.when(jj + 1 < n_win)
            def _():
                start(fill(jj + 1, 1 - b))

            wait(fill(jj, b))
            start(drain(jj, b))

    wait(drain(n_win - 1, 1))


def _sc_scatter_rows(rows, pos0, pos1, n_out):
    n, d = rows.shape
    per_w = n // SC_WORKERS
    n_win = per_w // SC_WINDOW
    shape3 = (SC_WORKERS, n_win, SC_WINDOW)

    @functools.partial(
        pl.kernel, mesh=_sc_mesh(), out_type=jax.ShapeDtypeStruct((n_out, d), rows.dtype),
        scratch_types=[pltpu.VMEM((n_win, SC_WINDOW), I32), pltpu.VMEM((n_win, SC_WINDOW), I32),
                       pltpu.VMEM((2, SC_WINDOW, d), rows.dtype),
                       pltpu.SemaphoreType.DMA((2,)), pltpu.SemaphoreType.DMA((2,))],
        name="sc_scatter_rows")
    def scatter(rows_hbm, p0_hbm, p1_hbm, out_hbm, p0_v, p1_v, buf, in_sem, out_sem):
        wid = lax.axis_index("s") * SC_CORES + lax.axis_index("c")
        base = wid * per_w
        pltpu.sync_copy(p0_hbm.at[wid], p0_v)
        pltpu.sync_copy(p1_hbm.at[wid], p1_v)

        def fill(j, b):
            src = rows_hbm.at[pl.ds(base + j * SC_WINDOW, SC_WINDOW)]
            return [pltpu.make_async_copy(src, buf.at[b], in_sem.at[b])]

        def drain(j, b):
            return [pltpu.make_async_copy(buf.at[b], out_hbm.at[p.at[j]], out_sem.at[b])
                    for p in (p0_v, p1_v)]

        _sc_two_buffer_loop(n_win, fill, drain)

    return scatter(rows, pos0.reshape(shape3), pos1.reshape(shape3))


def _sc_gather_rows(table, idx):
    m = idx.shape[0]
    d = table.shape[1]
    per_w = m // SC_WORKERS
    n_win = per_w // SC_WINDOW

    @functools.partial(
        pl.kernel, mesh=_sc_mesh(), out_type=jax.ShapeDtypeStruct((m, d), table.dtype),
        scratch_types=[pltpu.VMEM((n_win, SC_WINDOW), I32),
                       pltpu.VMEM((2, SC_WINDOW, d), table.dtype),
                       pltpu.SemaphoreType.DMA((2,)), pltpu.SemaphoreType.DMA((2,))],
        name="sc_gather_rows")
    def gather(table_hbm, idx_hbm, out_hbm, idx_v, buf, in_sem, out_sem):
        wid = lax.axis_index("s") * SC_CORES + lax.axis_index("c")
        base = wid * per_w
        pltpu.sync_copy(idx_hbm.at[wid], idx_v)

        def fill(j, b):
            return [pltpu.make_async_copy(table_hbm.at[idx_v.at[j]], buf.at[b], in_sem.at[b])]

        def drain(j, b):
            dst = out_hbm.at[pl.ds(base + j * SC_WINDOW, SC_WINDOW)]
            return [pltpu.make_async_copy(buf.at[b], dst, out_sem.at[b])]

        _sc_two_buffer_loop(n_win, fill, drain)

    return gather(table, idx.reshape(SC_WORKERS, n_win, SC_WINDOW))


def _expert_kernel(te_ref, nu_ref, valid_ref, x_ref, w1_ref, w3_ref, w2_ref, o_ref,
                   w1b_ref, w3b_ref, w2b_ref):
    j = pl.program_id(0)
    active = j < nu_ref[0]

    @pl.when(active & ((j == 0) | (te_ref[j] != te_ref[jnp.maximum(j - 1, 0)])))
    def _():
        w1b_ref[...] = w1_ref[...].astype(BF16)
        w3b_ref[...] = w3_ref[...].astype(BF16)
        w2b_ref[...] = w2_ref[...].astype(BF16)

    @pl.when(active)
    def _():
        nblk = 2
        rows = x_ref.shape[0] // nblk
        blocks = [slice(i * rows, (i + 1) * rows) for i in range(nblk)]
        row = lax.broadcasted_iota(I32, (rows, x_ref.shape[1]), 0)
        xs = [_unpack_bf16_halves(
            jnp.where(row + i * rows < valid_ref[j], x_ref[sl, :], 0)).astype(BF16)
            for i, sl in enumerate(blocks)]
        h1 = [jnp.dot(x, w1b_ref[...], preferred_element_type=F32) for x in xs]
        h3 = [jnp.dot(x, w3b_ref[...], preferred_element_type=F32) for x in xs]
        hid = [(a * jax.nn.sigmoid(a) * b).astype(BF16) for a, b in zip(h1, h3)]
        for sl, hb in zip(blocks, hid):
            o_ref[sl, :] = _pack_bf16_halves(
                jnp.dot(hb, w2b_ref[...], preferred_element_type=F32))

    @pl.when(j >= nu_ref[0])
    def _():
        o_ref[...] = jnp.zeros_like(o_ref)


def _experts(xs, te, nu, valid, w1, w3, w2):
    n_rows = xs.shape[0]
    d, de = w1.shape[-2:]
    nt = n_rows // EXPERT_TILE
    used = lambda j, nu: jnp.minimum(j, nu[0] - 1)
    grid_spec = pltpu.PrefetchScalarGridSpec(
        num_scalar_prefetch=3,
        grid=(nt,),
        in_specs=[
            pl.BlockSpec((EXPERT_TILE, d // 2), lambda j, te, nu, va: (used(j, nu), 0)),
            pl.BlockSpec((None, d, de), lambda j, te, nu, va: (te[used(j, nu)], 0, 0)),
            pl.BlockSpec((None, d, de), lambda j, te, nu, va: (te[used(j, nu)], 0, 0)),
            pl.BlockSpec((None, de, d), lambda j, te, nu, va: (te[used(j, nu)], 0, 0)),
        ],
        out_specs=pl.BlockSpec((EXPERT_TILE, d // 2), lambda j, te, nu, va: (j, 0)),
        scratch_shapes=[pltpu.VMEM((d, de), BF16), pltpu.VMEM((d, de), BF16),
                        pltpu.VMEM((de, d), BF16)],
    )
    return pl.pallas_call(
        _expert_kernel,
        grid_spec=grid_spec,
        out_shape=jax.ShapeDtypeStruct((n_rows, d // 2), I32),
        compiler_params=_cparams(("arbitrary",)),
        name="expert_mlp",
    )(te, nu, valid, xs, w1, w3, w2)


def _combine_kernel(x1_ref, rw_ref, g2_ref, fg_ref, y0_ref, y1_ref, o_ref):
    rw = rw_ref[...]
    moe = (rw[:, 0:1] * _unpack_bf16_halves(y0_ref[...])
           + rw[:, 1:2] * _unpack_bf16_halves(y1_ref[...]))
    x2 = x1_ref[...] + g2_ref[...] * moe
    ms = jnp.mean(x2 * x2, axis=-1, keepdims=True)
    o_ref[...] = x2 * lax.rsqrt(ms + RMS_EPS) * fg_ref[...]


def _combine(x1, rw, yg, g2, final_g, tc):
    bsz, t, d = x1.shape
    row = lambda n: pl.BlockSpec((None, tc, n), lambda b, i: (b, i, 0))
    slot = lambda s: pl.BlockSpec((None, None, tc, d // 2), lambda b, i: (s, b, i, 0))
    return pl.pallas_call(
        _combine_kernel,
        grid=(bsz, t // tc),
        in_specs=[row(d), row(LANES),
                  pl.BlockSpec((None, 1, d), lambda b, i: (b, 0, 0)),
                  pl.BlockSpec((1, d), lambda b, i: (0, 0)),
                  slot(0), slot(1)],
        out_specs=row(d),
        out_shape=jax.ShapeDtypeStruct((bsz, t, d), F32),
        compiler_params=_cparams(("arbitrary", "arbitrary")),
        name="combine_final_norm",
    )(x1, rw, g2, final_g.reshape(1, d), yg, yg)


def _row_tile(t, want):
    return want if t % want == 0 else t


def kernel(x, c, w_ada, b_ada, norm1_g, w_in, conv_w, conv_b, lru_wa, lru_ba, lru_wi, lru_bi, lru_lam, lru_norm_g, tok_mu, w0, w_up, a0, a_up, g_up, k_k, k_a, r_k, ln_x_w, ln_x_b, w_out, norm2_g, w_grp, b_grp, w_exp, b_exp, w1, w3, w2, final_g):
    bsz, t, d = x.shape
    n = bsz * t
    depth = w_ada.shape[0]
    assert depth == 1, "the combine kernel applies the final norm: only DEPTH == 1 is wired"
    tile = _row_tile(t, ROW_TILE)
    n_tiles = (n * TOP_K) // EXPERT_TILE + N_EXPERTS
    for l in range(depth):
        mod = _modulation(c, w_ada[l], b_ada[l]).reshape(bsz, 6, 1, d)
        sh1, sc1, g1, sh2, sc2, g2 = (mod[:, i] for i in range(6))
        y_lru, p_rw = _inproj_lru(x, sh1, sc1, norm1_g[l], w_in[l], conv_w[l], conv_b[l],
                                  lru_wa[l], lru_ba[l], lru_wi[l], lru_bi[l], lru_lam[l],
                                  lru_norm_g[l], tm=tile)
        y_rw = _rwkv7(p_rw, tok_mu[l], w0[l], w_up[l], a0[l], a_up[l], g_up[l], k_k[l], k_a[l],
                      r_k[l], ln_x_w[l], ln_x_b[l])
        x1, h2, rid, rw, cnt = _mix(x, y_lru, y_rw, w_out[l], g1, sh2, sc2, norm2_g[l],
                                    w_grp[l], b_grp[l], w_exp[l], b_exp[l], tm=tile)
        pos, te = _plan(rid.reshape(n, LANES), cnt, n_tiles)
        pos0, pos1 = pos[:, 0, :].reshape(n), pos[:, 1, :].reshape(n)
        xs = _sc_scatter_rows(h2.reshape(n, d // 2), pos0, pos1, n_tiles * EXPERT_TILE)
        ys = _experts(xs, te[:n_tiles, 0], te[0:1, 1], te[:n_tiles, 2], w1[l], w3[l], w2[l])
        yg = _sc_gather_rows(ys, jnp.concatenate([pos0, pos1]))
        x = _combine(x1, rw, yg.reshape(TOP_K, bsz, t, d // 2), g2, final_g,
                     _row_tile(t, COMBINE_TILE))
    return x
```

```python
import functools

import jax
import jax.numpy as jnp
from jax import lax
from jax.experimental import pallas as pl
from jax.experimental.pallas import tpu as pltpu
from jax.experimental.pallas import tpu_sc as plsc

F32 = jnp.float32
BF16 = jnp.bfloat16
I32 = jnp.int32

LRU_WIDTH = 512
LRU_HEAD_DIM = 64
CONV_WIDTH = 4
LRU_C = 8.0
RWKV_WIDTH = 512
HEAD_DIM = 64
DECAY_LORA = 64
AAA_LORA = 64
GATE_LORA = 128
RWKV_PROJ = 3 * RWKV_WIDTH + DECAY_LORA + AAA_LORA + GATE_LORA
N_GROUPS = 4
EXPERTS_PER_GROUP = 8
N_EXPERTS = N_GROUPS * EXPERTS_PER_GROUP
TOP_K = 2
RMS_EPS = 1e-6
GN_EPS = 64e-5

LANES = 128
SUBLANES = 8
CHUNK = 64
EXPERT_TILE = 512
ROW_TILE = 256
COMBINE_TILE = 1024
VMEM_LIMIT = 48 * 1024 * 1024

NN = (((1,), (0,)), ((), ()))
NT = (((1,), (1,)), ((), ()))
TN = (((0,), (0,)), ((), ()))


def _split(x, n):
    if x.dtype == BF16:
        return [x]
    parts = []
    rem = x
    for i in range(n):
        p = rem.astype(BF16)
        parts.append(p)
        if i + 1 < n:
            rem = rem - p.astype(F32)
    return parts


def _mm(a, b, dn=NN, pa=1, pb=1):
    aps = _split(a, pa)
    bps = _split(b, pb)
    order = max(len(aps), len(bps))
    terms = [(i, j) for i in range(len(aps)) for j in range(len(bps)) if i + j < order]
    ka = dn[0][0][0]
    kb = dn[0][1][0]
    if len(terms) > 1 and a.shape[ka] % LANES == 0:
        a_cat = jnp.concatenate([aps[i] for i, _ in terms], axis=ka)
        b_cat = jnp.concatenate([bps[j] for _, j in terms], axis=kb)
        return lax.dot_general(a_cat, b_cat, dn, preferred_element_type=F32)
    out = None
    for i, j in terms:
        t = lax.dot_general(aps[i], bps[j], dn, preferred_element_type=F32)
        out = t if out is None else out + t
    return out


def _presplit(b):
    hi, lo = _split(b, 2)
    return jnp.concatenate([hi, lo, hi], axis=0)


def _mm_presplit(a, b3):
    a_hi, a_lo = _split(a, 2)
    return jnp.dot(jnp.concatenate([a_hi, a_hi, a_lo], axis=1), b3, preferred_element_type=F32)


def _pack_bf16_halves(x):
    n = x.shape[1] // 2
    bits = lax.bitcast_convert_type(x.astype(BF16).astype(F32), I32)
    return bits[:, n:] | ((bits[:, :n] >> 16) & 0xFFFF)


def _unpack_bf16_halves(p):
    lo = lax.bitcast_convert_type(p << 16, F32)
    hi = lax.bitcast_convert_type(p & (-65536), F32)
    return jnp.concatenate([lo, hi], axis=1)


def _softplus(x):
    return jnp.maximum(x, 0.0) + jnp.log1p(jnp.exp(-jnp.abs(x)))


def _cparams(sem):
    return pltpu.CompilerParams(dimension_semantics=sem, vmem_limit_bytes=VMEM_LIMIT)


def _mod_kernel(c_ref, w_ref, b_ref, o_ref):
    c = c_ref[...]
    s = c * jax.nn.sigmoid(c)
    o_ref[...] = _mm(s, w_ref[...], pa=2, pb=2) + b_ref[...]


def _modulation(c, w_ada, b_ada):
    bsz, d = c.shape
    n_out = w_ada.shape[1]
    rows = -(-bsz // SUBLANES) * SUBLANES
    c_pad = jnp.zeros((rows, d), F32).at[:bsz].set(c)
    bn = d
    out = pl.pallas_call(
        _mod_kernel,
        grid=(n_out // bn,),
        in_specs=[
            pl.BlockSpec((rows, d), lambda j: (0, 0)),
            pl.BlockSpec((d, bn), lambda j: (0, j)),
            pl.BlockSpec((1, bn), lambda j: (0, j)),
        ],
        out_specs=pl.BlockSpec((rows, bn), lambda j: (0, j)),
        out_shape=jax.ShapeDtypeStruct((rows, n_out), F32),
        compiler_params=_cparams(("arbitrary",)),
        name="adaln_mod",
    )(c_pad, w_ada, b_ada.reshape(1, n_out))
    return out[:bsz]


def _adaln(x, g, shift, scale):
    ms = jnp.mean(x * x, axis=-1, keepdims=True)
    return (x * lax.rsqrt(ms + RMS_EPS)) * (g * (1.0 + scale)) + shift


def _inproj_lru_kernel(x_ref, sh_ref, sc_ref, g_ref, wl_ref, wr_ref,
                       cw_ref, cb_ref, wab_ref, bab_ref, lam_ref, ng_ref, ones_ref,
                       yl_ref, pr_ref, xprev_ref, hprev_ref):
    @pl.when(pl.program_id(1) == 0)
    def _():
        xprev_ref[...] = jnp.zeros_like(xprev_ref)
        hprev_ref[...] = jnp.zeros_like(hprev_ref)

    w = LRU_WIDTH
    tm = x_ref.shape[0]
    nblk = 2
    rows = tm // nblk
    blocks = [slice(i * rows, (i + 1) * rows) for i in range(nblk)]
    h = [_adaln(x_ref[sl, :], g_ref[...], sh_ref[...], sc_ref[...]).astype(BF16)
         for sl in blocks]
    ux = [jnp.dot(hb, wl_ref[:, :w], preferred_element_type=F32) for hb in h]

    def remaining_columns():
        for sl, hb in zip(blocks, h):
            pr_ref[sl, :] = jnp.dot(hb, wr_ref[...], preferred_element_type=F32)
        return [jnp.dot(hb, wl_ref[:, w:], preferred_element_type=F32) for hb in h]

    y = _lru_tile(ux, remaining_columns, cw_ref, cb_ref, wab_ref, bab_ref, lam_ref, ng_ref,
                  ones_ref, xprev_ref, hprev_ref)
    for sl, yb in zip(blocks, y):
        yl_ref[sl, :] = yb.astype(yl_ref.dtype)


def _inproj_lru(x, sh1, sc1, g, w_in, conv_w, conv_b, wa, ba, wi, bi, lam, norm_g, tm=256):
    bsz, t, d = x.shape
    w = LRU_WIDTH
    nl = 2 * w
    nr = RWKV_PROJ
    wl = w_in[:, :nl].astype(BF16)
    wr = w_in[:, nl:].astype(BF16)
    wab = jnp.concatenate([_block_diag(wa), _block_diag(wi)], axis=1).astype(BF16)
    bab = jnp.concatenate([ba.reshape(1, w), bi.reshape(1, w)], axis=1)
    vec = pl.BlockSpec((None, 1, d), lambda b, i: (b, 0, 0))
    const = lambda shape: pl.BlockSpec(shape, lambda b, i: (0, 0))
    return pl.pallas_call(
        _inproj_lru_kernel,
        grid=(bsz, t // tm),
        in_specs=[
            pl.BlockSpec((None, tm, d), lambda b, i: (b, i, 0)),
            vec, vec, const((1, d)), const((d, nl)), const((d, nr)),
            const((CONV_WIDTH, w)), const((1, w)), const((w, 2 * w)), const((1, 2 * w)),
            const((1, w)), const((1, w)), const((w, w)),
        ],
        out_specs=[
            pl.BlockSpec((None, tm, w), lambda b, i: (b, i, 0)),
            pl.BlockSpec((None, tm, nr), lambda b, i: (b, i, 0)),
        ],
        out_shape=[
            jax.ShapeDtypeStruct((bsz, t, w), BF16),
            jax.ShapeDtypeStruct((bsz, t, nr), F32),
        ],
        scratch_shapes=[pltpu.VMEM((SUBLANES, w), F32), pltpu.VMEM((SUBLANES, w), F32)],
        compiler_params=_cparams(("arbitrary", "arbitrary")),
        name="adaln1_inproj_rglru",
    )(x, sh1, sc1, g.reshape(1, d), wl, wr, conv_w, conv_b.reshape(1, w), wab, bab,
      lam.reshape(1, w), norm_g.reshape(1, w), _head_ones(w, LRU_HEAD_DIM))


def _gelu_tanh(x):
    c = 0.7978845608028654
    half = 0.5 * x
    return half * jnp.tanh(x * (c + (c * 0.044715) * (x * x))) + half


def _sigmoid(x):
    return 0.5 * jnp.tanh(0.5 * x) + 0.5


def _lru_tile(ux, gate_branch, cw_ref, cb_ref, wab_ref, bab_ref, lam_ref, ng_ref, ones_ref,
              xprev_ref, hprev_ref):
    w = LRU_WIDTH
    rows = ux[0].shape[0]
    prev8 = [xprev_ref[...]] + [u[rows - SUBLANES:, :] for u in ux[:-1]]
    ext = [jnp.concatenate([p, u], axis=0) for p, u in zip(prev8, ux)]
    xc = [cb_ref[...] + cw_ref[CONV_WIDTH - 1:CONV_WIDTH, :] * u for u in ux]
    for k in range(1, CONV_WIDTH):
        tap = cw_ref[CONV_WIDTH - 1 - k:CONV_WIDTH - k, :]
        xc = [x + tap * pltpu.roll(e, k, 0)[SUBLANES:, :] for x, e in zip(xc, ext)]
    xprev_ref[...] = ux[-1][rows - SUBLANES:, :]

    gates = [jnp.dot(x.astype(BF16), wab_ref[...], preferred_element_type=F32) + bab_ref[...]
             for x in xc]
    ug = gate_branch()
    r = [_sigmoid(g[:, :w]) for g in gates]
    ig = [_sigmoid(g[:, w:]) for g in gates]
    sp = _softplus(-lam_ref[...])
    log_a = [(-LRU_C) * x * sp for x in r]
    a = [jnp.exp(x) for x in log_a]
    th = [jnp.tanh(x) for x in log_a]
    q = [(-2.0 * x) / (1.0 - x) for x in th]
    root_q = [jnp.where(x > 0.0, x * lax.rsqrt(x), 0.0) for x in q]
    b = [s * (i * x) for s, i, x in zip(root_q, ig, xc)]

    row8 = lax.broadcasted_iota(I32, (rows, w), 0) & (SUBLANES - 1)
    acc_a, acc_b = a, b
    for s in (1, 2, 4):
        live = row8 >= s
        sh_a = [pltpu.roll(x, s, 0) for x in acc_a]
        sh_b = [pltpu.roll(x, s, 0) for x in acc_b]
        acc_b = [jnp.where(live, x * sb + y, y) for x, sb, y in zip(acc_a, sh_b, acc_b)]
        acc_a = [jnp.where(live, x * sa, x) for x, sa in zip(acc_a, sh_a)]
    carry = hprev_ref[SUBLANES - 1:SUBLANES, :]
    h = []
    for xa, xb in zip(acc_a, acc_b):
        groups = []
        for gi in range(rows // SUBLANES):
            lo = gi * SUBLANES
            groups.append(xa[lo:lo + SUBLANES, :] * carry + xb[lo:lo + SUBLANES, :])
            last = lo + SUBLANES - 1
            carry = xa[last:last + 1, :] * carry + xb[last:last + 1, :]
        h.append(jnp.concatenate(groups, axis=0))
    hprev_ref[...] = h[-1][rows - SUBLANES:, :]

    y = [x * _gelu_tanh(g) for x, g in zip(h, ug)]
    ms = [_mm(x * x, ones_ref[...]) * (1.0 / LRU_HEAD_DIM) for x in y]
    return [x * lax.rsqrt(m + RMS_EPS) * ng_ref[...] for x, m in zip(y, ms)]


def _block_diag(w):
    h, n, _ = w.shape
    eye = jnp.eye(h, dtype=w.dtype)
    return (eye[:, None, :, None] * w[:, :, None, :]).reshape(h * n, h * n)


def _head_ones(width, head):
    idx = jnp.arange(width) // head
    return (idx[:, None] == idx[None, :]).astype(BF16)


PG = (1, 1)
PI = (1, 1)
PS = (1, 1)
PH = (1, 2)


def _unit_lower_inverse(a_list, ri, ci):
    mm = functools.partial(_mm, pa=PI[0], pb=PI[1])
    eye = jnp.where(ri == ci, 1.0, 0.0)
    leaf = (ri >> 3) == (ci >> 3)
    a8 = [jnp.where(leaf, a, 0.0) for a in a_list]
    a8_2 = [mm(x, x) for x in a8]
    a8_4 = [mm(x, x) for x in a8_2]
    t = [mm(eye + x, eye + y) for x, y in zip(a8, a8_2)]
    t = [mm(x, eye + y) for x, y in zip(t, a8_4)]
    zero = jnp.zeros((LANES, LANES), F32)
    for sh in (3, 4, 5):
        s = 1 << sh
        off = ((ri >> (sh + 1)) == (ci >> (sh + 1))) & ((ri >> sh) != (ci >> sh))
        t_lo = [_second_blocks(x, s) for x in t]
        b_lo = [mm(_second_blocks(jnp.where(off, a, 0.0), s), x) for a, x in zip(a_list, t)]
        d_lo = [mm(x, _interleave_blocks(zero, y, s)) for x, y in zip(t_lo, b_lo)]
        t = [_interleave_blocks(x, y + z, s) for x, y, z in zip(t, t_lo, d_lo)]
    return t


def _second_blocks(x, s):
    return jnp.concatenate(
        [x[s * (2 * m + 1):s * (2 * m + 2)] for m in range(x.shape[0] // (2 * s))], axis=0)


def _interleave_blocks(first_src, second, s):
    parts = []
    for m in range(first_src.shape[0] // (2 * s)):
        parts.append(first_src[2 * s * m:2 * s * m + s])
        parts.append(second[s * m:s * (m + 1)])
    return jnp.concatenate(parts, axis=0)


def _rwkv_kernel(p_ref, mu_ref, pv_ref, wlo_ref, gup_ref, tril_ref, o_ref, uprev_ref, h_ref):
    w = RWKV_WIDTH
    tt = p_ref.shape[0]
    c = CHUNK
    n_pairs = w // LANES
    units = [(j, p) for j in range(tt // c) for p in range(n_pairs)]

    @pl.when(pl.program_id(1) == 0)
    def _():
        uprev_ref[...] = jnp.zeros_like(uprev_ref)
        h_ref[...] = jnp.zeros_like(h_ref)

    u = p_ref[...]
    ext = jnp.concatenate([uprev_ref[...], u], axis=0)
    prev = pltpu.roll(ext, 1, 0)[SUBLANES:, :]
    uprev_ref[...] = u[tt - SUBLANES:, :]
    um = u + (prev - u) * mu_ref[...]

    r = um[:, 0:w]
    k = um[:, w:2 * w]
    v = um[:, 2 * w:3 * w]
    z = um[:, 3 * w:3 * w + LANES]
    gd = um[:, 3 * w + LANES:]
    w0 = pv_ref[0:1, :]
    a0 = pv_ref[1:2, :]
    k_k = pv_ref[2:3, :]
    k_a = pv_ref[3:4, :]
    r_k = pv_ref[4:5, :]
    ln_w = pv_ref[5:6, :]
    ln_b = pv_ref[6:7, :]

    lane_t = lax.broadcasted_iota(I32, (tt, LANES), 1)
    zz = jnp.where(lane_t < HEAD_DIM, jnp.tanh(z), z)
    lora = _mm_presplit(zz, wlo_ref[...])
    ld = (-0.6065306597126334) * _sigmoid(w0 + lora[:, :w])
    a = _sigmoid(a0 + lora[:, w:])
    g = _mm_presplit(_sigmoid(gd), gup_ref[...])

    ri = lax.broadcasted_iota(I32, (LANES, LANES), 0)
    ci = lax.broadcasted_iota(I32, (LANES, LANES), 1)
    even = lax.broadcasted_iota(I32, (c, LANES), 1) < HEAD_DIM
    wide = 2 * LANES
    ones_bd = jnp.where(
        (lax.broadcasted_iota(I32, (wide, wide), 0) >> 6)
        == (lax.broadcasted_iota(I32, (wide, wide), 1) >> 6), 1.0, 0.0).astype(BF16)

    def head_sum(x):
        return jnp.concatenate(
            [_mm(x[:, q * wide:(q + 1) * wide], ones_bd) for q in range(w // wide)], axis=1)

    kk = k * k_k
    kk = kk * lax.rsqrt(jnp.maximum(head_sum(kk * kk), 1e-24))
    kp = k * (1.0 + (a - 1.0) * k_a)
    kka = kk * a
    bonus = head_sum(r * kp * r_k) * v
    cum = _mm(tril_ref[...], ld, pb=3)
    cum_c = [cum[j * c + c - 1:j * c + c, :] for j in range(tt // c)]
    p_c = [jnp.exp(x) for x in cum_c]
    p_inv = jnp.exp(-cum)
    al = -kk * jnp.exp(cum - ld)
    rt = r * jnp.exp(cum)
    bt = kka * p_inv
    kt = kp * p_inv

    def to_chunk_end(x):
        return jnp.concatenate(
            [x[j * c:(j + 1) * c, :] * p_c[j] for j in range(tt // c)], axis=0)

    bh = to_chunk_end(bt)
    kh = to_chunk_end(kt)

    def blk(x):
        return [x[j * c:(j + 1) * c, p * LANES:(p + 1) * LANES] for j, p in units]

    def halves(x):
        xs = blk(x)
        return [jnp.where(even, y, 0.0) for y in xs], [jnp.where(even, 0.0, y) for y in xs]

    def rows(top, bot):
        return [jnp.concatenate([x, y], axis=0) for x, y in zip(top, bot)]

    def unstack(x):
        return x[:c, :] + x[c:, :]

    al_e, al_o = halves(al)
    rt_e, rt_o = halves(rt)
    bt_b, kt_b = blk(bt), blk(kt)
    bh_e, bh_o = halves(bh)
    kh_e, kh_o = halves(kh)
    v_e, v_o = halves(v)
    al_n = rows(al_e, al_o)
    bh_n = rows(bh_e, bh_o)
    v_s = rows(v_o, v_e)
    kh_s = rows(kh_o, kh_e)
    rt_b = blk(rt)
    nu = range(len(units))

    g0 = [_mm(x, y, NT, pa=PG[0], pb=PG[1])
          for x, y in zip(rows(al_e, rt_e), rows(bt_b, kt_b))]
    g1 = [_mm(x, y, NT, pa=PG[0], pb=PG[1])
          for x, y in zip(rows(rt_o, al_o), rows(kt_b, bt_b))]
    top = ri < c
    left = ci < c
    tri_s = (ri & (c - 1)) > (ci & (c - 1))
    tri_i = (ri & (c - 1)) >= (ci & (c - 1))
    diag_q = top == left

    def pick(x0, x1, in_q0, tri):
        return [jnp.where(tri, jnp.where(in_q0, x, y), 0.0) for x, y in zip(x0, x1)]

    a_ab = pick(g0, g1, top, diag_q & tri_s)
    a_rk = pick(g1, g0, top, diag_q & tri_i)
    a_ak = pick(g0, g1, top, (~diag_q) & tri_s)
    a_rb = pick(g1, g0, top, (~diag_q) & tri_i)
    x1 = [_mm(a_ak[n], v_s[n], pa=PS[0], pb=PS[1]) for n in nu]
    akv = [_mm(a_rk[n], v_s[n], pa=PS[0], pb=PS[1]) for n in nu]
    khv = [_mm(kh_s[n], v_s[n], TN, pa=PS[0], pb=PS[1]) for n in nu]
    t_inv = _unit_lower_inverse(a_ab, ri, ci)
    tw = [_mm(t_inv[n], jnp.concatenate([al_n[n], x1[n]], axis=1), pa=PS[0], pb=PS[1])
          for n in nu]
    qo = [_mm(a_rb[n], tw[n], pa=PS[0], pb=PS[1]) for n in nu]
    mn = [_mm(bh_n[n], tw[n], TN, pa=PS[0], pb=PS[1]) for n in nu]
    q = [rt_b[n] + unstack(qo[n][:, :LANES]) for n in nu]
    o_loc = [unstack(qo[n][:, LANES:] + akv[n]) for n in nu]
    m_full = [mn[n][:, :LANES]
              + jnp.where(ri == ci, p_c[j][:, p * LANES:(p + 1) * LANES], 0.0)
              for n, (j, p) in enumerate(units)]
    n_loc = [mn[n][:, LANES:] + khv[n] for n in nu]

    h = [h_ref[p] for p in range(n_pairs)]
    o_rows = []
    for j in range(tt // c):
        o_parts = []
        for p in range(n_pairs):
            n = j * n_pairs + p
            o_parts.append(_mm(q[n], h[p], pa=PH[0], pb=PH[1]) + o_loc[n])
            h[p] = _mm(m_full[n], h[p], pa=PH[0], pb=PH[1]) + n_loc[n]
        o_rows.append(jnp.concatenate(o_parts, axis=1))
    for p in range(n_pairs):
        h_ref[p] = h[p]

    o = jnp.concatenate(o_rows, axis=0)
    mean = head_sum(o) * (1.0 / HEAD_DIM)
    d = o - mean
    var = head_sum(d * d) * (1.0 / HEAD_DIM)
    on = d * lax.rsqrt(var + GN_EPS) * ln_w + ln_b
    o_ref[...] = ((on + bonus) * g).astype(o_ref.dtype)


RWKV_TILE = 256


def _rwkv7(p_rw, mu, w0, w_up, a0, a_up, g_up, k_k, k_a, r_k, ln_w, ln_b):
    bsz, t, npj = p_rw.shape
    w = RWKV_WIDTH
    tt = RWKV_TILE
    pv = jnp.stack([w0, a0, k_k, k_a, r_k.reshape(w), ln_w, ln_b, jnp.zeros((w,), F32)])
    wlo = jnp.zeros((LANES, 2 * w), F32)
    wlo = wlo.at[:DECAY_LORA, :w].set(w_up).at[DECAY_LORA:, w:].set(a_up)
    row = jnp.arange(tt)
    tril = ((row[:, None] >= row[None, :])
            & (row[:, None] // CHUNK == row[None, :] // CHUNK)).astype(BF16)
    const = lambda shape: pl.BlockSpec(shape, lambda b, i: (0, 0))
    return pl.pallas_call(
        _rwkv_kernel,
        grid=(bsz, t // tt),
        in_specs=[
            pl.BlockSpec((None, tt, npj), lambda b, i: (b, i, 0)),
            const((1, npj)), const((SUBLANES, w)), const((3 * LANES, 2 * w)),
            const((3 * GATE_LORA, w)), const((tt, tt)),
        ],
        out_specs=pl.BlockSpec((None, tt, w), lambda b, i: (b, i, 0)),
        out_shape=jax.ShapeDtypeStruct((bsz, t, w), BF16),
        scratch_shapes=[pltpu.VMEM((SUBLANES, npj), F32),
                        pltpu.VMEM((w // LANES, LANES, LANES), F32)],
        compiler_params=_cparams(("arbitrary", "arbitrary")),
        name="rwkv7",
    )(p_rw, mu.reshape(1, npj), pv, _presplit(wlo), _presplit(g_up), tril)


def _mix_kernel(x_ref, yl_ref, yr_ref, wo1_ref, wo2_ref, g1_ref, sh_ref, sc_ref, ng_ref,
                wr_ref, br_ref, x1_ref, h2_ref, rid_ref, rw_ref, cnt_ref):
    tm = x_ref.shape[0]
    nblk = 2
    rows = tm // nblk
    blocks = [slice(i * rows, (i + 1) * rows) for i in range(nblk)]
    y = [jnp.dot(yl_ref[sl, :], wo1_ref[...], preferred_element_type=F32)
         + jnp.dot(yr_ref[sl, :], wo2_ref[...], preferred_element_type=F32) for sl in blocks]
    x1 = [x_ref[sl, :] + g1_ref[...] * yb for sl, yb in zip(blocks, y)]
    h2 = [_adaln(xb, ng_ref[...], sh_ref[...], sc_ref[...]).astype(BF16) for xb in x1]
    lg_all = [jnp.dot(jnp.concatenate([hb, hb], axis=1), wr_ref[...],
                      preferred_element_type=F32) + br_ref[...] for hb in h2]
    for sl, xb, hb in zip(blocks, x1, h2):
        x1_ref[sl, :] = xb
        h2_ref[sl, :] = _pack_bf16_halves(hb)

    neg = -jnp.inf
    lane = lax.broadcasted_iota(I32, (rows, LANES), 1)
    lane_f = lane.astype(F32)

    def first_argmax(xs):
        m = [jnp.max(x, axis=-1, keepdims=True) for x in xs]
        idx = [jnp.min(jnp.where(x == mi, lane_f, float(LANES)), axis=-1, keepdims=True)
               for x, mi in zip(xs, m)]
        return m, [i.astype(I32) for i in idx]

    lg = [jnp.where(lane < N_GROUPS, x, neg) for x in lg_all]
    gm, g_idx = first_argmax(lg)
    g_w = [1.0 / jnp.sum(jnp.exp(x - m), axis=-1, keepdims=True)
           for x, m in zip(lg, gm)]
    lo = [N_GROUPS + EXPERTS_PER_GROUP * g for g in g_idx]
    le = [jnp.where((lane >= l) & (lane < l + EXPERTS_PER_GROUP), x, neg)
          for x, l in zip(lg_all, lo)]
    v1, i1 = first_argmax(le)
    v2, i2 = first_argmax([jnp.where(lane == i, neg, x) for x, i in zip(le, i1)])

    @pl.when((pl.program_id(0) == 0) & (pl.program_id(1) == 0))
    def _():
        cnt_ref[...] = jnp.zeros_like(cnt_ref)

    e_lane = lane + N_GROUPS
    hits = sum(jnp.sum(jnp.where((e_lane == a) | (e_lane == b), 1.0, 0.0), axis=0, keepdims=True)
               for a, b in zip(i1, i2))
    cnt_ref[...] = cnt_ref[...] + hits.astype(I32)

    for blk in range(nblk):
        e2 = jnp.exp(v2[blk] - v1[blk])
        w1 = g_w[blk] / (1.0 + e2)
        w2 = g_w[blk] * e2 / (1.0 + e2)
        sl = blocks[blk]
        rid_ref[sl, :] = jnp.where(lane == 0, i1[blk] - N_GROUPS,
                                   jnp.where(lane == 1, i2[blk] - N_GROUPS, 0))
        rw_ref[sl, :] = jnp.where(lane == 0, w1, jnp.where(lane == 1, w2, 0.0))


def _mix(x, y_lru, y_rw, w_out, g1, sh2, sc2, ng, w_grp, b_grp, w_exp, b_exp, tm=256):
    bsz, t, d = x.shape
    wo = w_out.astype(BF16)
    wl = LRU_WIDTH
    wr = jnp.zeros((d, LANES), F32).at[:, :N_GROUPS].set(w_grp)
    wr = wr.at[:, N_GROUPS:N_GROUPS + N_EXPERTS].set(w_exp)
    wr = jnp.concatenate(_split(wr, 2), axis=0)
    br = jnp.zeros((1, LANES), F32).at[0, :N_GROUPS].set(b_grp)
    br = br.at[0, N_GROUPS:N_GROUPS + N_EXPERTS].set(b_exp)
    vec = pl.BlockSpec((None, 1, d), lambda b, i: (b, 0, 0))
    const = lambda shape: pl.BlockSpec(shape, lambda b, i: (0, 0))
    row = lambda n: pl.BlockSpec((None, tm, n), lambda b, i: (b, i, 0))
    return pl.pallas_call(
        _mix_kernel,
        grid=(bsz, t // tm),
        in_specs=[row(d), row(wl), row(d - wl), const((wl, d)), const((d - wl, d)),
                  vec, vec, vec, const((1, d)), const((2 * d, LANES)), const((1, LANES))],
        out_specs=[row(d), row(d // 2), row(LANES), row(LANES), const((SUBLANES, LANES))],
        out_shape=[jax.ShapeDtypeStruct((bsz, t, d), F32),
                   jax.ShapeDtypeStruct((bsz, t, d // 2), I32),
                   jax.ShapeDtypeStruct((bsz, t, LANES), I32),
                   jax.ShapeDtypeStruct((bsz, t, LANES), F32),
                   jax.ShapeDtypeStruct((SUBLANES, LANES), I32)],
        compiler_params=_cparams(("arbitrary", "arbitrary")),
        name="outproj_adaln2_router",
    )(x, y_lru, y_rw, wo[:wl], wo[wl:], g1, sh2, sc2, ng.reshape(1, d), wr, br)


def _plan_kernel(rid_ref, cnt_ref, tril_ref, pos_ref, te_ref, base_ref):
    i = pl.program_id(0)
    tp = rid_ref.shape[0]
    lane = lax.broadcasted_iota(I32, (tp, LANES), 1)
    rid = rid_ref[...]
    oh0 = lane == rid[:, 0:1]
    oh1 = lane == rid[:, 1:2]
    oh = jnp.where(oh0 | oh1, 1.0, 0.0)

    @pl.when(i == 0)
    def _():
        cnt = cnt_ref[...]
        shift = EXPERT_TILE.bit_length() - 1
        padded = ((cnt + (EXPERT_TILE - 1)) >> shift) << shift
        l8 = lax.broadcasted_iota(I32, (SUBLANES, LANES), 1)
        end = padded
        s = 1
        while s < LANES:
            end = end + jnp.where(l8 >= s, pltpu.roll(end, s, 1), 0)
            s *= 2
        base_ref[...] = end - padded
        nt = te_ref.shape[0]
        j = lax.broadcasted_iota(I32, (nt, LANES), 0) * EXPERT_TILE
        lt = lax.broadcasted_iota(I32, (nt, LANES), 1)
        done = jnp.where((lt < N_EXPERTS) & (end[0:1, :] <= j), 1.0, 0.0)
        e_of = jnp.minimum(jnp.sum(done, axis=-1, keepdims=True), float(N_EXPERTS - 1)).astype(I32)
        total = end[0:1, N_EXPERTS - 1:N_EXPERTS]
        mine = lt == e_of
        cnt_e = jnp.sum(jnp.where(mine, cnt[0:1, :], 0), axis=-1, keepdims=True)
        start_e = jnp.sum(jnp.where(mine, end[0:1, :] - padded[0:1, :], 0), axis=-1, keepdims=True)
        valid = jnp.clip(cnt_e - (j[:, 0:1] - start_e), 0, EXPERT_TILE)
        te_ref[...] = jnp.where(lt == 0, e_of, jnp.where(lt == 1, total >> shift,
                                                        jnp.where(lt == 2, valid, 0)))

    prefix = jnp.dot(tril_ref[...], oh.astype(BF16), preferred_element_type=F32)
    dest = base_ref[0:1, :] + prefix.astype(I32)
    pos0 = jnp.sum(jnp.where(oh0, dest, 0), axis=-1, keepdims=True)
    pos1 = jnp.sum(jnp.where(oh1, dest, 0), axis=-1, keepdims=True)
    both = jnp.where(lane == 0, pos0, jnp.where(lane == 1, pos1, 0))
    pos_ref[...] = both.T[:SUBLANES, :]
    base_ref[...] = base_ref[...] + jnp.sum(oh, axis=0, keepdims=True).astype(I32)


def _plan(rid, cnt, n_tiles, tp=512):
    n = rid.shape[0]
    tp = min(tp, n)
    nt_pad = -(-n_tiles // SUBLANES) * SUBLANES
    tril = (jnp.arange(tp)[:, None] > jnp.arange(tp)[None, :]).astype(BF16)
    pos, te = pl.pallas_call(
        _plan_kernel,
        grid=(n // tp,),
        in_specs=[pl.BlockSpec((tp, LANES), lambda i: (i, 0)),
                  pl.BlockSpec((SUBLANES, LANES), lambda i: (0, 0)),
                  pl.BlockSpec((tp, tp), lambda i: (0, 0))],
        out_specs=[pl.BlockSpec((None, SUBLANES, tp), lambda i: (i, 0, 0)),
                   pl.BlockSpec((nt_pad, LANES), lambda i: (0, 0))],
        out_shape=[jax.ShapeDtypeStruct((n // tp, SUBLANES, tp), I32),
                   jax.ShapeDtypeStruct((nt_pad, LANES), I32)],
        scratch_shapes=[pltpu.VMEM((SUBLANES, LANES), I32)],
        compiler_params=_cparams(("arbitrary",)),
        name="route_plan",
    )(rid, cnt, tril)
    return pos, te


SC_CORES = 2
SC_SUBCORES = 16
SC_WORKERS = SC_CORES * SC_SUBCORES
SC_WINDOW = 64


def _sc_mesh():
    return plsc.VectorSubcoreMesh(core_axis_name="c", subcore_axis_name="s",
                                  num_cores=SC_CORES, num_subcores=SC_SUBCORES)


def _sc_two_buffer_loop(n_win, fill, drain):
    assert n_win % 2 == 0

    def start(copies):
        for cp in copies:
            cp.start()

    def wait(copies):
        for cp in copies:
            cp.wait()

    start(fill(0, 0))

    @pl.loop(0, n_win, step=2)
    def _(j):
        for b in range(2):
            jj = j + b

            @pl.when(jj >= 1)
            def _():
                wait(drain(jj - 1, 1 - b))

            @pl.when(jj + 1 < n_win)
            def _():
                start(fill(jj + 1, 1 - b))

            wait(fill(jj, b))
            start(drain(jj, b))

    wait(drain(n_win - 1, 1))


def _sc_scatter_rows(rows, pos0, pos1, n_out):
    n, d = rows.shape
    per_w = n // SC_WORKERS
    n_win = per_w // SC_WINDOW
    shape3 = (SC_WORKERS, n_win, SC_WINDOW)

    @functools.partial(
        pl.kernel, mesh=_sc_mesh(), out_type=jax.ShapeDtypeStruct((n_out, d), rows.dtype),
        scratch_types=[pltpu.VMEM((n_win, SC_WINDOW), I32), pltpu.VMEM((n_win, SC_WINDOW), I32),
                       pltpu.VMEM((2, SC_WINDOW, d), rows.dtype),
                       pltpu.SemaphoreType.DMA((2,)), pltpu.SemaphoreType.DMA((2,))],
        name="sc_scatter_rows")
    def scatter(rows_hbm, p0_hbm, p1_hbm, out_hbm, p0_v, p1_v, buf, in_sem, out_sem):
        wid = lax.axis_index("s") * SC_CORES + lax.axis_index("c")
        base = wid * per_w
        pltpu.sync_copy(p0_hbm.at[wid], p0_v)
        pltpu.sync_copy(p1_hbm.at[wid], p1_v)

        def fill(j, b):
            src = rows_hbm.at[pl.ds(base + j * SC_WINDOW, SC_WINDOW)]
            return [pltpu.make_async_copy(src, buf.at[b], in_sem.at[b])]

        def drain(j, b):
            return [pltpu.make_async_copy(buf.at[b], out_hbm.at[p.at[j]], out_sem.at[b])
                    for p in (p0_v, p1_v)]

        _sc_two_buffer_loop(n_win, fill, drain)

    return scatter(rows, pos0.reshape(shape3), pos1.reshape(shape3))


def _sc_gather_rows(table, idx):
    m = idx.shape[0]
    d = table.shape[1]
    per_w = m // SC_WORKERS
    n_win = per_w // SC_WINDOW

    @functools.partial(
        pl.kernel, mesh=_sc_mesh(), out_type=jax.ShapeDtypeStruct((m, d), table.dtype),
        scratch_types=[pltpu.VMEM((n_win, SC_WINDOW), I32),
                       pltpu.VMEM((2, SC_WINDOW, d), table.dtype),
                       pltpu.SemaphoreType.DMA((2,)), pltpu.SemaphoreType.DMA((2,))],
        name="sc_gather_rows")
    def gather(table_hbm, idx_hbm, out_hbm, idx_v, buf, in_sem, out_sem):
        wid = lax.axis_index("s") * SC_CORES + lax.axis_index("c")
        base = wid * per_w
        pltpu.sync_copy(idx_hbm.at[wid], idx_v)

        def fill(j, b):
            return [pltpu.make_async_copy(table_hbm.at[idx_v.at[j]], buf.at[b], in_sem.at[b])]

        def drain(j, b):
            dst = out_hbm.at[pl.ds(base + j * SC_WINDOW, SC_WINDOW)]
            return [pltpu.make_async_copy(buf.at[b], dst, out_sem.at[b])]

        _sc_two_buffer_loop(n_win, fill, drain)

    return gather(table, idx.reshape(SC_WORKERS, n_win, SC_WINDOW))


def _expert_kernel(te_ref, nu_ref, valid_ref, x_ref, w1_ref, w3_ref, w2_ref, o_ref,
                   w1b_ref, w3b_ref, w2b_ref):
    j = pl.program_id(0)
    active = j < nu_ref[0]

    @pl.when(active & ((j == 0) | (te_ref[j] != te_ref[jnp.maximum(j - 1, 0)])))
    def _():
        w1b_ref[...] = w1_ref[...].astype(BF16)
        w3b_ref[...] = w3_ref[...].astype(BF16)
        w2b_ref[...] = w2_ref[...].astype(BF16)

    @pl.when(active)
    def _():
        nblk = 2
        rows = x_ref.shape[0] // nblk
        blocks = [slice(i * rows, (i + 1) * rows) for i in range(nblk)]
        row = lax.broadcasted_iota(I32, (rows, x_ref.shape[1]), 0)
        xs = [_unpack_bf16_halves(
            jnp.where(row + i * rows < valid_ref[j], x_ref[sl, :], 0)).astype(BF16)
            for i, sl in enumerate(blocks)]
        h1 = [jnp.dot(x, w1b_ref[...], preferred_element_type=F32) for x in xs]
        h3 = [jnp.dot(x, w3b_ref[...], preferred_element_type=F32) for x in xs]
        hid = [(a * jax.nn.sigmoid(a) * b).astype(BF16) for a, b in zip(h1, h3)]
        for sl, hb in zip(blocks, hid):
            o_ref[sl, :] = _pack_bf16_halves(
                jnp.dot(hb, w2b_ref[...], preferred_element_type=F32))

    @pl.when(j >= nu_ref[0])
    def _():
        o_ref[...] = jnp.zeros_like(o_ref)


def _experts(xs, te, nu, valid, w1, w3, w2):
    n_rows = xs.shape[0]
    d, de = w1.shape[-2:]
    nt = n_rows // EXPERT_TILE
    used = lambda j, nu: jnp.minimum(j, nu[0] - 1)
    grid_spec = pltpu.PrefetchScalarGridSpec(
        num_scalar_prefetch=3,
        grid=(nt,),
        in_specs=[
            pl.BlockSpec((EXPERT_TILE, d // 2), lambda j, te, nu, va: (used(j, nu), 0)),
            pl.BlockSpec((None, d, de), lambda j, te, nu, va: (te[used(j, nu)], 0, 0)),
            pl.BlockSpec((None, d, de), lambda j, te, nu, va: (te[used(j, nu)], 0, 0)),
            pl.BlockSpec((None, de, d), lambda j, te, nu, va: (te[used(j, nu)], 0, 0)),
        ],
        out_specs=pl.BlockSpec((EXPERT_TILE, d // 2), lambda j, te, nu, va: (j, 0)),
        scratch_shapes=[pltpu.VMEM((d, de), BF16), pltpu.VMEM((d, de), BF16),
                        pltpu.VMEM((de, d), BF16)],
    )
    return pl.pallas_call(
        _expert_kernel,
        grid_spec=grid_spec,
        out_shape=jax.ShapeDtypeStruct((n_rows, d // 2), I32),
        compiler_params=_cparams(("arbitrary",)),
        name="expert_mlp",
    )(te, nu, valid, xs, w1, w3, w2)


def _combine_kernel(x1_ref, rw_ref, g2_ref, fg_ref, y0_ref, y1_ref, o_ref):
    rw = rw_ref[...]
    moe = (rw[:, 0:1] * _unpack_bf16_halves(y0_ref[...])
           + rw[:, 1:2] * _unpack_bf16_halves(y1_ref[...]))
    x2 = x1_ref[...] + g2_ref[...] * moe
    ms = jnp.mean(x2 * x2, axis=-1, keepdims=True)
    o_ref[...] = x2 * lax.rsqrt(ms + RMS_EPS) * fg_ref[...]


def _combine(x1, rw, yg, g2, final_g, tc):
    bsz, t, d = x1.shape
    row = lambda n: pl.BlockSpec((None, tc, n), lambda b, i: (b, i, 0))
    slot = lambda s: pl.BlockSpec((None, None, tc, d // 2), lambda b, i: (s, b, i, 0))
    return pl.pallas_call(
        _combine_kernel,
        grid=(bsz, t // tc),
        in_specs=[row(d), row(LANES),
                  pl.BlockSpec((None, 1, d), lambda b, i: (b, 0, 0)),
                  pl.BlockSpec((1, d), lambda b, i: (0, 0)),
                  slot(0), slot(1)],
        out_specs=row(d),
        out_shape=jax.ShapeDtypeStruct((bsz, t, d), F32),
        compiler_params=_cparams(("arbitrary", "arbitrary")),
        name="combine_final_norm",
    )(x1, rw, g2, final_g.reshape(1, d), yg, yg)


def _row_tile(t, want):
    return want if t % want == 0 else t


def kernel(x, c, w_ada, b_ada, norm1_g, w_in, conv_w, conv_b, lru_wa, lru_ba, lru_wi, lru_bi, lru_lam, lru_norm_g, tok_mu, w0, w_up, a0, a_up, g_up, k_k, k_a, r_k, ln_x_w, ln_x_b, w_out, norm2_g, w_grp, b_grp, w_exp, b_exp, w1, w3, w2, final_g):
    bsz, t, d = x.shape
    n = bsz * t
    depth = w_ada.shape[0]
    assert depth == 1, "the combine kernel applies the final norm: only DEPTH == 1 is wired"
    tile = _row_tile(t, ROW_TILE)
    n_tiles = (n * TOP_K) // EXPERT_TILE + N_EXPERTS
    for l in range(depth):
        mod = _modulation(c, w_ada[l], b_ada[l]).reshape(bsz, 6, 1, d)
        sh1, sc1, g1, sh2, sc2, g2 = (mod[:, i] for i in range(6))
        y_lru, p_rw = _inproj_lru(x, sh1, sc1, norm1_g[l], w_in[l], conv_w[l], conv_b[l],
                                  lru_wa[l], lru_ba[l], lru_wi[l], lru_bi[l], lru_lam[l],
                                  lru_norm_g[l], tm=tile)
        y_rw = _rwkv7(p_rw, tok_mu[l], w0[l], w_up[l], a0[l], a_up[l], g_up[l], k_k[l], k_a[l],
                      r_k[l], ln_x_w[l], ln_x_b[l])
        x1, h2, rid, rw, cnt = _mix(x, y_lru, y_rw, w_out[l], g1, sh2, sc2, norm2_g[l],
                                    w_grp[l], b_grp[l], w_exp[l], b_exp[l], tm=tile)
        pos, te = _plan(rid.reshape(n, LANES), cnt, n_tiles)
        pos0, pos1 = pos[:, 0, :].reshape(n), pos[:, 1, :].reshape(n)
        xs = _sc_scatter_rows(h2.reshape(n, d // 2), pos0, pos1, n_tiles * EXPERT_TILE)
        ys = _experts(xs, te[:n_tiles, 0], te[0:1, 1], te[:n_tiles, 2], w1[l], w3[l], w2[l])
        yg = _sc_gather_rows(ys, jnp.concatenate([pos0, pos1]))
        x = _combine(x1, rw, yg.reshape(TOP_K, bsz, t, d // 2), g2, final_g,
                     _row_tile(t, COMBINE_TILE))
    return x
```

```python
import functools

import jax
import jax.numpy as jnp
from jax import lax
from jax.experimental import pallas as pl
from jax.experimental.pallas import tpu as pltpu
from jax.experimental.pallas import tpu_sc as plsc

F32 = jnp.float32
BF16 = jnp.bfloat16
I32 = jnp.int32

LRU_WIDTH = 512
LRU_HEAD_DIM = 64
CONV_WIDTH = 4
LRU_C = 8.0
RWKV_WIDTH = 512
HEAD_DIM = 64
DECAY_LORA = 64
AAA_LORA = 64
GATE_LORA = 128
RWKV_PROJ = 3 * RWKV_WIDTH + DECAY_LORA + AAA_LORA + GATE_LORA
N_GROUPS = 4
EXPERTS_PER_GROUP = 8
N_EXPERTS = N_GROUPS * EXPERTS_PER_GROUP
TOP_K = 2
RMS_EPS = 1e-6
GN_EPS = 64e-5

LANES = 128
SUBLANES = 8
CHUNK = 64
EXPERT_TILE = 512
ROW_TILE = 256
MIX_TILE = 512
MIX_BLOCK_ROWS = 128
COMBINE_TILE = 1024
VMEM_LIMIT = 48 * 1024 * 1024

NN = (((1,), (0,)), ((), ()))
NT = (((1,), (1,)), ((), ()))
TN = (((0,), (0,)), ((), ()))


def _split(x, n):
    if x.dtype == BF16:
        return [x]
    parts = []
    rem = x
    for i in range(n):
        p = rem.astype(BF16)
        parts.append(p)
        if i + 1 < n:
            rem = rem - p.astype(F32)
    return parts


def _mm(a, b, dn=NN, pa=1, pb=1):
    aps = _split(a, pa)
    bps = _split(b, pb)
    order = max(len(aps), len(bps))
    terms = [(i, j) for i in range(len(aps)) for j in range(len(bps)) if i + j < order]
    ka = dn[0][0][0]
    kb = dn[0][1][0]
    if len(terms) > 1 and a.shape[ka] % LANES == 0:
        a_cat = jnp.concatenate([aps[i] for i, _ in terms], axis=ka)
        b_cat = jnp.concatenate([bps[j] for _, j in terms], axis=kb)
        return lax.dot_general(a_cat, b_cat, dn, preferred_element_type=F32)
    out = None
    for i, j in terms:
        t = lax.dot_general(aps[i], bps[j], dn, preferred_element_type=F32)
        out = t if out is None else out + t
    return out


def _presplit(b):
    hi, lo = _split(b, 2)
    return jnp.concatenate([hi, lo, hi], axis=0)


def _mm_presplit(a, b3):
    a_hi, a_lo = _split(a, 2)
    return jnp.dot(jnp.concatenate([a_hi, a_hi, a_lo], axis=1), b3, preferred_element_type=F32)


def _pack_bf16_halves(x):
    n = x.shape[1] // 2
    bits = lax.bitcast_convert_type(x.astype(BF16).astype(F32), I32)
    return bits[:, n:] | ((bits[:, :n] >> 16) & 0xFFFF)


def _unpack_bf16_halves(p):
    lo = lax.bitcast_convert_type(p << 16, F32)
    hi = lax.bitcast_convert_type(p & (-65536), F32)
    return jnp.concatenate([lo, hi], axis=1)


def _softplus(x):
    return jnp.maximum(x, 0.0) + jnp.log1p(jnp.exp(-jnp.abs(x)))


def _cparams(sem):
    return pltpu.CompilerParams(dimension_semantics=sem, vmem_limit_bytes=VMEM_LIMIT)


def _mod_kernel(c_ref, w_ref, b_ref, o_ref):
    c = c_ref[...]
    s = c * jax.nn.sigmoid(c)
    o_ref[...] = _mm(s, w_ref[...], pa=2, pb=2) + b_ref[...]


def _modulation(c, w_ada, b_ada):
    bsz, d = c.shape
    n_out = w_ada.shape[1]
    rows = -(-bsz // SUBLANES) * SUBLANES
    c_pad = jnp.zeros((rows, d), F32).at[:bsz].set(c)
    bn = d
    out = pl.pallas_call(
        _mod_kernel,
        grid=(n_out // bn,),
        in_specs=[
            pl.BlockSpec((rows, d), lambda j: (0, 0)),
            pl.BlockSpec((d, bn), lambda j: (0, j)),
            pl.BlockSpec((1, bn), lambda j: (0, j)),
        ],
        out_specs=pl.BlockSpec((rows, bn), lambda j: (0, j)),
        out_shape=jax.ShapeDtypeStruct((rows, n_out), F32),
        compiler_params=_cparams(("arbitrary",)),
        name="adaln_mod",
    )(c_pad, w_ada, b_ada.reshape(1, n_out))
    return out[:bsz]


def _adaln(x, g, shift, scale):
    ms = jnp.mean(x * x, axis=-1, keepdims=True)
    return (x * lax.rsqrt(ms + RMS_EPS)) * (g * (1.0 + scale)) + shift


def _inproj_lru_kernel(x_ref, sh_ref, sc_ref, g_ref, wl_ref, wr_ref,
                       cw_ref, cb_ref, wab_ref, bab_ref, lam_ref, ng_ref, ones_ref,
                       yl_ref, pr_ref, xprev_ref, hprev_ref):
    @pl.when(pl.program_id(1) == 0)
    def _():
        xprev_ref[...] = jnp.zeros_like(xprev_ref)
        hprev_ref[...] = jnp.zeros_like(hprev_ref)

    w = LRU_WIDTH
    tm = x_ref.shape[0]
    nblk = 2
    rows = tm // nblk
    blocks = [slice(i * rows, (i + 1) * rows) for i in range(nblk)]
    h = [_adaln(x_ref[sl, :], g_ref[...], sh_ref[...], sc_ref[...]).astype(BF16)
         for sl in blocks]
    ux = [jnp.dot(hb, wl_ref[:, :w], preferred_element_type=F32) for hb in h]

    def remaining_columns():
        for sl, hb in zip(blocks, h):
            pr_ref[sl, :] = jnp.dot(hb, wr_ref[...], preferred_element_type=F32)
        return [jnp.dot(hb, wl_ref[:, w:], preferred_element_type=F32) for hb in h]

    y = _lru_tile(ux, remaining_columns, cw_ref, cb_ref, wab_ref, bab_ref, lam_ref, ng_ref,
                  ones_ref, xprev_ref, hprev_ref)
    for sl, yb in zip(blocks, y):
        yl_ref[sl, :] = yb.astype(yl_ref.dtype)


def _inproj_lru(x, sh1, sc1, g, w_in, conv_w, conv_b, wa, ba, wi, bi, lam, norm_g, tm=256):
    bsz, t, d = x.shape
    w = LRU_WIDTH
    nl = 2 * w
    nr = RWKV_PROJ
    wl = w_in[:, :nl].astype(BF16)
    wr = w_in[:, nl:].astype(BF16)
    wab = jnp.concatenate([_block_diag(wa), _block_diag(wi)], axis=1).astype(BF16)
    bab = jnp.concatenate([ba.reshape(1, w), bi.reshape(1, w)], axis=1)
    vec = pl.BlockSpec((None, 1, d), lambda b, i: (b, 0, 0))
    const = lambda shape: pl.BlockSpec(shape, lambda b, i: (0, 0))
    return pl.pallas_call(
        _inproj_lru_kernel,
        grid=(bsz, t // tm),
        in_specs=[
            pl.BlockSpec((None, tm, d), lambda b, i: (b, i, 0)),
            vec, vec, const((1, d)), const((d, nl)), const((d, nr)),
            const((CONV_WIDTH, w)), const((1, w)), const((w, 2 * w)), const((1, 2 * w)),
            const((1, w)), const((1, w)), const((w, w)),
        ],
        out_specs=[
            pl.BlockSpec((None, tm, w), lambda b, i: (b, i, 0)),
            pl.BlockSpec((None, tm, nr), lambda b, i: (b, i, 0)),
        ],
        out_shape=[
            jax.ShapeDtypeStruct((bsz, t, w), BF16),
            jax.ShapeDtypeStruct((bsz, t, nr), F32),
        ],
        scratch_shapes=[pltpu.VMEM((SUBLANES, w), F32), pltpu.VMEM((SUBLANES, w), F32)],
        compiler_params=_cparams(("arbitrary", "arbitrary")),
        name="adaln1_inproj_rglru",
    )(x, sh1, sc1, g.reshape(1, d), wl, wr, conv_w, conv_b.reshape(1, w), wab, bab,
      lam.reshape(1, w), norm_g.reshape(1, w), _head_ones(w, LRU_HEAD_DIM))


def _gelu_tanh(x):
    c = 0.7978845608028654
    half = 0.5 * x
    return half * jnp.tanh(x * (c + (c * 0.044715) * (x * x))) + half


def _sigmoid(x):
    return 0.5 * jnp.tanh(0.5 * x) + 0.5


def _lru_tile(ux, gate_branch, cw_ref, cb_ref, wab_ref, bab_ref, lam_ref, ng_ref, ones_ref,
              xprev_ref, hprev_ref):
    w = LRU_WIDTH
    rows = ux[0].shape[0]
    prev8 = [xprev_ref[...]] + [u[rows - SUBLANES:, :] for u in ux[:-1]]
    ext = [jnp.concatenate([p, u], axis=0) for p, u in zip(prev8, ux)]
    xc = [cb_ref[...] + cw_ref[CONV_WIDTH - 1:CONV_WIDTH, :] * u for u in ux]
    for k in range(1, CONV_WIDTH):
        tap = cw_ref[CONV_WIDTH - 1 - k:CONV_WIDTH - k, :]
        xc = [x + tap * pltpu.roll(e, k, 0)[SUBLANES:, :] for x, e in zip(xc, ext)]
    xprev_ref[...] = ux[-1][rows - SUBLANES:, :]

    gates = [jnp.dot(x.astype(BF16), wab_ref[...], preferred_element_type=F32) + bab_ref[...]
             for x in xc]
    ug = gate_branch()
    r = [_sigmoid(g[:, :w]) for g in gates]
    ig = [_sigmoid(g[:, w:]) for g in gates]
    sp = _softplus(-lam_ref[...])
    log_a = [(-LRU_C) * x * sp for x in r]
    a = [jnp.exp(x) for x in log_a]
    th = [jnp.tanh(x) for x in log_a]
    q = [(-2.0 * x) / (1.0 - x) for x in th]
    root_q = [jnp.where(x > 0.0, x * lax.rsqrt(x), 0.0) for x in q]
    b = [s * (i * x) for s, i, x in zip(root_q, ig, xc)]

    row8 = lax.broadcasted_iota(I32, (rows, w), 0) & (SUBLANES - 1)
    acc_a, acc_b = a, b
    for s in (1, 2, 4):
        live = row8 >= s
        sh_a = [pltpu.roll(x, s, 0) for x in acc_a]
        sh_b = [pltpu.roll(x, s, 0) for x in acc_b]
        acc_b = [jnp.where(live, x * sb + y, y) for x, sb, y in zip(acc_a, sh_b, acc_b)]
        acc_a = [jnp.where(live, x * sa, x) for x, sa in zip(acc_a, sh_a)]
    carry = hprev_ref[SUBLANES - 1:SUBLANES, :]
    h = []
    for xa, xb in zip(acc_a, acc_b):
        groups = []
        for gi in range(rows // SUBLANES):
            lo = gi * SUBLANES
            groups.append(xa[lo:lo + SUBLANES, :] * carry + xb[lo:lo + SUBLANES, :])
            last = lo + SUBLANES - 1
            carry = xa[last:last + 1, :] * carry + xb[last:last + 1, :]
        h.append(jnp.concatenate(groups, axis=0))
    hprev_ref[...] = h[-1][rows - SUBLANES:, :]

    y = [x * _gelu_tanh(g) for x, g in zip(h, ug)]
    ms = [_mm(x * x, ones_ref[...]) * (1.0 / LRU_HEAD_DIM) for x in y]
    return [x * lax.rsqrt(m + RMS_EPS) * ng_ref[...] for x, m in zip(y, ms)]


def _block_diag(w):
    h, n, _ = w.shape
    eye = jnp.eye(h, dtype=w.dtype)
    return (eye[:, None, :, None] * w[:, :, None, :]).reshape(h * n, h * n)


def _head_ones(width, head):
    idx = jnp.arange(width) // head
    return (idx[:, None] == idx[None, :]).astype(BF16)


PG = (1, 1)
PI = (1, 1)
PS = (1, 1)
PH = (1, 2)


def _unit_lower_inverse(a_list, ri, ci):
    mm = functools.partial(_mm, pa=PI[0], pb=PI[1])
    eye = jnp.where(ri == ci, 1.0, 0.0)
    leaf = (ri >> 3) == (ci >> 3)
    a8 = [jnp.where(leaf, a, 0.0) for a in a_list]
    a8_2 = [mm(x, x) for x in a8]
    a8_4 = [mm(x, x) for x in a8_2]
    t = [mm(eye + x, eye + y) for x, y in zip(a8, a8_2)]
    t = [mm(x, eye + y) for x, y in zip(t, a8_4)]
    zero = jnp.zeros((LANES, LANES), F32)
    for sh in (3, 4, 5):
        s = 1 << sh
        off = ((ri >> (sh + 1)) == (ci >> (sh + 1))) & ((ri >> sh) != (ci >> sh))
        t_lo = [_second_blocks(x, s) for x in t]
        b_lo = [mm(_second_blocks(jnp.where(off, a, 0.0), s), x) for a, x in zip(a_list, t)]
        d_lo = [mm(x, _interleave_blocks(zero, y, s)) for x, y in zip(t_lo, b_lo)]
        t = [_interleave_blocks(x, y + z, s) for x, y, z in zip(t, t_lo, d_lo)]
    return t


def _second_blocks(x, s):
    return jnp.concatenate(
        [x[s * (2 * m + 1):s * (2 * m + 2)] for m in range(x.shape[0] // (2 * s))], axis=0)


def _interleave_blocks(first_src, second, s):
    parts = []
    for m in range(first_src.shape[0] // (2 * s)):
        parts.append(first_src[2 * s * m:2 * s * m + s])
        parts.append(second[s * m:s * (m + 1)])
    return jnp.concatenate(parts, axis=0)


def _rwkv_kernel(p_ref, mu_ref, pv_ref, wlo_ref, gup_ref, tril_ref, o_ref, uprev_ref, h_ref):
    w = RWKV_WIDTH
    tt = p_ref.shape[0]
    c = CHUNK
    n_pairs = w // LANES
    units = [(j, p) for j in range(tt // c) for p in range(n_pairs)]

    @pl.when(pl.program_id(1) == 0)
    def _():
        uprev_ref[...] = jnp.zeros_like(uprev_ref)
        h_ref[...] = jnp.zeros_like(h_ref)

    u = p_ref[...]
    ext = jnp.concatenate([uprev_ref[...], u], axis=0)
    prev = pltpu.roll(ext, 1, 0)[SUBLANES:, :]
    uprev_ref[...] = u[tt - SUBLANES:, :]
    um = u + (prev - u) * mu_ref[...]

    r = um[:, 0:w]
    k = um[:, w:2 * w]
    v = um[:, 2 * w:3 * w]
    z = um[:, 3 * w:3 * w + LANES]
    gd = um[:, 3 * w + LANES:]
    w0 = pv_ref[0:1, :]
    a0 = pv_ref[1:2, :]
    k_k = pv_ref[2:3, :]
    k_a = pv_ref[3:4, :]
    r_k = pv_ref[4:5, :]
    ln_w = pv_ref[5:6, :]
    ln_b = pv_ref[6:7, :]

    lane_t = lax.broadcasted_iota(I32, (tt, LANES), 1)
    zz = jnp.where(lane_t < HEAD_DIM, jnp.tanh(z), z)
    lora = _mm_presplit(zz, wlo_ref[...])
    ld = (-0.6065306597126334) * _sigmoid(w0 + lora[:, :w])
    a = _sigmoid(a0 + lora[:, w:])
    g = _mm_presplit(_sigmoid(gd), gup_ref[...])

    ri = lax.broadcasted_iota(I32, (LANES, LANES), 0)
    ci = lax.broadcasted_iota(I32, (LANES, LANES), 1)
    even = lax.broadcasted_iota(I32, (c, LANES), 1) < HEAD_DIM
    wide = 2 * LANES
    ones_bd = jnp.where(
        (lax.broadcasted_iota(I32, (wide, wide), 0) >> 6)
        == (lax.broadcasted_iota(I32, (wide, wide), 1) >> 6), 1.0, 0.0).astype(BF16)

    def head_sum(x):
        return jnp.concatenate(
            [_mm(x[:, q * wide:(q + 1) * wide], ones_bd) for q in range(w // wide)], axis=1)

    kk = k * k_k
    kk = kk * lax.rsqrt(jnp.maximum(head_sum(kk * kk), 1e-24))
    kp = k * (1.0 + (a - 1.0) * k_a)
    kka = kk * a
    bonus = head_sum(r * kp * r_k) * v
    cum = _mm(tril_ref[...], ld, pb=3)
    cum_c = [cum[j * c + c - 1:j * c + c, :] for j in range(tt // c)]
    p_c = [jnp.exp(x) for x in cum_c]
    p_inv = jnp.exp(-cum)
    al = -kk * jnp.exp(cum - ld)
    rt = r * jnp.exp(cum)
    bt = kka * p_inv
    kt = kp * p_inv

    def to_chunk_end(x):
        return jnp.concatenate(
            [x[j * c:(j + 1) * c, :] * p_c[j] for j in range(tt // c)], axis=0)

    bh = to_chunk_end(bt)
    kh = to_chunk_end(kt)

    def blk(x):
        return [x[j * c:(j + 1) * c, p * LANES:(p + 1) * LANES] for j, p in units]

    def halves(x):
        xs = blk(x)
        return [jnp.where(even, y, 0.0) for y in xs], [jnp.where(even, 0.0, y) for y in xs]

    def rows(top, bot):
        return [jnp.concatenate([x, y], axis=0) for x, y in zip(top, bot)]

    def unstack(x):
        return x[:c, :] + x[c:, :]

    al_e, al_o = halves(al)
    rt_e, rt_o = halves(rt)
    bt_b, kt_b = blk(bt), blk(kt)
    bh_e, bh_o = halves(bh)
    kh_e, kh_o = halves(kh)
    v_e, v_o = halves(v)
    al_n = rows(al_e, al_o)
    bh_n = rows(bh_e, bh_o)
    v_s = rows(v_o, v_e)
    kh_s = rows(kh_o, kh_e)
    rt_b = blk(rt)
    nu = range(len(units))

    g0 = [_mm(x, y, NT, pa=PG[0], pb=PG[1])
          for x, y in zip(rows(al_e, rt_e), rows(bt_b, kt_b))]
    g1 = [_mm(x, y, NT, pa=PG[0], pb=PG[1])
          for x, y in zip(rows(rt_o, al_o), rows(kt_b, bt_b))]
    top = ri < c
    left = ci < c
    tri_s = (ri & (c - 1)) > (ci & (c - 1))
    tri_i = (ri & (c - 1)) >= (ci & (c - 1))
    diag_q = top == left

    def pick(x0, x1, in_q0, tri):
        return [jnp.where(tri, jnp.where(in_q0, x, y), 0.0) for x, y in zip(x0, x1)]

    a_ab = pick(g0, g1, top, diag_q & tri_s)
    a_rk = pick(g1, g0, top, diag_q & tri_i)
    a_ak = pick(g0, g1, top, (~diag_q) & tri_s)
    a_rb = pick(g1, g0, top, (~diag_q) & tri_i)
    x1 = [_mm(a_ak[n], v_s[n], pa=PS[0], pb=PS[1]) for n in nu]
    akv = [_mm(a_rk[n], v_s[n], pa=PS[0], pb=PS[1]) for n in nu]
    khv = [_mm(kh_s[n], v_s[n], TN, pa=PS[0], pb=PS[1]) for n in nu]
    t_inv = _unit_lower_inverse(a_ab, ri, ci)
    tw = [_mm(t_inv[n], jnp.concatenate([al_n[n], x1[n]], axis=1), pa=PS[0], pb=PS[1])
          for n in nu]
    qo = [_mm(a_rb[n], tw[n], pa=PS[0], pb=PS[1]) for n in nu]
    mn = [_mm(bh_n[n], tw[n], TN, pa=PS[0], pb=PS[1]) for n in nu]
    q = [rt_b[n] + unstack(qo[n][:, :LANES]) for n in nu]
    o_loc = [unstack(qo[n][:, LANES:] + akv[n]) for n in nu]
    m_full = [mn[n][:, :LANES]
              + jnp.where(ri == ci, p_c[j][:, p * LANES:(p + 1) * LANES], 0.0)
              for n, (j, p) in enumerate(units)]
    n_loc = [mn[n][:, LANES:] + khv[n] for n in nu]

    h = [h_ref[p] for p in range(n_pairs)]
    o_rows = []
    for j in range(tt // c):
        o_parts = []
        for p in range(n_pairs):
            n = j * n_pairs + p
            o_parts.append(_mm(q[n], h[p], pa=PH[0], pb=PH[1]) + o_loc[n])
            h[p] = _mm(m_full[n], h[p], pa=PH[0], pb=PH[1]) + n_loc[n]
        o_rows.append(jnp.concatenate(o_parts, axis=1))
    for p in range(n_pairs):
        h_ref[p] = h[p]

    o = jnp.concatenate(o_rows, axis=0)
    mean = head_sum(o) * (1.0 / HEAD_DIM)
    d = o - mean
    var = head_sum(d * d) * (1.0 / HEAD_DIM)
    on = d * lax.rsqrt(var + GN_EPS) * ln_w + ln_b
    o_ref[...] = ((on + bonus) * g).astype(o_ref.dtype)


RWKV_TILE = 256


def _rwkv7(p_rw, mu, w0, w_up, a0, a_up, g_up, k_k, k_a, r_k, ln_w, ln_b):
    bsz, t, npj = p_rw.shape
    w = RWKV_WIDTH
    tt = RWKV_TILE
    pv = jnp.stack([w0, a0, k_k, k_a, r_k.reshape(w), ln_w, ln_b, jnp.zeros((w,), F32)])
    wlo = jnp.zeros((LANES, 2 * w), F32)
    wlo = wlo.at[:DECAY_LORA, :w].set(w_up).at[DECAY_LORA:, w:].set(a_up)
    row = jnp.arange(tt)
    tril = ((row[:, None] >= row[None, :])
            & (row[:, None] // CHUNK == row[None, :] // CHUNK)).astype(BF16)
    const = lambda shape: pl.BlockSpec(shape, lambda b, i: (0, 0))
    return pl.pallas_call(
        _rwkv_kernel,
        grid=(bsz, t // tt),
        in_specs=[
            pl.BlockSpec((None, tt, npj), lambda b, i: (b, i, 0)),
            const((1, npj)), const((SUBLANES, w)), const((3 * LANES, 2 * w)),
            const((3 * GATE_LORA, w)), const((tt, tt)),
        ],
        out_specs=pl.BlockSpec((None, tt, w), lambda b, i: (b, i, 0)),
        out_shape=jax.ShapeDtypeStruct((bsz, t, w), BF16),
        scratch_shapes=[pltpu.VMEM((SUBLANES, npj), F32),
                        pltpu.VMEM((w // LANES, LANES, LANES), F32)],
        compiler_params=_cparams(("arbitrary", "arbitrary")),
        name="rwkv7",
    )(p_rw, mu.reshape(1, npj), pv, _presplit(wlo), _presplit(g_up), tril)


def _mix_kernel(x_ref, yl_ref, yr_ref, wo1_ref, wo2_ref, g1_ref, sh_ref, sc_ref, ng_ref,
                wr_ref, br_ref, x1_ref, h2_ref, rid_ref, rw_ref, cnt_ref):
    tm = x_ref.shape[0]
    rows = min(tm, MIX_BLOCK_ROWS)
    nblk = tm // rows
    blocks = [slice(i * rows, (i + 1) * rows) for i in range(nblk)]
    y = [jnp.dot(yl_ref[sl, :], wo1_ref[...], preferred_element_type=F32)
         + jnp.dot(yr_ref[sl, :], wo2_ref[...], preferred_element_type=F32) for sl in blocks]
    x1 = [x_ref[sl, :] + g1_ref[...] * yb for sl, yb in zip(blocks, y)]
    h2 = [_adaln(xb, ng_ref[...], sh_ref[...], sc_ref[...]).astype(BF16) for xb in x1]
    lg_all = [jnp.dot(jnp.concatenate([hb, hb], axis=1), wr_ref[...],
                      preferred_element_type=F32) + br_ref[...] for hb in h2]
    for sl, xb, hb in zip(blocks, x1, h2):
        x1_ref[sl, :] = xb
        h2_ref[sl, :] = _pack_bf16_halves(hb)

    neg = -jnp.inf
    lane = lax.broadcasted_iota(I32, (rows, LANES), 1)
    lane_f = lane.astype(F32)

    def first_argmax(xs):
        m = [jnp.max(x, axis=-1, keepdims=True) for x in xs]
        idx = [jnp.min(jnp.where(x == mi, lane_f, float(LANES)), axis=-1, keepdims=True)
               for x, mi in zip(xs, m)]
        return m, [i.astype(I32) for i in idx]

    lg = [jnp.where(lane < N_GROUPS, x, neg) for x in lg_all]
    gm, g_idx = first_argmax(lg)
    g_w = [1.0 / jnp.sum(jnp.exp(x - m), axis=-1, keepdims=True)
           for x, m in zip(lg, gm)]
    lo = [N_GROUPS + EXPERTS_PER_GROUP * g for g in g_idx]
    le = [jnp.where((lane >= l) & (lane < l + EXPERTS_PER_GROUP), x, neg)
          for x, l in zip(lg_all, lo)]
    v1, i1 = first_argmax(le)
    v2, i2 = first_argmax([jnp.where(lane == i, neg, x) for x, i in zip(le, i1)])

    @pl.when((pl.program_id(0) == 0) & (pl.program_id(1) == 0))
    def _():
        cnt_ref[...] = jnp.zeros_like(cnt_ref)

    e_lane = lane + N_GROUPS
    hits = sum(jnp.sum(jnp.where((e_lane == a) | (e_lane == b), 1.0, 0.0), axis=0, keepdims=True)
               for a, b in zip(i1, i2))
    cnt_ref[...] = cnt_ref[...] + hits.astype(I32)

    for blk in range(nblk):
        e2 = jnp.exp(v2[blk] - v1[blk])
        w1 = g_w[blk] / (1.0 + e2)
        w2 = g_w[blk] * e2 / (1.0 + e2)
        sl = blocks[blk]
        rid_ref[sl, :] = jnp.where(lane == 0, i1[blk] - N_GROUPS,
                                   jnp.where(lane == 1, i2[blk] - N_GROUPS, 0))
        rw_ref[sl, :] = jnp.where(lane == 0, w1, jnp.where(lane == 1, w2, 0.0))


def _mix(x, y_lru, y_rw, w_out, g1, sh2, sc2, ng, w_grp, b_grp, w_exp, b_exp, tm=256):
    bsz, t, d = x.shape
    wo = w_out.astype(BF16)
    wl = LRU_WIDTH
    wr = jnp.zeros((d, LANES), F32).at[:, :N_GROUPS].set(w_grp)
    wr = wr.at[:, N_GROUPS:N_GROUPS + N_EXPERTS].set(w_exp)
    wr = jnp.concatenate(_split(wr, 2), axis=0)
    br = jnp.zeros((1, LANES), F32).at[0, :N_GROUPS].set(b_grp)
    br = br.at[0, N_GROUPS:N_GROUPS + N_EXPERTS].set(b_exp)
    vec = pl.BlockSpec((None, 1, d), lambda b, i: (b, 0, 0))
    const = lambda shape: pl.BlockSpec(shape, lambda b, i: (0, 0))
    row = lambda n: pl.BlockSpec((None, tm, n), lambda b, i: (b, i, 0))
    return pl.pallas_call(
        _mix_kernel,
        grid=(bsz, t // tm),
        in_specs=[row(d), row(wl), row(d - wl), const((wl, d)), const((d - wl, d)),
                  vec, vec, vec, const((1, d)), const((2 * d, LANES)), const((1, LANES))],
        out_specs=[row(d), row(d // 2), row(LANES), row(LANES), const((SUBLANES, LANES))],
        out_shape=[jax.ShapeDtypeStruct((bsz, t, d), F32),
                   jax.ShapeDtypeStruct((bsz, t, d // 2), I32),
                   jax.ShapeDtypeStruct((bsz, t, LANES), I32),
                   jax.ShapeDtypeStruct((bsz, t, LANES), F32),
                   jax.ShapeDtypeStruct((SUBLANES, LANES), I32)],
        compiler_params=_cparams(("arbitrary", "arbitrary")),
        name="outproj_adaln2_router",
    )(x, y_lru, y_rw, wo[:wl], wo[wl:], g1, sh2, sc2, ng.reshape(1, d), wr, br)


def _plan_kernel(rid_ref, cnt_ref, tril_ref, pos_ref, te_ref, base_ref):
    i = pl.program_id(0)
    tp = rid_ref.shape[0]
    lane = lax.broadcasted_iota(I32, (tp, LANES), 1)
    rid = rid_ref[...]
    oh0 = lane == rid[:, 0:1]
    oh1 = lane == rid[:, 1:2]
    oh = jnp.where(oh0 | oh1, 1.0, 0.0)

    @pl.when(i == 0)
    def _():
        cnt = cnt_ref[...]
        shift = EXPERT_TILE.bit_length() - 1
        padded = ((cnt + (EXPERT_TILE - 1)) >> shift) << shift
        l8 = lax.broadcasted_iota(I32, (SUBLANES, LANES), 1)
        end = padded
        s = 1
        while s < LANES:
            end = end + jnp.where(l8 >= s, pltpu.roll(end, s, 1), 0)
            s *= 2
        base_ref[...] = end - padded
        nt = te_ref.shape[0]
        j = lax.broadcasted_iota(I32, (nt, LANES), 0) * EXPERT_TILE
        lt = lax.broadcasted_iota(I32, (nt, LANES), 1)
        done = jnp.where((lt < N_EXPERTS) & (end[0:1, :] <= j), 1.0, 0.0)
        e_of = jnp.minimum(jnp.sum(done, axis=-1, keepdims=True), float(N_EXPERTS - 1)).astype(I32)
        total = end[0:1, N_EXPERTS - 1:N_EXPERTS]
        mine = lt == e_of
        cnt_e = jnp.sum(jnp.where(mine, cnt[0:1, :], 0), axis=-1, keepdims=True)
        start_e = jnp.sum(jnp.where(mine, end[0:1, :] - padded[0:1, :], 0), axis=-1, keepdims=True)
        valid = jnp.clip(cnt_e - (j[:, 0:1] - start_e), 0, EXPERT_TILE)
        te_ref[...] = jnp.where(lt == 0, e_of, jnp.where(lt == 1, total >> shift,
                                                        jnp.where(lt == 2, valid, 0)))

    prefix = jnp.dot(tril_ref[...], oh.astype(BF16), preferred_element_type=F32)
    dest = base_ref[0:1, :] + prefix.astype(I32)
    pos0 = jnp.sum(jnp.where(oh0, dest, 0), axis=-1, keepdims=True)
    pos1 = jnp.sum(jnp.where(oh1, dest, 0), axis=-1, keepdims=True)
    both = jnp.where(lane == 0, pos0, jnp.where(lane == 1, pos1, 0))
    pos_ref[...] = both.T[:SUBLANES, :]
    base_ref[...] = base_ref[...] + jnp.sum(oh, axis=0, keepdims=True).astype(I32)


def _plan(rid, cnt, n_tiles, tp=512):
    n = rid.shape[0]
    tp = min(tp, n)
    nt_pad = -(-n_tiles // SUBLANES) * SUBLANES
    tril = (jnp.arange(tp)[:, None] > jnp.arange(tp)[None, :]).astype(BF16)
    pos, te = pl.pallas_call(
        _plan_kernel,
        grid=(n // tp,),
        in_specs=[pl.BlockSpec((tp, LANES), lambda i: (i, 0)),
                  pl.BlockSpec((SUBLANES, LANES), lambda i: (0, 0)),
                  pl.BlockSpec((tp, tp), lambda i: (0, 0))],
        out_specs=[pl.BlockSpec((None, SUBLANES, tp), lambda i: (i, 0, 0)),
                   pl.BlockSpec((nt_pad, LANES), lambda i: (0, 0))],
        out_shape=[jax.ShapeDtypeStruct((n // tp, SUBLANES, tp), I32),
                   jax.ShapeDtypeStruct((nt_pad, LANES), I32)],
        scratch_shapes=[pltpu.VMEM((SUBLANES, LANES), I32)],
        compiler_params=_cparams(("arbitrary",)),
        name="route_plan",
    )(rid, cnt, tril)
    return pos, te


SC_CORES = 2
SC_SUBCORES = 16
SC_WORKERS = SC_CORES * SC_SUBCORES
SC_WINDOW = 64


def _sc_mesh():
    return plsc.VectorSubcoreMesh(core_axis_name="c", subcore_axis_name="s",
                                  num_cores=SC_CORES, num_subcores=SC_SUBCORES)


def _sc_two_buffer_loop(n_win, fill, drain):
    assert n_win % 2 == 0

    def start(copies):
        for cp in copies:
            cp.start()

    def wait(copies):
        for cp in copies:
            cp.wait()

    start(fill(0, 0))

    @pl.loop(0, n_win, step=2)
    def _(j):
        for b in range(2):
            jj = j + b

            @pl.when(jj >= 1)
            def _():
                wait(drain(jj - 1, 1 - b))

            @pl.when(jj + 1 < n_win)
            def _():
                start(fill(jj + 1, 1 - b))

            wait(fill(jj, b))
            start(drain(jj, b))

    wait(drain(n_win - 1, 1))


def _sc_scatter_rows(rows, pos0, pos1, n_out):
    n, d = rows.shape
    per_w = n // SC_WORKERS
    n_win = per_w // SC_WINDOW
    shape3 = (SC_WORKERS, n_win, SC_WINDOW)

    @functools.partial(
        pl.kernel, mesh=_sc_mesh(), out_type=jax.ShapeDtypeStruct((n_out, d), rows.dtype),
        scratch_types=[pltpu.VMEM((n_win, SC_WINDOW), I32), pltpu.VMEM((n_win, SC_WINDOW), I32),
                       pltpu.VMEM((2, SC_WINDOW, d), rows.dtype),
                       pltpu.SemaphoreType.DMA((2,)), pltpu.SemaphoreType.DMA((2,))],
        name="sc_scatter_rows")
    def scatter(rows_hbm, p0_hbm, p1_hbm, out_hbm, p0_v, p1_v, buf, in_sem, out_sem):
        wid = lax.axis_index("s") * SC_CORES + lax.axis_index("c")
        base = wid * per_w
        pltpu.sync_copy(p0_hbm.at[wid], p0_v)
        pltpu.sync_copy(p1_hbm.at[wid], p1_v)

        def fill(j, b):
            src = rows_hbm.at[pl.ds(base + j * SC_WINDOW, SC_WINDOW)]
            return [pltpu.make_async_copy(src, buf.at[b], in_sem.at[b])]

        def drain(j, b):
            return [pltpu.make_async_copy(buf.at[b], out_hbm.at[p.at[j]], out_sem.at[b])
                    for p in (p0_v, p1_v)]

        _sc_two_buffer_loop(n_win, fill, drain)

    return scatter(rows, pos0.reshape(shape3), pos1.reshape(shape3))


def _sc_gather_rows(table, idx):
    m = idx.shape[0]
    d = table.shape[1]
    per_w = m // SC_WORKERS
    n_win = per_w // SC_WINDOW

    @functools.partial(
        pl.kernel, mesh=_sc_mesh(), out_type=jax.ShapeDtypeStruct((m, d), table.dtype),
        scratch_types=[pltpu.VMEM((n_win, SC_WINDOW), I32),
                       pltpu.VMEM((2, SC_WINDOW, d), table.dtype),
                       pltpu.SemaphoreType.DMA((2,)), pltpu.SemaphoreType.DMA((2,))],
        name="sc_gather_rows")
    def gather(table_hbm, idx_hbm, out_hbm, idx_v, buf, in_sem, out_sem):
        wid = lax.axis_index("s") * SC_CORES + lax.axis_index("c")
        base = wid * per_w
        pltpu.sync_copy(idx_hbm.at[wid], idx_v)

        def fill(j, b):
            return [pltpu.make_async_copy(table_hbm.at[idx_v.at[j]], buf.at[b], in_sem.at[b])]

        def drain(j, b):
            dst = out_hbm.at[pl.ds(base + j * SC_WINDOW, SC_WINDOW)]
            return [pltpu.make_async_copy(buf.at[b], dst, out_sem.at[b])]

        _sc_two_buffer_loop(n_win, fill, drain)

    return gather(table, idx.reshape(SC_WORKERS, n_win, SC_WINDOW))


def _expert_kernel(te_ref, nu_ref, valid_ref, x_ref, w1_ref, w3_ref, w2_ref, o_ref,
                   w1b_ref, w3b_ref, w2b_ref):
    j = pl.program_id(0)
    active = j < nu_ref[0]

    @pl.when(active & ((j == 0) | (te_ref[j] != te_ref[jnp.maximum(j - 1, 0)])))
    def _():
        w1b_ref[...] = w1_ref[...].astype(BF16)
        w3b_ref[...] = w3_ref[...].astype(BF16)
        w2b_ref[...] = w2_ref[...].astype(BF16)

    @pl.when(active)
    def _():
        nblk = 2
        rows = x_ref.shape[0] // nblk
        blocks = [slice(i * rows, (i + 1) * rows) for i in range(nblk)]
        row = lax.broadcasted_iota(I32, (rows, x_ref.shape[1]), 0)
        xs = [_unpack_bf16_halves(
            jnp.where(row + i * rows < valid_ref[j], x_ref[sl, :], 0)).astype(BF16)
            for i, sl in enumerate(blocks)]
        h1 = [jnp.dot(x, w1b_ref[...], preferred_element_type=F32) for x in xs]
        h3 = [jnp.dot(x, w3b_ref[...], preferred_element_type=F32) for x in xs]
        hid = [(a * jax.nn.sigmoid(a) * b).astype(BF16) for a, b in zip(h1, h3)]
        for sl, hb in zip(blocks, hid):
            o_ref[sl, :] = _pack_bf16_halves(
                jnp.dot(hb, w2b_ref[...], preferred_element_type=F32))

    @pl.when(j >= nu_ref[0])
    def _():
        o_ref[...] = jnp.zeros_like(o_ref)


def _experts(xs, te, nu, valid, w1, w3, w2):
    n_rows = xs.shape[0]
    d, de = w1.shape[-2:]
    nt = n_rows // EXPERT_TILE
    used = lambda j, nu: jnp.minimum(j, nu[0] - 1)
    grid_spec = pltpu.PrefetchScalarGridSpec(
        num_scalar_prefetch=3,
        grid=(nt,),
        in_specs=[
            pl.BlockSpec((EXPERT_TILE, d // 2), lambda j, te, nu, va: (used(j, nu), 0)),
            pl.BlockSpec((None, d, de), lambda j, te, nu, va: (te[used(j, nu)], 0, 0)),
            pl.BlockSpec((None, d, de), lambda j, te, nu, va: (te[used(j, nu)], 0, 0)),
            pl.BlockSpec((None, de, d), lambda j, te, nu, va: (te[used(j, nu)], 0, 0)),
        ],
        out_specs=pl.BlockSpec((EXPERT_TILE, d // 2), lambda j, te, nu, va: (j, 0)),
        scratch_shapes=[pltpu.VMEM((d, de), BF16), pltpu.VMEM((d, de), BF16),
                        pltpu.VMEM((de, d), BF16)],
    )
    return pl.pallas_call(
        _expert_kernel,
        grid_spec=grid_spec,
        out_shape=jax.ShapeDtypeStruct((n_rows, d // 2), I32),
        compiler_params=_cparams(("arbitrary",)),
        name="expert_mlp",
    )(te, nu, valid, xs, w1, w3, w2)


def _combine_kernel(x1_ref, rw_ref, g2_ref, fg_ref, y0_ref, y1_ref, o_ref):
    rw = rw_ref[...]
    moe = (rw[:, 0:1] * _unpack_bf16_halves(y0_ref[...])
           + rw[:, 1:2] * _unpack_bf16_halves(y1_ref[...]))
    x2 = x1_ref[...] + g2_ref[...] * moe
    ms = jnp.mean(x2 * x2, axis=-1, keepdims=True)
    o_ref[...] = x2 * lax.rsqrt(ms + RMS_EPS) * fg_ref[...]


def _combine(x1, rw, yg, g2, final_g, tc):
    bsz, t, d = x1.shape
    row = lambda n: pl.BlockSpec((None, tc, n), lambda b, i: (b, i, 0))
    slot = lambda s: pl.BlockSpec((None, None, tc, d // 2), lambda b, i: (s, b, i, 0))
    return pl.pallas_call(
        _combine_kernel,
        grid=(bsz, t // tc),
        in_specs=[row(d), row(LANES),
                  pl.BlockSpec((None, 1, d), lambda b, i: (b, 0, 0)),
                  pl.BlockSpec((1, d), lambda b, i: (0, 0)),
                  slot(0), slot(1)],
        out_specs=row(d),
        out_shape=jax.ShapeDtypeStruct((bsz, t, d), F32),
        compiler_params=_cparams(("arbitrary", "arbitrary")),
        name="combine_final_norm",
    )(x1, rw, g2, final_g.reshape(1, d), yg, yg)


def _row_tile(t, want):
    return want if t % want == 0 else t


def kernel(x, c, w_ada, b_ada, norm1_g, w_in, conv_w, conv_b, lru_wa, lru_ba, lru_wi, lru_bi, lru_lam, lru_norm_g, tok_mu, w0, w_up, a0, a_up, g_up, k_k, k_a, r_k, ln_x_w, ln_x_b, w_out, norm2_g, w_grp, b_grp, w_exp, b_exp, w1, w3, w2, final_g):
    bsz, t, d = x.shape
    n = bsz * t
    depth = w_ada.shape[0]
    assert depth == 1, "the combine kernel applies the final norm: only DEPTH == 1 is wired"
    tile = _row_tile(t, ROW_TILE)
    n_tiles = (n * TOP_K) // EXPERT_TILE + N_EXPERTS
    for l in range(depth):
        mod = _modulation(c, w_ada[l], b_ada[l]).reshape(bsz, 6, 1, d)
        sh1, sc1, g1, sh2, sc2, g2 = (mod[:, i] for i in range(6))
        y_lru, p_rw = _inproj_lru(x, sh1, sc1, norm1_g[l], w_in[l], conv_w[l], conv_b[l],
                                  lru_wa[l], lru_ba[l], lru_wi[l], lru_bi[l], lru_lam[l],
                                  lru_norm_g[l], tm=tile)
        y_rw = _rwkv7(p_rw, tok_mu[l], w0[l], w_up[l], a0[l], a_up[l], g_up[l], k_k[l], k_a[l],
                      r_k[l], ln_x_w[l], ln_x_b[l])
        x1, h2, rid, rw, cnt = _mix(x, y_lru, y_rw, w_out[l], g1, sh2, sc2, norm2_g[l],
                                    w_grp[l], b_grp[l], w_exp[l], b_exp[l],
                                    tm=_row_tile(t, MIX_TILE))
        pos, te = _plan(rid.reshape(n, LANES), cnt, n_tiles)
        pos0, pos1 = pos[:, 0, :].reshape(n), pos[:, 1, :].reshape(n)
        xs = _sc_scatter_rows(h2.reshape(n, d // 2), pos0, pos1, n_tiles * EXPERT_TILE)
        ys = _experts(xs, te[:n_tiles, 0], te[0:1, 1], te[:n_tiles, 2], w1[l], w3[l], w2[l])
        yg = _sc_gather_rows(ys, jnp.concatenate([pos0, pos1]))
        x = _combine(x1, rw, yg.reshape(TOP_K, bsz, t, d // 2), g2, final_g,
                     _row_tile(t, COMBINE_TILE))
    return x
```

```python
import functools

import jax
import jax.numpy as jnp
from jax import lax
from jax.experimental import pallas as pl
from jax.experimental.pallas import tpu as pltpu
from jax.experimental.pallas import tpu_sc as plsc

F32 = jnp.float32
BF16 = jnp.bfloat16
I32 = jnp.int32

LRU_WIDTH = 512
LRU_HEAD_DIM = 64
CONV_WIDTH = 4
LRU_C = 8.0
RWKV_WIDTH = 512
HEAD_DIM = 64
DECAY_LORA = 64
AAA_LORA = 64
GATE_LORA = 128
RWKV_PROJ = 3 * RWKV_WIDTH + DECAY_LORA + AAA_LORA + GATE_LORA
N_GROUPS = 4
EXPERTS_PER_GROUP = 8
N_EXPERTS = N_GROUPS * EXPERTS_PER_GROUP
TOP_K = 2
RMS_EPS = 1e-6
GN_EPS = 64e-5

LANES = 128
SUBLANES = 8
CHUNK = 64
EXPERT_TILE = 512
ROW_TILE = 256
MIX_TILE = 1024
MIX_BLOCK_ROWS = 128
COMBINE_TILE = 1024
VMEM_LIMIT = 48 * 1024 * 1024

NN = (((1,), (0,)), ((), ()))
NT = (((1,), (1,)), ((), ()))
TN = (((0,), (0,)), ((), ()))


def _split(x, n):
    if x.dtype == BF16:
        return [x]
    parts = []
    rem = x
    for i in range(n):
        p = rem.astype(BF16)
        parts.append(p)
        if i + 1 < n:
            rem = rem - p.astype(F32)
    return parts


def _mm(a, b, dn=NN, pa=1, pb=1):
    aps = _split(a, pa)
    bps = _split(b, pb)
    order = max(len(aps), len(bps))
    terms = [(i, j) for i in range(len(aps)) for j in range(len(bps)) if i + j < order]
    ka = dn[0][0][0]
    kb = dn[0][1][0]
    if len(terms) > 1 and a.shape[ka] % LANES == 0:
        a_cat = jnp.concatenate([aps[i] for i, _ in terms], axis=ka)
        b_cat = jnp.concatenate([bps[j] for _, j in terms], axis=kb)
        return lax.dot_general(a_cat, b_cat, dn, preferred_element_type=F32)
    out = None
    for i, j in terms:
        t = lax.dot_general(aps[i], bps[j], dn, preferred_element_type=F32)
        out = t if out is None else out + t
    return out


def _presplit(b):
    hi, lo = _split(b, 2)
    return jnp.concatenate([hi, lo, hi], axis=0)


def _mm_presplit(a, b3):
    a_hi, a_lo = _split(a, 2)
    return jnp.dot(jnp.concatenate([a_hi, a_hi, a_lo], axis=1), b3, preferred_element_type=F32)


def _pack_bf16_halves(x):
    n = x.shape[1] // 2
    bits = lax.bitcast_convert_type(x.astype(BF16).astype(F32), I32)
    return bits[:, n:] | ((bits[:, :n] >> 16) & 0xFFFF)


def _unpack_bf16_halves(p):
    lo = lax.bitcast_convert_type(p << 16, F32)
    hi = lax.bitcast_convert_type(p & (-65536), F32)
    return jnp.concatenate([lo, hi], axis=1)


def _softplus(x):
    return jnp.maximum(x, 0.0) + jnp.log1p(jnp.exp(-jnp.abs(x)))


def _cparams(sem):
    return pltpu.CompilerParams(dimension_semantics=sem, vmem_limit_bytes=VMEM_LIMIT)


def _mod_kernel(c_ref, w_ref, b_ref, o_ref):
    c = c_ref[...]
    s = c * jax.nn.sigmoid(c)
    o_ref[...] = _mm(s, w_ref[...], pa=2, pb=2) + b_ref[...]


def _modulation(c, w_ada, b_ada):
    bsz, d = c.shape
    n_out = w_ada.shape[1]
    rows = -(-bsz // SUBLANES) * SUBLANES
    c_pad = jnp.zeros((rows, d), F32).at[:bsz].set(c)
    bn = d
    out = pl.pallas_call(
        _mod_kernel,
        grid=(n_out // bn,),
        in_specs=[
            pl.BlockSpec((rows, d), lambda j: (0, 0)),
            pl.BlockSpec((d, bn), lambda j: (0, j)),
            pl.BlockSpec((1, bn), lambda j: (0, j)),
        ],
        out_specs=pl.BlockSpec((rows, bn), lambda j: (0, j)),
        out_shape=jax.ShapeDtypeStruct((rows, n_out), F32),
        compiler_params=_cparams(("arbitrary",)),
        name="adaln_mod",
    )(c_pad, w_ada, b_ada.reshape(1, n_out))
    return out[:bsz]


def _adaln(x, g, shift, scale):
    ms = jnp.mean(x * x, axis=-1, keepdims=True)
    return (x * lax.rsqrt(ms + RMS_EPS)) * (g * (1.0 + scale)) + shift


def _inproj_lru_kernel(x_ref, sh_ref, sc_ref, g_ref, wl_ref, wr_ref,
                       cw_ref, cb_ref, wab_ref, bab_ref, lam_ref, ng_ref, ones_ref,
                       yl_ref, pr_ref, xprev_ref, hprev_ref):
    @pl.when(pl.program_id(1) == 0)
    def _():
        xprev_ref[...] = jnp.zeros_like(xprev_ref)
        hprev_ref[...] = jnp.zeros_like(hprev_ref)

    w = LRU_WIDTH
    tm = x_ref.shape[0]
    nblk = 2
    rows = tm // nblk
    blocks = [slice(i * rows, (i + 1) * rows) for i in range(nblk)]
    h = [_adaln(x_ref[sl, :], g_ref[...], sh_ref[...], sc_ref[...]).astype(BF16)
         for sl in blocks]
    ux = [jnp.dot(hb, wl_ref[:, :w], preferred_element_type=F32) for hb in h]

    def remaining_columns():
        for sl, hb in zip(blocks, h):
            pr_ref[sl, :] = jnp.dot(hb, wr_ref[...], preferred_element_type=F32)
        return [jnp.dot(hb, wl_ref[:, w:], preferred_element_type=F32) for hb in h]

    y = _lru_tile(ux, remaining_columns, cw_ref, cb_ref, wab_ref, bab_ref, lam_ref, ng_ref,
                  ones_ref, xprev_ref, hprev_ref)
    for sl, yb in zip(blocks, y):
        yl_ref[sl, :] = yb.astype(yl_ref.dtype)


def _inproj_lru(x, sh1, sc1, g, w_in, conv_w, conv_b, wa, ba, wi, bi, lam, norm_g, tm=256):
    bsz, t, d = x.shape
    w = LRU_WIDTH
    nl = 2 * w
    nr = RWKV_PROJ
    wl = w_in[:, :nl].astype(BF16)
    wr = w_in[:, nl:].astype(BF16)
    wab = jnp.concatenate([_block_diag(wa), _block_diag(wi)], axis=1).astype(BF16)
    bab = jnp.concatenate([ba.reshape(1, w), bi.reshape(1, w)], axis=1)
    vec = pl.BlockSpec((None, 1, d), lambda b, i: (b, 0, 0))
    const = lambda shape: pl.BlockSpec(shape, lambda b, i: (0, 0))
    return pl.pallas_call(
        _inproj_lru_kernel,
        grid=(bsz, t // tm),
        in_specs=[
            pl.BlockSpec((None, tm, d), lambda b, i: (b, i, 0)),
            vec, vec, const((1, d)), const((d, nl)), const((d, nr)),
            const((CONV_WIDTH, w)), const((1, w)), const((w, 2 * w)), const((1, 2 * w)),
            const((1, w)), const((1, w)), const((w, w)),
        ],
        out_specs=[
            pl.BlockSpec((None, tm, w), lambda b, i: (b, i, 0)),
            pl.BlockSpec((None, tm, nr), lambda b, i: (b, i, 0)),
        ],
        out_shape=[
            jax.ShapeDtypeStruct((bsz, t, w), BF16),
            jax.ShapeDtypeStruct((bsz, t, nr), F32),
        ],
        scratch_shapes=[pltpu.VMEM((SUBLANES, w), F32), pltpu.VMEM((SUBLANES, w), F32)],
        compiler_params=_cparams(("arbitrary", "arbitrary")),
        name="adaln1_inproj_rglru",
    )(x, sh1, sc1, g.reshape(1, d), wl, wr, conv_w, conv_b.reshape(1, w), wab, bab,
      lam.reshape(1, w), norm_g.reshape(1, w), _head_ones(w, LRU_HEAD_DIM))


def _gelu_tanh(x):
    c = 0.7978845608028654
    half = 0.5 * x
    return half * jnp.tanh(x * (c + (c * 0.044715) * (x * x))) + half


def _sigmoid(x):
    return 0.5 * jnp.tanh(0.5 * x) + 0.5


def _lru_tile(ux, gate_branch, cw_ref, cb_ref, wab_ref, bab_ref, lam_ref, ng_ref, ones_ref,
              xprev_ref, hprev_ref):
    w = LRU_WIDTH
    rows = ux[0].shape[0]
    prev8 = [xprev_ref[...]] + [u[rows - SUBLANES:, :] for u in ux[:-1]]
    ext = [jnp.concatenate([p, u], axis=0) for p, u in zip(prev8, ux)]
    xc = [cb_ref[...] + cw_ref[CONV_WIDTH - 1:CONV_WIDTH, :] * u for u in ux]
    for k in range(1, CONV_WIDTH):
        tap = cw_ref[CONV_WIDTH - 1 - k:CONV_WIDTH - k, :]
        xc = [x + tap * pltpu.roll(e, k, 0)[SUBLANES:, :] for x, e in zip(xc, ext)]
    xprev_ref[...] = ux[-1][rows - SUBLANES:, :]

    gates = [jnp.dot(x.astype(BF16), wab_ref[...], preferred_element_type=F32) + bab_ref[...]
             for x in xc]
    ug = gate_branch()
    r = [_sigmoid(g[:, :w]) for g in gates]
    ig = [_sigmoid(g[:, w:]) for g in gates]
    sp = _softplus(-lam_ref[...])
    log_a = [(-LRU_C) * x * sp for x in r]
    a = [jnp.exp(x) for x in log_a]
    th = [jnp.tanh(x) for x in log_a]
    q = [(-2.0 * x) / (1.0 - x) for x in th]
    root_q = [jnp.where(x > 0.0, x * lax.rsqrt(x), 0.0) for x in q]
    b = [s * (i * x) for s, i, x in zip(root_q, ig, xc)]

    row8 = lax.broadcasted_iota(I32, (rows, w), 0) & (SUBLANES - 1)
    acc_a, acc_b = a, b
    for s in (1, 2, 4):
        live = row8 >= s
        sh_a = [pltpu.roll(x, s, 0) for x in acc_a]
        sh_b = [pltpu.roll(x, s, 0) for x in acc_b]
        acc_b = [jnp.where(live, x * sb + y, y) for x, sb, y in zip(acc_a, sh_b, acc_b)]
        acc_a = [jnp.where(live, x * sa, x) for x, sa in zip(acc_a, sh_a)]
    carry = hprev_ref[SUBLANES - 1:SUBLANES, :]
    h = []
    for xa, xb in zip(acc_a, acc_b):
        groups = []
        for gi in range(rows // SUBLANES):
            lo = gi * SUBLANES
            groups.append(xa[lo:lo + SUBLANES, :] * carry + xb[lo:lo + SUBLANES, :])
            last = lo + SUBLANES - 1
            carry = xa[last:last + 1, :] * carry + xb[last:last + 1, :]
        h.append(jnp.concatenate(groups, axis=0))
    hprev_ref[...] = h[-1][rows - SUBLANES:, :]

    y = [x * _gelu_tanh(g) for x, g in zip(h, ug)]
    ms = [_mm(x * x, ones_ref[...]) * (1.0 / LRU_HEAD_DIM) for x in y]
    return [x * lax.rsqrt(m + RMS_EPS) * ng_ref[...] for x, m in zip(y, ms)]


def _block_diag(w):
    h, n, _ = w.shape
    eye = jnp.eye(h, dtype=w.dtype)
    return (eye[:, None, :, None] * w[:, :, None, :]).reshape(h * n, h * n)


def _head_ones(width, head):
    idx = jnp.arange(width) // head
    return (idx[:, None] == idx[None, :]).astype(BF16)


PG = (1, 1)
PI = (1, 1)
PS = (1, 1)
PH = (1, 2)


def _unit_lower_inverse(a_list, ri, ci):
    mm = functools.partial(_mm, pa=PI[0], pb=PI[1])
    eye = jnp.where(ri == ci, 1.0, 0.0)
    leaf = (ri >> 3) == (ci >> 3)
    a8 = [jnp.where(leaf, a, 0.0) for a in a_list]
    a8_2 = [mm(x, x) for x in a8]
    a8_4 = [mm(x, x) for x in a8_2]
    t = [mm(eye + x, eye + y) for x, y in zip(a8, a8_2)]
    t = [mm(x, eye + y) for x, y in zip(t, a8_4)]
    zero = jnp.zeros((LANES, LANES), F32)
    for sh in (3, 4, 5):
        s = 1 << sh
        off = ((ri >> (sh + 1)) == (ci >> (sh + 1))) & ((ri >> sh) != (ci >> sh))
        t_lo = [_second_blocks(x, s) for x in t]
        b_lo = [mm(_second_blocks(jnp.where(off, a, 0.0), s), x) for a, x in zip(a_list, t)]
        d_lo = [mm(x, _interleave_blocks(zero, y, s)) for x, y in zip(t_lo, b_lo)]
        t = [_interleave_blocks(x, y + z, s) for x, y, z in zip(t, t_lo, d_lo)]
    return t


def _second_blocks(x, s):
    return jnp.concatenate(
        [x[s * (2 * m + 1):s * (2 * m + 2)] for m in range(x.shape[0] // (2 * s))], axis=0)


def _interleave_blocks(first_src, second, s):
    parts = []
    for m in range(first_src.shape[0] // (2 * s)):
        parts.append(first_src[2 * s * m:2 * s * m + s])
        parts.append(second[s * m:s * (m + 1)])
    return jnp.concatenate(parts, axis=0)


def _rwkv_kernel(p_ref, mu_ref, pv_ref, wlo_ref, gup_ref, tril_ref, o_ref, uprev_ref, h_ref):
    w = RWKV_WIDTH
    tt = p_ref.shape[0]
    c = CHUNK
    n_pairs = w // LANES
    units = [(j, p) for j in range(tt // c) for p in range(n_pairs)]

    @pl.when(pl.program_id(1) == 0)
    def _():
        uprev_ref[...] = jnp.zeros_like(uprev_ref)
        h_ref[...] = jnp.zeros_like(h_ref)

    u = p_ref[...]
    ext = jnp.concatenate([uprev_ref[...], u], axis=0)
    prev = pltpu.roll(ext, 1, 0)[SUBLANES:, :]
    uprev_ref[...] = u[tt - SUBLANES:, :]
    um = u + (prev - u) * mu_ref[...]

    r = um[:, 0:w]
    k = um[:, w:2 * w]
    v = um[:, 2 * w:3 * w]
    z = um[:, 3 * w:3 * w + LANES]
    gd = um[:, 3 * w + LANES:]
    w0 = pv_ref[0:1, :]
    a0 = pv_ref[1:2, :]
    k_k = pv_ref[2:3, :]
    k_a = pv_ref[3:4, :]
    r_k = pv_ref[4:5, :]
    ln_w = pv_ref[5:6, :]
    ln_b = pv_ref[6:7, :]

    lane_t = lax.broadcasted_iota(I32, (tt, LANES), 1)
    zz = jnp.where(lane_t < HEAD_DIM, jnp.tanh(z), z)
    lora = _mm_presplit(zz, wlo_ref[...])
    ld = (-0.6065306597126334) * _sigmoid(w0 + lora[:, :w])
    a = _sigmoid(a0 + lora[:, w:])
    g = _mm_presplit(_sigmoid(gd), gup_ref[...])

    ri = lax.broadcasted_iota(I32, (LANES, LANES), 0)
    ci = lax.broadcasted_iota(I32, (LANES, LANES), 1)
    even = lax.broadcasted_iota(I32, (c, LANES), 1) < HEAD_DIM
    wide = 2 * LANES
    ones_bd = jnp.where(
        (lax.broadcasted_iota(I32, (wide, wide), 0) >> 6)
        == (lax.broadcasted_iota(I32, (wide, wide), 1) >> 6), 1.0, 0.0).astype(BF16)

    def head_sum(x):
        return jnp.concatenate(
            [_mm(x[:, q * wide:(q + 1) * wide], ones_bd) for q in range(w // wide)], axis=1)

    kk = k * k_k
    kk = kk * lax.rsqrt(jnp.maximum(head_sum(kk * kk), 1e-24))
    kp = k * (1.0 + (a - 1.0) * k_a)
    kka = kk * a
    bonus = head_sum(r * kp * r_k) * v
    cum = _mm(tril_ref[...], ld, pb=3)
    cum_c = [cum[j * c + c - 1:j * c + c, :] for j in range(tt // c)]
    p_c = [jnp.exp(x) for x in cum_c]
    p_inv = jnp.exp(-cum)
    al = -kk * jnp.exp(cum - ld)
    rt = r * jnp.exp(cum)
    bt = kka * p_inv
    kt = kp * p_inv

    def to_chunk_end(x):
        return jnp.concatenate(
            [x[j * c:(j + 1) * c, :] * p_c[j] for j in range(tt // c)], axis=0)

    bh = to_chunk_end(bt)
    kh = to_chunk_end(kt)

    def blk(x):
        return [x[j * c:(j + 1) * c, p * LANES:(p + 1) * LANES] for j, p in units]

    def halves(x):
        xs = blk(x)
        return [jnp.where(even, y, 0.0) for y in xs], [jnp.where(even, 0.0, y) for y in xs]

    def rows(top, bot):
        return [jnp.concatenate([x, y], axis=0) for x, y in zip(top, bot)]

    def unstack(x):
        return x[:c, :] + x[c:, :]

    al_e, al_o = halves(al)
    rt_e, rt_o = halves(rt)
    bt_b, kt_b = blk(bt), blk(kt)
    bh_e, bh_o = halves(bh)
    kh_e, kh_o = halves(kh)
    v_e, v_o = halves(v)
    al_n = rows(al_e, al_o)
    bh_n = rows(bh_e, bh_o)
    v_s = rows(v_o, v_e)
    kh_s = rows(kh_o, kh_e)
    rt_b = blk(rt)
    nu = range(len(units))

    g0 = [_mm(x, y, NT, pa=PG[0], pb=PG[1])
          for x, y in zip(rows(al_e, rt_e), rows(bt_b, kt_b))]
    g1 = [_mm(x, y, NT, pa=PG[0], pb=PG[1])
          for x, y in zip(rows(rt_o, al_o), rows(kt_b, bt_b))]
    top = ri < c
    left = ci < c
    tri_s = (ri & (c - 1)) > (ci & (c - 1))
    tri_i = (ri & (c - 1)) >= (ci & (c - 1))
    diag_q = top == left

    def pick(x0, x1, in_q0, tri):
        return [jnp.where(tri, jnp.where(in_q0, x, y), 0.0) for x, y in zip(x0, x1)]

    a_ab = pick(g0, g1, top, diag_q & tri_s)
    a_rk = pick(g1, g0, top, diag_q & tri_i)
    a_ak = pick(g0, g1, top, (~diag_q) & tri_s)
    a_rb = pick(g1, g0, top, (~diag_q) & tri_i)
    x1 = [_mm(a_ak[n], v_s[n], pa=PS[0], pb=PS[1]) for n in nu]
    akv = [_mm(a_rk[n], v_s[n], pa=PS[0], pb=PS[1]) for n in nu]
    khv = [_mm(kh_s[n], v_s[n], TN, pa=PS[0], pb=PS[1]) for n in nu]
    t_inv = _unit_lower_inverse(a_ab, ri, ci)
    tw = [_mm(t_inv[n], jnp.concatenate([al_n[n], x1[n]], axis=1), pa=PS[0], pb=PS[1])
          for n in nu]
    qo = [_mm(a_rb[n], tw[n], pa=PS[0], pb=PS[1]) for n in nu]
    mn = [_mm(bh_n[n], tw[n], TN, pa=PS[0], pb=PS[1]) for n in nu]
    q = [rt_b[n] + unstack(qo[n][:, :LANES]) for n in nu]
    o_loc = [unstack(qo[n][:, LANES:] + akv[n]) for n in nu]
    m_full = [mn[n][:, :LANES]
              + jnp.where(ri == ci, p_c[j][:, p * LANES:(p + 1) * LANES], 0.0)
              for n, (j, p) in enumerate(units)]
    n_loc = [mn[n][:, LANES:] + khv[n] for n in nu]

    h = [h_ref[p] for p in range(n_pairs)]
    o_rows = []
    for j in range(tt // c):
        o_parts = []
        for p in range(n_pairs):
            n = j * n_pairs + p
            o_parts.append(_mm(q[n], h[p], pa=PH[0], pb=PH[1]) + o_loc[n])
            h[p] = _mm(m_full[n], h[p], pa=PH[0], pb=PH[1]) + n_loc[n]
        o_rows.append(jnp.concatenate(o_parts, axis=1))
    for p in range(n_pairs):
        h_ref[p] = h[p]

    o = jnp.concatenate(o_rows, axis=0)
    mean = head_sum(o) * (1.0 / HEAD_DIM)
    d = o - mean
    var = head_sum(d * d) * (1.0 / HEAD_DIM)
    on = d * lax.rsqrt(var + GN_EPS) * ln_w + ln_b
    o_ref[...] = ((on + bonus) * g).astype(o_ref.dtype)


RWKV_TILE = 256


def _rwkv7(p_rw, mu, w0, w_up, a0, a_up, g_up, k_k, k_a, r_k, ln_w, ln_b):
    bsz, t, npj = p_rw.shape
    w = RWKV_WIDTH
    tt = RWKV_TILE
    pv = jnp.stack([w0, a0, k_k, k_a, r_k.reshape(w), ln_w, ln_b, jnp.zeros((w,), F32)])
    wlo = jnp.zeros((LANES, 2 * w), F32)
    wlo = wlo.at[:DECAY_LORA, :w].set(w_up).at[DECAY_LORA:, w:].set(a_up)
    row = jnp.arange(tt)
    tril = ((row[:, None] >= row[None, :])
            & (row[:, None] // CHUNK == row[None, :] // CHUNK)).astype(BF16)
    const = lambda shape: pl.BlockSpec(shape, lambda b, i: (0, 0))
    return pl.pallas_call(
        _rwkv_kernel,
        grid=(bsz, t // tt),
        in_specs=[
            pl.BlockSpec((None, tt, npj), lambda b, i: (b, i, 0)),
            const((1, npj)), const((SUBLANES, w)), const((3 * LANES, 2 * w)),
            const((3 * GATE_LORA, w)), const((tt, tt)),
        ],
        out_specs=pl.BlockSpec((None, tt, w), lambda b, i: (b, i, 0)),
        out_shape=jax.ShapeDtypeStruct((bsz, t, w), BF16),
        scratch_shapes=[pltpu.VMEM((SUBLANES, npj), F32),
                        pltpu.VMEM((w // LANES, LANES, LANES), F32)],
        compiler_params=_cparams(("arbitrary", "arbitrary")),
        name="rwkv7",
    )(p_rw, mu.reshape(1, npj), pv, _presplit(wlo), _presplit(g_up), tril)


def _mix_kernel(x_ref, yl_ref, yr_ref, wo1_ref, wo2_ref, g1_ref, sh_ref, sc_ref, ng_ref,
                wr_ref, br_ref, x1_ref, h2_ref, rid_ref, rw_ref, cnt_ref):
    tm = x_ref.shape[0]
    rows = min(tm, MIX_BLOCK_ROWS)
    nblk = tm // rows
    blocks = [slice(i * rows, (i + 1) * rows) for i in range(nblk)]
    y = [jnp.dot(yl_ref[sl, :], wo1_ref[...], preferred_element_type=F32)
         + jnp.dot(yr_ref[sl, :], wo2_ref[...], preferred_element_type=F32) for sl in blocks]
    x1 = [x_ref[sl, :] + g1_ref[...] * yb for sl, yb in zip(blocks, y)]
    h2 = [_adaln(xb, ng_ref[...], sh_ref[...], sc_ref[...]).astype(BF16) for xb in x1]
    lg_all = [jnp.dot(jnp.concatenate([hb, hb], axis=1), wr_ref[...],
                      preferred_element_type=F32) + br_ref[...] for hb in h2]
    for sl, xb, hb in zip(blocks, x1, h2):
        x1_ref[sl, :] = xb
        h2_ref[sl, :] = _pack_bf16_halves(hb)

    neg = -jnp.inf
    lane = lax.broadcasted_iota(I32, (rows, LANES), 1)
    lane_f = lane.astype(F32)

    def first_argmax(xs):
        m = [jnp.max(x, axis=-1, keepdims=True) for x in xs]
        idx = [jnp.min(jnp.where(x == mi, lane_f, float(LANES)), axis=-1, keepdims=True)
               for x, mi in zip(xs, m)]
        return m, [i.astype(I32) for i in idx]

    lg = [jnp.where(lane < N_GROUPS, x, neg) for x in lg_all]
    gm, g_idx = first_argmax(lg)
    g_w = [1.0 / jnp.sum(jnp.exp(x - m), axis=-1, keepdims=True)
           for x, m in zip(lg, gm)]
    lo = [N_GROUPS + EXPERTS_PER_GROUP * g for g in g_idx]
    le = [jnp.where((lane >= l) & (lane < l + EXPERTS_PER_GROUP), x, neg)
          for x, l in zip(lg_all, lo)]
    v1, i1 = first_argmax(le)
    v2, i2 = first_argmax([jnp.where(lane == i, neg, x) for x, i in zip(le, i1)])

    @pl.when((pl.program_id(0) == 0) & (pl.program_id(1) == 0))
    def _():
        cnt_ref[...] = jnp.zeros_like(cnt_ref)

    e_lane = lane + N_GROUPS
    hits = sum(jnp.sum(jnp.where((e_lane == a) | (e_lane == b), 1.0, 0.0), axis=0, keepdims=True)
               for a, b in zip(i1, i2))
    cnt_ref[...] = cnt_ref[...] + hits.astype(I32)

    for blk in range(nblk):
        e2 = jnp.exp(v2[blk] - v1[blk])
        w1 = g_w[blk] / (1.0 + e2)
        w2 = g_w[blk] * e2 / (1.0 + e2)
        sl = blocks[blk]
        rid_ref[sl, :] = jnp.where(lane == 0, i1[blk] - N_GROUPS,
                                   jnp.where(lane == 1, i2[blk] - N_GROUPS, 0))
        rw_ref[sl, :] = jnp.where(lane == 0, w1, jnp.where(lane == 1, w2, 0.0))


def _mix(x, y_lru, y_rw, w_out, g1, sh2, sc2, ng, w_grp, b_grp, w_exp, b_exp, tm=256):
    bsz, t, d = x.shape
    wo = w_out.astype(BF16)
    wl = LRU_WIDTH
    wr = jnp.zeros((d, LANES), F32).at[:, :N_GROUPS].set(w_grp)
    wr = wr.at[:, N_GROUPS:N_GROUPS + N_EXPERTS].set(w_exp)
    wr = jnp.concatenate(_split(wr, 2), axis=0)
    br = jnp.zeros((1, LANES), F32).at[0, :N_GROUPS].set(b_grp)
    br = br.at[0, N_GROUPS:N_GROUPS + N_EXPERTS].set(b_exp)
    vec = pl.BlockSpec((None, 1, d), lambda b, i: (b, 0, 0))
    const = lambda shape: pl.BlockSpec(shape, lambda b, i: (0, 0))
    row = lambda n: pl.BlockSpec((None, tm, n), lambda b, i: (b, i, 0))
    return pl.pallas_call(
        _mix_kernel,
        grid=(bsz, t // tm),
        in_specs=[row(d), row(wl), row(d - wl), const((wl, d)), const((d - wl, d)),
                  vec, vec, vec, const((1, d)), const((2 * d, LANES)), const((1, LANES))],
        out_specs=[row(d), row(d // 2), row(LANES), row(LANES), const((SUBLANES, LANES))],
        out_shape=[jax.ShapeDtypeStruct((bsz, t, d), F32),
                   jax.ShapeDtypeStruct((bsz, t, d // 2), I32),
                   jax.ShapeDtypeStruct((bsz, t, LANES), I32),
                   jax.ShapeDtypeStruct((bsz, t, LANES), F32),
                   jax.ShapeDtypeStruct((SUBLANES, LANES), I32)],
        compiler_params=_cparams(("arbitrary", "arbitrary")),
        name="outproj_adaln2_router",
    )(x, y_lru, y_rw, wo[:wl], wo[wl:], g1, sh2, sc2, ng.reshape(1, d), wr, br)


def _plan_kernel(rid_ref, cnt_ref, tril_ref, pos_ref, te_ref, base_ref):
    i = pl.program_id(0)
    tp = rid_ref.shape[0]
    lane = lax.broadcasted_iota(I32, (tp, LANES), 1)
    rid = rid_ref[...]
    oh0 = lane == rid[:, 0:1]
    oh1 = lane == rid[:, 1:2]
    oh = jnp.where(oh0 | oh1, 1.0, 0.0)

    @pl.when(i == 0)
    def _():
        cnt = cnt_ref[...]
        shift = EXPERT_TILE.bit_length() - 1
        padded = ((cnt + (EXPERT_TILE - 1)) >> shift) << shift
        l8 = lax.broadcasted_iota(I32, (SUBLANES, LANES), 1)
        end = padded
        s = 1
        while s < LANES:
            end = end + jnp.where(l8 >= s, pltpu.roll(end, s, 1), 0)
            s *= 2
        base_ref[...] = end - padded
        nt = te_ref.shape[0]
        j = lax.broadcasted_iota(I32, (nt, LANES), 0) * EXPERT_TILE
        lt = lax.broadcasted_iota(I32, (nt, LANES), 1)
        done = jnp.where((lt < N_EXPERTS) & (end[0:1, :] <= j), 1.0, 0.0)
        e_of = jnp.minimum(jnp.sum(done, axis=-1, keepdims=True), float(N_EXPERTS - 1)).astype(I32)
        total = end[0:1, N_EXPERTS - 1:N_EXPERTS]
        mine = lt == e_of
        cnt_e = jnp.sum(jnp.where(mine, cnt[0:1, :], 0), axis=-1, keepdims=True)
        start_e = jnp.sum(jnp.where(mine, end[0:1, :] - padded[0:1, :], 0), axis=-1, keepdims=True)
        valid = jnp.clip(cnt_e - (j[:, 0:1] - start_e), 0, EXPERT_TILE)
        te_ref[...] = jnp.where(lt == 0, e_of, jnp.where(lt == 1, total >> shift,
                                                        jnp.where(lt == 2, valid, 0)))

    prefix = jnp.dot(tril_ref[...], oh.astype(BF16), preferred_element_type=F32)
    dest = base_ref[0:1, :] + prefix.astype(I32)
    pos0 = jnp.sum(jnp.where(oh0, dest, 0), axis=-1, keepdims=True)
    pos1 = jnp.sum(jnp.where(oh1, dest, 0), axis=-1, keepdims=True)
    both = jnp.where(lane == 0, pos0, jnp.where(lane == 1, pos1, 0))
    pos_ref[...] = both.T[:SUBLANES, :]
    base_ref[...] = base_ref[...] + jnp.sum(oh, axis=0, keepdims=True).astype(I32)


def _plan(rid, cnt, n_tiles, tp=1024):
    n = rid.shape[0]
    tp = min(tp, n)
    nt_pad = -(-n_tiles // SUBLANES) * SUBLANES
    tril = (jnp.arange(tp)[:, None] > jnp.arange(tp)[None, :]).astype(BF16)
    pos, te = pl.pallas_call(
        _plan_kernel,
        grid=(n // tp,),
        in_specs=[pl.BlockSpec((tp, LANES), lambda i: (i, 0)),
                  pl.BlockSpec((SUBLANES, LANES), lambda i: (0, 0)),
                  pl.BlockSpec((tp, tp), lambda i: (0, 0))],
        out_specs=[pl.BlockSpec((None, SUBLANES, tp), lambda i: (i, 0, 0)),
                   pl.BlockSpec((nt_pad, LANES), lambda i: (0, 0))],
        out_shape=[jax.ShapeDtypeStruct((n // tp, SUBLANES, tp), I32),
                   jax.ShapeDtypeStruct((nt_pad, LANES), I32)],
        scratch_shapes=[pltpu.VMEM((SUBLANES, LANES), I32)],
        compiler_params=_cparams(("arbitrary",)),
        name="route_plan",
    )(rid, cnt, tril)
    return pos, te


SC_CORES = 2
SC_SUBCORES = 16
SC_WORKERS = SC_CORES * SC_SUBCORES
SC_WINDOW = 64


def _sc_mesh():
    return plsc.VectorSubcoreMesh(core_axis_name="c", subcore_axis_name="s",
                                  num_cores=SC_CORES, num_subcores=SC_SUBCORES)


def _sc_two_buffer_loop(n_win, fill, drain):
    assert n_win % 2 == 0

    def start(copies):
        for cp in copies:
            cp.start()

    def wait(copies):
        for cp in copies:
            cp.wait()

    start(fill(0, 0))

    @pl.loop(0, n_win, step=2)
    def _(j):
        for b in range(2):
            jj = j + b

            @pl.when(jj >= 1)
            def _():
                wait(drain(jj - 1, 1 - b))

            @pl.when(jj + 1 < n_win)
            def _():
                start(fill(jj + 1, 1 - b))

            wait(fill(jj, b))
            start(drain(jj, b))

    wait(drain(n_win - 1, 1))


def _sc_scatter_rows(rows, pos0, pos1, n_out):
    n, d = rows.shape
    per_w = n // SC_WORKERS
    n_win = per_w // SC_WINDOW
    shape3 = (SC_WORKERS, n_win, SC_WINDOW)

    @functools.partial(
        pl.kernel, mesh=_sc_mesh(), out_type=jax.ShapeDtypeStruct((n_out, d), rows.dtype),
        scratch_types=[pltpu.VMEM((n_win, SC_WINDOW), I32), pltpu.VMEM((n_win, SC_WINDOW), I32),
                       pltpu.VMEM((2, SC_WINDOW, d), rows.dtype),
                       pltpu.SemaphoreType.DMA((2,)), pltpu.SemaphoreType.DMA((2,))],
        name="sc_scatter_rows")
    def scatter(rows_hbm, p0_hbm, p1_hbm, out_hbm, p0_v, p1_v, buf, in_sem, out_sem):
        wid = lax.axis_index("s") * SC_CORES + lax.axis_index("c")
        base = wid * per_w
        pltpu.sync_copy(p0_hbm.at[wid], p0_v)
        pltpu.sync_copy(p1_hbm.at[wid], p1_v)

        def fill(j, b):
            src = rows_hbm.at[pl.ds(base + j * SC_WINDOW, SC_WINDOW)]
            return [pltpu.make_async_copy(src, buf.at[b], in_sem.at[b])]

        def drain(j, b):
            return [pltpu.make_async_copy(buf.at[b], out_hbm.at[p.at[j]], out_sem.at[b])
                    for p in (p0_v, p1_v)]

        _sc_two_buffer_loop(n_win, fill, drain)

    return scatter(rows, pos0.reshape(shape3), pos1.reshape(shape3))


def _sc_gather_rows(table, idx):
    m = idx.shape[0]
    d = table.shape[1]
    per_w = m // SC_WORKERS
    n_win = per_w // SC_WINDOW

    @functools.partial(
        pl.kernel, mesh=_sc_mesh(), out_type=jax.ShapeDtypeStruct((m, d), table.dtype),
        scratch_types=[pltpu.VMEM((n_win, SC_WINDOW), I32),
                       pltpu.VMEM((2, SC_WINDOW, d), table.dtype),
                       pltpu.SemaphoreType.DMA((2,)), pltpu.SemaphoreType.DMA((2,))],
        name="sc_gather_rows")
    def gather(table_hbm, idx_hbm, out_hbm, idx_v, buf, in_sem, out_sem):
        wid = lax.axis_index("s") * SC_CORES + lax.axis_index("c")
        base = wid * per_w
        pltpu.sync_copy(idx_hbm.at[wid], idx_v)

        def fill(j, b):
            return [pltpu.make_async_copy(table_hbm.at[idx_v.at[j]], buf.at[b], in_sem.at[b])]

        def drain(j, b):
            dst = out_hbm.at[pl.ds(base + j * SC_WINDOW, SC_WINDOW)]
            return [pltpu.make_async_copy(buf.at[b], dst, out_sem.at[b])]

        _sc_two_buffer_loop(n_win, fill, drain)

    return gather(table, idx.reshape(SC_WORKERS, n_win, SC_WINDOW))


def _expert_kernel(te_ref, nu_ref, valid_ref, x_ref, w1_ref, w3_ref, w2_ref, o_ref,
                   w1b_ref, w3b_ref, w2b_ref):
    j = pl.program_id(0)
    active = j < nu_ref[0]

    @pl.when(active & ((j == 0) | (te_ref[j] != te_ref[jnp.maximum(j - 1, 0)])))
    def _():
        w1b_ref[...] = w1_ref[...].astype(BF16)
        w3b_ref[...] = w3_ref[...].astype(BF16)
        w2b_ref[...] = w2_ref[...].astype(BF16)

    @pl.when(active)
    def _():
        nblk = 2
        rows = x_ref.shape[0] // nblk
        blocks = [slice(i * rows, (i + 1) * rows) for i in range(nblk)]
        row = lax.broadcasted_iota(I32, (rows, x_ref.shape[1]), 0)
        xs = [_unpack_bf16_halves(
            jnp.where(row + i * rows < valid_ref[j], x_ref[sl, :], 0)).astype(BF16)
            for i, sl in enumerate(blocks)]
        h1 = [jnp.dot(x, w1b_ref[...], preferred_element_type=F32) for x in xs]
        h3 = [jnp.dot(x, w3b_ref[...], preferred_element_type=F32) for x in xs]
        hid = [(a * jax.nn.sigmoid(a) * b).astype(BF16) for a, b in zip(h1, h3)]
        for sl, hb in zip(blocks, hid):
            o_ref[sl, :] = _pack_bf16_halves(
                jnp.dot(hb, w2b_ref[...], preferred_element_type=F32))

    @pl.when(j >= nu_ref[0])
    def _():
        o_ref[...] = jnp.zeros_like(o_ref)


def _experts(xs, te, nu, valid, w1, w3, w2):
    n_rows = xs.shape[0]
    d, de = w1.shape[-2:]
    nt = n_rows // EXPERT_TILE
    used = lambda j, nu: jnp.minimum(j, nu[0] - 1)
    grid_spec = pltpu.PrefetchScalarGridSpec(
        num_scalar_prefetch=3,
        grid=(nt,),
        in_specs=[
            pl.BlockSpec((EXPERT_TILE, d // 2), lambda j, te, nu, va: (used(j, nu), 0)),
            pl.BlockSpec((None, d, de), lambda j, te, nu, va: (te[used(j, nu)], 0, 0)),
            pl.BlockSpec((None, d, de), lambda j, te, nu, va: (te[used(j, nu)], 0, 0)),
            pl.BlockSpec((None, de, d), lambda j, te, nu, va: (te[used(j, nu)], 0, 0)),
        ],
        out_specs=pl.BlockSpec((EXPERT_TILE, d // 2), lambda j, te, nu, va: (j, 0)),
        scratch_shapes=[pltpu.VMEM((d, de), BF16), pltpu.VMEM((d, de), BF16),
                        pltpu.VMEM((de, d), BF16)],
    )
    return pl.pallas_call(
        _expert_kernel,
        grid_spec=grid_spec,
        out_shape=jax.ShapeDtypeStruct((n_rows, d // 2), I32),
        compiler_params=_cparams(("arbitrary",)),
        name="expert_mlp",
    )(te, nu, valid, xs, w1, w3, w2)


def _combine_kernel(x1_ref, rw_ref, g2_ref, fg_ref, y0_ref, y1_ref, o_ref):
    rw = rw_ref[...]
    moe = (rw[:, 0:1] * _unpack_bf16_halves(y0_ref[...])
           + rw[:, 1:2] * _unpack_bf16_halves(y1_ref[...]))
    x2 = x1_ref[...] + g2_ref[...] * moe
    ms = jnp.mean(x2 * x2, axis=-1, keepdims=True)
    o_ref[...] = x2 * lax.rsqrt(ms + RMS_EPS) * fg_ref[...]


def _combine(x1, rw, yg, g2, final_g, tc):
    bsz, t, d = x1.shape
    row = lambda n: pl.BlockSpec((None, tc, n), lambda b, i: (b, i, 0))
    slot = lambda s: pl.BlockSpec((None, None, tc, d // 2), lambda b, i: (s, b, i, 0))
    return pl.pallas_call(
        _combine_kernel,
        grid=(bsz, t // tc),
        in_specs=[row(d), row(LANES),
                  pl.BlockSpec((None, 1, d), lambda b, i: (b, 0, 0)),
                  pl.BlockSpec((1, d), lambda b, i: (0, 0)),
                  slot(0), slot(1)],
        out_specs=row(d),
        out_shape=jax.ShapeDtypeStruct((bsz, t, d), F32),
        compiler_params=_cparams(("arbitrary", "arbitrary")),
        name="combine_final_norm",
    )(x1, rw, g2, final_g.reshape(1, d), yg, yg)


def _row_tile(t, want):
    return want if t % want == 0 else t


def kernel(x, c, w_ada, b_ada, norm1_g, w_in, conv_w, conv_b, lru_wa, lru_ba, lru_wi, lru_bi, lru_lam, lru_norm_g, tok_mu, w0, w_up, a0, a_up, g_up, k_k, k_a, r_k, ln_x_w, ln_x_b, w_out, norm2_g, w_grp, b_grp, w_exp, b_exp, w1, w3, w2, final_g):
    bsz, t, d = x.shape
    n = bsz * t
    depth = w_ada.shape[0]
    assert depth == 1, "the combine kernel applies the final norm: only DEPTH == 1 is wired"
    tile = _row_tile(t, ROW_TILE)
    n_tiles = (n * TOP_K) // EXPERT_TILE + N_EXPERTS
    for l in range(depth):
        mod = _modulation(c, w_ada[l], b_ada[l]).reshape(bsz, 6, 1, d)
        sh1, sc1, g1, sh2, sc2, g2 = (mod[:, i] for i in range(6))
        y_lru, p_rw = _inproj_lru(x, sh1, sc1, norm1_g[l], w_in[l], conv_w[l], conv_b[l],
                                  lru_wa[l], lru_ba[l], lru_wi[l], lru_bi[l], lru_lam[l],
                                  lru_norm_g[l], tm=tile)
        y_rw = _rwkv7(p_rw, tok_mu[l], w0[l], w_up[l], a0[l], a_up[l], g_up[l], k_k[l], k_a[l],
                      r_k[l], ln_x_w[l], ln_x_b[l])
        x1, h2, rid, rw, cnt = _mix(x, y_lru, y_rw, w_out[l], g1, sh2, sc2, norm2_g[l],
                                    w_grp[l], b_grp[l], w_exp[l], b_exp[l],
                                    tm=_row_tile(t, MIX_TILE))
        pos, te = _plan(rid.reshape(n, LANES), cnt, n_tiles)
        pos0, pos1 = pos[:, 0, :].reshape(n), pos[:, 1, :].reshape(n)
        xs = _sc_scatter_rows(h2.reshape(n, d // 2), pos0, pos1, n_tiles * EXPERT_TILE)
        ys = _experts(xs, te[:n_tiles, 0], te[0:1, 1], te[:n_tiles, 2], w1[l], w3[l], w2[l])
        yg = _sc_gather_rows(ys, jnp.concatenate([pos0, pos1]))
        x = _combine(x1, rw, yg.reshape(TOP_K, bsz, t, d // 2), g2, final_g,
                     _row_tile(t, COMBINE_TILE))
    return x
```

```python
import functools

import jax
import jax.numpy as jnp
from jax import lax
from jax.experimental import pallas as pl
from jax.experimental.pallas import tpu as pltpu
from jax.experimental.pallas import tpu_sc as plsc

F32 = jnp.float32
BF16 = jnp.bfloat16
I32 = jnp.int32

LRU_WIDTH = 512
LRU_HEAD_DIM = 64
CONV_WIDTH = 4
LRU_C = 8.0
RWKV_WIDTH = 512
HEAD_DIM = 64
DECAY_LORA = 64
AAA_LORA = 64
GATE_LORA = 128
RWKV_PROJ = 3 * RWKV_WIDTH + DECAY_LORA + AAA_LORA + GATE_LORA
N_GROUPS = 4
EXPERTS_PER_GROUP = 8
N_EXPERTS = N_GROUPS * EXPERTS_PER_GROUP
TOP_K = 2
RMS_EPS = 1e-6
GN_EPS = 64e-5

LANES = 128
SUBLANES = 8
CHUNK = 64
EXPERT_TILE = 512
ROW_TILE = 256
MIX_TILE = 1024
MIX_BLOCK_ROWS = 128
COMBINE_TILE = 1024
VMEM_LIMIT = 48 * 1024 * 1024

NN = (((1,), (0,)), ((), ()))
NT = (((1,), (1,)), ((), ()))
TN = (((0,), (0,)), ((), ()))


def _split(x, n):
    if x.dtype == BF16:
        return [x]
    parts = []
    rem = x
    for i in range(n):
        p = rem.astype(BF16)
        parts.append(p)
        if i + 1 < n:
            rem = rem - p.astype(F32)
    return parts


def _mm(a, b, dn=NN, pa=1, pb=1):
    aps = _split(a, pa)
    bps = _split(b, pb)
    order = max(len(aps), len(bps))
    terms = [(i, j) for i in range(len(aps)) for j in range(len(bps)) if i + j < order]
    ka = dn[0][0][0]
    kb = dn[0][1][0]
    if len(terms) > 1 and a.shape[ka] % LANES == 0:
        a_cat = jnp.concatenate([aps[i] for i, _ in terms], axis=ka)
        b_cat = jnp.concatenate([bps[j] for _, j in terms], axis=kb)
        return lax.dot_general(a_cat, b_cat, dn, preferred_element_type=F32)
    out = None
    for i, j in terms:
        t = lax.dot_general(aps[i], bps[j], dn, preferred_element_type=F32)
        out = t if out is None else out + t
    return out


def _presplit(b):
    hi, lo = _split(b, 2)
    return jnp.concatenate([hi, lo, hi], axis=0)


def _mm_presplit(a, b3):
    a_hi, a_lo = _split(a, 2)
    return jnp.dot(jnp.concatenate([a_hi, a_hi, a_lo], axis=1), b3, preferred_element_type=F32)


def _pack_bf16_halves(x):
    n = x.shape[1] // 2
    bits = lax.bitcast_convert_type(x.astype(BF16).astype(F32), I32)
    return bits[:, n:] | ((bits[:, :n] >> 16) & 0xFFFF)


def _unpack_bf16_halves(p):
    lo = lax.bitcast_convert_type(p << 16, F32)
    hi = lax.bitcast_convert_type(p & (-65536), F32)
    return jnp.concatenate([lo, hi], axis=1)


def _softplus(x):
    return jnp.maximum(x, 0.0) + jnp.log1p(jnp.exp(-jnp.abs(x)))


def _cparams(sem):
    return pltpu.CompilerParams(dimension_semantics=sem, vmem_limit_bytes=VMEM_LIMIT)


def _mod_kernel(c_ref, w_ref, b_ref, o_ref):
    c = c_ref[...]
    s = c * jax.nn.sigmoid(c)
    o_ref[...] = _mm(s, w_ref[...], pa=2, pb=2) + b_ref[...]


def _modulation(c, w_ada, b_ada):
    bsz, d = c.shape
    n_out = w_ada.shape[1]
    rows = -(-bsz // SUBLANES) * SUBLANES
    c_pad = jnp.zeros((rows, d), F32).at[:bsz].set(c)
    bn = d
    out = pl.pallas_call(
        _mod_kernel,
        grid=(n_out // bn,),
        in_specs=[
            pl.BlockSpec((rows, d), lambda j: (0, 0)),
            pl.BlockSpec((d, bn), lambda j: (0, j)),
            pl.BlockSpec((1, bn), lambda j: (0, j)),
        ],
        out_specs=pl.BlockSpec((rows, bn), lambda j: (0, j)),
        out_shape=jax.ShapeDtypeStruct((rows, n_out), F32),
        compiler_params=_cparams(("arbitrary",)),
        name="adaln_mod",
    )(c_pad, w_ada, b_ada.reshape(1, n_out))
    return out[:bsz]


def _adaln(x, g, shift, scale):
    ms = jnp.mean(x * x, axis=-1, keepdims=True)
    return (x * lax.rsqrt(ms + RMS_EPS)) * (g * (1.0 + scale)) + shift


def _inproj_lru_kernel(x_ref, sh_ref, sc_ref, g_ref, wl_ref, wr_ref,
                       cw_ref, cb_ref, wab_ref, bab_ref, lam_ref, ng_ref, ones_ref,
                       yl_ref, pr_ref, xprev_ref, hprev_ref):
    @pl.when(pl.program_id(1) == 0)
    def _():
        xprev_ref[...] = jnp.zeros_like(xprev_ref)
        hprev_ref[...] = jnp.zeros_like(hprev_ref)

    w = LRU_WIDTH
    tm = x_ref.shape[0]
    nblk = 2
    rows = tm // nblk
    blocks = [slice(i * rows, (i + 1) * rows) for i in range(nblk)]
    h = [_adaln(x_ref[sl, :], g_ref[...], sh_ref[...], sc_ref[...]).astype(BF16)
         for sl in blocks]
    ux = [jnp.dot(hb, wl_ref[:, :w], preferred_element_type=F32) for hb in h]

    def remaining_columns():
        for sl, hb in zip(blocks, h):
            pr_ref[sl, :] = jnp.dot(hb, wr_ref[...], preferred_element_type=F32)
        return [jnp.dot(hb, wl_ref[:, w:], preferred_element_type=F32) for hb in h]

    y = _lru_tile(ux, remaining_columns, cw_ref, cb_ref, wab_ref, bab_ref, lam_ref, ng_ref,
                  ones_ref, xprev_ref, hprev_ref)
    for sl, yb in zip(blocks, y):
        yl_ref[sl, :] = yb.astype(yl_ref.dtype)


def _inproj_lru(x, sh1, sc1, g, w_in, conv_w, conv_b, wa, ba, wi, bi, lam, norm_g, tm=256):
    bsz, t, d = x.shape
    w = LRU_WIDTH
    nl = 2 * w
    nr = RWKV_PROJ
    wl = w_in[:, :nl].astype(BF16)
    wr = w_in[:, nl:].astype(BF16)
    wab = jnp.concatenate([_block_diag(wa), _block_diag(wi)], axis=1).astype(BF16)
    bab = jnp.concatenate([ba.reshape(1, w), bi.reshape(1, w)], axis=1)
    vec = pl.BlockSpec((None, 1, d), lambda b, i: (b, 0, 0))
    const = lambda shape: pl.BlockSpec(shape, lambda b, i: (0, 0))
    return pl.pallas_call(
        _inproj_lru_kernel,
        grid=(bsz, t // tm),
        in_specs=[
            pl.BlockSpec((None, tm, d), lambda b, i: (b, i, 0)),
            vec, vec, const((1, d)), const((d, nl)), const((d, nr)),
            const((CONV_WIDTH, w)), const((1, w)), const((w, 2 * w)), const((1, 2 * w)),
            const((1, w)), const((1, w)), const((w, w)),
        ],
        out_specs=[
            pl.BlockSpec((None, tm, w), lambda b, i: (b, i, 0)),
            pl.BlockSpec((None, tm, nr), lambda b, i: (b, i, 0)),
        ],
        out_shape=[
            jax.ShapeDtypeStruct((bsz, t, w), BF16),
            jax.ShapeDtypeStruct((bsz, t, nr), F32),
        ],
        scratch_shapes=[pltpu.VMEM((SUBLANES, w), F32), pltpu.VMEM((SUBLANES, w), F32)],
        compiler_params=_cparams(("arbitrary", "arbitrary")),
        name="adaln1_inproj_rglru",
    )(x, sh1, sc1, g.reshape(1, d), wl, wr, conv_w, conv_b.reshape(1, w), wab, bab,
      lam.reshape(1, w), norm_g.reshape(1, w), _head_ones(w, LRU_HEAD_DIM))


def _gelu_tanh(x):
    c = 0.7978845608028654
    half = 0.5 * x
    return half * jnp.tanh(x * (c + (c * 0.044715) * (x * x))) + half


def _sigmoid(x):
    return 0.5 * jnp.tanh(0.5 * x) + 0.5


def _lru_tile(ux, gate_branch, cw_ref, cb_ref, wab_ref, bab_ref, lam_ref, ng_ref, ones_ref,
              xprev_ref, hprev_ref):
    w = LRU_WIDTH
    rows = ux[0].shape[0]
    prev8 = [xprev_ref[...]] + [u[rows - SUBLANES:, :] for u in ux[:-1]]
    ext = [jnp.concatenate([p, u], axis=0) for p, u in zip(prev8, ux)]
    xc = [cb_ref[...] + cw_ref[CONV_WIDTH - 1:CONV_WIDTH, :] * u for u in ux]
    for k in range(1, CONV_WIDTH):
        tap = cw_ref[CONV_WIDTH - 1 - k:CONV_WIDTH - k, :]
        xc = [x + tap * pltpu.roll(e, k, 0)[SUBLANES:, :] for x, e in zip(xc, ext)]
    xprev_ref[...] = ux[-1][rows - SUBLANES:, :]

    gates = [jnp.dot(x.astype(BF16), wab_ref[...], preferred_element_type=F32) + bab_ref[...]
             for x in xc]
    ug = gate_branch()
    r = [_sigmoid(g[:, :w]) for g in gates]
    ig = [_sigmoid(g[:, w:]) for g in gates]
    sp = _softplus(-lam_ref[...])
    log_a = [(-LRU_C) * x * sp for x in r]
    a = [jnp.exp(x) for x in log_a]
    th = [jnp.tanh(x) for x in log_a]
    q = [(-2.0 * x) / (1.0 - x) for x in th]
    root_q = [jnp.where(x > 0.0, x * lax.rsqrt(x), 0.0) for x in q]
    b = [s * (i * x) for s, i, x in zip(root_q, ig, xc)]

    row8 = lax.broadcasted_iota(I32, (rows, w), 0) & (SUBLANES - 1)
    acc_a, acc_b = a, b
    for s in (1, 2, 4):
        live = row8 >= s
        sh_a = [pltpu.roll(x, s, 0) for x in acc_a]
        sh_b = [pltpu.roll(x, s, 0) for x in acc_b]
        acc_b = [jnp.where(live, x * sb + y, y) for x, sb, y in zip(acc_a, sh_b, acc_b)]
        acc_a = [jnp.where(live, x * sa, x) for x, sa in zip(acc_a, sh_a)]
    carry = hprev_ref[SUBLANES - 1:SUBLANES, :]
    h = []
    for xa, xb in zip(acc_a, acc_b):
        groups = []
        for gi in range(rows // SUBLANES):
            lo = gi * SUBLANES
            groups.append(xa[lo:lo + SUBLANES, :] * carry + xb[lo:lo + SUBLANES, :])
            last = lo + SUBLANES - 1
            carry = xa[last:last + 1, :] * carry + xb[last:last + 1, :]
        h.append(jnp.concatenate(groups, axis=0))
    hprev_ref[...] = h[-1][rows - SUBLANES:, :]

    y = [x * _gelu_tanh(g) for x, g in zip(h, ug)]
    ms = [_mm(x * x, ones_ref[...]) * (1.0 / LRU_HEAD_DIM) for x in y]
    return [x * lax.rsqrt(m + RMS_EPS) * ng_ref[...] for x, m in zip(y, ms)]


def _block_diag(w):
    h, n, _ = w.shape
    eye = jnp.eye(h, dtype=w.dtype)
    return (eye[:, None, :, None] * w[:, :, None, :]).reshape(h * n, h * n)


def _head_ones(width, head):
    idx = jnp.arange(width) // head
    return (idx[:, None] == idx[None, :]).astype(BF16)


PG = (1, 1)
PI = (1, 1)
PS = (1, 1)
PH = (1, 2)


def _unit_lower_inverse(a_list, ri, ci):
    mm = functools.partial(_mm, pa=PI[0], pb=PI[1])
    eye = jnp.where(ri == ci, 1.0, 0.0)
    leaf = (ri >> 3) == (ci >> 3)
    a8 = [jnp.where(leaf, a, 0.0) for a in a_list]
    a8_2 = [mm(x, x) for x in a8]
    a8_4 = [mm(x, x) for x in a8_2]
    t = [mm(eye + x, eye + y) for x, y in zip(a8, a8_2)]
    t = [mm(x, eye + y) for x, y in zip(t, a8_4)]
    zero = jnp.zeros((LANES, LANES), F32)
    for sh in (3, 4, 5):
        s = 1 << sh
        off = ((ri >> (sh + 1)) == (ci >> (sh + 1))) & ((ri >> sh) != (ci >> sh))
        t_lo = [_second_blocks(x, s) for x in t]
        b_lo = [mm(_second_blocks(jnp.where(off, a, 0.0), s), x) for a, x in zip(a_list, t)]
        d_lo = [mm(x, _interleave_blocks(zero, y, s)) for x, y in zip(t_lo, b_lo)]
        t = [_interleave_blocks(x, y + z, s) for x, y, z in zip(t, t_lo, d_lo)]
    return t


def _second_blocks(x, s):
    return jnp.concatenate(
        [x[s * (2 * m + 1):s * (2 * m + 2)] for m in range(x.shape[0] // (2 * s))], axis=0)


def _interleave_blocks(first_src, second, s):
    parts = []
    for m in range(first_src.shape[0] // (2 * s)):
        parts.append(first_src[2 * s * m:2 * s * m + s])
        parts.append(second[s * m:s * (m + 1)])
    return jnp.concatenate(parts, axis=0)


def _rwkv_kernel(p_ref, mu_ref, pv_ref, wlo_ref, gup_ref, tril_ref, o_ref, uprev_ref, h_ref):
    w = RWKV_WIDTH
    tt = p_ref.shape[0]
    c = CHUNK
    n_pairs = w // LANES
    units = [(j, p) for j in range(tt // c) for p in range(n_pairs)]

    @pl.when(pl.program_id(1) == 0)
    def _():
        uprev_ref[...] = jnp.zeros_like(uprev_ref)
        h_ref[...] = jnp.zeros_like(h_ref)

    def shifted(lo, hi):
        u = p_ref[:, lo:hi]
        ext = jnp.concatenate([uprev_ref[:, lo:hi], u], axis=0)
        prev = pltpu.roll(ext, 1, 0)[SUBLANES:, :]
        return u + (prev - u) * mu_ref[:, lo:hi]

    r = shifted(0, w)
    k = shifted(w, 2 * w)
    v = shifted(2 * w, 3 * w)
    z = shifted(3 * w, 3 * w + LANES)
    gd = shifted(3 * w + LANES, p_ref.shape[1])
    uprev_ref[...] = p_ref[tt - SUBLANES:, :]
    w0 = pv_ref[0:1, :]
    a0 = pv_ref[1:2, :]
    k_k = pv_ref[2:3, :]
    k_a = pv_ref[3:4, :]
    r_k = pv_ref[4:5, :]
    ln_w = pv_ref[5:6, :]
    ln_b = pv_ref[6:7, :]

    lane_t = lax.broadcasted_iota(I32, (tt, LANES), 1)
    zz = jnp.where(lane_t < HEAD_DIM, jnp.tanh(z), z)
    lora = _mm_presplit(zz, wlo_ref[...])
    ld = (-0.6065306597126334) * _sigmoid(w0 + lora[:, :w])
    a = _sigmoid(a0 + lora[:, w:])
    g = _mm_presplit(_sigmoid(gd), gup_ref[...])

    ri = lax.broadcasted_iota(I32, (LANES, LANES), 0)
    ci = lax.broadcasted_iota(I32, (LANES, LANES), 1)
    even = lax.broadcasted_iota(I32, (c, LANES), 1) < HEAD_DIM
    wide = 2 * LANES
    ones_bd = jnp.where(
        (lax.broadcasted_iota(I32, (wide, wide), 0) >> 6)
        == (lax.broadcasted_iota(I32, (wide, wide), 1) >> 6), 1.0, 0.0).astype(BF16)

    def head_sum(x):
        return jnp.concatenate(
            [_mm(x[:, q * wide:(q + 1) * wide], ones_bd) for q in range(w // wide)], axis=1)

    kk = k * k_k
    kk = kk * lax.rsqrt(jnp.maximum(head_sum(kk * kk), 1e-24))
    kp = k * (1.0 + (a - 1.0) * k_a)
    kka = kk * a
    bonus = head_sum(r * kp * r_k) * v
    cum = _mm(tril_ref[...], ld, pb=3)
    cum_c = [cum[j * c + c - 1:j * c + c, :] for j in range(tt // c)]
    p_c = [jnp.exp(x) for x in cum_c]
    p_inv = jnp.exp(-cum)
    al = -kk * jnp.exp(cum - ld)
    rt = r * jnp.exp(cum)
    bt = kka * p_inv
    kt = kp * p_inv

    def to_chunk_end(x):
        return jnp.concatenate(
            [x[j * c:(j + 1) * c, :] * p_c[j] for j in range(tt // c)], axis=0)

    bh = to_chunk_end(bt)
    kh = to_chunk_end(kt)

    def blk(x):
        return [x[j * c:(j + 1) * c, p * LANES:(p + 1) * LANES] for j, p in units]

    def halves(x):
        xs = blk(x)
        return [jnp.where(even, y, 0.0) for y in xs], [jnp.where(even, 0.0, y) for y in xs]

    def rows(top, bot):
        return [jnp.concatenate([x, y], axis=0) for x, y in zip(top, bot)]

    def unstack(x):
        return x[:c, :] + x[c:, :]

    al_e, al_o = halves(al)
    rt_e, rt_o = halves(rt)
    bt_b, kt_b = blk(bt), blk(kt)
    bh_e, bh_o = halves(bh)
    kh_e, kh_o = halves(kh)
    v_e, v_o = halves(v)
    al_n = rows(al_e, al_o)
    bh_n = rows(bh_e, bh_o)
    v_s = rows(v_o, v_e)
    kh_s = rows(kh_o, kh_e)
    rt_b = blk(rt)
    nu = range(len(units))

    g0 = [_mm(x, y, NT, pa=PG[0], pb=PG[1])
          for x, y in zip(rows(al_e, rt_e), rows(bt_b, kt_b))]
    g1 = [_mm(x, y, NT, pa=PG[0], pb=PG[1])
          for x, y in zip(rows(rt_o, al_o), rows(kt_b, bt_b))]
    top = ri < c
    left = ci < c
    tri_s = (ri & (c - 1)) > (ci & (c - 1))
    tri_i = (ri & (c - 1)) >= (ci & (c - 1))
    diag_q = top == left

    def pick(x0, x1, in_q0, tri):
        return [jnp.where(tri, jnp.where(in_q0, x, y), 0.0) for x, y in zip(x0, x1)]

    a_ab = pick(g0, g1, top, diag_q & tri_s)
    a_rk = pick(g1, g0, top, diag_q & tri_i)
    a_ak = pick(g0, g1, top, (~diag_q) & tri_s)
    a_rb = pick(g1, g0, top, (~diag_q) & tri_i)
    x1 = [_mm(a_ak[n], v_s[n], pa=PS[0], pb=PS[1]) for n in nu]
    akv = [_mm(a_rk[n], v_s[n], pa=PS[0], pb=PS[1]) for n in nu]
    khv = [_mm(kh_s[n], v_s[n], TN, pa=PS[0], pb=PS[1]) for n in nu]
    t_inv = _unit_lower_inverse(a_ab, ri, ci)
    tw = [_mm(t_inv[n], jnp.concatenate([al_n[n], x1[n]], axis=1), pa=PS[0], pb=PS[1])
          for n in nu]
    qo = [_mm(a_rb[n], tw[n], pa=PS[0], pb=PS[1]) for n in nu]
    mn = [_mm(bh_n[n], tw[n], TN, pa=PS[0], pb=PS[1]) for n in nu]
    q = [rt_b[n] + unstack(qo[n][:, :LANES]) for n in nu]
    o_loc = [unstack(qo[n][:, LANES:] + akv[n]) for n in nu]
    m_full = [mn[n][:, :LANES]
              + jnp.where(ri == ci, p_c[j][:, p * LANES:(p + 1) * LANES], 0.0)
              for n, (j, p) in enumerate(units)]
    n_loc = [mn[n][:, LANES:] + khv[n] for n in nu]

    h = [h_ref[p] for p in range(n_pairs)]
    o_rows = []
    for j in range(tt // c):
        o_parts = []
        for p in range(n_pairs):
            n = j * n_pairs + p
            o_parts.append(_mm(q[n], h[p], pa=PH[0], pb=PH[1]) + o_loc[n])
            h[p] = _mm(m_full[n], h[p], pa=PH[0], pb=PH[1]) + n_loc[n]
        o_rows.append(jnp.concatenate(o_parts, axis=1))
    for p in range(n_pairs):
        h_ref[p] = h[p]

    o = jnp.concatenate(o_rows, axis=0)
    mean = head_sum(o) * (1.0 / HEAD_DIM)
    d = o - mean
    var = head_sum(d * d) * (1.0 / HEAD_DIM)
    on = d * lax.rsqrt(var + GN_EPS) * ln_w + ln_b
    o_ref[...] = ((on + bonus) * g).astype(o_ref.dtype)


RWKV_TILE = 256


def _rwkv7(p_rw, mu, w0, w_up, a0, a_up, g_up, k_k, k_a, r_k, ln_w, ln_b):
    bsz, t, npj = p_rw.shape
    w = RWKV_WIDTH
    tt = RWKV_TILE
    pv = jnp.stack([w0, a0, k_k, k_a, r_k.reshape(w), ln_w, ln_b, jnp.zeros((w,), F32)])
    wlo = jnp.zeros((LANES, 2 * w), F32)
    wlo = wlo.at[:DECAY_LORA, :w].set(w_up).at[DECAY_LORA:, w:].set(a_up)
    row = jnp.arange(tt)
    tril = ((row[:, None] >= row[None, :])
            & (row[:, None] // CHUNK == row[None, :] // CHUNK)).astype(BF16)
    const = lambda shape: pl.BlockSpec(shape, lambda b, i: (0, 0))
    return pl.pallas_call(
        _rwkv_kernel,
        grid=(bsz, t // tt),
        in_specs=[
            pl.BlockSpec((None, tt, npj), lambda b, i: (b, i, 0)),
            const((1, npj)), const((SUBLANES, w)), const((3 * LANES, 2 * w)),
            const((3 * GATE_LORA, w)), const((tt, tt)),
        ],
        out_specs=pl.BlockSpec((None, tt, w), lambda b, i: (b, i, 0)),
        out_shape=jax.ShapeDtypeStruct((bsz, t, w), BF16),
        scratch_shapes=[pltpu.VMEM((SUBLANES, npj), F32),
                        pltpu.VMEM((w // LANES, LANES, LANES), F32)],
        compiler_params=_cparams(("arbitrary", "arbitrary")),
        name="rwkv7",
    )(p_rw, mu.reshape(1, npj), pv, _presplit(wlo), _presplit(g_up), tril)


def _mix_kernel(x_ref, yl_ref, yr_ref, wo1_ref, wo2_ref, g1_ref, sh_ref, sc_ref, ng_ref,
                wr_ref, br_ref, x1_ref, h2_ref, rid_ref, rw_ref, cnt_ref):
    tm = x_ref.shape[0]
    rows = min(tm, MIX_BLOCK_ROWS)
    nblk = tm // rows
    blocks = [slice(i * rows, (i + 1) * rows) for i in range(nblk)]
    y = [jnp.dot(yl_ref[sl, :], wo1_ref[...], preferred_element_type=F32)
         + jnp.dot(yr_ref[sl, :], wo2_ref[...], preferred_element_type=F32) for sl in blocks]
    x1 = [x_ref[sl, :] + g1_ref[...] * yb for sl, yb in zip(blocks, y)]
    h2 = [_adaln(xb, ng_ref[...], sh_ref[...], sc_ref[...]).astype(BF16) for xb in x1]
    lg_all = [jnp.dot(jnp.concatenate([hb, hb], axis=1), wr_ref[...],
                      preferred_element_type=F32) + br_ref[...] for hb in h2]
    for sl, xb, hb in zip(blocks, x1, h2):
        x1_ref[sl, :] = xb
        h2_ref[sl, :] = _pack_bf16_halves(hb)

    neg = -jnp.inf
    lane = lax.broadcasted_iota(I32, (rows, LANES), 1)
    lane_f = lane.astype(F32)

    def first_argmax(xs):
        m = [jnp.max(x, axis=-1, keepdims=True) for x in xs]
        idx = [jnp.min(jnp.where(x == mi, lane_f, float(LANES)), axis=-1, keepdims=True)
               for x, mi in zip(xs, m)]
        return m, [i.astype(I32) for i in idx]

    lg = [jnp.where(lane < N_GROUPS, x, neg) for x in lg_all]
    gm, g_idx = first_argmax(lg)
    g_w = [1.0 / jnp.sum(jnp.exp(x - m), axis=-1, keepdims=True)
           for x, m in zip(lg, gm)]
    lo = [N_GROUPS + EXPERTS_PER_GROUP * g for g in g_idx]
    le = [jnp.where((lane >= l) & (lane < l + EXPERTS_PER_GROUP), x, neg)
          for x, l in zip(lg_all, lo)]
    v1, i1 = first_argmax(le)
    v2, i2 = first_argmax([jnp.where(lane == i, neg, x) for x, i in zip(le, i1)])

    @pl.when((pl.program_id(0) == 0) & (pl.program_id(1) == 0))
    def _():
        cnt_ref[...] = jnp.zeros_like(cnt_ref)

    e_lane = lane + N_GROUPS
    hits = sum(jnp.sum(jnp.where((e_lane == a) | (e_lane == b), 1.0, 0.0), axis=0, keepdims=True)
               for a, b in zip(i1, i2))
    cnt_ref[...] = cnt_ref[...] + hits.astype(I32)

    for blk in range(nblk):
        e2 = jnp.exp(v2[blk] - v1[blk])
        w1 = g_w[blk] / (1.0 + e2)
        w2 = g_w[blk] * e2 / (1.0 + e2)
        sl = blocks[blk]
        rid_ref[sl, :] = jnp.where(lane == 0, i1[blk] - N_GROUPS,
                                   jnp.where(lane == 1, i2[blk] - N_GROUPS, 0))
        rw_ref[sl, :] = jnp.where(lane == 0, w1, jnp.where(lane == 1, w2, 0.0))


def _mix(x, y_lru, y_rw, w_out, g1, sh2, sc2, ng, w_grp, b_grp, w_exp, b_exp, tm=256):
    bsz, t, d = x.shape
    wo = w_out.astype(BF16)
    wl = LRU_WIDTH
    wr = jnp.zeros((d, LANES), F32).at[:, :N_GROUPS].set(w_grp)
    wr = wr.at[:, N_GROUPS:N_GROUPS + N_EXPERTS].set(w_exp)
    wr = jnp.concatenate(_split(wr, 2), axis=0)
    br = jnp.zeros((1, LANES), F32).at[0, :N_GROUPS].set(b_grp)
    br = br.at[0, N_GROUPS:N_GROUPS + N_EXPERTS].set(b_exp)
    vec = pl.BlockSpec((None, 1, d), lambda b, i: (b, 0, 0))
    const = lambda shape: pl.BlockSpec(shape, lambda b, i: (0, 0))
    row = lambda n: pl.BlockSpec((None, tm, n), lambda b, i: (b, i, 0))
    return pl.pallas_call(
        _mix_kernel,
        grid=(bsz, t // tm),
        in_specs=[row(d), row(wl), row(d - wl), const((wl, d)), const((d - wl, d)),
                  vec, vec, vec, const((1, d)), const((2 * d, LANES)), const((1, LANES))],
        out_specs=[row(d), row(d // 2), row(LANES), row(LANES), const((SUBLANES, LANES))],
        out_shape=[jax.ShapeDtypeStruct((bsz, t, d), F32),
                   jax.ShapeDtypeStruct((bsz, t, d // 2), I32),
                   jax.ShapeDtypeStruct((bsz, t, LANES), I32),
                   jax.ShapeDtypeStruct((bsz, t, LANES), F32),
                   jax.ShapeDtypeStruct((SUBLANES, LANES), I32)],
        compiler_params=_cparams(("arbitrary", "arbitrary")),
        name="outproj_adaln2_router",
    )(x, y_lru, y_rw, wo[:wl], wo[wl:], g1, sh2, sc2, ng.reshape(1, d), wr, br)


def _plan_kernel(rid_ref, cnt_ref, tril_ref, pos_ref, te_ref, base_ref):
    i = pl.program_id(0)
    tp = rid_ref.shape[0]
    lane = lax.broadcasted_iota(I32, (tp, LANES), 1)
    rid = rid_ref[...]
    oh0 = lane == rid[:, 0:1]
    oh1 = lane == rid[:, 1:2]
    oh = jnp.where(oh0 | oh1, 1.0, 0.0)

    @pl.when(i == 0)
    def _():
        cnt = cnt_ref[...]
        shift = EXPERT_TILE.bit_length() - 1
        padded = ((cnt + (EXPERT_TILE - 1)) >> shift) << shift
        l8 = lax.broadcasted_iota(I32, (SUBLANES, LANES), 1)
        end = padded
        s = 1
        while s < LANES:
            end = end + jnp.where(l8 >= s, pltpu.roll(end, s, 1), 0)
            s *= 2
        base_ref[...] = end - padded
        nt = te_ref.shape[0]
        j = lax.broadcasted_iota(I32, (nt, LANES), 0) * EXPERT_TILE
        lt = lax.broadcasted_iota(I32, (nt, LANES), 1)
        done = jnp.where((lt < N_EXPERTS) & (end[0:1, :] <= j), 1.0, 0.0)
        e_of = jnp.minimum(jnp.sum(done, axis=-1, keepdims=True), float(N_EXPERTS - 1)).astype(I32)
        total = end[0:1, N_EXPERTS - 1:N_EXPERTS]
        mine = lt == e_of
        cnt_e = jnp.sum(jnp.where(mine, cnt[0:1, :], 0), axis=-1, keepdims=True)
        start_e = jnp.sum(jnp.where(mine, end[0:1, :] - padded[0:1, :], 0), axis=-1, keepdims=True)
        valid = jnp.clip(cnt_e - (j[:, 0:1] - start_e), 0, EXPERT_TILE)
        te_ref[...] = jnp.where(lt == 0, e_of, jnp.where(lt == 1, total >> shift,
                                                        jnp.where(lt == 2, valid, 0)))

    prefix = jnp.dot(tril_ref[...], oh.astype(BF16), preferred_element_type=F32)
    dest = base_ref[0:1, :] + prefix.astype(I32)
    pos0 = jnp.sum(jnp.where(oh0, dest, 0), axis=-1, keepdims=True)
    pos1 = jnp.sum(jnp.where(oh1, dest, 0), axis=-1, keepdims=True)
    both = jnp.where(lane == 0, pos0, jnp.where(lane == 1, pos1, 0))
    pos_ref[...] = both.T[:SUBLANES, :]
    base_ref[...] = base_ref[...] + jnp.sum(oh, axis=0, keepdims=True).astype(I32)


def _plan(rid, cnt, n_tiles, tp=1024):
    n = rid.shape[0]
    tp = min(tp, n)
    nt_pad = -(-n_tiles // SUBLANES) * SUBLANES
    tril = (jnp.arange(tp)[:, None] > jnp.arange(tp)[None, :]).astype(BF16)
    pos, te = pl.pallas_call(
        _plan_kernel,
        grid=(n // tp,),
        in_specs=[pl.BlockSpec((tp, LANES), lambda i: (i, 0)),
                  pl.BlockSpec((SUBLANES, LANES), lambda i: (0, 0)),
                  pl.BlockSpec((tp, tp), lambda i: (0, 0))],
        out_specs=[pl.BlockSpec((None, SUBLANES, tp), lambda i: (i, 0, 0)),
                   pl.BlockSpec((nt_pad, LANES), lambda i: (0, 0))],
        out_shape=[jax.ShapeDtypeStruct((n // tp, SUBLANES, tp), I32),
                   jax.ShapeDtypeStruct((nt_pad, LANES), I32)],
        scratch_shapes=[pltpu.VMEM((SUBLANES, LANES), I32)],
        compiler_params=_cparams(("arbitrary",)),
        name="route_plan",
    )(rid, cnt, tril)
    return pos, te


SC_CORES = 2
SC_SUBCORES = 16
SC_WORKERS = SC_CORES * SC_SUBCORES
SC_WINDOW = 64


def _sc_mesh():
    return plsc.VectorSubcoreMesh(core_axis_name="c", subcore_axis_name="s",
                                  num_cores=SC_CORES, num_subcores=SC_SUBCORES)


def _sc_two_buffer_loop(n_win, fill, drain):
    assert n_win % 2 == 0

    def start(copies):
        for cp in copies:
            cp.start()

    def wait(copies):
        for cp in copies:
            cp.wait()

    start(fill(0, 0))

    @pl.loop(0, n_win, step=2)
    def _(j):
        for b in range(2):
            jj = j + b

            @pl.when(jj >= 1)
            def _():
                wait(drain(jj - 1, 1 - b))

            @pl.when(jj + 1 < n_win)
            def _():
                start(fill(jj + 1, 1 - b))

            wait(fill(jj, b))
            start(drain(jj, b))

    wait(drain(n_win - 1, 1))


def _sc_scatter_rows(rows, pos0, pos1, n_out):
    n, d = rows.shape
    per_w = n // SC_WORKERS
    n_win = per_w // SC_WINDOW
    shape3 = (SC_WORKERS, n_win, SC_WINDOW)

    @functools.partial(
        pl.kernel, mesh=_sc_mesh(), out_type=jax.ShapeDtypeStruct((n_out, d), rows.dtype),
        scratch_types=[pltpu.VMEM((n_win, SC_WINDOW), I32), pltpu.VMEM((n_win, SC_WINDOW), I32),
                       pltpu.VMEM((2, SC_WINDOW, d), rows.dtype),
                       pltpu.SemaphoreType.DMA((2,)), pltpu.SemaphoreType.DMA((2,))],
        name="sc_scatter_rows")
    def scatter(rows_hbm, p0_hbm, p1_hbm, out_hbm, p0_v, p1_v, buf, in_sem, out_sem):
        wid = lax.axis_index("s") * SC_CORES + lax.axis_index("c")
        base = wid * per_w
        pltpu.sync_copy(p0_hbm.at[wid], p0_v)
        pltpu.sync_copy(p1_hbm.at[wid], p1_v)

        def fill(j, b):
            src = rows_hbm.at[pl.ds(base + j * SC_WINDOW, SC_WINDOW)]
            return [pltpu.make_async_copy(src, buf.at[b], in_sem.at[b])]

        def drain(j, b):
            return [pltpu.make_async_copy(buf.at[b], out_hbm.at[p.at[j]], out_sem.at[b])
                    for p in (p0_v, p1_v)]

        _sc_two_buffer_loop(n_win, fill, drain)

    return scatter(rows, pos0.reshape(shape3), pos1.reshape(shape3))


def _sc_gather_rows(table, idx):
    m = idx.shape[0]
    d = table.shape[1]
    per_w = m // SC_WORKERS
    n_win = per_w // SC_WINDOW

    @functools.partial(
        pl.kernel, mesh=_sc_mesh(), out_type=jax.ShapeDtypeStruct((m, d), table.dtype),
        scratch_types=[pltpu.VMEM((n_win, SC_WINDOW), I32),
                       pltpu.VMEM((2, SC_WINDOW, d), table.dtype),
                       pltpu.SemaphoreType.DMA((2,)), pltpu.SemaphoreType.DMA((2,))],
        name="sc_gather_rows")
    def gather(table_hbm, idx_hbm, out_hbm, idx_v, buf, in_sem, out_sem):
        wid = lax.axis_index("s") * SC_CORES + lax.axis_index("c")
        base = wid * per_w
        pltpu.sync_copy(idx_hbm.at[wid], idx_v)

        def fill(j, b):
            return [pltpu.make_async_copy(table_hbm.at[idx_v.at[j]], buf.at[b], in_sem.at[b])]

        def drain(j, b):
            dst = out_hbm.at[pl.ds(base + j * SC_WINDOW, SC_WINDOW)]
            return [pltpu.make_async_copy(buf.at[b], dst, out_sem.at[b])]

        _sc_two_buffer_loop(n_win, fill, drain)

    return gather(table, idx.reshape(SC_WORKERS, n_win, SC_WINDOW))


def _expert_kernel(te_ref, nu_ref, valid_ref, x_ref, w1_ref, w3_ref, w2_ref, o_ref,
                   w1b_ref, w3b_ref, w2b_ref):
    j = pl.program_id(0)
    active = j < nu_ref[0]

    @pl.when(active & ((j == 0) | (te_ref[j] != te_ref[jnp.maximum(j - 1, 0)])))
    def _():
        w1b_ref[...] = w1_ref[...].astype(BF16)
        w3b_ref[...] = w3_ref[...].astype(BF16)
        w2b_ref[...] = w2_ref[...].astype(BF16)

    @pl.when(active)
    def _():
        nblk = 2
        rows = x_ref.shape[0] // nblk
        blocks = [slice(i * rows, (i + 1) * rows) for i in range(nblk)]
        row = lax.broadcasted_iota(I32, (rows, x_ref.shape[1]), 0)
        xs = [_unpack_bf16_halves(
            jnp.where(row + i * rows < valid_ref[j], x_ref[sl, :], 0)).astype(BF16)
            for i, sl in enumerate(blocks)]
        h1 = [jnp.dot(x, w1b_ref[...], preferred_element_type=F32) for x in xs]
        h3 = [jnp.dot(x, w3b_ref[...], preferred_element_type=F32) for x in xs]
        hid = [(a * jax.nn.sigmoid(a) * b).astype(BF16) for a, b in zip(h1, h3)]
        for sl, hb in zip(blocks, hid):
            o_ref[sl, :] = _pack_bf16_halves(
                jnp.dot(hb, w2b_ref[...], preferred_element_type=F32))

    @pl.when(j >= nu_ref[0])
    def _():
        o_ref[...] = jnp.zeros_like(o_ref)


def _experts(xs, te, nu, valid, w1, w3, w2):
    n_rows = xs.shape[0]
    d, de = w1.shape[-2:]
    nt = n_rows // EXPERT_TILE
    used = lambda j, nu: jnp.minimum(j, nu[0] - 1)
    grid_spec = pltpu.PrefetchScalarGridSpec(
        num_scalar_prefetch=3,
        grid=(nt,),
        in_specs=[
            pl.BlockSpec((EXPERT_TILE, d // 2), lambda j, te, nu, va: (used(j, nu), 0)),
            pl.BlockSpec((None, d, de), lambda j, te, nu, va: (te[used(j, nu)], 0, 0)),
            pl.BlockSpec((None, d, de), lambda j, te, nu, va: (te[used(j, nu)], 0, 0)),
            pl.BlockSpec((None, de, d), lambda j, te, nu, va: (te[used(j, nu)], 0, 0)),
        ],
        out_specs=pl.BlockSpec((EXPERT_TILE, d // 2), lambda j, te, nu, va: (j, 0)),
        scratch_shapes=[pltpu.VMEM((d, de), BF16), pltpu.VMEM((d, de), BF16),
                        pltpu.VMEM((de, d), BF16)],
    )
    return pl.pallas_call(
        _expert_kernel,
        grid_spec=grid_spec,
        out_shape=jax.ShapeDtypeStruct((n_rows, d // 2), I32),
        compiler_params=_cparams(("arbitrary",)),
        name="expert_mlp",
    )(te, nu, valid, xs, w1, w3, w2)


def _combine_kernel(x1_ref, rw_ref, g2_ref, fg_ref, y0_ref, y1_ref, o_ref):
    rw = rw_ref[...]
    moe = (rw[:, 0:1] * _unpack_bf16_halves(y0_ref[...])
           + rw[:, 1:2] * _unpack_bf16_halves(y1_ref[...]))
    x2 = x1_ref[...] + g2_ref[...] * moe
    ms = jnp.mean(x2 * x2, axis=-1, keepdims=True)
    o_ref[...] = x2 * lax.rsqrt(ms + RMS_EPS) * fg_ref[...]


def _combine(x1, rw, yg, g2, final_g, tc):
    bsz, t, d = x1.shape
    row = lambda n: pl.BlockSpec((None, tc, n), lambda b, i: (b, i, 0))
    slot = lambda s: pl.BlockSpec((None, None, tc, d // 2), lambda b, i: (s, b, i, 0))
    return pl.pallas_call(
        _combine_kernel,
        grid=(bsz, t // tc),
        in_specs=[row(d), row(LANES),
                  pl.BlockSpec((None, 1, d), lambda b, i: (b, 0, 0)),
                  pl.BlockSpec((1, d), lambda b, i: (0, 0)),
                  slot(0), slot(1)],
        out_specs=row(d),
        out_shape=jax.ShapeDtypeStruct((bsz, t, d), F32),
        compiler_params=_cparams(("arbitrary", "arbitrary")),
        name="combine_final_norm",
    )(x1, rw, g2, final_g.reshape(1, d), yg, yg)


def _row_tile(t, want):
    return want if t % want == 0 else t


def kernel(x, c, w_ada, b_ada, norm1_g, w_in, conv_w, conv_b, lru_wa, lru_ba, lru_wi, lru_bi, lru_lam, lru_norm_g, tok_mu, w0, w_up, a0, a_up, g_up, k_k, k_a, r_k, ln_x_w, ln_x_b, w_out, norm2_g, w_grp, b_grp, w_exp, b_exp, w1, w3, w2, final_g):
    bsz, t, d = x.shape
    n = bsz * t
    depth = w_ada.shape[0]
    assert depth == 1, "the combine kernel applies the final norm: only DEPTH == 1 is wired"
    tile = _row_tile(t, ROW_TILE)
    n_tiles = (n * TOP_K) // EXPERT_TILE + N_EXPERTS
    for l in range(depth):
        mod = _modulation(c, w_ada[l], b_ada[l]).reshape(bsz, 6, 1, d)
        sh1, sc1, g1, sh2, sc2, g2 = (mod[:, i] for i in range(6))
        y_lru, p_rw = _inproj_lru(x, sh1, sc1, norm1_g[l], w_in[l], conv_w[l], conv_b[l],
                                  lru_wa[l], lru_ba[l], lru_wi[l], lru_bi[l], lru_lam[l],
                                  lru_norm_g[l], tm=tile)
        y_rw = _rwkv7(p_rw, tok_mu[l], w0[l], w_up[l], a0[l], a_up[l], g_up[l], k_k[l], k_a[l],
                      r_k[l], ln_x_w[l], ln_x_b[l])
        x1, h2, rid, rw, cnt = _mix(x, y_lru, y_rw, w_out[l], g1, sh2, sc2, norm2_g[l],
                                    w_grp[l], b_grp[l], w_exp[l], b_exp[l],
                                    tm=_row_tile(t, MIX_TILE))
        pos, te = _plan(rid.reshape(n, LANES), cnt, n_tiles)
        pos0, pos1 = pos[:, 0, :].reshape(n), pos[:, 1, :].reshape(n)
        xs = _sc_scatter_rows(h2.reshape(n, d // 2), pos0, pos1, n_tiles * EXPERT_TILE)
        ys = _experts(xs, te[:n_tiles, 0], te[0:1, 1], te[:n_tiles, 2], w1[l], w3[l], w2[l])
        yg = _sc_gather_rows(ys, jnp.concatenate([pos0, pos1]))
        x = _combine(x1, rw, yg.reshape(TOP_K, bsz, t, d // 2), g2, final_g,
                     _row_tile(t, COMBINE_TILE))
    return x
```
